```python
import jax
import jax.numpy as jnp
from jax import lax
import numpy as np

D_MODEL = 2048
BATCH = 2
SEQ = 8192
DEPTH = 1

D_MIX = D_MODEL
CONV_W = D_MIX // 2
CONV_TAPS = 3
HEAD_DIM = 128
ATTN_W = D_MIX - CONV_W
N_HEADS = ATTN_W // HEAD_DIM
N_KV_HEADS = 2
GROUP = N_HEADS // N_KV_HEADS
KV_W = N_KV_HEADS * HEAD_DIM
CMP_BLOCK = 32
CMP_STRIDE = 16
SEL_BLOCK = 64
N_SELECT = 16
WINDOW = 512
Q_BLOCK = 128
N_BRANCH = 3
ROPE_THETA = 10000.0
RMS_EPS = 1e-6
FFN_HIDDEN = -(-(8 * D_MODEL) // (3 * 256)) * 256
PROJ_SIZES = (CONV_W, CONV_W, CONV_W, ATTN_W) + (KV_W,) * 6 + (N_HEADS * N_BRANCH,)
D_PROJ = sum(PROJ_SIZES)

kernel_name = "hybrid_shortconv_nsa_swiglu"


def rms_norm(x, g):
    xf = x.astype(jnp.float32)
    y = xf * lax.rsqrt(jnp.mean(xf * xf, axis=-1, keepdims=True) + RMS_EPS)
    return (y * g.astype(jnp.float32)).astype(x.dtype)


def rope(x, positions):
    d = x.shape[-1]
    half = d // 2
    inv = 1.0 / (ROPE_THETA ** (jnp.arange(half, dtype=jnp.float32) / half))
    ang = positions.astype(jnp.float32)[:, None] * inv[None, :]
    cos = jnp.cos(ang)[None, :, None, :]
    sin = jnp.sin(ang)[None, :, None, :]
    xf = x.astype(jnp.float32)
    x1, x2 = xf[..., :half], xf[..., half:]
    return jnp.concatenate([x1 * cos - x2 * sin, x2 * cos + x1 * sin], axis=-1).astype(x.dtype)


def masked_softmax(s, mask):
    s = jnp.where(mask, s.astype(jnp.float32), -jnp.inf)
    m = jnp.max(s, axis=-1, keepdims=True)
    m = jnp.where(jnp.isfinite(m), m, 0.0)
    e = jnp.exp(s - m)
    return e / jnp.maximum(jnp.sum(e, axis=-1, keepdims=True), 1e-30)


def causal_depthwise_conv(u, w):
    S = u.shape[1]
    up = jnp.pad(u, ((0, 0), (CONV_TAPS - 1, 0), (0, 0)))
    y = up[:, 0:S] * w[0]
    for k in range(1, CONV_TAPS):
        y = y + up[:, k:k + S] * w[k]
    return y


def compress(kv, pe, w1, w2):
    B, S, Hk, d = kv.shape
    n_cmp = (S - CMP_BLOCK) // CMP_STRIDE + 1
    idx = jnp.arange(n_cmp)[:, None] * CMP_STRIDE + jnp.arange(CMP_BLOCK)[None, :]
    blocks = kv[:, idx] + pe[:, None, :]
    flat = blocks.transpose(0, 3, 1, 2, 4).reshape(B, Hk, n_cmp, CMP_BLOCK * d)
    return jax.nn.silu(flat @ w1) @ w2


def nsa_attention(q, kc, vc, ks, vs, kw, vw):
    B, Hk, G, S, d = q.shape
    scale = d ** -0.5
    n_cmp = kc.shape[2]
    n_sel = S // SEL_BLOCK
    k_top = min(N_SELECT, n_sel)
    cmp_start = jnp.arange(n_cmp) * CMP_STRIDE
    cmp_end = cmp_start + CMP_BLOCK - 1
    sel_ids = jnp.arange(n_sel)
    sel_start = sel_ids * SEL_BLOCK
    overlap = ((cmp_start[:, None] < sel_start[None, :] + SEL_BLOCK)
               & (cmp_start[:, None] + CMP_BLOCK > sel_start[None, :])).astype(jnp.float32)
    ks_blk = ks.reshape(B, Hk, n_sel, SEL_BLOCK, d)
    vs_blk = vs.reshape(B, Hk, n_sel, SEL_BLOCK, d)
    kw_pad = jnp.pad(kw, ((0, 0), (0, 0), (WINDOW, 0), (0, 0)))
    vw_pad = jnp.pad(vw, ((0, 0), (0, 0), (WINDOW, 0), (0, 0)))
    b_ix = jnp.arange(B)[:, None, None, None]
    h_ix = jnp.arange(Hk)[None, :, None, None]

    def block_fn(q0):
        t = q0 + jnp.arange(Q_BLOCK)
        qb = lax.dynamic_slice_in_dim(q, q0, Q_BLOCK, axis=3)
        s_c = jnp.einsum('bhgqd,bhnd->bhgqn', qb, kc) * scale
        p_c = masked_softmax(s_c, cmp_end[None, :] <= t[:, None])
        o_c = jnp.einsum('bhgqn,bhnd->bhgqd', p_c.astype(vc.dtype), vc)
        imp = jnp.einsum('bhgqn,ns->bhqs', p_c, overlap)
        cur = t // SEL_BLOCK
        future = sel_ids[None, :] > cur[:, None]
        forced = ((sel_ids[None, :] == 0) | (sel_ids[None, :] == cur[:, None])
                  | (sel_ids[None, :] == cur[:, None] - 1))
        imp = jnp.where(future, -jnp.inf, jnp.where(forced, jnp.inf, imp))
        _, top_idx = lax.top_k(imp, k_top)
        top_ok = top_idx <= cur[None, None, :, None]
        kg = ks_blk[b_ix, h_ix, top_idx]
        vg = vs_blk[b_ix, h_ix, top_idx]
        s_s = jnp.einsum('bhgqd,bhqkld->bhgqkl', qb, kg) * scale
        pos_s = top_idx[..., None] * SEL_BLOCK + jnp.arange(SEL_BLOCK)
        m_s = (pos_s <= t[None, None, :, None, None]) & top_ok[..., None]
        p_s = masked_softmax(s_s.reshape(B, Hk, G, Q_BLOCK, k_top * SEL_BLOCK),
                             m_s.reshape(B, Hk, 1, Q_BLOCK, k_top * SEL_BLOCK)).reshape(s_s.shape)
        o_s = jnp.einsum('bhgqkl,bhqkld->bhgqd', p_s.astype(vg.dtype), vg)
        kwb = lax.dynamic_slice_in_dim(kw_pad, q0, WINDOW + Q_BLOCK, axis=2)
        vwb = lax.dynamic_slice_in_dim(vw_pad, q0, WINDOW + Q_BLOCK, axis=2)
        pos_w = q0 - WINDOW + jnp.arange(WINDOW + Q_BLOCK)
        m_w = ((pos_w[None, :] <= t[:, None]) & (pos_w[None, :] > t[:, None] - WINDOW)
               & (pos_w[None, :] >= 0))
        s_w = jnp.einsum('bhgqd,bhkd->bhgqk', qb, kwb) * scale
        o_w = jnp.einsum('bhgqk,bhkd->bhgqd', masked_softmax(s_w, m_w).astype(vwb.dtype), vwb)
        return o_c, o_s, o_w

    starts = jnp.arange(S // Q_BLOCK) * Q_BLOCK
    o_c, o_s, o_w = lax.map(block_fn, starts)

    def to_bshd(o):
        return o.transpose(1, 0, 4, 2, 3, 5).reshape(B, S, Hk * G, d)

    return to_bshd(o_c), to_bshd(o_s), to_bshd(o_w)


def setup_inputs(seed: int = 0) -> dict:
    key = jax.random.key(seed)
    ks = jax.random.split(key, 20)

    def nrm(k, shape, scale):
        return jax.random.normal(k, shape, jnp.float32) * scale

    def gain(k, shape):
        return 1.0 + nrm(k, shape, 0.02)

    return {
        "x": nrm(ks[0], (BATCH, SEQ, D_MODEL), 1.0),
        "norm_mix": gain(ks[1], (DEPTH, D_MODEL)),
        "w_in": nrm(ks[2], (DEPTH, D_MODEL, D_PROJ), D_MODEL ** -0.5),
        "conv_w": nrm(ks[3], (DEPTH, CONV_TAPS, CONV_W), CONV_TAPS ** -0.5),
        "cmp_pe_k": nrm(ks[4], (DEPTH, CMP_BLOCK, HEAD_DIM), 0.02),
        "cmp_w1_k": nrm(ks[5], (DEPTH, CMP_BLOCK * HEAD_DIM, HEAD_DIM), (CMP_BLOCK * HEAD_DIM) ** -0.5),
        "cmp_w2_k": nrm(ks[6], (DEPTH, HEAD_DIM, HEAD_DIM), HEAD_DIM ** -0.5),
        "cmp_pe_v": nrm(ks[7], (DEPTH, CMP_BLOCK, HEAD_DIM), 0.02),
        "cmp_w1_v": nrm(ks[8], (DEPTH, CMP_BLOCK * HEAD_DIM, HEAD_DIM), (CMP_BLOCK * HEAD_DIM) ** -0.5),
        "cmp_w2_v": nrm(ks[9], (DEPTH, HEAD_DIM, HEAD_DIM), HEAD_DIM ** -0.5),
        "norm_conv_out": gain(ks[10], (DEPTH, CONV_W)),
        "norm_attn_out": gain(ks[11], (DEPTH, ATTN_W)),
        "w_out": nrm(ks[12], (DEPTH, D_MIX, D_MODEL), D_MIX ** -0.5),
        "norm_ffn": gain(ks[13], (DEPTH, D_MODEL)),
        "w_gate": nrm(ks[14], (DEPTH, D_MODEL, FFN_HIDDEN), D_MODEL ** -0.5),
        "w_up": nrm(ks[15], (DEPTH, D_MODEL, FFN_HIDDEN), D_MODEL ** -0.5),
        "w_down": nrm(ks[16], (DEPTH, FFN_HIDDEN, D_MODEL), FFN_HIDDEN ** -0.5),
        "norm_final": gain(ks[17], (D_MODEL,)),
    }


def reference(x, norm_mix, w_in, conv_w, cmp_pe_k, cmp_w1_k, cmp_w2_k, cmp_pe_v, cmp_w1_v,
              cmp_w2_v, norm_conv_out, norm_attn_out, w_out, norm_ffn, w_gate, w_up, w_down,
              norm_final):
    B, S, _ = x.shape
    positions = jnp.arange(S)
    split_idx = np.cumsum(PROJ_SIZES)[:-1].tolist()
    h = x
    for l in range(DEPTH):
        a = rms_norm(h, norm_mix[l])
        z = a @ w_in[l]
        (cb, cc, ch, zq, zkc, zvc, zks, zvs, zkw, zvw, zg) = jnp.split(z, split_idx, axis=-1)

        y_conv = cb * causal_depthwise_conv(cc * ch, conv_w[l])

        q = rope(zq.reshape(B, S, N_HEADS, HEAD_DIM), positions)
        q = q.reshape(B, S, N_KV_HEADS, GROUP, HEAD_DIM).transpose(0, 2, 3, 1, 4)

        def kv_heads(t):
            return t.reshape(B, S, N_KV_HEADS, HEAD_DIM)

        k_cmp = compress(rope(kv_heads(zkc), positions), cmp_pe_k[l], cmp_w1_k[l], cmp_w2_k[l])
        v_cmp = compress(kv_heads(zvc), cmp_pe_v[l], cmp_w1_v[l], cmp_w2_v[l])
        k_sel = rope(kv_heads(zks), positions).transpose(0, 2, 1, 3)
        v_sel = kv_heads(zvs).transpose(0, 2, 1, 3)
        k_win = rope(kv_heads(zkw), positions).transpose(0, 2, 1, 3)
        v_win = kv_heads(zvw).transpose(0, 2, 1, 3)
        o_c, o_s, o_w = nsa_attention(q, k_cmp, v_cmp, k_sel, v_sel, k_win, v_win)
        g = jax.nn.sigmoid(zg.reshape(B, S, N_HEADS, N_BRANCH))
        y_attn = (g[..., 0:1] * o_c + g[..., 1:2] * o_s + g[..., 2:3] * o_w).reshape(B, S, ATTN_W)

        y_mix = jnp.concatenate([rms_norm(y_conv, norm_conv_out[l]),
                                 rms_norm(y_attn, norm_attn_out[l])], axis=-1)
        h = h + y_mix @ w_out[l]

        f = rms_norm(h, norm_ffn[l])
        h = h + (jax.nn.silu(f @ w_gate[l]) * (f @ w_up[l])) @ w_down[l]
    return rms_norm(h, norm_final)
```

```python
import functools

import jax
import jax.numpy as jnp
import numpy as np
from jax import lax
from jax.experimental import pallas as pl
from jax.experimental.pallas import tpu as pltpu

F32 = jnp.float32
BF16 = jnp.bfloat16

HEAD_DIM = 128
N_HEADS = 8
N_KV_HEADS = 2
GROUP = N_HEADS // N_KV_HEADS
N_BRANCH = 3
CONV_TAPS = 3
CMP_BLOCK = 32
CMP_STRIDE = 16
SEL_BLOCK = 64
N_SELECT = 16
WINDOW = 512
Q_BLOCK = 128
ROPE_THETA = 10000.0
RMS_EPS = 1e-6

CONV_W = 1024
ATTN_W = N_HEADS * HEAD_DIM
KV_W = N_KV_HEADS * HEAD_DIM
N_GATES = N_HEADS * N_BRANCH

V7X_VMEM_BYTES = 64 * 1024 * 1024
V7X_VMEM_BUDGET = V7X_VMEM_BYTES - 8 * 1024 * 1024
LANES = 128
SUBLANES = 8
PROJ_TILE_N = 256

MASK_BIAS = -float(2 ** 30)


def _params(semantics, vmem_bytes):
    return pltpu.CompilerParams(dimension_semantics=semantics,
                                vmem_limit_bytes=int(min(vmem_bytes, V7X_VMEM_BUDGET)))


def _rms(x, g):
    return x * lax.rsqrt(jnp.mean(x * x, axis=-1, keepdims=True) + RMS_EPS) * g


def _split3(x):
    hi = x.astype(BF16)
    r1 = x - hi.astype(F32)
    mid = r1.astype(BF16)
    lo = (r1 - mid.astype(F32)).astype(BF16)
    return hi, mid, lo


def _dot(a, b):
    return jnp.dot(a, b, preferred_element_type=F32)


def _dot_nt(a, b):
    return lax.dot_general(a, b, (((1,), (1,)), ((), ())), preferred_element_type=F32)


def _dot_f32(a, b):
    a0, a1, a2 = _split3(a)
    b0, b1, b2 = _split3(b)
    return (_dot(a0, b0) + (_dot(a0, b1) + _dot(a1, b0))
            + (_dot(a0, b2) + _dot(a1, b1) + _dot(a2, b0)))


def _inproj_body(x_ref, g_ref, w_ref, cos_ref, sin_ref, zc_ref, q_ref, kv_ref, zg_ref, a_scr,
                 *, n_conv, n_q, n_kv, q_scale):
    j = pl.program_id(1)

    @pl.when(j == 0)
    def _():
        a_scr[...] = _rms(x_ref[...], g_ref[...]).astype(BF16)

    z = _dot(a_scr[...], w_ref[...])

    def rope(z2):
        cos = cos_ref[...]
        sin = sin_ref[...]
        heads = []
        for h in range(PROJ_TILE_N // HEAD_DIM):
            zh = z2[:, h * HEAD_DIM:(h + 1) * HEAD_DIM]
            heads.append(zh * cos + pltpu.roll(zh, HEAD_DIM // 2, axis=1) * sin)
        return jnp.concatenate(heads, axis=1)

    kv_j = j - (n_conv + n_q)
    is_kv = (kv_j >= 0) & (kv_j < n_kv)

    @pl.when(j < n_conv)
    def _():
        zc_ref[...] = z

    @pl.when((j >= n_conv) & (j < n_conv + n_q))
    def _():
        q_ref[...] = (rope(z) * q_scale).astype(BF16)

    @pl.when(is_kv & (kv_j % 2 == 0))
    def _():
        kv_ref[...] = rope(z).astype(BF16)

    @pl.when(is_kv & (kv_j % 2 == 1))
    def _():
        kv_ref[...] = z.astype(BF16)

    @pl.when(j == n_conv + n_q + n_kv)
    def _():
        zg_ref[...] = z


def _in_projection(x2, g, w_pad, cos_t, sin_t, seq):
    T, D = x2.shape
    n_conv = 3 * CONV_W // PROJ_TILE_N
    n_q = ATTN_W // PROJ_TILE_N
    n_kv = 6 * KV_W // PROJ_TILE_N
    n_tiles = n_conv + n_q + n_kv + 1
    assert w_pad.shape == (D, n_tiles * PROJ_TILE_N)
    tm = min(1024, seq)
    pos_blocks = seq // tm
    tn = PROJ_TILE_N
    body = functools.partial(_inproj_body, n_conv=n_conv, n_q=n_q, n_kv=n_kv,
                             q_scale=HEAD_DIM ** -0.5)
    vmem = 2 * (tm * D * 4 + D * tn * 2 + 2 * tm * HEAD_DIM * 4 + 2 * tm * tn * 4 + 2 * tm * tn * 2) \
        + tm * D * 2 + 4 * tm * D * 4
    return pl.pallas_call(
        body,
        grid=(T // tm, n_tiles),
        in_specs=[
            pl.BlockSpec((tm, D), lambda i, j: (i, 0)),
            pl.BlockSpec((1, D), lambda i, j: (0, 0)),
            pl.BlockSpec((D, tn), lambda i, j: (0, j)),
            pl.BlockSpec((tm, HEAD_DIM), lambda i, j: (i % pos_blocks, 0)),
            pl.BlockSpec((tm, HEAD_DIM), lambda i, j: (i % pos_blocks, 0)),
        ],
        out_specs=[
            pl.BlockSpec((tm, tn), lambda i, j: (i, jnp.minimum(j, n_conv - 1))),
            pl.BlockSpec((tm, tn), lambda i, j: (i, jnp.clip(j - n_conv, 0, n_q - 1))),
            pl.BlockSpec((tm, tn), lambda i, j: (i, jnp.clip(j - n_conv - n_q, 0, n_kv - 1))),
            pl.BlockSpec((tm, tn), lambda i, j: (i, 0)),
        ],
        out_shape=[
            jax.ShapeDtypeStruct((T, 3 * CONV_W), F32),
            jax.ShapeDtypeStruct((T, ATTN_W), BF16),
            jax.ShapeDtypeStruct((T, 6 * KV_W), BF16),
            jax.ShapeDtypeStruct((T, tn), F32),
        ],
        scratch_shapes=[pltpu.VMEM((tm, D), BF16)],
        compiler_params=_params(("parallel", "arbitrary"), vmem),
        name="in_projection",
    )(x2, g, w_pad, cos_t, sin_t)


def _conv_body(cb_ref, cc_ref, ch_ref, hc_ref, hh_ref, w_ref, g_ref, o_ref, *, blocks_per_seq):
    i = pl.program_id(0)
    u = cc_ref[...] * ch_ref[...]
    halo = hc_ref[...] * hh_ref[...]
    halo = jnp.where(i % blocks_per_seq == 0, 0.0, halo)
    row = lax.broadcasted_iota(jnp.int32, u.shape, 0)
    h1 = halo[SUBLANES - 1:SUBLANES]
    h2 = halo[SUBLANES - 2:SUBLANES - 1]
    u1 = jnp.where(row == 0, h1, pltpu.roll(u, 1, axis=0))
    u2 = jnp.where(row == 0, h2, jnp.where(row == 1, h1, pltpu.roll(u, 2, axis=0)))
    w = w_ref[...]
    y = cb_ref[...] * (u2 * w[0:1] + u1 * w[1:2] + u * w[2:3])
    o_ref[...] = _rms(y, g_ref[...]).astype(BF16)


def _short_conv(zc, conv_w, g, seq):
    T = zc.shape[0]
    C = CONV_W
    tm = min(512, seq)
    hb = tm // SUBLANES
    body = functools.partial(_conv_body, blocks_per_seq=seq // tm)
    vmem = 2 * (3 * tm * C * 4 + 2 * SUBLANES * C * 4 + tm * C * 2) + 6 * tm * C * 4
    return pl.pallas_call(
        body,
        grid=(T // tm,),
        in_specs=[
            pl.BlockSpec((tm, C), lambda i: (i, 0)),
            pl.BlockSpec((tm, C), lambda i: (i, 1)),
            pl.BlockSpec((tm, C), lambda i: (i, 2)),
            pl.BlockSpec((SUBLANES, C), lambda i: (jnp.maximum(i * hb - 1, 0), 1)),
            pl.BlockSpec((SUBLANES, C), lambda i: (jnp.maximum(i * hb - 1, 0), 2)),
            pl.BlockSpec((CONV_TAPS, C), lambda i: (0, 0)),
            pl.BlockSpec((1, C), lambda i: (0, 0)),
        ],
        out_specs=pl.BlockSpec((tm, C), lambda i: (i, 0)),
        out_shape=jax.ShapeDtypeStruct((T, C), BF16),
        compiler_params=_params(("parallel",), vmem),
        name="short_conv",
    )(zc, zc, zc, zc, zc, conv_w, g)


def _compress_body(x_ref, pe_ref, w1_ref, w2_ref, o_ref, *, n_valid):
    x = x_ref[0, 0]
    half = x.shape[1]
    w1 = w1_ref[0]
    w1_hi = w1.astype(BF16)
    w1_lo = (w1 - w1_hi.astype(F32)).astype(BF16)
    first = _dot(x, w1_hi[:half]) + _dot(x, w1_lo[:half])
    second = _dot(x, w1_hi[half:]) + _dot(x, w1_lo[half:])
    n = x.shape[0]
    second = pltpu.roll(second, n - 1, axis=0)
    pe = jnp.broadcast_to(pe_ref[0], (SUBLANES, 2 * half))
    pe_term = _dot_f32(pe, w1)[0:1]
    h = first + second + pe_term
    out = _dot_f32(jax.nn.silu(h), w2_ref[0])
    row = lax.broadcasted_iota(jnp.int32, out.shape, 0)
    o_ref[0, 0] = jnp.where(row < n_valid, out, 0.0)


def _compress(chunks, pe, w1, w2):
    two, BH, n, width = chunks.shape
    d = HEAD_DIM
    body = functools.partial(_compress_body, n_valid=n - 1)
    vmem = 2 * (n * width * 2 + 2 * width * 4 + 2 * width * d * 4 + d * d * 4 + n * d * 4) \
        + 8 * width * d * 4
    return pl.pallas_call(
        body,
        grid=(two, BH),
        in_specs=[
            pl.BlockSpec((1, 1, n, width), lambda s, b: (s, b, 0, 0)),
            pl.BlockSpec((1, 1, 2 * width), lambda s, b: (s, 0, 0)),
            pl.BlockSpec((1, 2 * width, d), lambda s, b: (s, 0, 0)),
            pl.BlockSpec((1, d, d), lambda s, b: (s, 0, 0)),
        ],
        out_specs=pl.BlockSpec((1, 1, n, d), lambda s, b: (s, b, 0, 0)),
        out_shape=jax.ShapeDtypeStruct((two, BH, n, d), F32),
        compiler_params=_params(("parallel", "parallel"), vmem),
        name="compress",
    )(chunks, pe, w1, w2)


def _stack_heads(q):
    return jnp.concatenate([q[:, g * HEAD_DIM:(g + 1) * HEAD_DIM] for g in range(GROUP)], axis=0)


def _row_time(shape, q0):
    return q0 + (lax.broadcasted_iota(jnp.int32, shape, 0) & (Q_BLOCK - 1))


def _cmp_body(q_ref, kc_ref, vc_ref, ovt_ref, o_ref, bias_ref):
    q0 = pl.program_id(2) * Q_BLOCK
    qs = _stack_heads(q_ref[0])
    kc = kc_ref[0, 0].astype(BF16)
    vc = vc_ref[0, 0].astype(BF16)
    s = _dot_nt(qs, kc)
    t = _row_time(s.shape, q0)
    cmp_end = lax.broadcasted_iota(jnp.int32, s.shape, 1) * CMP_STRIDE + (CMP_BLOCK - 1)
    s = jnp.where(cmp_end <= t, s, -jnp.inf)
    m = jnp.max(s, axis=-1, keepdims=True)
    m = jnp.where(m == -jnp.inf, 0.0, m)
    e = jnp.exp(s - m)
    p = e / jnp.maximum(jnp.sum(e, axis=-1, keepdims=True), 1e-30)
    o = _dot(p.astype(BF16), vc)
    for g in range(GROUP):
        o_ref[0, :, g * HEAD_DIM:(g + 1) * HEAD_DIM] = o[g * Q_BLOCK:(g + 1) * Q_BLOCK]

    pg = p[0:Q_BLOCK]
    for g in range(1, GROUP):
        pg = pg + p[g * Q_BLOCK:(g + 1) * Q_BLOCK]
    ovt = ovt_ref[...]
    p0, p1, p2 = _split3(pg)
    imp = _dot_nt(ovt, p0) + _dot_nt(ovt, p1) + _dot_nt(ovt, p2)

    sel_i = lax.broadcasted_iota(jnp.int32, imp.shape, 0)
    sel_f = sel_i.astype(F32)
    cur = (q0 + lax.broadcasted_iota(jnp.int32, imp.shape, 1)) // SEL_BLOCK
    forced = (sel_i == 0) | (sel_i == cur) | (sel_i == cur - 1)
    v = jnp.where(sel_i > cur, -jnp.inf, jnp.where(forced, jnp.inf, imp))
    n_sel = imp.shape[0]
    chosen = jnp.zeros(imp.shape, F32)
    for _ in range(min(N_SELECT, n_sel)):
        top = jnp.max(v, axis=0, keepdims=True)
        first = jnp.min(jnp.where(v == top, sel_f, float(n_sel)), axis=0, keepdims=True)
        pick = sel_f == first
        chosen = jnp.where(pick, 1.0, chosen)
        v = jnp.where(pick, -jnp.inf, v)
    bias_t = jnp.where((chosen > 0.0) & (sel_i <= cur), 0.0, MASK_BIAS)
    bias_ref[0, 0] = bias_t.T.astype(BF16)


def _cmp_attention(q, kvc, ovt, B, seq):
    n_cmp = kvc.shape[2]
    n_sel = ovt.shape[0]
    GW = GROUP * HEAD_DIM
    vmem = 2 * (Q_BLOCK * GW * 2 + 2 * n_cmp * HEAD_DIM * 4 + n_sel * n_cmp * 2
                + Q_BLOCK * GW * 4 + Q_BLOCK * n_sel * 2) + 10 * GROUP * Q_BLOCK * n_cmp * 4
    return pl.pallas_call(
        _cmp_body,
        grid=(B, N_KV_HEADS, seq // Q_BLOCK),
        in_specs=[
            pl.BlockSpec((1, Q_BLOCK, GW), lambda b, h, i: (b, i, h)),
            pl.BlockSpec((1, 1, n_cmp, HEAD_DIM), lambda b, h, i: (0, b * N_KV_HEADS + h, 0, 0)),
            pl.BlockSpec((1, 1, n_cmp, HEAD_DIM), lambda b, h, i: (1, b * N_KV_HEADS + h, 0, 0)),
            pl.BlockSpec((n_sel, n_cmp), lambda b, h, i: (0, 0)),
        ],
        out_specs=[
            pl.BlockSpec((1, Q_BLOCK, GW), lambda b, h, i: (b, i, h)),
            pl.BlockSpec((1, 1, Q_BLOCK, n_sel), lambda b, h, i: (b, h, i, 0)),
        ],
        out_shape=[
            jax.ShapeDtypeStruct((B, seq, ATTN_W), F32),
            jax.ShapeDtypeStruct((B, N_KV_HEADS, seq, n_sel), BF16),
        ],
        compiler_params=_params(("parallel", "parallel", "parallel"), vmem),
        name="cmp_attention",
    )(q, kvc, kvc, ovt)


def _sel_body(q_ref, bias_ref, k_ref, v_ref, e_ref, o_ref, *, tk):
    qb = pl.program_id(2)
    q0 = qb * Q_BLOCK
    q = q_ref[0]
    bias = bias_ref[0, 0]
    qx = jnp.concatenate(
        [jnp.concatenate([q[:, g * HEAD_DIM:(g + 1) * HEAD_DIM], bias], axis=1) for g in range(GROUP)],
        axis=0)
    rows = qx.shape[0]
    n_tiles = q0 // tk + 1

    def tile(j, carry, causal):
        m, l, acc = carry
        start = pl.multiple_of(j * tk, tk)
        kx = jnp.concatenate([k_ref[0, pl.ds(start, tk), :], e_ref[pl.ds(start, tk), :]], axis=1)
        s = _dot_nt(qx, kx)
        if causal:
            key = start + lax.broadcasted_iota(jnp.int32, s.shape, 1)
            s = jnp.where(key <= _row_time(s.shape, q0), s, -jnp.inf)
        m_new = jnp.maximum(m, jnp.max(s, axis=-1, keepdims=True))
        alpha = jnp.exp(m - m_new)
        p = jnp.exp(s - m_new)
        l = alpha * l + jnp.sum(p, axis=-1, keepdims=True)
        acc = alpha * acc + _dot(p.astype(BF16), v_ref[0, pl.ds(start, tk), :])
        return m_new, l, acc

    init = (jnp.full((rows, 1), -jnp.inf, F32), jnp.zeros((rows, 1), F32),
            jnp.zeros((rows, HEAD_DIM), F32))
    carry = lax.fori_loop(0, n_tiles - 1, lambda j, c: tile(j, c, False), init)
    _, l, acc = tile(n_tiles - 1, carry, True)
    o = acc / l
    for g in range(GROUP):
        o_ref[0, :, g * HEAD_DIM:(g + 1) * HEAD_DIM] = o[g * Q_BLOCK:(g + 1) * Q_BLOCK]


def _sel_attention(q, bias, kv, onehot, B, seq):
    n_sel = bias.shape[-1]
    GW = GROUP * HEAD_DIM
    tk = min(512, seq)
    k_col = 2 * N_KV_HEADS
    v_col = 3 * N_KV_HEADS
    body = functools.partial(_sel_body, tk=tk)
    vmem = 2 * (Q_BLOCK * GW * 2 + Q_BLOCK * n_sel * 2 + 2 * seq * HEAD_DIM * 2 + seq * n_sel * 2
                + Q_BLOCK * GW * 4) + 10 * GROUP * Q_BLOCK * tk * 4
    return pl.pallas_call(
        body,
        grid=(B, N_KV_HEADS, seq // Q_BLOCK),
        in_specs=[
            pl.BlockSpec((1, Q_BLOCK, GW), lambda b, h, i: (b, i, h)),
            pl.BlockSpec((1, 1, Q_BLOCK, n_sel), lambda b, h, i: (b, h, i, 0)),
            pl.BlockSpec((1, seq, HEAD_DIM), lambda b, h, i: (b, 0, k_col + h)),
            pl.BlockSpec((1, seq, HEAD_DIM), lambda b, h, i: (b, 0, v_col + h)),
            pl.BlockSpec((seq, n_sel), lambda b, h, i: (0, 0)),
        ],
        out_specs=pl.BlockSpec((1, Q_BLOCK, GW), lambda b, h, i: (b, i, h)),
        out_shape=jax.ShapeDtypeStruct((B, seq, ATTN_W), F32),
        compiler_params=_params(("parallel", "parallel", "arbitrary"), vmem),
        name="sel_attention",
    )(q, bias, kv, kv, onehot)


def _win_body(q_ref, k_ref, v_ref, o_ref, *, span):
    q0 = pl.program_id(2) * Q_BLOCK
    qs = _stack_heads(q_ref[0])
    start = pl.multiple_of(jnp.maximum(q0 - WINDOW, 0), Q_BLOCK)
    k = k_ref[0, pl.ds(start, span), :]
    v = v_ref[0, pl.ds(start, span), :]
    s = _dot_nt(qs, k)
    t = _row_time(s.shape, q0)
    pos = start + lax.broadcasted_iota(jnp.int32, s.shape, 1)
    s = jnp.where((pos <= t) & (pos > t - WINDOW), s, -jnp.inf)
    m = jnp.max(s, axis=-1, keepdims=True)
    e = jnp.exp(s - m)
    p = e / jnp.sum(e, axis=-1, keepdims=True)
    o = _dot(p.astype(BF16), v)
    for g in range(GROUP):
        o_ref[0, :, g * HEAD_DIM:(g + 1) * HEAD_DIM] = o[g * Q_BLOCK:(g + 1) * Q_BLOCK]


def _win_attention(q, kv, B, seq):
    GW = GROUP * HEAD_DIM
    span = WINDOW + Q_BLOCK
    assert seq >= span
    k_col = 4 * N_KV_HEADS
    v_col = 5 * N_KV_HEADS
    body = functools.partial(_win_body, span=span)
    vmem = 2 * (Q_BLOCK * GW * 2 + 2 * seq * HEAD_DIM * 2 + Q_BLOCK * GW * 4) \
        + 10 * GROUP * Q_BLOCK * span * 4
    return pl.pallas_call(
        body,
        grid=(B, N_KV_HEADS, seq // Q_BLOCK),
        in_specs=[
            pl.BlockSpec((1, Q_BLOCK, GW), lambda b, h, i: (b, i, h)),
            pl.BlockSpec((1, seq, HEAD_DIM), lambda b, h, i: (b, 0, k_col + h)),
            pl.BlockSpec((1, seq, HEAD_DIM), lambda b, h, i: (b, 0, v_col + h)),
        ],
        out_specs=pl.BlockSpec((1, Q_BLOCK, GW), lambda b, h, i: (b, i, h)),
        out_shape=jax.ShapeDtypeStruct((B, seq, ATTN_W), F32),
        compiler_params=_params(("parallel", "parallel", "parallel"), vmem),
        name="win_attention",
    )(q, kv, kv)


def _outproj_body(yc_ref, oc_ref, os_ref, ow_ref, zg_ref, g_ref, w_ref, x_ref, h_ref, a_scr):
    @pl.when(pl.program_id(1) == 0)
    def _():
        gates = jax.nn.sigmoid(zg_ref[:, 0:LANES])
        heads = []
        for h in range(N_HEADS):
            sl = slice(h * HEAD_DIM, (h + 1) * HEAD_DIM)
            c = h * N_BRANCH
            heads.append(gates[:, c:c + 1] * oc_ref[:, sl] + gates[:, c + 1:c + 2] * os_ref[:, sl]
                         + gates[:, c + 2:c + 3] * ow_ref[:, sl])
        y_attn = jnp.concatenate(heads, axis=1)
        a_scr[:, 0:CONV_W] = yc_ref[...]
        a_scr[:, CONV_W:] = _rms(y_attn, g_ref[...]).astype(BF16)

    h_ref[...] = x_ref[...] + _dot(a_scr[...], w_ref[...])


def _out_projection(yc, oc, osel, ow, zg, g, w, x2):
    T, D = x2.shape
    K = w.shape[0]
    tm = min(512, T)
    tn = min(512, D)
    vmem = 2 * (tm * CONV_W * 2 + 3 * tm * ATTN_W * 4 + tm * PROJ_TILE_N * 4 + K * tn * 2
                + 2 * tm * tn * 4) + tm * K * 2 + 4 * tm * ATTN_W * 4
    return pl.pallas_call(
        _outproj_body,
        grid=(T // tm, D // tn),
        in_specs=[
            pl.BlockSpec((tm, CONV_W), lambda i, j: (i, 0)),
            pl.BlockSpec((tm, ATTN_W), lambda i, j: (i, 0)),
            pl.BlockSpec((tm, ATTN_W), lambda i, j: (i, 0)),
            pl.BlockSpec((tm, ATTN_W), lambda i, j: (i, 0)),
            pl.BlockSpec((tm, PROJ_TILE_N), lambda i, j: (i, 0)),
            pl.BlockSpec((1, ATTN_W), lambda i, j: (0, 0)),
            pl.BlockSpec((K, tn), lambda i, j: (0, j)),
            pl.BlockSpec((tm, tn), lambda i, j: (i, j)),
        ],
        out_specs=pl.BlockSpec((tm, tn), lambda i, j: (i, j)),
        out_shape=jax.ShapeDtypeStruct((T, D), F32),
        scratch_shapes=[pltpu.VMEM((tm, K), BF16)],
        compiler_params=_params(("parallel", "arbitrary"), vmem),
        name="out_projection",
    )(yc, oc, osel, ow, zg, g, w, x2)


def _ffn_up_body(h_ref, g_ref, wg_ref, wu_ref, o_ref, a_scr):
    @pl.when(pl.program_id(1) == 0)
    def _():
        a_scr[...] = _rms(h_ref[...], g_ref[...]).astype(BF16)

    a = a_scr[...]
    o_ref[...] = (jax.nn.silu(_dot(a, wg_ref[...])) * _dot(a, wu_ref[...])).astype(BF16)


def _ffn_up(h, g, wg, wu):
    T, D = h.shape
    F = wg.shape[1]
    tm = min(1024, T)
    tn = 512
    assert F % tn == 0
    vmem = 2 * (tm * D * 4 + 2 * D * tn * 2 + tm * tn * 2) + tm * D * 2 + 3 * tm * D * 4
    return pl.pallas_call(
        _ffn_up_body,
        grid=(T // tm, F // tn),
        in_specs=[
            pl.BlockSpec((tm, D), lambda i, j: (i, 0)),
            pl.BlockSpec((1, D), lambda i, j: (0, 0)),
            pl.BlockSpec((D, tn), lambda i, j: (0, j)),
            pl.BlockSpec((D, tn), lambda i, j: (0, j)),
        ],
        out_specs=pl.BlockSpec((tm, tn), lambda i, j: (i, j)),
        out_shape=jax.ShapeDtypeStruct((T, F), BF16),
        scratch_shapes=[pltpu.VMEM((tm, D), BF16)],
        compiler_params=_params(("parallel", "arbitrary"), vmem),
        name="ffn_up",
    )(h, g, wg, wu)


def _ffn_down_body(u_ref, w_ref, h_ref, g_ref, o_ref, acc_scr, *, final_norm):
    k = pl.program_id(1)

    @pl.when(k == 0)
    def _():
        acc_scr[...] = jnp.zeros_like(acc_scr)

    acc_scr[...] += _dot(u_ref[...], w_ref[...])

    @pl.when(k == pl.num_programs(1) - 1)
    def _():
        out = h_ref[...] + acc_scr[...]
        o_ref[...] = _rms(out, g_ref[...]) if final_norm else out


def _ffn_down(u, w, h, g_final, final_norm):
    T, F = u.shape
    D = w.shape[1]
    tm = min(512, T)
    tk = 512
    assert F % tk == 0
    body = functools.partial(_ffn_down_body, final_norm=final_norm)
    vmem = 2 * (tm * tk * 2 + tk * D * 2 + 2 * tm * D * 4) + tm * D * 4 + 3 * tm * D * 4
    return pl.pallas_call(
        body,
        grid=(T // tm, F // tk),
        in_specs=[
            pl.BlockSpec((tm, tk), lambda i, k: (i, k)),
            pl.BlockSpec((tk, D), lambda i, k: (k, 0)),
            pl.BlockSpec((tm, D), lambda i, k: (i, 0)),
            pl.BlockSpec((1, D), lambda i, k: (0, 0)),
        ],
        out_specs=pl.BlockSpec((tm, D), lambda i, k: (i, 0)),
        out_shape=jax.ShapeDtypeStruct((T, D), F32),
        scratch_shapes=[pltpu.VMEM((tm, D), F32)],
        compiler_params=_params(("parallel", "arbitrary"), vmem),
        name="ffn_down",
    )(u, w, h, g_final)


def _rope_tables(seq):
    half = HEAD_DIM // 2
    inv = 1.0 / (ROPE_THETA ** (jnp.arange(half, dtype=F32) / half))
    ang = jnp.arange(seq).astype(F32)[:, None] * inv[None, :]
    cos = jnp.cos(ang)
    sin = jnp.sin(ang)
    return jnp.concatenate([cos, cos], axis=1), jnp.concatenate([-sin, sin], axis=1)


def _overlap_t(n_sel, n_cmp):
    cs = np.arange(n_cmp)[None, :] * CMP_STRIDE
    ss = np.arange(n_sel)[:, None] * SEL_BLOCK
    return jnp.asarray(((cs < ss + SEL_BLOCK) & (cs + CMP_BLOCK > ss)).astype(np.float32), dtype=BF16)


def _block_onehot(seq, n_sel):
    return jnp.asarray((np.arange(seq)[:, None] // SEL_BLOCK == np.arange(n_sel)[None, :])
                       .astype(np.float32), dtype=BF16)


def kernel(x, norm_mix, w_in, conv_w, cmp_pe_k, cmp_w1_k, cmp_w2_k, cmp_pe_v, cmp_w1_v, cmp_w2_v,
           norm_conv_out, norm_attn_out, w_out, norm_ffn, w_gate, w_up, w_down, norm_final):
    B, S, D = x.shape
    T = B * S
    depth = norm_mix.shape[0]
    assert S % Q_BLOCK == 0 and S % CMP_STRIDE == 0
    n_chunks = S // CMP_STRIDE
    n_sel = S // SEL_BLOCK
    cos_t, sin_t = _rope_tables(S)
    ovt = _overlap_t(n_sel, n_chunks)
    onehot = _block_onehot(S, n_sel)
    chunk_w = CMP_STRIDE * HEAD_DIM

    h = x.reshape(T, D)
    for l in range(depth):
        n_cols = 3 * CONV_W + ATTN_W + 6 * KV_W + PROJ_TILE_N
        w_pad = jnp.pad(w_in[l], ((0, 0), (0, n_cols - w_in.shape[2]))).astype(BF16)
        zc, q, kv, zg = _in_projection(h, norm_mix[l][None], w_pad, cos_t, sin_t, S)

        y_conv = _short_conv(zc, conv_w[l], norm_conv_out[l][None], S)

        kvc_in = kv[:, :2 * KV_W].reshape(B, S, 2, N_KV_HEADS, HEAD_DIM).transpose(2, 0, 3, 1, 4)
        kvc_in = kvc_in.reshape(2, B * N_KV_HEADS, n_chunks, chunk_w)
        pe = jnp.stack([cmp_pe_k[l], cmp_pe_v[l]]).reshape(2, 1, CMP_BLOCK * HEAD_DIM)
        kvc = _compress(kvc_in, pe, jnp.stack([cmp_w1_k[l], cmp_w1_v[l]]),
                        jnp.stack([cmp_w2_k[l], cmp_w2_v[l]]))

        q3 = q.reshape(B, S, ATTN_W)
        kv3 = kv.reshape(B, S, 6 * KV_W)
        o_c, bias = _cmp_attention(q3, kvc, ovt, B, S)
        o_s = _sel_attention(q3, bias, kv3, onehot, B, S)
        o_w = _win_attention(q3, kv3, B, S)

        h = _out_projection(y_conv, o_c.reshape(T, ATTN_W), o_s.reshape(T, ATTN_W),
                            o_w.reshape(T, ATTN_W), zg, norm_attn_out[l][None],
                            w_out[l].astype(BF16), h)

        u = _ffn_up(h, norm_ffn[l][None], w_gate[l].astype(BF16), w_up[l].astype(BF16))
        last = l == depth - 1
        h = _ffn_down(u, w_down[l].astype(BF16), h, norm_final[None], last)
    return h.reshape(B, S, D)
```

```python
import functools

import jax
import jax.numpy as jnp
import numpy as np
from jax import lax
from jax.experimental import pallas as pl
from jax.experimental.pallas import tpu as pltpu

F32 = jnp.float32
BF16 = jnp.bfloat16

HEAD_DIM = 128
N_HEADS = 8
N_KV_HEADS = 2
GROUP = N_HEADS // N_KV_HEADS
N_BRANCH = 3
CONV_TAPS = 3
CMP_BLOCK = 32
CMP_STRIDE = 16
SEL_BLOCK = 64
N_SELECT = 16
WINDOW = 512
Q_BLOCK = 128
ROPE_THETA = 10000.0
RMS_EPS = 1e-6

CONV_W = 1024
ATTN_W = N_HEADS * HEAD_DIM
KV_W = N_KV_HEADS * HEAD_DIM
N_GATES = N_HEADS * N_BRANCH

V7X_VMEM_BYTES = 64 * 1024 * 1024
V7X_VMEM_BUDGET = V7X_VMEM_BYTES - 8 * 1024 * 1024
LANES = 128
SUBLANES = 8
PROJ_TILE_N = 256

MASK_BIAS = -float(2 ** 30)
LOG2_E = 1.4426950408889634
ONES_ROWS = 16


def _params(semantics, vmem_bytes):
    return pltpu.CompilerParams(dimension_semantics=semantics,
                                vmem_limit_bytes=int(min(vmem_bytes, V7X_VMEM_BUDGET)))


def _rms(x, g):
    return x * lax.rsqrt(jnp.mean(x * x, axis=-1, keepdims=True) + RMS_EPS) * g


def _split3(x):
    hi = x.astype(BF16)
    r1 = x - hi.astype(F32)
    mid = r1.astype(BF16)
    lo = (r1 - mid.astype(F32)).astype(BF16)
    return hi, mid, lo


def _dot(a, b):
    return jnp.dot(a, b, preferred_element_type=F32)


def _dot_nt(a, b):
    return lax.dot_general(a, b, (((1,), (1,)), ((), ())), preferred_element_type=F32)


def _dot_f32(a, b):
    a0, a1, a2 = _split3(a)
    b0, b1, b2 = _split3(b)
    return (_dot(a0, b0) + (_dot(a0, b1) + _dot(a1, b0))
            + (_dot(a0, b2) + _dot(a1, b1) + _dot(a2, b0)))


def _inproj_body(x_ref, g_ref, w_ref, cos_ref, sin_ref, zc_ref, q_ref, kv_ref, zg_ref, a_scr,
                 *, n_conv, n_q, n_kv, q_scale):
    j = pl.program_id(1)

    @pl.when(j == 0)
    def _():
        a_scr[...] = _rms(x_ref[...], g_ref[...]).astype(BF16)

    z = _dot(a_scr[...], w_ref[...])

    def rope(z2):
        cos = cos_ref[...]
        sin = sin_ref[...]
        heads = []
        for h in range(PROJ_TILE_N // HEAD_DIM):
            zh = z2[:, h * HEAD_DIM:(h + 1) * HEAD_DIM]
            heads.append(zh * cos + pltpu.roll(zh, HEAD_DIM // 2, axis=1) * sin)
        return jnp.concatenate(heads, axis=1)

    kv_j = j - (n_conv + n_q)
    is_kv = (kv_j >= 0) & (kv_j < n_kv)

    @pl.when(j < n_conv)
    def _():
        zc_ref[...] = z

    @pl.when((j >= n_conv) & (j < n_conv + n_q))
    def _():
        q_ref[...] = (rope(z) * q_scale).astype(BF16)

    @pl.when(is_kv & (kv_j % 2 == 0))
    def _():
        kv_ref[...] = rope(z).astype(BF16)

    @pl.when(is_kv & (kv_j % 2 == 1))
    def _():
        kv_ref[...] = z.astype(BF16)

    @pl.when(j == n_conv + n_q + n_kv)
    def _():
        zg_ref[...] = z


def _in_projection(x2, g, w_pad, cos_t, sin_t, seq):
    T, D = x2.shape
    n_conv = 3 * CONV_W // PROJ_TILE_N
    n_q = ATTN_W // PROJ_TILE_N
    n_kv = 6 * KV_W // PROJ_TILE_N
    n_tiles = n_conv + n_q + n_kv + 1
    assert w_pad.shape == (D, n_tiles * PROJ_TILE_N)
    tm = min(1024, seq)
    pos_blocks = seq // tm
    tn = PROJ_TILE_N
    body = functools.partial(_inproj_body, n_conv=n_conv, n_q=n_q, n_kv=n_kv,
                             q_scale=HEAD_DIM ** -0.5 * LOG2_E)
    vmem = 2 * (tm * D * 4 + D * tn * 2 + 2 * tm * HEAD_DIM * 4 + 2 * tm * tn * 4 + 2 * tm * tn * 2) \
        + tm * D * 2 + 4 * tm * D * 4
    return pl.pallas_call(
        body,
        grid=(T // tm, n_tiles),
        in_specs=[
            pl.BlockSpec((tm, D), lambda i, j: (i, 0)),
            pl.BlockSpec((1, D), lambda i, j: (0, 0)),
            pl.BlockSpec((D, tn), lambda i, j: (0, j)),
            pl.BlockSpec((tm, HEAD_DIM), lambda i, j: (i % pos_blocks, 0)),
            pl.BlockSpec((tm, HEAD_DIM), lambda i, j: (i % pos_blocks, 0)),
        ],
        out_specs=[
            pl.BlockSpec((tm, tn), lambda i, j: (i, jnp.minimum(j, n_conv - 1))),
            pl.BlockSpec((tm, tn), lambda i, j: (i, jnp.clip(j - n_conv, 0, n_q - 1))),
            pl.BlockSpec((tm, tn), lambda i, j: (i, jnp.clip(j - n_conv - n_q, 0, n_kv - 1))),
            pl.BlockSpec((tm, tn), lambda i, j: (i, 0)),
        ],
        out_shape=[
            jax.ShapeDtypeStruct((T, 3 * CONV_W), F32),
            jax.ShapeDtypeStruct((T, ATTN_W), BF16),
            jax.ShapeDtypeStruct((T, 6 * KV_W), BF16),
            jax.ShapeDtypeStruct((T, tn), F32),
        ],
        scratch_shapes=[pltpu.VMEM((tm, D), BF16)],
        compiler_params=_params(("parallel", "arbitrary"), vmem),
        name="in_projection",
    )(x2, g, w_pad, cos_t, sin_t)


def _conv_body(cb_ref, cc_ref, ch_ref, hc_ref, hh_ref, w_ref, g_ref, o_ref, *, blocks_per_seq):
    i = pl.program_id(0)
    u = cc_ref[...] * ch_ref[...]
    halo = hc_ref[...] * hh_ref[...]
    halo = jnp.where(i % blocks_per_seq == 0, 0.0, halo)
    row = lax.broadcasted_iota(jnp.int32, u.shape, 0)
    h1 = halo[SUBLANES - 1:SUBLANES]
    h2 = halo[SUBLANES - 2:SUBLANES - 1]
    u1 = jnp.where(row == 0, h1, pltpu.roll(u, 1, axis=0))
    u2 = jnp.where(row == 0, h2, jnp.where(row == 1, h1, pltpu.roll(u, 2, axis=0)))
    w = w_ref[...]
    y = cb_ref[...] * (u2 * w[0:1] + u1 * w[1:2] + u * w[2:3])
    o_ref[...] = _rms(y, g_ref[...]).astype(BF16)


def _short_conv(zc, conv_w, g, seq):
    T = zc.shape[0]
    C = CONV_W
    tm = min(512, seq)
    hb = tm // SUBLANES
    body = functools.partial(_conv_body, blocks_per_seq=seq // tm)
    vmem = 2 * (3 * tm * C * 4 + 2 * SUBLANES * C * 4 + tm * C * 2) + 6 * tm * C * 4
    return pl.pallas_call(
        body,
        grid=(T // tm,),
        in_specs=[
            pl.BlockSpec((tm, C), lambda i: (i, 0)),
            pl.BlockSpec((tm, C), lambda i: (i, 1)),
            pl.BlockSpec((tm, C), lambda i: (i, 2)),
            pl.BlockSpec((SUBLANES, C), lambda i: (jnp.maximum(i * hb - 1, 0), 1)),
            pl.BlockSpec((SUBLANES, C), lambda i: (jnp.maximum(i * hb - 1, 0), 2)),
            pl.BlockSpec((CONV_TAPS, C), lambda i: (0, 0)),
            pl.BlockSpec((1, C), lambda i: (0, 0)),
        ],
        out_specs=pl.BlockSpec((tm, C), lambda i: (i, 0)),
        out_shape=jax.ShapeDtypeStruct((T, C), BF16),
        compiler_params=_params(("parallel",), vmem),
        name="short_conv",
    )(zc, zc, zc, zc, zc, conv_w, g)


def _compress_body(x_ref, pe_ref, w1_ref, w2_ref, o_ref, *, n_valid):
    x = x_ref[0, 0]
    half = x.shape[1]
    w1 = w1_ref[0]
    w1_hi = w1.astype(BF16)
    w1_lo = (w1 - w1_hi.astype(F32)).astype(BF16)
    first = _dot(x, w1_hi[:half]) + _dot(x, w1_lo[:half])
    second = _dot(x, w1_hi[half:]) + _dot(x, w1_lo[half:])
    n = x.shape[0]
    second = pltpu.roll(second, n - 1, axis=0)
    pe = jnp.broadcast_to(pe_ref[0], (SUBLANES, 2 * half))
    pe_term = _dot_f32(pe, w1)[0:1]
    h = first + second + pe_term
    out = _dot_f32(jax.nn.silu(h), w2_ref[0])
    row = lax.broadcasted_iota(jnp.int32, out.shape, 0)
    o_ref[0, 0] = jnp.where(row < n_valid, out, 0.0)


def _compress(chunks, pe, w1, w2):
    two, BH, n, width = chunks.shape
    d = HEAD_DIM
    body = functools.partial(_compress_body, n_valid=n - 1)
    vmem = 2 * (n * width * 2 + 2 * width * 4 + 2 * width * d * 4 + d * d * 4 + n * d * 4) \
        + 8 * width * d * 4
    return pl.pallas_call(
        body,
        grid=(two, BH),
        in_specs=[
            pl.BlockSpec((1, 1, n, width), lambda s, b: (s, b, 0, 0)),
            pl.BlockSpec((1, 1, 2 * width), lambda s, b: (s, 0, 0)),
            pl.BlockSpec((1, 2 * width, d), lambda s, b: (s, 0, 0)),
            pl.BlockSpec((1, d, d), lambda s, b: (s, 0, 0)),
        ],
        out_specs=pl.BlockSpec((1, 1, n, d), lambda s, b: (s, b, 0, 0)),
        out_shape=jax.ShapeDtypeStruct((two, BH, n, d), F32),
        compiler_params=_params(("parallel", "parallel"), vmem),
        name="compress",
    )(chunks, pe, w1, w2)


def _stack_heads(q):
    return jnp.concatenate([q[:, g * HEAD_DIM:(g + 1) * HEAD_DIM] for g in range(GROUP)], axis=0)


def _row_time(shape, q0):
    return q0 + (lax.broadcasted_iota(jnp.int32, shape, 0) & (Q_BLOCK - 1))


def _cmp_body(q_ref, kc_ref, vc_ref, ovt_ref, o_ref, bias_ref):
    q0 = pl.program_id(2) * Q_BLOCK
    qs = _stack_heads(q_ref[0])
    kc = kc_ref[0, 0].astype(BF16)
    vc = vc_ref[0, 0].astype(BF16)
    s = _dot_nt(qs, kc)
    t = _row_time(s.shape, q0)
    cmp_end = lax.broadcasted_iota(jnp.int32, s.shape, 1) * CMP_STRIDE + (CMP_BLOCK - 1)
    s = jnp.where(cmp_end <= t, s, -jnp.inf)
    m = jnp.max(s, axis=-1, keepdims=True)
    m = jnp.where(m == -jnp.inf, 0.0, m)
    e = jnp.exp2(s - m)
    p = e / jnp.maximum(jnp.sum(e, axis=-1, keepdims=True), 1e-30)
    o = _dot(p.astype(BF16), vc)
    for g in range(GROUP):
        o_ref[0, :, g * HEAD_DIM:(g + 1) * HEAD_DIM] = o[g * Q_BLOCK:(g + 1) * Q_BLOCK]

    pg = p[0:Q_BLOCK]
    for g in range(1, GROUP):
        pg = pg + p[g * Q_BLOCK:(g + 1) * Q_BLOCK]
    ovt = ovt_ref[...]
    p0, p1, p2 = _split3(pg)
    imp = _dot_nt(ovt, p0) + _dot_nt(ovt, p1) + _dot_nt(ovt, p2)

    sel_i = lax.broadcasted_iota(jnp.int32, imp.shape, 0)
    sel_f = sel_i.astype(F32)
    cur = (q0 + lax.broadcasted_iota(jnp.int32, imp.shape, 1)) // SEL_BLOCK
    forced = (sel_i == 0) | (sel_i == cur) | (sel_i == cur - 1)
    v = jnp.where(sel_i > cur, -jnp.inf, jnp.where(forced, jnp.inf, imp))
    n_sel = imp.shape[0]
    chosen = jnp.zeros(imp.shape, F32)
    for _ in range(min(N_SELECT, n_sel)):
        top = jnp.max(v, axis=0, keepdims=True)
        first = jnp.min(jnp.where(v == top, sel_f, float(n_sel)), axis=0, keepdims=True)
        pick = sel_f == first
        chosen = jnp.where(pick, 1.0, chosen)
        v = jnp.where(pick, -jnp.inf, v)
    bias_t = jnp.where((chosen > 0.0) & (sel_i <= cur), 0.0, MASK_BIAS)
    bias_ref[0, 0] = bias_t.astype(BF16)


def _cmp_attention(q, kvc, ovt, B, seq):
    n_cmp = kvc.shape[2]
    n_sel = ovt.shape[0]
    GW = GROUP * HEAD_DIM
    vmem = 2 * (Q_BLOCK * GW * 2 + 2 * n_cmp * HEAD_DIM * 4 + n_sel * n_cmp * 2
                + Q_BLOCK * GW * 4 + Q_BLOCK * n_sel * 2) + 10 * GROUP * Q_BLOCK * n_cmp * 4
    return pl.pallas_call(
        _cmp_body,
        grid=(B, N_KV_HEADS, seq // Q_BLOCK),
        in_specs=[
            pl.BlockSpec((1, Q_BLOCK, GW), lambda b, h, i: (b, i, h)),
            pl.BlockSpec((1, 1, n_cmp, HEAD_DIM), lambda b, h, i: (0, b * N_KV_HEADS + h, 0, 0)),
            pl.BlockSpec((1, 1, n_cmp, HEAD_DIM), lambda b, h, i: (1, b * N_KV_HEADS + h, 0, 0)),
            pl.BlockSpec((n_sel, n_cmp), lambda b, h, i: (0, 0)),
        ],
        out_specs=[
            pl.BlockSpec((1, Q_BLOCK, GW), lambda b, h, i: (b, i, h)),
            pl.BlockSpec((1, 1, n_sel, Q_BLOCK), lambda b, h, i: (b, h, 0, i)),
        ],
        out_shape=[
            jax.ShapeDtypeStruct((B, seq, ATTN_W), F32),
            jax.ShapeDtypeStruct((B, N_KV_HEADS, n_sel, seq), BF16),
        ],
        compiler_params=_params(("parallel", "parallel", "parallel"), vmem),
        name="cmp_attention",
    )(q, kvc, kvc, ovt)


def _sel_body(qt_ref, bias_ref, k_ref, vt_ref, e_ref, o_ref, s_scr, m_scr, acc_scr,
              *, tk, chains):
    q0 = pl.program_id(2) * Q_BLOCK
    bias = bias_ref[0, 0]
    heads = [jnp.concatenate([qt_ref[0, g * HEAD_DIM:(g + 1) * HEAD_DIM, :], bias], axis=0)
             for g in range(GROUP)]
    cols = GROUP * Q_BLOCK // chains
    per = GROUP // chains
    qx = [jnp.concatenate(heads[c * per:(c + 1) * per], axis=1) for c in range(chains)]

    def scores(j, slot):
        start = pl.multiple_of(j * tk, tk)
        kx = jnp.concatenate([k_ref[0, pl.ds(start, tk), :], e_ref[pl.ds(start, tk), :]], axis=1)
        for c in range(chains):
            s_scr[slot, :, c * cols:(c + 1) * cols] = _dot(kx, qx[c])

    def update(j, slot, causal):
        start = pl.multiple_of(j * tk, tk)
        vt = vt_ref[0, 0, :, pl.ds(start, tk)]
        for c in range(chains):
            sl = slice(c * cols, (c + 1) * cols)
            s = s_scr[slot, :, sl]
            if causal:
                key = start + lax.broadcasted_iota(jnp.int32, s.shape, 0)
                t = q0 + (lax.broadcasted_iota(jnp.int32, s.shape, 1) & (Q_BLOCK - 1))
                s = jnp.where(key <= t, s, -jnp.inf)
            m = m_scr[:, sl]
            m_new = jnp.maximum(m, jnp.max(s, axis=0, keepdims=True))
            alpha = jnp.exp2(m - m_new)
            p = jnp.exp2(s - m_new)
            m_scr[:, sl] = m_new
            acc_scr[:, sl] = alpha * acc_scr[:, sl] + _dot(vt, p.astype(BF16))

    m_scr[...] = jnp.full(m_scr.shape, -jnp.inf, F32)
    acc_scr[...] = jnp.zeros(acc_scr.shape, F32)
    scores(0, 0)
    last = q0 // tk

    @pl.loop(0, last // 2)
    def _(i):
        scores(2 * i + 1, 1)
        update(2 * i, 0, False)
        scores(2 * i + 2, 0)
        update(2 * i + 1, 1, False)

    @pl.when(last % 2 == 1)
    def _():
        scores(last, 1)
        update(last - 1, 0, False)
        update(last, 1, True)

    @pl.when(last % 2 == 0)
    def _():
        update(last, 0, True)

    o = acc_scr[0:HEAD_DIM, :] / acc_scr[HEAD_DIM:HEAD_DIM + 1, :]
    for g in range(GROUP):
        o_ref[0, :, g * HEAD_DIM:(g + 1) * HEAD_DIM] = o[:, g * Q_BLOCK:(g + 1) * Q_BLOCK].T


def _sel_attention(qt, bias_t, kv, vt, onehot, B, seq):
    n_sel = bias_t.shape[2]
    GW = GROUP * HEAD_DIM
    tk = min(512, seq)
    k_col = 2 * N_KV_HEADS
    body = functools.partial(_sel_body, tk=tk, chains=2)
    vmem = 2 * (Q_BLOCK * GW * 2 + Q_BLOCK * n_sel * 2 + 2 * seq * HEAD_DIM * 2 + seq * n_sel * 2
                + Q_BLOCK * GW * 4) + 10 * GROUP * Q_BLOCK * tk * 4
    return pl.pallas_call(
        body,
        grid=(B, N_KV_HEADS, seq // Q_BLOCK),
        in_specs=[
            pl.BlockSpec((1, GW, Q_BLOCK), lambda b, h, i: (b, h, i)),
            pl.BlockSpec((1, 1, n_sel, Q_BLOCK), lambda b, h, i: (b, h, 0, i)),
            pl.BlockSpec((1, seq, HEAD_DIM), lambda b, h, i: (b, 0, k_col + h)),
            pl.BlockSpec((1, 1, HEAD_DIM + ONES_ROWS, seq), lambda b, h, i: (b, h, 0, 0)),
            pl.BlockSpec((seq, n_sel), lambda b, h, i: (0, 0)),
        ],
        out_specs=pl.BlockSpec((1, Q_BLOCK, GW), lambda b, h, i: (b, i, h)),
        out_shape=jax.ShapeDtypeStruct((B, seq, ATTN_W), F32),
        scratch_shapes=[pltpu.VMEM((2, tk, GROUP * Q_BLOCK), F32),
                        pltpu.VMEM((1, GROUP * Q_BLOCK), F32),
                        pltpu.VMEM((HEAD_DIM + ONES_ROWS, GROUP * Q_BLOCK), F32)],
        compiler_params=_params(("parallel", "parallel", "arbitrary"), vmem),
        name="sel_attention",
    )(qt, bias_t, kv, vt, onehot)


def _win_body(q_ref, k_ref, v_ref, o_ref, *, span):
    q0 = pl.program_id(2) * Q_BLOCK
    qs = _stack_heads(q_ref[0])
    start = pl.multiple_of(jnp.maximum(q0 - WINDOW, 0), Q_BLOCK)
    k = k_ref[0, pl.ds(start, span), :]
    v = v_ref[0, pl.ds(start, span), :]
    s = _dot_nt(qs, k)
    t = _row_time(s.shape, q0)
    pos = start + lax.broadcasted_iota(jnp.int32, s.shape, 1)
    s = jnp.where((pos <= t) & (pos > t - WINDOW), s, -jnp.inf)
    m = jnp.max(s, axis=-1, keepdims=True)
    e = jnp.exp2(s - m)
    p = e / jnp.sum(e, axis=-1, keepdims=True)
    o = _dot(p.astype(BF16), v)
    for g in range(GROUP):
        o_ref[0, :, g * HEAD_DIM:(g + 1) * HEAD_DIM] = o[g * Q_BLOCK:(g + 1) * Q_BLOCK]


def _win_attention(q, kv, B, seq):
    GW = GROUP * HEAD_DIM
    span = WINDOW + Q_BLOCK
    assert seq >= span
    k_col = 4 * N_KV_HEADS
    v_col = 5 * N_KV_HEADS
    body = functools.partial(_win_body, span=span)
    vmem = 2 * (Q_BLOCK * GW * 2 + 2 * seq * HEAD_DIM * 2 + Q_BLOCK * GW * 4) \
        + 10 * GROUP * Q_BLOCK * span * 4
    return pl.pallas_call(
        body,
        grid=(B, N_KV_HEADS, seq // Q_BLOCK),
        in_specs=[
            pl.BlockSpec((1, Q_BLOCK, GW), lambda b, h, i: (b, i, h)),
            pl.BlockSpec((1, seq, HEAD_DIM), lambda b, h, i: (b, 0, k_col + h)),
            pl.BlockSpec((1, seq, HEAD_DIM), lambda b, h, i: (b, 0, v_col + h)),
        ],
        out_specs=pl.BlockSpec((1, Q_BLOCK, GW), lambda b, h, i: (b, i, h)),
        out_shape=jax.ShapeDtypeStruct((B, seq, ATTN_W), F32),
        compiler_params=_params(("parallel", "parallel", "parallel"), vmem),
        name="win_attention",
    )(q, kv, kv)


def _outproj_body(yc_ref, oc_ref, os_ref, ow_ref, zg_ref, g_ref, w_ref, x_ref, h_ref, a_scr):
    @pl.when(pl.program_id(1) == 0)
    def _():
        gates = jax.nn.sigmoid(zg_ref[:, 0:LANES])
        heads = []
        for h in range(N_HEADS):
            sl = slice(h * HEAD_DIM, (h + 1) * HEAD_DIM)
            c = h * N_BRANCH
            heads.append(gates[:, c:c + 1] * oc_ref[:, sl] + gates[:, c + 1:c + 2] * os_ref[:, sl]
                         + gates[:, c + 2:c + 3] * ow_ref[:, sl])
        y_attn = jnp.concatenate(heads, axis=1)
        a_scr[:, 0:CONV_W] = yc_ref[...]
        a_scr[:, CONV_W:] = _rms(y_attn, g_ref[...]).astype(BF16)

    h_ref[...] = x_ref[...] + _dot(a_scr[...], w_ref[...])


def _out_projection(yc, oc, osel, ow, zg, g, w, x2):
    T, D = x2.shape
    K = w.shape[0]
    tm = min(512, T)
    tn = min(512, D)
    vmem = 2 * (tm * CONV_W * 2 + 3 * tm * ATTN_W * 4 + tm * PROJ_TILE_N * 4 + K * tn * 2
                + 2 * tm * tn * 4) + tm * K * 2 + 4 * tm * ATTN_W * 4
    return pl.pallas_call(
        _outproj_body,
        grid=(T // tm, D // tn),
        in_specs=[
            pl.BlockSpec((tm, CONV_W), lambda i, j: (i, 0)),
            pl.BlockSpec((tm, ATTN_W), lambda i, j: (i, 0)),
            pl.BlockSpec((tm, ATTN_W), lambda i, j: (i, 0)),
            pl.BlockSpec((tm, ATTN_W), lambda i, j: (i, 0)),
            pl.BlockSpec((tm, PROJ_TILE_N), lambda i, j: (i, 0)),
            pl.BlockSpec((1, ATTN_W), lambda i, j: (0, 0)),
            pl.BlockSpec((K, tn), lambda i, j: (0, j)),
            pl.BlockSpec((tm, tn), lambda i, j: (i, j)),
        ],
        out_specs=pl.BlockSpec((tm, tn), lambda i, j: (i, j)),
        out_shape=jax.ShapeDtypeStruct((T, D), F32),
        scratch_shapes=[pltpu.VMEM((tm, K), BF16)],
        compiler_params=_params(("parallel", "arbitrary"), vmem),
        name="out_projection",
    )(yc, oc, osel, ow, zg, g, w, x2)


def _ffn_up_body(h_ref, g_ref, wg_ref, wu_ref, o_ref, a_scr):
    @pl.when(pl.program_id(1) == 0)
    def _():
        a_scr[...] = _rms(h_ref[...], g_ref[...]).astype(BF16)

    a = a_scr[...]
    o_ref[...] = (jax.nn.silu(_dot(a, wg_ref[...])) * _dot(a, wu_ref[...])).astype(BF16)


def _ffn_up(h, g, wg, wu):
    T, D = h.shape
    F = wg.shape[1]
    tm = min(1024, T)
    tn = 512
    assert F % tn == 0
    vmem = 2 * (tm * D * 4 + 2 * D * tn * 2 + tm * tn * 2) + tm * D * 2 + 3 * tm * D * 4
    return pl.pallas_call(
        _ffn_up_body,
        grid=(T // tm, F // tn),
        in_specs=[
            pl.BlockSpec((tm, D), lambda i, j: (i, 0)),
            pl.BlockSpec((1, D), lambda i, j: (0, 0)),
            pl.BlockSpec((D, tn), lambda i, j: (0, j)),
            pl.BlockSpec((D, tn), lambda i, j: (0, j)),
        ],
        out_specs=pl.BlockSpec((tm, tn), lambda i, j: (i, j)),
        out_shape=jax.ShapeDtypeStruct((T, F), BF16),
        scratch_shapes=[pltpu.VMEM((tm, D), BF16)],
        compiler_params=_params(("parallel", "arbitrary"), vmem),
        name="ffn_up",
    )(h, g, wg, wu)


def _ffn_down_body(u_ref, w_ref, h_ref, g_ref, o_ref, acc_scr, *, final_norm):
    k = pl.program_id(1)

    @pl.when(k == 0)
    def _():
        acc_scr[...] = jnp.zeros_like(acc_scr)

    acc_scr[...] += _dot(u_ref[...], w_ref[...])

    @pl.when(k == pl.num_programs(1) - 1)
    def _():
        out = h_ref[...] + acc_scr[...]
        o_ref[...] = _rms(out, g_ref[...]) if final_norm else out


def _ffn_down(u, w, h, g_final, final_norm):
    T, F = u.shape
    D = w.shape[1]
    tm = min(512, T)
    tk = 512
    assert F % tk == 0
    body = functools.partial(_ffn_down_body, final_norm=final_norm)
    vmem = 2 * (tm * tk * 2 + tk * D * 2 + 2 * tm * D * 4) + tm * D * 4 + 3 * tm * D * 4
    return pl.pallas_call(
        body,
        grid=(T // tm, F // tk),
        in_specs=[
            pl.BlockSpec((tm, tk), lambda i, k: (i, k)),
            pl.BlockSpec((tk, D), lambda i, k: (k, 0)),
            pl.BlockSpec((tm, D), lambda i, k: (i, 0)),
            pl.BlockSpec((1, D), lambda i, k: (0, 0)),
        ],
        out_specs=pl.BlockSpec((tm, D), lambda i, k: (i, 0)),
        out_shape=jax.ShapeDtypeStruct((T, D), F32),
        scratch_shapes=[pltpu.VMEM((tm, D), F32)],
        compiler_params=_params(("parallel", "arbitrary"), vmem),
        name="ffn_down",
    )(u, w, h, g_final)


def _rope_tables(seq):
    half = HEAD_DIM // 2
    inv = 1.0 / (ROPE_THETA ** (jnp.arange(half, dtype=F32) / half))
    ang = jnp.arange(seq).astype(F32)[:, None] * inv[None, :]
    cos = jnp.cos(ang)
    sin = jnp.sin(ang)
    return jnp.concatenate([cos, cos], axis=1), jnp.concatenate([-sin, sin], axis=1)


def _overlap_t(n_sel, n_cmp):
    cs = np.arange(n_cmp)[None, :] * CMP_STRIDE
    ss = np.arange(n_sel)[:, None] * SEL_BLOCK
    return jnp.asarray(((cs < ss + SEL_BLOCK) & (cs + CMP_BLOCK > ss)).astype(np.float32), dtype=BF16)


def _block_onehot(seq, n_sel):
    return jnp.asarray((np.arange(seq)[:, None] // SEL_BLOCK == np.arange(n_sel)[None, :])
                       .astype(np.float32), dtype=BF16)


def kernel(x, norm_mix, w_in, conv_w, cmp_pe_k, cmp_w1_k, cmp_w2_k, cmp_pe_v, cmp_w1_v, cmp_w2_v,
           norm_conv_out, norm_attn_out, w_out, norm_ffn, w_gate, w_up, w_down, norm_final):
    B, S, D = x.shape
    T = B * S
    depth = norm_mix.shape[0]
    assert S % Q_BLOCK == 0 and S % CMP_STRIDE == 0
    n_chunks = S // CMP_STRIDE
    n_sel = S // SEL_BLOCK
    cos_t, sin_t = _rope_tables(S)
    ovt = _overlap_t(n_sel, n_chunks)
    onehot = _block_onehot(S, n_sel)
    chunk_w = CMP_STRIDE * HEAD_DIM

    h = x.reshape(T, D)
    for l in range(depth):
        n_cols = 3 * CONV_W + ATTN_W + 6 * KV_W + PROJ_TILE_N
        w_pad = jnp.pad(w_in[l], ((0, 0), (0, n_cols - w_in.shape[2]))).astype(BF16)
        zc, q, kv, zg = _in_projection(h, norm_mix[l][None], w_pad, cos_t, sin_t, S)

        y_conv = _short_conv(zc, conv_w[l], norm_conv_out[l][None], S)

        kvc_in = kv[:, :2 * KV_W].reshape(B, S, 2, N_KV_HEADS, HEAD_DIM).transpose(2, 0, 3, 1, 4)
        kvc_in = kvc_in.reshape(2, B * N_KV_HEADS, n_chunks, chunk_w)
        pe = jnp.stack([cmp_pe_k[l], cmp_pe_v[l]]).reshape(2, 1, CMP_BLOCK * HEAD_DIM)
        kvc = _compress(kvc_in, pe, jnp.stack([cmp_w1_k[l], cmp_w1_v[l]]),
                        jnp.stack([cmp_w2_k[l], cmp_w2_v[l]]))

        q3 = q.reshape(B, S, ATTN_W)
        kv3 = kv.reshape(B, S, 6 * KV_W)
        o_c, bias_t = _cmp_attention(q3, kvc, ovt, B, S)
        qt = q3.transpose(0, 2, 1)
        vt_sel = kv3[:, :, 3 * KV_W:4 * KV_W].transpose(0, 2, 1).reshape(B, N_KV_HEADS, HEAD_DIM, S)
        vt_sel = jnp.concatenate([vt_sel, jnp.ones((B, N_KV_HEADS, ONES_ROWS, S), BF16)], axis=2)
        o_s = _sel_attention(qt, bias_t, kv3, vt_sel, onehot, B, S)
        o_w = _win_attention(q3, kv3, B, S)

        h = _out_projection(y_conv, o_c.reshape(T, ATTN_W), o_s.reshape(T, ATTN_W),
                            o_w.reshape(T, ATTN_W), zg, norm_attn_out[l][None],
                            w_out[l].astype(BF16), h)

        u = _ffn_up(h, norm_ffn[l][None], w_gate[l].astype(BF16), w_up[l].astype(BF16))
        last = l == depth - 1
        h = _ffn_down(u, w_down[l].astype(BF16), h, norm_final[None], last)
    return h.reshape(B, S, D)
```

```python
import functools

import jax
import jax.numpy as jnp
import numpy as np
from jax import lax
from jax.experimental import pallas as pl
from jax.experimental.pallas import tpu as pltpu

F32 = jnp.float32
BF16 = jnp.bfloat16

HEAD_DIM = 128
N_HEADS = 8
N_KV_HEADS = 2
GROUP = N_HEADS // N_KV_HEADS
N_BRANCH = 3
CONV_TAPS = 3
CMP_BLOCK = 32
CMP_STRIDE = 16
SEL_BLOCK = 64
N_SELECT = 16
WINDOW = 512
Q_BLOCK = 128
ROPE_THETA = 10000.0
RMS_EPS = 1e-6

CONV_W = 1024
ATTN_W = N_HEADS * HEAD_DIM
KV_W = N_KV_HEADS * HEAD_DIM
N_GATES = N_HEADS * N_BRANCH

V7X_VMEM_BYTES = 64 * 1024 * 1024
V7X_VMEM_BUDGET = V7X_VMEM_BYTES - 6 * 1024 * 1024
LANES = 128
SUBLANES = 8
PROJ_TILE_N = 512

MASK_BIAS = -float(2 ** 30)
LOG2_E = 1.4426950408889634
ONES_ROWS = 16


def _params(semantics, vmem_bytes):
    return pltpu.CompilerParams(dimension_semantics=semantics,
                                vmem_limit_bytes=int(min(vmem_bytes, V7X_VMEM_BUDGET)))


def _rms(x, g):
    return x * lax.rsqrt(jnp.mean(x * x, axis=-1, keepdims=True) + RMS_EPS) * g


def _split3(x):
    hi = x.astype(BF16)
    r1 = x - hi.astype(F32)
    mid = r1.astype(BF16)
    lo = (r1 - mid.astype(F32)).astype(BF16)
    return hi, mid, lo


def _dot(a, b):
    return jnp.dot(a, b, preferred_element_type=F32)


def _dot_nt(a, b):
    return lax.dot_general(a, b, (((1,), (1,)), ((), ())), preferred_element_type=F32)


def _dot_f32(a, b):
    a0, a1, a2 = _split3(a)
    b0, b1, b2 = _split3(b)
    return (_dot(a0, b0) + (_dot(a0, b1) + _dot(a1, b0))
            + (_dot(a0, b2) + _dot(a1, b1) + _dot(a2, b0)))


N_CONV_TILES = 3 * CONV_W // PROJ_TILE_N
N_Q_TILES = ATTN_W // PROJ_TILE_N
N_K_TILES = 2
N_V_TILES = 2
N_PROJ_TILES = N_CONV_TILES + N_Q_TILES + N_K_TILES + N_V_TILES
KV_OUT_W = N_K_TILES * PROJ_TILE_N


def _proj_weight(w):
    o = 3 * CONV_W + ATTN_W

    def col(n):
        return w[:, o + n * KV_W:o + (n + 1) * KV_W]

    def zeros(n):
        return jnp.zeros((w.shape[0], n), w.dtype)

    gates = w[:, o + 6 * KV_W:]
    parts = [w[:, :o], col(0), col(2), col(4), zeros(KV_W),
             col(1), col(3), col(5), gates, zeros(KV_W - gates.shape[1])]
    return jnp.concatenate(parts, axis=1).astype(BF16)


def _inproj_body(x_ref, g_ref, w_ref, cos_ref, sin_ref, zc_ref, q_ref, k_ref, v_ref, zg_ref, a_scr,
                 *, q_scale):
    j = pl.program_id(1)

    @pl.when(j == 0)
    def _():
        a_scr[...] = _rms(x_ref[...], g_ref[...]).astype(BF16)

    z = _dot(a_scr[...], w_ref[...])

    def rope(z4):
        cos = cos_ref[...]
        sin = sin_ref[...]
        heads = []
        for h in range(PROJ_TILE_N // HEAD_DIM):
            zh = z4[:, h * HEAD_DIM:(h + 1) * HEAD_DIM]
            heads.append(zh * cos + pltpu.roll(zh, HEAD_DIM // 2, axis=1) * sin)
        return jnp.concatenate(heads, axis=1)

    q_start = N_CONV_TILES
    k_start = q_start + N_Q_TILES
    v_start = k_start + N_K_TILES

    @pl.when(j < q_start)
    def _():
        zc_ref[...] = z

    @pl.when((j >= q_start) & (j < k_start))
    def _():
        q_ref[...] = (rope(z) * q_scale).astype(BF16)

    @pl.when((j >= k_start) & (j < v_start))
    def _():
        k_ref[...] = rope(z).astype(BF16)

    @pl.when(j >= v_start)
    def _():
        v_ref[...] = z.astype(BF16)

    @pl.when(j == N_PROJ_TILES - 1)
    def _():
        zg_ref[...] = z[:, KV_W:KV_W + LANES]


def _in_projection(x2, g, w_perm, cos_t, sin_t, seq):
    T, D = x2.shape
    tn = PROJ_TILE_N
    assert w_perm.shape == (D, N_PROJ_TILES * tn)
    tm = min(1024, seq)
    pos_blocks = seq // tm
    q_start = N_CONV_TILES
    k_start = q_start + N_Q_TILES
    v_start = k_start + N_K_TILES
    body = functools.partial(_inproj_body, q_scale=HEAD_DIM ** -0.5 * LOG2_E)
    vmem = 2 * (tm * D * 4 + D * tn * 2 + 2 * tm * HEAD_DIM * 4 + tm * tn * 4 + 3 * tm * tn * 2
                + tm * LANES * 4) + tm * D * 2 + 2 * tm * D * 4
    return pl.pallas_call(
        body,
        grid=(T // tm, N_PROJ_TILES),
        in_specs=[
            pl.BlockSpec((tm, D), lambda i, j: (i, 0)),
            pl.BlockSpec((1, D), lambda i, j: (0, 0)),
            pl.BlockSpec((D, tn), lambda i, j: (0, j)),
            pl.BlockSpec((tm, HEAD_DIM), lambda i, j: (i % pos_blocks, 0)),
            pl.BlockSpec((tm, HEAD_DIM), lambda i, j: (i % pos_blocks, 0)),
        ],
        out_specs=[
            pl.BlockSpec((tm, tn), lambda i, j: (i, jnp.minimum(j, q_start - 1))),
            pl.BlockSpec((tm, tn), lambda i, j: (i, jnp.clip(j - q_start, 0, N_Q_TILES - 1))),
            pl.BlockSpec((tm, tn), lambda i, j: (i, jnp.clip(j - k_start, 0, N_K_TILES - 1))),
            pl.BlockSpec((tm, tn), lambda i, j: (i, jnp.clip(j - v_start, 0, N_V_TILES - 1))),
            pl.BlockSpec((tm, LANES), lambda i, j: (i, 0)),
        ],
        out_shape=[
            jax.ShapeDtypeStruct((T, 3 * CONV_W), F32),
            jax.ShapeDtypeStruct((T, ATTN_W), BF16),
            jax.ShapeDtypeStruct((T, KV_OUT_W), BF16),
            jax.ShapeDtypeStruct((T, KV_OUT_W), BF16),
            jax.ShapeDtypeStruct((T, LANES), F32),
        ],
        scratch_shapes=[pltpu.VMEM((tm, D), BF16)],
        compiler_params=_params(("parallel", "arbitrary"), vmem),
        name="in_projection",
    )(x2, g, w_perm, cos_t, sin_t)


def _conv_body(cb_ref, cc_ref, ch_ref, hc_ref, hh_ref, w_ref, g_ref, o_ref, *, blocks_per_seq):
    i = pl.program_id(0)
    u = cc_ref[...] * ch_ref[...]
    halo = hc_ref[...] * hh_ref[...]
    halo = jnp.where(i % blocks_per_seq == 0, 0.0, halo)
    row = lax.broadcasted_iota(jnp.int32, u.shape, 0)
    h1 = halo[SUBLANES - 1:SUBLANES]
    h2 = halo[SUBLANES - 2:SUBLANES - 1]
    u1 = jnp.where(row == 0, h1, pltpu.roll(u, 1, axis=0))
    u2 = jnp.where(row == 0, h2, jnp.where(row == 1, h1, pltpu.roll(u, 2, axis=0)))
    w = w_ref[...]
    y = cb_ref[...] * (u2 * w[0:1] + u1 * w[1:2] + u * w[2:3])
    o_ref[...] = _rms(y, g_ref[...]).astype(BF16)


def _short_conv(zc, conv_w, g, seq):
    T = zc.shape[0]
    C = CONV_W
    tm = min(512, seq)
    hb = tm // SUBLANES
    body = functools.partial(_conv_body, blocks_per_seq=seq // tm)
    vmem = 2 * (3 * tm * C * 4 + 2 * SUBLANES * C * 4 + tm * C * 2) + 6 * tm * C * 4
    return pl.pallas_call(
        body,
        grid=(T // tm,),
        in_specs=[
            pl.BlockSpec((tm, C), lambda i: (i, 0)),
            pl.BlockSpec((tm, C), lambda i: (i, 1)),
            pl.BlockSpec((tm, C), lambda i: (i, 2)),
            pl.BlockSpec((SUBLANES, C), lambda i: (jnp.maximum(i * hb - 1, 0), 1)),
            pl.BlockSpec((SUBLANES, C), lambda i: (jnp.maximum(i * hb - 1, 0), 2)),
            pl.BlockSpec((CONV_TAPS, C), lambda i: (0, 0)),
            pl.BlockSpec((1, C), lambda i: (0, 0)),
        ],
        out_specs=pl.BlockSpec((tm, C), lambda i: (i, 0)),
        out_shape=jax.ShapeDtypeStruct((T, C), BF16),
        compiler_params=_params(("parallel",), vmem),
        name="short_conv",
    )(zc, zc, zc, zc, zc, conv_w, g)


def _compress_body(x_ref, pe_ref, w1_ref, w2_ref, o_ref, *, n_valid):
    x = x_ref[0, 0]
    half = x.shape[1]
    w1 = w1_ref[0]
    w1_hi = w1.astype(BF16)
    w1_lo = (w1 - w1_hi.astype(F32)).astype(BF16)
    first = _dot(x, w1_hi[:half]) + _dot(x, w1_lo[:half])
    second = _dot(x, w1_hi[half:]) + _dot(x, w1_lo[half:])
    n = x.shape[0]
    second = pltpu.roll(second, n - 1, axis=0)
    pe = jnp.broadcast_to(pe_ref[0], (SUBLANES, 2 * half))
    pe_term = _dot_f32(pe, w1)[0:1]
    h = first + second + pe_term
    out = _dot_f32(jax.nn.silu(h), w2_ref[0])
    row = lax.broadcasted_iota(jnp.int32, out.shape, 0)
    o_ref[0, 0] = jnp.where(row < n_valid, out, 0.0)


def _compress(chunks, pe, w1, w2):
    two, BH, n, width = chunks.shape
    d = HEAD_DIM
    body = functools.partial(_compress_body, n_valid=n - 1)
    vmem = 2 * (n * width * 2 + 2 * width * 4 + 2 * width * d * 4 + d * d * 4 + n * d * 4) \
        + 8 * width * d * 4
    return pl.pallas_call(
        body,
        grid=(two, BH),
        in_specs=[
            pl.BlockSpec((1, 1, n, width), lambda s, b: (s, b, 0, 0)),
            pl.BlockSpec((1, 1, 2 * width), lambda s, b: (s, 0, 0)),
            pl.BlockSpec((1, 2 * width, d), lambda s, b: (s, 0, 0)),
            pl.BlockSpec((1, d, d), lambda s, b: (s, 0, 0)),
        ],
        out_specs=pl.BlockSpec((1, 1, n, d), lambda s, b: (s, b, 0, 0)),
        out_shape=jax.ShapeDtypeStruct((two, BH, n, d), F32),
        compiler_params=_params(("parallel", "parallel"), vmem),
        name="compress",
    )(chunks, pe, w1, w2)


def _stack_heads(q):
    return jnp.concatenate([q[:, g * HEAD_DIM:(g + 1) * HEAD_DIM] for g in range(GROUP)], axis=0)


def _row_time(shape, q0):
    return q0 + (lax.broadcasted_iota(jnp.int32, shape, 0) & (Q_BLOCK - 1))


def _cmp_body(q_ref, kc_ref, vc_ref, ovt_ref, o_ref, bias_ref):
    q0 = pl.program_id(2) * Q_BLOCK
    qs = _stack_heads(q_ref[0])
    kc = kc_ref[0, 0].astype(BF16)
    vc = vc_ref[0, 0].astype(BF16)
    s = _dot_nt(qs, kc)
    t = _row_time(s.shape, q0)
    cmp_end = lax.broadcasted_iota(jnp.int32, s.shape, 1) * CMP_STRIDE + (CMP_BLOCK - 1)
    s = jnp.where(cmp_end <= t, s, -jnp.inf)
    m = jnp.max(s, axis=-1, keepdims=True)
    m = jnp.where(m == -jnp.inf, 0.0, m)
    e = jnp.exp2(s - m)
    p = e / jnp.maximum(jnp.sum(e, axis=-1, keepdims=True), 1e-30)
    o = _dot(p.astype(BF16), vc)
    for g in range(GROUP):
        o_ref[0, :, g * HEAD_DIM:(g + 1) * HEAD_DIM] = o[g * Q_BLOCK:(g + 1) * Q_BLOCK]

    pg = p[0:Q_BLOCK]
    for g in range(1, GROUP):
        pg = pg + p[g * Q_BLOCK:(g + 1) * Q_BLOCK]
    ovt = ovt_ref[...]
    p0, p1, p2 = _split3(pg)
    imp = _dot_nt(ovt, p0) + _dot_nt(ovt, p1) + _dot_nt(ovt, p2)

    sel_i = lax.broadcasted_iota(jnp.int32, imp.shape, 0)
    sel_f = sel_i.astype(F32)
    cur = (q0 + lax.broadcasted_iota(jnp.int32, imp.shape, 1)) // SEL_BLOCK
    forced = (sel_i == 0) | (sel_i == cur) | (sel_i == cur - 1)
    v = jnp.where(sel_i > cur, -jnp.inf, jnp.where(forced, jnp.inf, imp))
    n_sel = imp.shape[0]
    chosen = jnp.zeros(imp.shape, F32)
    for _ in range(min(N_SELECT, n_sel)):
        top = jnp.max(v, axis=0, keepdims=True)
        first = jnp.min(jnp.where(v == top, sel_f, float(n_sel)), axis=0, keepdims=True)
        pick = sel_f == first
        chosen = jnp.where(pick, 1.0, chosen)
        v = jnp.where(pick, -jnp.inf, v)
    bias_t = jnp.where((chosen > 0.0) & (sel_i <= cur), 0.0, MASK_BIAS)
    bias_ref[0, 0] = bias_t.astype(BF16)


def _cmp_attention(q, kvc, ovt, B, seq):
    n_cmp = kvc.shape[2]
    n_sel = ovt.shape[0]
    GW = GROUP * HEAD_DIM
    vmem = 2 * (Q_BLOCK * GW * 2 + 2 * n_cmp * HEAD_DIM * 4 + n_sel * n_cmp * 2
                + Q_BLOCK * GW * 4 + Q_BLOCK * n_sel * 2) + 10 * GROUP * Q_BLOCK * n_cmp * 4
    return pl.pallas_call(
        _cmp_body,
        grid=(B, N_KV_HEADS, seq // Q_BLOCK),
        in_specs=[
            pl.BlockSpec((1, Q_BLOCK, GW), lambda b, h, i: (b, i, h)),
            pl.BlockSpec((1, 1, n_cmp, HEAD_DIM), lambda b, h, i: (0, b * N_KV_HEADS + h, 0, 0)),
            pl.BlockSpec((1, 1, n_cmp, HEAD_DIM), lambda b, h, i: (1, b * N_KV_HEADS + h, 0, 0)),
            pl.BlockSpec((n_sel, n_cmp), lambda b, h, i: (0, 0)),
        ],
        out_specs=[
            pl.BlockSpec((1, Q_BLOCK, GW), lambda b, h, i: (b, i, h)),
            pl.BlockSpec((1, 1, n_sel, Q_BLOCK), lambda b, h, i: (b, h, 0, i)),
        ],
        out_shape=[
            jax.ShapeDtypeStruct((B, seq, ATTN_W), F32),
            jax.ShapeDtypeStruct((B, N_KV_HEADS, n_sel, seq), BF16),
        ],
        compiler_params=_params(("parallel", "parallel", "parallel"), vmem),
        name="cmp_attention",
    )(q, kvc, kvc, ovt)


def _sel_body(qt_ref, bias_ref, k_ref, vt_ref, e_ref, o_ref, s_scr, m_scr, acc_scr,
              *, tk, chains):
    q0 = pl.program_id(2) * Q_BLOCK
    bias = bias_ref[0, 0]
    heads = [jnp.concatenate([qt_ref[0, g * HEAD_DIM:(g + 1) * HEAD_DIM, :], bias], axis=0)
             for g in range(GROUP)]
    cols = GROUP * Q_BLOCK // chains
    per = GROUP // chains
    qx = [jnp.concatenate(heads[c * per:(c + 1) * per], axis=1) for c in range(chains)]

    def scores(j, slot):
        start = pl.multiple_of(j * tk, tk)
        kx = jnp.concatenate([k_ref[0, pl.ds(start, tk), :], e_ref[pl.ds(start, tk), :]], axis=1)
        for c in range(chains):
            s_scr[slot, :, c * cols:(c + 1) * cols] = _dot(kx, qx[c])

    def update(j, slot, causal):
        start = pl.multiple_of(j * tk, tk)
        vt = vt_ref[0, 0, :, pl.ds(start, tk)]
        for c in range(chains):
            sl = slice(c * cols, (c + 1) * cols)
            s = s_scr[slot, :, sl]
            if causal:
                key = start + lax.broadcasted_iota(jnp.int32, s.shape, 0)
                t = q0 + (lax.broadcasted_iota(jnp.int32, s.shape, 1) & (Q_BLOCK - 1))
                s = jnp.where(key <= t, s, -jnp.inf)
            m = m_scr[:, sl]
            m_new = jnp.maximum(m, jnp.max(s, axis=0, keepdims=True))
            alpha = jnp.exp2(m - m_new)
            p = jnp.exp2(s - m_new)
            m_scr[:, sl] = m_new
            acc_scr[:, sl] = alpha * acc_scr[:, sl] + _dot(vt, p.astype(BF16))

    m_scr[...] = jnp.full(m_scr.shape, -jnp.inf, F32)
    acc_scr[...] = jnp.zeros(acc_scr.shape, F32)
    scores(0, 0)
    last = q0 // tk

    @pl.loop(0, last // 2)
    def _(i):
        scores(2 * i + 1, 1)
        update(2 * i, 0, False)
        scores(2 * i + 2, 0)
        update(2 * i + 1, 1, False)

    @pl.when(last % 2 == 1)
    def _():
        scores(last, 1)
        update(last - 1, 0, False)
        update(last, 1, True)

    @pl.when(last % 2 == 0)
    def _():
        update(last, 0, True)

    o = acc_scr[0:HEAD_DIM, :] / acc_scr[HEAD_DIM:HEAD_DIM + 1, :]
    for g in range(GROUP):
        o_ref[0, :, g * HEAD_DIM:(g + 1) * HEAD_DIM] = o[:, g * Q_BLOCK:(g + 1) * Q_BLOCK].T


def _sel_attention(qt, bias_t, kk, vt, onehot, B, seq):
    n_sel = bias_t.shape[2]
    GW = GROUP * HEAD_DIM
    tk = min(512, seq)
    k_col = N_KV_HEADS
    body = functools.partial(_sel_body, tk=tk, chains=2)
    vmem = 2 * (Q_BLOCK * GW * 2 + Q_BLOCK * n_sel * 2 + 2 * seq * HEAD_DIM * 2 + seq * n_sel * 2
                + Q_BLOCK * GW * 4) + 10 * GROUP * Q_BLOCK * tk * 4
    return pl.pallas_call(
        body,
        grid=(B, N_KV_HEADS, seq // Q_BLOCK),
        in_specs=[
            pl.BlockSpec((1, GW, Q_BLOCK), lambda b, h, i: (b, h, i)),
            pl.BlockSpec((1, 1, n_sel, Q_BLOCK), lambda b, h, i: (b, h, 0, i)),
            pl.BlockSpec((1, seq, HEAD_DIM), lambda b, h, i: (b, 0, k_col + h)),
            pl.BlockSpec((1, 1, HEAD_DIM + ONES_ROWS, seq), lambda b, h, i: (b, h, 0, 0)),
            pl.BlockSpec((seq, n_sel), lambda b, h, i: (0, 0)),
        ],
        out_specs=pl.BlockSpec((1, Q_BLOCK, GW), lambda b, h, i: (b, i, h)),
        out_shape=jax.ShapeDtypeStruct((B, seq, ATTN_W), F32),
        scratch_shapes=[pltpu.VMEM((2, tk, GROUP * Q_BLOCK), F32),
                        pltpu.VMEM((1, GROUP * Q_BLOCK), F32),
                        pltpu.VMEM((HEAD_DIM + ONES_ROWS, GROUP * Q_BLOCK), F32)],
        compiler_params=_params(("parallel", "parallel", "arbitrary"), vmem),
        name="sel_attention",
    )(qt, bias_t, kk, vt, onehot)


def _win_body(q_ref, k_ref, v_ref, o_ref, *, span):
    q0 = pl.program_id(2) * Q_BLOCK
    qs = _stack_heads(q_ref[0])
    start = pl.multiple_of(jnp.maximum(q0 - WINDOW, 0), Q_BLOCK)
    k = k_ref[0, pl.ds(start, span), :]
    v = v_ref[0, pl.ds(start, span), :]
    s = _dot_nt(qs, k)
    t = _row_time(s.shape, q0)
    pos = start + lax.broadcasted_iota(jnp.int32, s.shape, 1)
    s = jnp.where((pos <= t) & (pos > t - WINDOW), s, -jnp.inf)
    m = jnp.max(s, axis=-1, keepdims=True)
    e = jnp.exp2(s - m)
    p = e / jnp.sum(e, axis=-1, keepdims=True)
    o = _dot(p.astype(BF16), v)
    for g in range(GROUP):
        o_ref[0, :, g * HEAD_DIM:(g + 1) * HEAD_DIM] = o[g * Q_BLOCK:(g + 1) * Q_BLOCK]


def _win_attention(q, kk, vv, B, seq):
    GW = GROUP * HEAD_DIM
    span = WINDOW + Q_BLOCK
    assert seq >= span
    k_col = 2 * N_KV_HEADS
    v_col = 2 * N_KV_HEADS
    body = functools.partial(_win_body, span=span)
    vmem = 2 * (Q_BLOCK * GW * 2 + 2 * seq * HEAD_DIM * 2 + Q_BLOCK * GW * 4) \
        + 10 * GROUP * Q_BLOCK * span * 4
    return pl.pallas_call(
        body,
        grid=(B, N_KV_HEADS, seq // Q_BLOCK),
        in_specs=[
            pl.BlockSpec((1, Q_BLOCK, GW), lambda b, h, i: (b, i, h)),
            pl.BlockSpec((1, seq, HEAD_DIM), lambda b, h, i: (b, 0, k_col + h)),
            pl.BlockSpec((1, seq, HEAD_DIM), lambda b, h, i: (b, 0, v_col + h)),
        ],
        out_specs=pl.BlockSpec((1, Q_BLOCK, GW), lambda b, h, i: (b, i, h)),
        out_shape=jax.ShapeDtypeStruct((B, seq, ATTN_W), F32),
        compiler_params=_params(("parallel", "parallel", "parallel"), vmem),
        name="win_attention",
    )(q, kk, vv)


def _outproj_body(yc_ref, oc_ref, os_ref, ow_ref, zg_ref, g_ref, w_ref, x_ref, h_ref, a_scr):
    @pl.when(pl.program_id(1) == 0)
    def _():
        gates = jax.nn.sigmoid(zg_ref[...])
        heads = []
        for h in range(N_HEADS):
            sl = slice(h * HEAD_DIM, (h + 1) * HEAD_DIM)
            c = h * N_BRANCH
            heads.append(gates[:, c:c + 1] * oc_ref[:, sl] + gates[:, c + 1:c + 2] * os_ref[:, sl]
                         + gates[:, c + 2:c + 3] * ow_ref[:, sl])
        y_attn = jnp.concatenate(heads, axis=1)
        a_scr[:, 0:CONV_W] = yc_ref[...]
        a_scr[:, CONV_W:] = _rms(y_attn, g_ref[...]).astype(BF16)

    h_ref[...] = x_ref[...] + _dot(a_scr[...], w_ref[...])


def _out_projection(yc, oc, osel, ow, zg, g, w, x2):
    T, D = x2.shape
    K = w.shape[0]
    tm = min(512, T)
    tn = min(512, D)
    vmem = 2 * (tm * CONV_W * 2 + 3 * tm * ATTN_W * 4 + tm * LANES * 4 + K * tn * 2
                + 2 * tm * tn * 4) + tm * K * 2 + 4 * tm * ATTN_W * 4
    return pl.pallas_call(
        _outproj_body,
        grid=(T // tm, D // tn),
        in_specs=[
            pl.BlockSpec((tm, CONV_W), lambda i, j: (i, 0)),
            pl.BlockSpec((tm, ATTN_W), lambda i, j: (i, 0)),
            pl.BlockSpec((tm, ATTN_W), lambda i, j: (i, 0)),
            pl.BlockSpec((tm, ATTN_W), lambda i, j: (i, 0)),
            pl.BlockSpec((tm, LANES), lambda i, j: (i, 0)),
            pl.BlockSpec((1, ATTN_W), lambda i, j: (0, 0)),
            pl.BlockSpec((K, tn), lambda i, j: (0, j)),
            pl.BlockSpec((tm, tn), lambda i, j: (i, j)),
        ],
        out_specs=pl.BlockSpec((tm, tn), lambda i, j: (i, j)),
        out_shape=jax.ShapeDtypeStruct((T, D), F32),
        scratch_shapes=[pltpu.VMEM((tm, K), BF16)],
        compiler_params=_params(("parallel", "arbitrary"), vmem),
        name="out_projection",
    )(yc, oc, osel, ow, zg, g, w, x2)


def _ffn_body(h_ref, g_ref, wg_ref, wu_ref, wd_ref, gf_ref, o_ref, a_scr, *, final_norm, tn):
    j = pl.program_id(1)

    @pl.when(j == 0)
    def _():
        a_scr[...] = _rms(h_ref[...], g_ref[...]).astype(BF16)

    a = a_scr[...]
    u = (jax.nn.silu(_dot(a, wg_ref[...])) * _dot(a, wu_ref[...])).astype(BF16)
    D = o_ref.shape[1]

    @pl.when(j == 0)
    def _():
        for n in range(D // tn):
            o_ref[:, n * tn:(n + 1) * tn] = _dot(u, wd_ref[:, n * tn:(n + 1) * tn])

    @pl.when(j > 0)
    def _():
        for n in range(D // tn):
            o_ref[:, n * tn:(n + 1) * tn] += _dot(u, wd_ref[:, n * tn:(n + 1) * tn])

    @pl.when(j == pl.num_programs(1) - 1)
    def _():
        out = h_ref[...] + o_ref[...]
        o_ref[...] = _rms(out, gf_ref[...]) if final_norm else out


def _ffn(h, g, wg, wu, wd, g_final, final_norm):
    T, D = h.shape
    F = wg.shape[1]
    tm = min(1024, T)
    tf = 512
    tn = 512
    assert F % tf == 0 and D % tn == 0
    body = functools.partial(_ffn_body, final_norm=final_norm, tn=tn)
    vmem = tm * D * 4 + 2 * (3 * D * tf * 2 + tm * D * 4) + tm * D * 2 + 6 * tm * tf * 4 + 2 * tm * tn * 4
    return pl.pallas_call(
        body,
        grid=(T // tm, F // tf),
        in_specs=[
            pl.BlockSpec((tm, D), lambda i, j: (i, 0), pipeline_mode=pl.Buffered(1)),
            pl.BlockSpec((1, D), lambda i, j: (0, 0)),
            pl.BlockSpec((D, tf), lambda i, j: (0, j)),
            pl.BlockSpec((D, tf), lambda i, j: (0, j)),
            pl.BlockSpec((tf, D), lambda i, j: (j, 0)),
            pl.BlockSpec((1, D), lambda i, j: (0, 0)),
        ],
        out_specs=pl.BlockSpec((tm, D), lambda i, j: (i, 0)),
        out_shape=jax.ShapeDtypeStruct((T, D), F32),
        scratch_shapes=[pltpu.VMEM((tm, D), BF16)],
        compiler_params=_params(("parallel", "arbitrary"), vmem),
        name="ffn",
    )(h, g, wg, wu, wd, g_final)


def _rope_tables(seq):
    half = HEAD_DIM // 2
    inv = 1.0 / (ROPE_THETA ** (jnp.arange(half, dtype=F32) / half))
    ang = jnp.arange(seq).astype(F32)[:, None] * inv[None, :]
    cos = jnp.cos(ang)
    sin = jnp.sin(ang)
    return jnp.concatenate([cos, cos], axis=1), jnp.concatenate([-sin, sin], axis=1)


def _overlap_t(n_sel, n_cmp):
    cs = np.arange(n_cmp)[None, :] * CMP_STRIDE
    ss = np.arange(n_sel)[:, None] * SEL_BLOCK
    return jnp.asarray(((cs < ss + SEL_BLOCK) & (cs + CMP_BLOCK > ss)).astype(np.float32), dtype=BF16)


def _block_onehot(seq, n_sel):
    return jnp.asarray((np.arange(seq)[:, None] // SEL_BLOCK == np.arange(n_sel)[None, :])
                       .astype(np.float32), dtype=BF16)


def kernel(x, norm_mix, w_in, conv_w, cmp_pe_k, cmp_w1_k, cmp_w2_k, cmp_pe_v, cmp_w1_v, cmp_w2_v,
           norm_conv_out, norm_attn_out, w_out, norm_ffn, w_gate, w_up, w_down, norm_final):
    B, S, D = x.shape
    T = B * S
    depth = norm_mix.shape[0]
    assert S % Q_BLOCK == 0 and S % CMP_STRIDE == 0
    n_chunks = S // CMP_STRIDE
    n_sel = S // SEL_BLOCK
    cos_t, sin_t = _rope_tables(S)
    ovt = _overlap_t(n_sel, n_chunks)
    onehot = _block_onehot(S, n_sel)
    chunk_w = CMP_STRIDE * HEAD_DIM

    h = x.reshape(T, D)
    for l in range(depth):
        zc, q, kk, vv, zg = _in_projection(h, norm_mix[l][None], _proj_weight(w_in[l]), cos_t, sin_t, S)

        y_conv = _short_conv(zc, conv_w[l], norm_conv_out[l][None], S)

        kvc_in = jnp.stack([kk[:, :KV_W], vv[:, :KV_W]])
        kvc_in = kvc_in.reshape(2, B, S, N_KV_HEADS, HEAD_DIM).transpose(0, 1, 3, 2, 4)
        kvc_in = kvc_in.reshape(2, B * N_KV_HEADS, n_chunks, chunk_w)
        pe = jnp.stack([cmp_pe_k[l], cmp_pe_v[l]]).reshape(2, 1, CMP_BLOCK * HEAD_DIM)
        kvc = _compress(kvc_in, pe, jnp.stack([cmp_w1_k[l], cmp_w1_v[l]]),
                        jnp.stack([cmp_w2_k[l], cmp_w2_v[l]]))

        q3 = q.reshape(B, S, ATTN_W)
        kk3 = kk.reshape(B, S, KV_OUT_W)
        vv3 = vv.reshape(B, S, KV_OUT_W)
        o_c, bias_t = _cmp_attention(q3, kvc, ovt, B, S)
        qt = q3.transpose(0, 2, 1)
        vt_sel = vv3[:, :, KV_W:2 * KV_W].transpose(0, 2, 1).reshape(B, N_KV_HEADS, HEAD_DIM, S)
        vt_sel = jnp.concatenate([vt_sel, jnp.ones((B, N_KV_HEADS, ONES_ROWS, S), BF16)], axis=2)
        o_s = _sel_attention(qt, bias_t, kk3, vt_sel, onehot, B, S)
        o_w = _win_attention(q3, kk3, vv3, B, S)

        h = _out_projection(y_conv, o_c.reshape(T, ATTN_W), o_s.reshape(T, ATTN_W),
                            o_w.reshape(T, ATTN_W), zg, norm_attn_out[l][None],
                            w_out[l].astype(BF16), h)

        h = _ffn(h, norm_ffn[l][None], w_gate[l].astype(BF16), w_up[l].astype(BF16),
                 w_down[l].astype(BF16), norm_final[None], l == depth - 1)
    return h.reshape(B, S, D)
```

```python
import functools

import jax
import jax.numpy as jnp
import numpy as np
from jax import lax
from jax.experimental import pallas as pl
from jax.experimental.pallas import tpu as pltpu

F32 = jnp.float32
BF16 = jnp.bfloat16

HEAD_DIM = 128
N_HEADS = 8
N_KV_HEADS = 2
GROUP = N_HEADS // N_KV_HEADS
N_BRANCH = 3
CONV_TAPS = 3
CMP_BLOCK = 32
CMP_STRIDE = 16
SEL_BLOCK = 64
N_SELECT = 16
WINDOW = 512
Q_BLOCK = 128
ROPE_THETA = 10000.0
RMS_EPS = 1e-6

CONV_W = 1024
ATTN_W = N_HEADS * HEAD_DIM
KV_W = N_KV_HEADS * HEAD_DIM
N_GATES = N_HEADS * N_BRANCH

V7X_VMEM_BYTES = 64 * 1024 * 1024
V7X_VMEM_BUDGET = V7X_VMEM_BYTES - 6 * 1024 * 1024
LANES = 128
SUBLANES = 8
PROJ_TILE_N = 512

MASK_BIAS = -float(2 ** 30)
LOG2_E = 1.4426950408889634
ONES_ROWS = 16
GATE_ROWS = 16


def _params(semantics, vmem_bytes):
    return pltpu.CompilerParams(dimension_semantics=semantics,
                                vmem_limit_bytes=int(min(vmem_bytes, V7X_VMEM_BUDGET)))


def _rms(x, g):
    return x * lax.rsqrt(jnp.mean(x * x, axis=-1, keepdims=True) + RMS_EPS) * g


def _split3(x):
    hi = x.astype(BF16)
    r1 = x - hi.astype(F32)
    mid = r1.astype(BF16)
    lo = (r1 - mid.astype(F32)).astype(BF16)
    return hi, mid, lo


def _dot(a, b):
    return jnp.dot(a, b, preferred_element_type=F32)


def _dot_f32(a, b):
    a0, a1, a2 = _split3(a)
    b0, b1, b2 = _split3(b)
    return (_dot(a0, b0) + (_dot(a0, b1) + _dot(a1, b0))
            + (_dot(a0, b2) + _dot(a1, b1) + _dot(a2, b0)))


N_CONV_TILES = 3 * CONV_W // PROJ_TILE_N
N_Q_TILES = ATTN_W // PROJ_TILE_N
N_K_TILES = 2
N_V_TILES = 2
N_PROJ_TILES = N_CONV_TILES + N_Q_TILES + N_K_TILES + N_V_TILES
KV_OUT_W = N_K_TILES * PROJ_TILE_N


def _proj_weight(w):
    o = 3 * CONV_W + ATTN_W

    def col(n):
        return w[:, o + n * KV_W:o + (n + 1) * KV_W]

    def zeros(n):
        return jnp.zeros((w.shape[0], n), w.dtype)

    gates = w[:, o + 6 * KV_W:]
    parts = [w[:, :o], col(0), col(2), col(4), zeros(KV_W),
             col(1), col(3), col(5), gates, zeros(KV_W - gates.shape[1])]
    return jnp.concatenate(parts, axis=1).astype(BF16)


def _inproj_body(x_ref, g_ref, w_ref, cos_ref, sin_ref, zc_ref, q_ref, k_ref, v_ref, zg_ref, a_scr,
                 *, q_scale):
    j = pl.program_id(1)

    @pl.when(j == 0)
    def _():
        a_scr[...] = _rms(x_ref[...], g_ref[...]).astype(BF16)

    z = _dot(a_scr[...], w_ref[...])

    def rope(z4):
        cos = cos_ref[...]
        sin = sin_ref[...]
        heads = []
        for h in range(PROJ_TILE_N // HEAD_DIM):
            zh = z4[:, h * HEAD_DIM:(h + 1) * HEAD_DIM]
            heads.append(zh * cos + pltpu.roll(zh, HEAD_DIM // 2, axis=1) * sin)
        return jnp.concatenate(heads, axis=1)

    q_start = N_CONV_TILES
    k_start = q_start + N_Q_TILES
    v_start = k_start + N_K_TILES

    @pl.when(j < q_start)
    def _():
        zc_ref[...] = z

    @pl.when((j >= q_start) & (j < k_start))
    def _():
        q_ref[...] = (rope(z) * q_scale).astype(BF16)

    @pl.when((j >= k_start) & (j < v_start))
    def _():
        k_ref[...] = rope(z).astype(BF16)

    @pl.when(j >= v_start)
    def _():
        v_ref[...] = z.astype(BF16)

    @pl.when(j == N_PROJ_TILES - 1)
    def _():
        zg_ref[...] = z[:, KV_W:KV_W + LANES]


def _in_projection(x2, g, w_perm, cos_t, sin_t, seq):
    T, D = x2.shape
    tn = PROJ_TILE_N
    assert w_perm.shape == (D, N_PROJ_TILES * tn)
    tm = min(1024, seq)
    pos_blocks = seq // tm
    q_start = N_CONV_TILES
    k_start = q_start + N_Q_TILES
    v_start = k_start + N_K_TILES
    body = functools.partial(_inproj_body, q_scale=HEAD_DIM ** -0.5 * LOG2_E)
    vmem = 2 * (tm * D * 4 + D * tn * 2 + 2 * tm * HEAD_DIM * 4 + tm * tn * 4 + 3 * tm * tn * 2
                + tm * LANES * 4) + tm * D * 2 + 2 * tm * D * 4
    return pl.pallas_call(
        body,
        grid=(T // tm, N_PROJ_TILES),
        in_specs=[
            pl.BlockSpec((tm, D), lambda i, j: (i, 0)),
            pl.BlockSpec((1, D), lambda i, j: (0, 0)),
            pl.BlockSpec((D, tn), lambda i, j: (0, j)),
            pl.BlockSpec((tm, HEAD_DIM), lambda i, j: (i % pos_blocks, 0)),
            pl.BlockSpec((tm, HEAD_DIM), lambda i, j: (i % pos_blocks, 0)),
        ],
        out_specs=[
            pl.BlockSpec((tm, tn), lambda i, j: (i, jnp.minimum(j, q_start - 1))),
            pl.BlockSpec((tm, tn), lambda i, j: (i, jnp.clip(j - q_start, 0, N_Q_TILES - 1))),
            pl.BlockSpec((tm, tn), lambda i, j: (i, jnp.clip(j - k_start, 0, N_K_TILES - 1))),
            pl.BlockSpec((tm, tn), lambda i, j: (i, jnp.clip(j - v_start, 0, N_V_TILES - 1))),
            pl.BlockSpec((tm, LANES), lambda i, j: (i, 0)),
        ],
        out_shape=[
            jax.ShapeDtypeStruct((T, 3 * CONV_W), F32),
            jax.ShapeDtypeStruct((T, ATTN_W), BF16),
            jax.ShapeDtypeStruct((T, KV_OUT_W), BF16),
            jax.ShapeDtypeStruct((T, KV_OUT_W), BF16),
            jax.ShapeDtypeStruct((T, LANES), F32),
        ],
        scratch_shapes=[pltpu.VMEM((tm, D), BF16)],
        compiler_params=_params(("parallel", "arbitrary"), vmem),
        name="in_projection",
    )(x2, g, w_perm, cos_t, sin_t)


def _conv_body(cb_ref, cc_ref, ch_ref, hc_ref, hh_ref, w_ref, g_ref, o_ref, *, blocks_per_seq):
    i = pl.program_id(0)
    u = cc_ref[...] * ch_ref[...]
    halo = hc_ref[...] * hh_ref[...]
    halo = jnp.where(i % blocks_per_seq == 0, 0.0, halo)
    row = lax.broadcasted_iota(jnp.int32, u.shape, 0)
    h1 = halo[SUBLANES - 1:SUBLANES]
    h2 = halo[SUBLANES - 2:SUBLANES - 1]
    u1 = jnp.where(row == 0, h1, pltpu.roll(u, 1, axis=0))
    u2 = jnp.where(row == 0, h2, jnp.where(row == 1, h1, pltpu.roll(u, 2, axis=0)))
    w = w_ref[...]
    y = cb_ref[...] * (u2 * w[0:1] + u1 * w[1:2] + u * w[2:3])
    o_ref[...] = _rms(y, g_ref[...]).astype(BF16)


def _short_conv(zc, conv_w, g, seq):
    T = zc.shape[0]
    C = CONV_W
    tm = min(512, seq)
    hb = tm // SUBLANES
    body = functools.partial(_conv_body, blocks_per_seq=seq // tm)
    vmem = 2 * (3 * tm * C * 4 + 2 * SUBLANES * C * 4 + tm * C * 2) + 6 * tm * C * 4
    return pl.pallas_call(
        body,
        grid=(T // tm,),
        in_specs=[
            pl.BlockSpec((tm, C), lambda i: (i, 0)),
            pl.BlockSpec((tm, C), lambda i: (i, 1)),
            pl.BlockSpec((tm, C), lambda i: (i, 2)),
            pl.BlockSpec((SUBLANES, C), lambda i: (jnp.maximum(i * hb - 1, 0), 1)),
            pl.BlockSpec((SUBLANES, C), lambda i: (jnp.maximum(i * hb - 1, 0), 2)),
            pl.BlockSpec((CONV_TAPS, C), lambda i: (0, 0)),
            pl.BlockSpec((1, C), lambda i: (0, 0)),
        ],
        out_specs=pl.BlockSpec((tm, C), lambda i: (i, 0)),
        out_shape=jax.ShapeDtypeStruct((T, C), BF16),
        compiler_params=_params(("parallel",), vmem),
        name="short_conv",
    )(zc, zc, zc, zc, zc, conv_w, g)


def _compress_body(x_ref, pe_ref, w1_ref, w2_ref, o_ref, *, n_valid):
    x = x_ref[0, 0]
    half = x.shape[1]
    w1 = w1_ref[0]
    w1_hi = w1.astype(BF16)
    w1_lo = (w1 - w1_hi.astype(F32)).astype(BF16)
    first = _dot(x, w1_hi[:half]) + _dot(x, w1_lo[:half])
    second = _dot(x, w1_hi[half:]) + _dot(x, w1_lo[half:])
    n = x.shape[0]
    second = pltpu.roll(second, n - 1, axis=0)
    pe = jnp.broadcast_to(pe_ref[0], (SUBLANES, 2 * half))
    pe_term = _dot_f32(pe, w1)[0:1]
    h = first + second + pe_term
    out = _dot_f32(jax.nn.silu(h), w2_ref[0])
    row = lax.broadcasted_iota(jnp.int32, out.shape, 0)
    o_ref[0, 0] = jnp.where(row < n_valid, out, 0.0)


def _compress(chunks, pe, w1, w2):
    two, BH, n, width = chunks.shape
    d = HEAD_DIM
    body = functools.partial(_compress_body, n_valid=n - 1)
    vmem = 2 * (n * width * 2 + 2 * width * 4 + 2 * width * d * 4 + d * d * 4 + n * d * 4) \
        + 8 * width * d * 4
    return pl.pallas_call(
        body,
        grid=(two, BH),
        in_specs=[
            pl.BlockSpec((1, 1, n, width), lambda s, b: (s, b, 0, 0)),
            pl.BlockSpec((1, 1, 2 * width), lambda s, b: (s, 0, 0)),
            pl.BlockSpec((1, 2 * width, d), lambda s, b: (s, 0, 0)),
            pl.BlockSpec((1, d, d), lambda s, b: (s, 0, 0)),
        ],
        out_specs=pl.BlockSpec((1, 1, n, d), lambda s, b: (s, b, 0, 0)),
        out_shape=jax.ShapeDtypeStruct((two, BH, n, d), F32),
        compiler_params=_params(("parallel", "parallel"), vmem),
        name="compress",
    )(chunks, pe, w1, w2)


def _select_blocks(imp, q0):
    n_sel = imp.shape[0]
    sel_i = lax.broadcasted_iota(jnp.int32, imp.shape, 0)
    sel_f = sel_i.astype(F32)
    cur = (q0 + lax.broadcasted_iota(jnp.int32, imp.shape, 1)) // SEL_BLOCK
    forced = (sel_i == 0) | (sel_i == cur) | (sel_i == cur - 1)
    v = jnp.where(sel_i > cur, -jnp.inf, jnp.where(forced, jnp.inf, imp))
    chosen = jnp.zeros(imp.shape, F32)
    for _ in range(min(N_SELECT, n_sel)):
        top = jnp.max(v, axis=0, keepdims=True)
        first = jnp.min(jnp.where(v == top, sel_f, float(n_sel)), axis=0, keepdims=True)
        pick = sel_f == first
        chosen = jnp.where(pick, 1.0, chosen)
        v = jnp.where(pick, -jnp.inf, v)
    return jnp.where((chosen > 0.0) & (sel_i <= cur), 0.0, MASK_BIAS)


def _attn_body(qt_ref, zgt_ref, kc_ref, vct_ref, ovt_ref, ks_ref, vst_ref, e_ref, kw_ref, vwt_ref,
               y_ref, s_scr, m_scr, acc_scr, oc_scr, ow_scr, *, tk, chains, span):
    q0 = pl.program_id(2) * Q_BLOCK
    GQ = GROUP * Q_BLOCK
    qt = [qt_ref[0, g * HEAD_DIM:(g + 1) * HEAD_DIM, :] for g in range(GROUP)]
    q_all = jnp.concatenate(qt, axis=1)

    def lane_time(shape):
        return q0 + (lax.broadcasted_iota(jnp.int32, shape, 1) & (Q_BLOCK - 1))

    s = _dot(kc_ref[0, 0].astype(BF16), q_all)
    cmp_end = lax.broadcasted_iota(jnp.int32, s.shape, 0) * CMP_STRIDE + (CMP_BLOCK - 1)
    s = jnp.where(cmp_end <= lane_time(s.shape), s, -jnp.inf)
    m = jnp.max(s, axis=0, keepdims=True)
    m = jnp.where(m == -jnp.inf, 0.0, m)
    e = jnp.exp2(s - m)
    p = e / jnp.maximum(jnp.sum(e, axis=0, keepdims=True), 1e-30)
    oc_scr[...] = _dot(vct_ref[0, 0].astype(BF16), p.astype(BF16))
    pg = p[:, 0:Q_BLOCK]
    for g in range(1, GROUP):
        pg = pg + p[:, g * Q_BLOCK:(g + 1) * Q_BLOCK]
    ovt = ovt_ref[...]
    p0, p1, p2 = _split3(pg)
    bias = _select_blocks(_dot(ovt, p0) + _dot(ovt, p1) + _dot(ovt, p2), q0).astype(BF16)

    w_start = pl.multiple_of(jnp.maximum(q0 - WINDOW, 0), Q_BLOCK)
    sw = _dot(kw_ref[0, pl.ds(w_start, span), :], q_all)
    pos = w_start + lax.broadcasted_iota(jnp.int32, sw.shape, 0)
    t = lane_time(sw.shape)
    sw = jnp.where((pos <= t) & (pos > t - WINDOW), sw, -jnp.inf)
    ew = jnp.exp2(sw - jnp.max(sw, axis=0, keepdims=True))
    ow = _dot(vwt_ref[0, 0, :, pl.ds(w_start, span)], ew.astype(BF16))
    ow_scr[...] = ow[0:HEAD_DIM] / ow[HEAD_DIM:HEAD_DIM + 1]

    cols = GQ // chains
    per = GROUP // chains
    qx = [jnp.concatenate([jnp.concatenate([qt[g], bias], axis=0) for g in range(c * per, (c + 1) * per)],
                          axis=1) for c in range(chains)]

    def scores(j, slot):
        start = pl.multiple_of(j * tk, tk)
        kx = jnp.concatenate([ks_ref[0, pl.ds(start, tk), :], e_ref[pl.ds(start, tk), :]], axis=1)
        for c in range(chains):
            s_scr[slot, :, c * cols:(c + 1) * cols] = _dot(kx, qx[c])

    def update(j, slot, causal):
        start = pl.multiple_of(j * tk, tk)
        vt = vst_ref[0, 0, :, pl.ds(start, tk)]
        for c in range(chains):
            sl = slice(c * cols, (c + 1) * cols)
            sc = s_scr[slot, :, sl]
            if causal:
                key = start + lax.broadcasted_iota(jnp.int32, sc.shape, 0)
                sc = jnp.where(key <= lane_time(sc.shape), sc, -jnp.inf)
            m_old = m_scr[:, sl]
            m_new = jnp.maximum(m_old, jnp.max(sc, axis=0, keepdims=True))
            alpha = jnp.exp2(m_old - m_new)
            pj = jnp.exp2(sc - m_new)
            m_scr[:, sl] = m_new
            acc_scr[:, sl] = alpha * acc_scr[:, sl] + _dot(vt, pj.astype(BF16))

    m_scr[...] = jnp.full(m_scr.shape, -jnp.inf, F32)
    acc_scr[...] = jnp.zeros(acc_scr.shape, F32)
    scores(0, 0)
    last = q0 // tk

    @pl.loop(0, last // 2)
    def _(i):
        scores(2 * i + 1, 1)
        update(2 * i, 0, False)
        scores(2 * i + 2, 0)
        update(2 * i + 1, 1, False)

    @pl.when(last % 2 == 1)
    def _():
        scores(last, 1)
        update(last - 1, 0, False)
        update(last, 1, True)

    @pl.when(last % 2 == 0)
    def _():
        update(last, 0, True)

    o_sel = acc_scr[0:HEAD_DIM, :] / acc_scr[HEAD_DIM:HEAD_DIM + 1, :]
    gates = jax.nn.sigmoid(zgt_ref[0, 0])
    for g in range(GROUP):
        sl = slice(g * Q_BLOCK, (g + 1) * Q_BLOCK)
        r = g * N_BRANCH
        y = (gates[r:r + 1] * oc_scr[:, sl] + gates[r + 1:r + 2] * o_sel[:, sl]
             + gates[r + 2:r + 3] * ow_scr[:, sl])
        y_ref[0, :, g * HEAD_DIM:(g + 1) * HEAD_DIM] = y.T


def _attention(qt, zgt, kvc, vct, ovt, kk, vst, onehot, vwt, B, seq):
    n_cmp = kvc.shape[2]
    n_sel = ovt.shape[0]
    GW = GROUP * HEAD_DIM
    GQ = GROUP * Q_BLOCK
    VR = HEAD_DIM + ONES_ROWS
    GR = zgt.shape[2]
    tk = min(512, seq)
    span = WINDOW + Q_BLOCK
    assert seq >= span
    body = functools.partial(_attn_body, tk=tk, chains=2, span=span)
    resident = seq * HEAD_DIM * 2 * 2 + seq * n_sel * 2 + 2 * VR * seq * 2
    vmem = 2 * (GW * Q_BLOCK * 2 + GR * Q_BLOCK * 4 + 2 * n_cmp * HEAD_DIM * 4 + n_sel * n_cmp * 2
                + resident + Q_BLOCK * GW * 4) \
        + (2 * tk + VR + 2 * HEAD_DIM + SUBLANES) * GQ * 4 + 8 * (n_cmp + span) * GQ * 4
    return pl.pallas_call(
        body,
        grid=(B, N_KV_HEADS, seq // Q_BLOCK),
        in_specs=[
            pl.BlockSpec((1, GW, Q_BLOCK), lambda b, h, i: (b, h, i)),
            pl.BlockSpec((1, 1, GR, Q_BLOCK), lambda b, h, i: (b, h, 0, i)),
            pl.BlockSpec((1, 1, n_cmp, HEAD_DIM), lambda b, h, i: (0, b * N_KV_HEADS + h, 0, 0)),
            pl.BlockSpec((1, 1, HEAD_DIM, n_cmp), lambda b, h, i: (b, h, 0, 0)),
            pl.BlockSpec((n_sel, n_cmp), lambda b, h, i: (0, 0)),
            pl.BlockSpec((1, seq, HEAD_DIM), lambda b, h, i: (b, 0, N_KV_HEADS + h)),
            pl.BlockSpec((1, 1, VR, seq), lambda b, h, i: (b, h, 0, 0)),
            pl.BlockSpec((seq, n_sel), lambda b, h, i: (0, 0)),
            pl.BlockSpec((1, seq, HEAD_DIM), lambda b, h, i: (b, 0, 2 * N_KV_HEADS + h)),
            pl.BlockSpec((1, 1, VR, seq), lambda b, h, i: (b, h, 0, 0)),
        ],
        out_specs=pl.BlockSpec((1, Q_BLOCK, GW), lambda b, h, i: (b, i, h)),
        out_shape=jax.ShapeDtypeStruct((B, seq, ATTN_W), F32),
        scratch_shapes=[pltpu.VMEM((2, tk, GQ), F32),
                        pltpu.VMEM((1, GQ), F32),
                        pltpu.VMEM((VR, GQ), F32),
                        pltpu.VMEM((HEAD_DIM, GQ), F32),
                        pltpu.VMEM((HEAD_DIM, GQ), F32)],
        compiler_params=_params(("parallel", "parallel", "arbitrary"), vmem),
        name="attention",
    )(qt, zgt, kvc, vct, ovt, kk, vst, onehot, kk, vwt)


def _outproj_body(yc_ref, ya_ref, g_ref, w_ref, x_ref, h_ref, a_scr):
    @pl.when(pl.program_id(1) == 0)
    def _():
        a_scr[:, 0:CONV_W] = yc_ref[...]
        a_scr[:, CONV_W:] = _rms(ya_ref[...], g_ref[...]).astype(BF16)

    h_ref[...] = x_ref[...] + _dot(a_scr[...], w_ref[...])


def _out_projection(yc, ya, g, w, x2):
    T, D = x2.shape
    K = w.shape[0]
    tm = min(1024, T)
    tn = min(512, D)
    vmem = 2 * (tm * CONV_W * 2 + tm * ATTN_W * 4 + K * tn * 2 + 2 * tm * tn * 4) + tm * K * 2 \
        + 3 * tm * ATTN_W * 4
    return pl.pallas_call(
        _outproj_body,
        grid=(T // tm, D // tn),
        in_specs=[
            pl.BlockSpec((tm, CONV_W), lambda i, j: (i, 0)),
            pl.BlockSpec((tm, ATTN_W), lambda i, j: (i, 0)),
            pl.BlockSpec((1, ATTN_W), lambda i, j: (0, 0)),
            pl.BlockSpec((K, tn), lambda i, j: (0, j)),
            pl.BlockSpec((tm, tn), lambda i, j: (i, j)),
        ],
        out_specs=pl.BlockSpec((tm, tn), lambda i, j: (i, j)),
        out_shape=jax.ShapeDtypeStruct((T, D), F32),
        scratch_shapes=[pltpu.VMEM((tm, K), BF16)],
        compiler_params=_params(("parallel", "arbitrary"), vmem),
        name="out_projection",
    )(yc, ya, g, w, x2)


def _ffn_body(h_ref, g_ref, wg_ref, wu_ref, wd_ref, gf_ref, o_ref, a_scr, *, final_norm, tn):
    j = pl.program_id(1)

    @pl.when(j == 0)
    def _():
        a_scr[...] = _rms(h_ref[...], g_ref[...]).astype(BF16)

    a = a_scr[...]
    u = (jax.nn.silu(_dot(a, wg_ref[...])) * _dot(a, wu_ref[...])).astype(BF16)
    D = o_ref.shape[1]

    @pl.when(j == 0)
    def _():
        for n in range(D // tn):
            o_ref[:, n * tn:(n + 1) * tn] = _dot(u, wd_ref[:, n * tn:(n + 1) * tn])

    @pl.when(j > 0)
    def _():
        for n in range(D // tn):
            o_ref[:, n * tn:(n + 1) * tn] += _dot(u, wd_ref[:, n * tn:(n + 1) * tn])

    @pl.when(j == pl.num_programs(1) - 1)
    def _():
        out = h_ref[...] + o_ref[...]
        o_ref[...] = _rms(out, gf_ref[...]) if final_norm else out


def _ffn(h, g, wg, wu, wd, g_final, final_norm):
    T, D = h.shape
    F = wg.shape[1]
    tm = min(1024, T)
    tf = 512
    tn = 512
    assert F % tf == 0 and D % tn == 0
    body = functools.partial(_ffn_body, final_norm=final_norm, tn=tn)
    vmem = tm * D * 4 + 2 * (3 * D * tf * 2 + tm * D * 4) + tm * D * 2 + 6 * tm * tf * 4 + 2 * tm * tn * 4
    return pl.pallas_call(
        body,
        grid=(T // tm, F // tf),
        in_specs=[
            pl.BlockSpec((tm, D), lambda i, j: (i, 0), pipeline_mode=pl.Buffered(1)),
            pl.BlockSpec((1, D), lambda i, j: (0, 0)),
            pl.BlockSpec((D, tf), lambda i, j: (0, j)),
            pl.BlockSpec((D, tf), lambda i, j: (0, j)),
            pl.BlockSpec((tf, D), lambda i, j: (j, 0)),
            pl.BlockSpec((1, D), lambda i, j: (0, 0)),
        ],
        out_specs=pl.BlockSpec((tm, D), lambda i, j: (i, 0)),
        out_shape=jax.ShapeDtypeStruct((T, D), F32),
        scratch_shapes=[pltpu.VMEM((tm, D), BF16)],
        compiler_params=_params(("parallel", "arbitrary"), vmem),
        name="ffn",
    )(h, g, wg, wu, wd, g_final)


def _rope_tables(seq):
    half = HEAD_DIM // 2
    inv = 1.0 / (ROPE_THETA ** (jnp.arange(half, dtype=F32) / half))
    ang = jnp.arange(seq).astype(F32)[:, None] * inv[None, :]
    cos = jnp.cos(ang)
    sin = jnp.sin(ang)
    return jnp.concatenate([cos, cos], axis=1), jnp.concatenate([-sin, sin], axis=1)


def _overlap_t(n_sel, n_cmp):
    cs = np.arange(n_cmp)[None, :] * CMP_STRIDE
    ss = np.arange(n_sel)[:, None] * SEL_BLOCK
    return jnp.asarray(((cs < ss + SEL_BLOCK) & (cs + CMP_BLOCK > ss)).astype(np.float32), dtype=BF16)


def _block_onehot(seq, n_sel):
    return jnp.asarray((np.arange(seq)[:, None] // SEL_BLOCK == np.arange(n_sel)[None, :])
                       .astype(np.float32), dtype=BF16)


def kernel(x, norm_mix, w_in, conv_w, cmp_pe_k, cmp_w1_k, cmp_w2_k, cmp_pe_v, cmp_w1_v, cmp_w2_v,
           norm_conv_out, norm_attn_out, w_out, norm_ffn, w_gate, w_up, w_down, norm_final):
    B, S, D = x.shape
    T = B * S
    depth = norm_mix.shape[0]
    assert S % Q_BLOCK == 0 and S % CMP_STRIDE == 0
    n_chunks = S // CMP_STRIDE
    n_sel = S // SEL_BLOCK
    cos_t, sin_t = _rope_tables(S)
    ovt = _overlap_t(n_sel, n_chunks)
    onehot = _block_onehot(S, n_sel)
    chunk_w = CMP_STRIDE * HEAD_DIM

    h = x.reshape(T, D)
    for l in range(depth):
        zc, q, kk, vv, zg = _in_projection(h, norm_mix[l][None], _proj_weight(w_in[l]), cos_t, sin_t, S)

        y_conv = _short_conv(zc, conv_w[l], norm_conv_out[l][None], S)

        kvc_in = jnp.stack([kk[:, :KV_W], vv[:, :KV_W]])
        kvc_in = kvc_in.reshape(2, B, S, N_KV_HEADS, HEAD_DIM).transpose(0, 1, 3, 2, 4)
        kvc_in = kvc_in.reshape(2, B * N_KV_HEADS, n_chunks, chunk_w)
        pe = jnp.stack([cmp_pe_k[l], cmp_pe_v[l]]).reshape(2, 1, CMP_BLOCK * HEAD_DIM)
        kvc = _compress(kvc_in, pe, jnp.stack([cmp_w1_k[l], cmp_w1_v[l]]),
                        jnp.stack([cmp_w2_k[l], cmp_w2_v[l]]))

        qt = q.reshape(B, S, ATTN_W).transpose(0, 2, 1)
        kk3 = kk.reshape(B, S, KV_OUT_W)

        def v_t(n):
            vt = vv[:, n * KV_W:(n + 1) * KV_W].reshape(B, S, N_KV_HEADS, HEAD_DIM).transpose(0, 2, 3, 1)
            return jnp.concatenate([vt, jnp.ones((B, N_KV_HEADS, ONES_ROWS, S), BF16)], axis=2)

        vct = kvc[1].reshape(B, N_KV_HEADS, n_chunks, HEAD_DIM).transpose(0, 1, 3, 2)
        zgt = zg[:, :N_GATES].reshape(B, S, N_KV_HEADS, GROUP * N_BRANCH).transpose(0, 2, 3, 1)
        zgt = jnp.pad(zgt, ((0, 0), (0, 0), (0, GATE_ROWS - GROUP * N_BRANCH), (0, 0)))
        y_attn = _attention(qt, zgt, kvc, vct, ovt, kk3, v_t(1), onehot, v_t(2), B, S)

        h = _out_projection(y_conv, y_attn.reshape(T, ATTN_W), norm_attn_out[l][None],
                            w_out[l].astype(BF16), h)

        h = _ffn(h, norm_ffn[l][None], w_gate[l].astype(BF16), w_up[l].astype(BF16),
                 w_down[l].astype(BF16), norm_final[None], l == depth - 1)
    return h.reshape(B, S, D)
```

```python
import functools

import jax
import jax.numpy as jnp
import numpy as np
from jax import lax
from jax.experimental import pallas as pl
from jax.experimental.pallas import tpu as pltpu

F32 = jnp.float32
BF16 = jnp.bfloat16

HEAD_DIM = 128
N_HEADS = 8
N_KV_HEADS = 2
GROUP = N_HEADS // N_KV_HEADS
N_BRANCH = 3
CONV_TAPS = 3
CMP_BLOCK = 32
CMP_STRIDE = 16
SEL_BLOCK = 64
N_SELECT = 16
WINDOW = 512
ATTN_Q = 256
ROPE_THETA = 10000.0
RMS_EPS = 1e-6

CONV_W = 1024
ATTN_W = N_HEADS * HEAD_DIM
KV_W = N_KV_HEADS * HEAD_DIM
N_GATES = N_HEADS * N_BRANCH

V7X_VMEM_BYTES = 64 * 1024 * 1024
V7X_VMEM_BUDGET = V7X_VMEM_BYTES - 6 * 1024 * 1024
LANES = 128
SUBLANES = 8
PROJ_TILE_N = 512

MASK_BIAS = -float(2 ** 30)
LOG2_E = 1.4426950408889634
ONES_ROWS = 16
GATE_ROWS = 16


def _params(semantics, vmem_bytes):
    return pltpu.CompilerParams(dimension_semantics=semantics,
                                vmem_limit_bytes=int(min(vmem_bytes, V7X_VMEM_BUDGET)))


def _rms(x, g):
    return x * lax.rsqrt(jnp.mean(x * x, axis=-1, keepdims=True) + RMS_EPS) * g


def _split3(x):
    hi = x.astype(BF16)
    r1 = x - hi.astype(F32)
    mid = r1.astype(BF16)
    lo = (r1 - mid.astype(F32)).astype(BF16)
    return hi, mid, lo


def _dot(a, b):
    return jnp.dot(a, b, preferred_element_type=F32)


def _dot_f32(a, b):
    a0, a1, a2 = _split3(a)
    b0, b1, b2 = _split3(b)
    return (_dot(a0, b0) + (_dot(a0, b1) + _dot(a1, b0))
            + (_dot(a0, b2) + _dot(a1, b1) + _dot(a2, b0)))


N_CONV_TILES = 3 * CONV_W // PROJ_TILE_N
N_Q_TILES = ATTN_W // PROJ_TILE_N
N_K_TILES = 2
N_V_TILES = 2
N_PROJ_TILES = N_CONV_TILES + N_Q_TILES + N_K_TILES + N_V_TILES
KV_OUT_W = N_K_TILES * PROJ_TILE_N


def _proj_weight(w):
    o = 3 * CONV_W + ATTN_W

    def col(n):
        return w[:, o + n * KV_W:o + (n + 1) * KV_W]

    def zeros(n):
        return jnp.zeros((w.shape[0], n), w.dtype)

    gates = w[:, o + 6 * KV_W:]
    parts = [w[:, :o], col(0), col(2), col(4), zeros(KV_W),
             col(1), col(3), col(5), gates, zeros(KV_W - gates.shape[1])]
    return jnp.concatenate(parts, axis=1).astype(BF16)


def _inproj_body(x_ref, g_ref, w_ref, cos_ref, sin_ref, zc_ref, q_ref, k_ref, v_ref, zg_ref, a_scr,
                 *, q_scale):
    j = pl.program_id(1)

    @pl.when(j == 0)
    def _():
        a_scr[...] = _rms(x_ref[...], g_ref[...]).astype(BF16)

    z = _dot(a_scr[...], w_ref[...])

    def rope(z4):
        cos = cos_ref[...]
        sin = sin_ref[...]
        heads = []
        for h in range(PROJ_TILE_N // HEAD_DIM):
            zh = z4[:, h * HEAD_DIM:(h + 1) * HEAD_DIM]
            heads.append(zh * cos + pltpu.roll(zh, HEAD_DIM // 2, axis=1) * sin)
        return jnp.concatenate(heads, axis=1)

    q_start = N_CONV_TILES
    k_start = q_start + N_Q_TILES
    v_start = k_start + N_K_TILES

    @pl.when(j < q_start)
    def _():
        zc_ref[...] = z

    @pl.when((j >= q_start) & (j < k_start))
    def _():
        q_ref[...] = (rope(z) * q_scale).astype(BF16)

    @pl.when((j >= k_start) & (j < v_start))
    def _():
        k_ref[...] = rope(z).astype(BF16)

    @pl.when(j >= v_start)
    def _():
        v_ref[...] = z.astype(BF16)

    @pl.when(j == N_PROJ_TILES - 1)
    def _():
        zg_ref[...] = z[:, KV_W:KV_W + LANES]


def _in_projection(x2, g, w_perm, cos_t, sin_t, seq):
    T, D = x2.shape
    tn = PROJ_TILE_N
    assert w_perm.shape == (D, N_PROJ_TILES * tn)
    tm = min(1024, seq)
    pos_blocks = seq // tm
    q_start = N_CONV_TILES
    k_start = q_start + N_Q_TILES
    v_start = k_start + N_K_TILES
    body = functools.partial(_inproj_body, q_scale=HEAD_DIM ** -0.5 * LOG2_E)
    vmem = 2 * (tm * D * 4 + D * tn * 2 + 2 * tm * HEAD_DIM * 4 + tm * tn * 4 + 3 * tm * tn * 2
                + tm * LANES * 4) + tm * D * 2 + 2 * tm * D * 4
    return pl.pallas_call(
        body,
        grid=(T // tm, N_PROJ_TILES),
        in_specs=[
            pl.BlockSpec((tm, D), lambda i, j: (i, 0)),
            pl.BlockSpec((1, D), lambda i, j: (0, 0)),
            pl.BlockSpec((D, tn), lambda i, j: (0, j)),
            pl.BlockSpec((tm, HEAD_DIM), lambda i, j: (i % pos_blocks, 0)),
            pl.BlockSpec((tm, HEAD_DIM), lambda i, j: (i % pos_blocks, 0)),
        ],
        out_specs=[
            pl.BlockSpec((tm, tn), lambda i, j: (i, jnp.minimum(j, q_start - 1))),
            pl.BlockSpec((tm, tn), lambda i, j: (i, jnp.clip(j - q_start, 0, N_Q_TILES - 1))),
            pl.BlockSpec((tm, tn), lambda i, j: (i, jnp.clip(j - k_start, 0, N_K_TILES - 1))),
            pl.BlockSpec((tm, tn), lambda i, j: (i, jnp.clip(j - v_start, 0, N_V_TILES - 1))),
            pl.BlockSpec((tm, LANES), lambda i, j: (i, 0)),
        ],
        out_shape=[
            jax.ShapeDtypeStruct((T, 3 * CONV_W), F32),
            jax.ShapeDtypeStruct((T, ATTN_W), BF16),
            jax.ShapeDtypeStruct((T, KV_OUT_W), BF16),
            jax.ShapeDtypeStruct((T, KV_OUT_W), BF16),
            jax.ShapeDtypeStruct((T, LANES), F32),
        ],
        scratch_shapes=[pltpu.VMEM((tm, D), BF16)],
        compiler_params=_params(("parallel", "arbitrary"), vmem),
        name="in_projection",
    )(x2, g, w_perm, cos_t, sin_t)


def _conv_body(cb_ref, cc_ref, ch_ref, hc_ref, hh_ref, w_ref, g_ref, o_ref, *, blocks_per_seq):
    i = pl.program_id(0)
    u = cc_ref[...] * ch_ref[...]
    halo = hc_ref[...] * hh_ref[...]
    halo = jnp.where(i % blocks_per_seq == 0, 0.0, halo)
    row = lax.broadcasted_iota(jnp.int32, u.shape, 0)
    h1 = halo[SUBLANES - 1:SUBLANES]
    h2 = halo[SUBLANES - 2:SUBLANES - 1]
    u1 = jnp.where(row == 0, h1, pltpu.roll(u, 1, axis=0))
    u2 = jnp.where(row == 0, h2, jnp.where(row == 1, h1, pltpu.roll(u, 2, axis=0)))
    w = w_ref[...]
    y = cb_ref[...] * (u2 * w[0:1] + u1 * w[1:2] + u * w[2:3])
    o_ref[...] = _rms(y, g_ref[...]).astype(BF16)


def _short_conv(zc, conv_w, g, seq):
    T = zc.shape[0]
    C = CONV_W
    tm = min(512, seq)
    hb = tm // SUBLANES
    body = functools.partial(_conv_body, blocks_per_seq=seq // tm)
    vmem = 2 * (3 * tm * C * 4 + 2 * SUBLANES * C * 4 + tm * C * 2) + 6 * tm * C * 4
    return pl.pallas_call(
        body,
        grid=(T // tm,),
        in_specs=[
            pl.BlockSpec((tm, C), lambda i: (i, 0)),
            pl.BlockSpec((tm, C), lambda i: (i, 1)),
            pl.BlockSpec((tm, C), lambda i: (i, 2)),
            pl.BlockSpec((SUBLANES, C), lambda i: (jnp.maximum(i * hb - 1, 0), 1)),
            pl.BlockSpec((SUBLANES, C), lambda i: (jnp.maximum(i * hb - 1, 0), 2)),
            pl.BlockSpec((CONV_TAPS, C), lambda i: (0, 0)),
            pl.BlockSpec((1, C), lambda i: (0, 0)),
        ],
        out_specs=pl.BlockSpec((tm, C), lambda i: (i, 0)),
        out_shape=jax.ShapeDtypeStruct((T, C), BF16),
        compiler_params=_params(("parallel",), vmem),
        name="short_conv",
    )(zc, zc, zc, zc, zc, conv_w, g)


def _compress_body(x_ref, pe_ref, w1_ref, w2_ref, o_ref, *, n_valid):
    x = x_ref[0, 0]
    half = x.shape[1]
    w1 = w1_ref[0]
    w1_hi = w1.astype(BF16)
    w1_lo = (w1 - w1_hi.astype(F32)).astype(BF16)
    first = _dot(x, w1_hi[:half]) + _dot(x, w1_lo[:half])
    second = _dot(x, w1_hi[half:]) + _dot(x, w1_lo[half:])
    n = x.shape[0]
    second = pltpu.roll(second, n - 1, axis=0)
    pe = jnp.broadcast_to(pe_ref[0], (SUBLANES, 2 * half))
    pe_term = _dot_f32(pe, w1)[0:1]
    h = first + second + pe_term
    out = _dot_f32(jax.nn.silu(h), w2_ref[0])
    row = lax.broadcasted_iota(jnp.int32, out.shape, 0)
    o_ref[0, 0] = jnp.where(row < n_valid, out, 0.0)


def _compress(chunks, pe, w1, w2):
    two, BH, n, width = chunks.shape
    d = HEAD_DIM
    body = functools.partial(_compress_body, n_valid=n - 1)
    vmem = 2 * (n * width * 2 + 2 * width * 4 + 2 * width * d * 4 + d * d * 4 + n * d * 4) \
        + 8 * width * d * 4
    return pl.pallas_call(
        body,
        grid=(two, BH),
        in_specs=[
            pl.BlockSpec((1, 1, n, width), lambda s, b: (s, b, 0, 0)),
            pl.BlockSpec((1, 1, 2 * width), lambda s, b: (s, 0, 0)),
            pl.BlockSpec((1, 2 * width, d), lambda s, b: (s, 0, 0)),
            pl.BlockSpec((1, d, d), lambda s, b: (s, 0, 0)),
        ],
        out_specs=pl.BlockSpec((1, 1, n, d), lambda s, b: (s, b, 0, 0)),
        out_shape=jax.ShapeDtypeStruct((two, BH, n, d), F32),
        compiler_params=_params(("parallel", "parallel"), vmem),
        name="compress",
    )(chunks, pe, w1, w2)


def _select_blocks(imp, q0):
    n_sel = imp.shape[0]
    sel_i = lax.broadcasted_iota(jnp.int32, imp.shape, 0)
    sel_f = sel_i.astype(F32)
    cur = (q0 + lax.broadcasted_iota(jnp.int32, imp.shape, 1)) // SEL_BLOCK
    forced = (sel_i == 0) | (sel_i == cur) | (sel_i == cur - 1)
    v = jnp.where(sel_i > cur, -jnp.inf, jnp.where(forced, jnp.inf, imp))
    chosen = jnp.zeros(imp.shape, F32)
    for _ in range(min(N_SELECT, n_sel)):
        top = jnp.max(v, axis=0, keepdims=True)
        first = jnp.min(jnp.where(v == top, sel_f, float(n_sel)), axis=0, keepdims=True)
        pick = sel_f == first
        chosen = jnp.where(pick, 1.0, chosen)
        v = jnp.where(pick, -jnp.inf, v)
    return jnp.where((chosen > 0.0) & (sel_i <= cur), 0.0, MASK_BIAS)


def _attn_body(qt_ref, zgt_ref, kc_ref, vct_ref, ovt_ref, ks_ref, vst_ref, e_ref, kw_ref, vwt_ref,
               y_ref, s_scr, m_scr, acc_scr, oc_scr, ow_scr, *, tk, chains, span):
    q0 = pl.program_id(2) * ATTN_Q
    GQ = GROUP * ATTN_Q
    qt = [qt_ref[0, g * HEAD_DIM:(g + 1) * HEAD_DIM, :] for g in range(GROUP)]
    q_all = jnp.concatenate(qt, axis=1)

    def lane_time(shape):
        return q0 + (lax.broadcasted_iota(jnp.int32, shape, 1) & (ATTN_Q - 1))

    s = _dot(kc_ref[0, 0].astype(BF16), q_all)
    cmp_end = lax.broadcasted_iota(jnp.int32, s.shape, 0) * CMP_STRIDE + (CMP_BLOCK - 1)
    s = jnp.where(cmp_end <= lane_time(s.shape), s, -jnp.inf)
    m = jnp.max(s, axis=0, keepdims=True)
    m = jnp.where(m == -jnp.inf, 0.0, m)
    e = jnp.exp2(s - m)
    p = e / jnp.maximum(jnp.sum(e, axis=0, keepdims=True), 1e-30)
    oc_scr[...] = _dot(vct_ref[0, 0].astype(BF16), p.astype(BF16))
    pg = p[:, 0:ATTN_Q]
    for g in range(1, GROUP):
        pg = pg + p[:, g * ATTN_Q:(g + 1) * ATTN_Q]
    ovt = ovt_ref[...]
    p0, p1, p2 = _split3(pg)
    bias = _select_blocks(_dot(ovt, p0) + _dot(ovt, p1) + _dot(ovt, p2), q0).astype(BF16)

    w_start = pl.multiple_of(jnp.maximum(q0 - WINDOW, 0), ATTN_Q)
    sw = _dot(kw_ref[0, pl.ds(w_start, span), :], q_all)
    pos = w_start + lax.broadcasted_iota(jnp.int32, sw.shape, 0)
    t = lane_time(sw.shape)
    sw = jnp.where((pos <= t) & (pos > t - WINDOW), sw, -jnp.inf)
    ew = jnp.exp2(sw - jnp.max(sw, axis=0, keepdims=True))
    ow = _dot(vwt_ref[0, 0, :, pl.ds(w_start, span)], ew.astype(BF16))
    ow_scr[...] = ow[0:HEAD_DIM] / ow[HEAD_DIM:HEAD_DIM + 1]

    cols = GQ // chains
    per = GROUP // chains
    qx = [jnp.concatenate([jnp.concatenate([qt[g], bias], axis=0) for g in range(c * per, (c + 1) * per)],
                          axis=1) for c in range(chains)]

    def scores(j, slot):
        start = pl.multiple_of(j * tk, tk)
        kx = jnp.concatenate([ks_ref[0, pl.ds(start, tk), :], e_ref[pl.ds(start, tk), :]], axis=1)
        for c in range(chains):
            s_scr[slot, :, c * cols:(c + 1) * cols] = _dot(kx, qx[c])

    def update(j, slot, causal):
        start = pl.multiple_of(j * tk, tk)
        vt = vst_ref[0, 0, :, pl.ds(start, tk)]
        for c in range(chains):
            sl = slice(c * cols, (c + 1) * cols)
            sc = s_scr[slot, :, sl]
            if causal:
                key = start + lax.broadcasted_iota(jnp.int32, sc.shape, 0)
                sc = jnp.where(key <= lane_time(sc.shape), sc, -jnp.inf)
            m_old = m_scr[:, sl]
            m_new = jnp.maximum(m_old, jnp.max(sc, axis=0, keepdims=True))
            alpha = jnp.exp2(m_old - m_new)
            pj = jnp.exp2(sc - m_new)
            m_scr[:, sl] = m_new
            acc_scr[:, sl] = alpha * acc_scr[:, sl] + _dot(vt, pj.astype(BF16))

    m_scr[...] = jnp.full(m_scr.shape, -jnp.inf, F32)
    acc_scr[...] = jnp.zeros(acc_scr.shape, F32)
    scores(0, 0)
    last = q0 // tk

    @pl.loop(0, last // 2)
    def _(i):
        scores(2 * i + 1, 1)
        update(2 * i, 0, False)
        scores(2 * i + 2, 0)
        update(2 * i + 1, 1, False)

    @pl.when(last % 2 == 1)
    def _():
        scores(last, 1)
        update(last - 1, 0, False)
        update(last, 1, True)

    @pl.when(last % 2 == 0)
    def _():
        update(last, 0, True)

    o_sel = acc_scr[0:HEAD_DIM, :] / acc_scr[HEAD_DIM:HEAD_DIM + 1, :]
    gates = jax.nn.sigmoid(zgt_ref[0, 0])
    for g in range(GROUP):
        sl = slice(g * ATTN_Q, (g + 1) * ATTN_Q)
        r = g * N_BRANCH
        y = (gates[r:r + 1] * oc_scr[:, sl] + gates[r + 1:r + 2] * o_sel[:, sl]
             + gates[r + 2:r + 3] * ow_scr[:, sl])
        y_ref[0, :, g * HEAD_DIM:(g + 1) * HEAD_DIM] = y.T


def _attention(qt, zgt, kvc, vct, ovt, kk, vst, onehot, vwt, B, seq):
    n_cmp = kvc.shape[2]
    n_sel = ovt.shape[0]
    GW = GROUP * HEAD_DIM
    GQ = GROUP * ATTN_Q
    VR = HEAD_DIM + ONES_ROWS
    GR = zgt.shape[2]
    tk = min(512, seq)
    span = WINDOW + ATTN_Q
    assert seq >= span
    body = functools.partial(_attn_body, tk=tk, chains=2, span=span)
    resident = seq * HEAD_DIM * 2 * 2 + seq * n_sel * 2 + 2 * VR * seq * 2
    vmem = 2 * (GW * ATTN_Q * 2 + GR * ATTN_Q * 4 + 2 * n_cmp * HEAD_DIM * 4 + n_sel * n_cmp * 2
                + resident + ATTN_Q * GW * 4) \
        + (2 * tk + VR + 2 * HEAD_DIM + SUBLANES) * GQ * 4 + 8 * (n_cmp + span) * GQ * 4
    return pl.pallas_call(
        body,
        grid=(B, N_KV_HEADS, seq // ATTN_Q),
        in_specs=[
            pl.BlockSpec((1, GW, ATTN_Q), lambda b, h, i: (b, h, i)),
            pl.BlockSpec((1, 1, GR, ATTN_Q), lambda b, h, i: (b, h, 0, i)),
            pl.BlockSpec((1, 1, n_cmp, HEAD_DIM), lambda b, h, i: (0, b * N_KV_HEADS + h, 0, 0)),
            pl.BlockSpec((1, 1, HEAD_DIM, n_cmp), lambda b, h, i: (b, h, 0, 0)),
            pl.BlockSpec((n_sel, n_cmp), lambda b, h, i: (0, 0)),
            pl.BlockSpec((1, seq, HEAD_DIM), lambda b, h, i: (b, 0, N_KV_HEADS + h)),
            pl.BlockSpec((1, 1, VR, seq), lambda b, h, i: (b, h, 0, 0)),
            pl.BlockSpec((seq, n_sel), lambda b, h, i: (0, 0)),
            pl.BlockSpec((1, seq, HEAD_DIM), lambda b, h, i: (b, 0, 2 * N_KV_HEADS + h)),
            pl.BlockSpec((1, 1, VR, seq), lambda b, h, i: (b, h, 0, 0)),
        ],
        out_specs=pl.BlockSpec((1, ATTN_Q, GW), lambda b, h, i: (b, i, h)),
        out_shape=jax.ShapeDtypeStruct((B, seq, ATTN_W), F32),
        scratch_shapes=[pltpu.VMEM((2, tk, GQ), F32),
                        pltpu.VMEM((1, GQ), F32),
                        pltpu.VMEM((VR, GQ), F32),
                        pltpu.VMEM((HEAD_DIM, GQ), F32),
                        pltpu.VMEM((HEAD_DIM, GQ), F32)],
        compiler_params=_params(("parallel", "parallel", "arbitrary"), vmem),
        name="attention",
    )(qt, zgt, kvc, vct, ovt, kk, vst, onehot, kk, vwt)


def _outproj_body(yc_ref, ya_ref, g_ref, w_ref, x_ref, h_ref, a_scr):
    @pl.when(pl.program_id(1) == 0)
    def _():
        a_scr[:, 0:CONV_W] = yc_ref[...]
        a_scr[:, CONV_W:] = _rms(ya_ref[...], g_ref[...]).astype(BF16)

    h_ref[...] = x_ref[...] + _dot(a_scr[...], w_ref[...])


def _out_projection(yc, ya, g, w, x2):
    T, D = x2.shape
    K = w.shape[0]
    tm = min(1024, T)
    tn = min(512, D)
    vmem = 2 * (tm * CONV_W * 2 + tm * ATTN_W * 4 + K * tn * 2 + 2 * tm * tn * 4) + tm * K * 2 \
        + 3 * tm * ATTN_W * 4
    return pl.pallas_call(
        _outproj_body,
        grid=(T // tm, D // tn),
        in_specs=[
            pl.BlockSpec((tm, CONV_W), lambda i, j: (i, 0)),
            pl.BlockSpec((tm, ATTN_W), lambda i, j: (i, 0)),
            pl.BlockSpec((1, ATTN_W), lambda i, j: (0, 0)),
            pl.BlockSpec((K, tn), lambda i, j: (0, j)),
            pl.BlockSpec((tm, tn), lambda i, j: (i, j)),
        ],
        out_specs=pl.BlockSpec((tm, tn), lambda i, j: (i, j)),
        out_shape=jax.ShapeDtypeStruct((T, D), F32),
        scratch_shapes=[pltpu.VMEM((tm, K), BF16)],
        compiler_params=_params(("parallel", "arbitrary"), vmem),
        name="out_projection",
    )(yc, ya, g, w, x2)


def _ffn_body(h_ref, g_ref, wg_ref, wu_ref, wd_ref, gf_ref, o_ref, a_scr, *, final_norm, tn):
    j = pl.program_id(1)

    @pl.when(j == 0)
    def _():
        a_scr[...] = _rms(h_ref[...], g_ref[...]).astype(BF16)

    a = a_scr[...]
    u = (jax.nn.silu(_dot(a, wg_ref[...])) * _dot(a, wu_ref[...])).astype(BF16)
    D = o_ref.shape[1]

    @pl.when(j == 0)
    def _():
        for n in range(D // tn):
            o_ref[:, n * tn:(n + 1) * tn] = _dot(u, wd_ref[:, n * tn:(n + 1) * tn])

    @pl.when(j > 0)
    def _():
        for n in range(D // tn):
            o_ref[:, n * tn:(n + 1) * tn] += _dot(u, wd_ref[:, n * tn:(n + 1) * tn])

    @pl.when(j == pl.num_programs(1) - 1)
    def _():
        out = h_ref[...] + o_ref[...]
        o_ref[...] = _rms(out, gf_ref[...]) if final_norm else out


def _ffn(h, g, wg, wu, wd, g_final, final_norm):
    T, D = h.shape
    F = wg.shape[1]
    tm = min(1024, T)
    tf = 512
    tn = 512
    assert F % tf == 0 and D % tn == 0
    body = functools.partial(_ffn_body, final_norm=final_norm, tn=tn)
    vmem = tm * D * 4 + 2 * (3 * D * tf * 2 + tm * D * 4) + tm * D * 2 + 6 * tm * tf * 4 + 2 * tm * tn * 4
    return pl.pallas_call(
        body,
        grid=(T // tm, F // tf),
        in_specs=[
            pl.BlockSpec((tm, D), lambda i, j: (i, 0), pipeline_mode=pl.Buffered(1)),
            pl.BlockSpec((1, D), lambda i, j: (0, 0)),
            pl.BlockSpec((D, tf), lambda i, j: (0, j)),
            pl.BlockSpec((D, tf), lambda i, j: (0, j)),
            pl.BlockSpec((tf, D), lambda i, j: (j, 0)),
            pl.BlockSpec((1, D), lambda i, j: (0, 0)),
        ],
        out_specs=pl.BlockSpec((tm, D), lambda i, j: (i, 0)),
        out_shape=jax.ShapeDtypeStruct((T, D), F32),
        scratch_shapes=[pltpu.VMEM((tm, D), BF16)],
        compiler_params=_params(("parallel", "arbitrary"), vmem),
        name="ffn",
    )(h, g, wg, wu, wd, g_final)


def _rope_tables(seq):
    half = HEAD_DIM // 2
    inv = 1.0 / (ROPE_THETA ** (jnp.arange(half, dtype=F32) / half))
    ang = jnp.arange(seq).astype(F32)[:, None] * inv[None, :]
    cos = jnp.cos(ang)
    sin = jnp.sin(ang)
    return jnp.concatenate([cos, cos], axis=1), jnp.concatenate([-sin, sin], axis=1)


def _overlap_t(n_sel, n_cmp):
    cs = np.arange(n_cmp)[None, :] * CMP_STRIDE
    ss = np.arange(n_sel)[:, None] * SEL_BLOCK
    return jnp.asarray(((cs < ss + SEL_BLOCK) & (cs + CMP_BLOCK > ss)).astype(np.float32), dtype=BF16)


def _block_onehot(seq, n_sel):
    return jnp.asarray((np.arange(seq)[:, None] // SEL_BLOCK == np.arange(n_sel)[None, :])
                       .astype(np.float32), dtype=BF16)


def kernel(x, norm_mix, w_in, conv_w, cmp_pe_k, cmp_w1_k, cmp_w2_k, cmp_pe_v, cmp_w1_v, cmp_w2_v,
           norm_conv_out, norm_attn_out, w_out, norm_ffn, w_gate, w_up, w_down, norm_final):
    B, S, D = x.shape
    T = B * S
    depth = norm_mix.shape[0]
    assert S % ATTN_Q == 0 and S % CMP_STRIDE == 0
    n_chunks = S // CMP_STRIDE
    n_sel = S // SEL_BLOCK
    cos_t, sin_t = _rope_tables(S)
    ovt = _overlap_t(n_sel, n_chunks)
    onehot = _block_onehot(S, n_sel)
    chunk_w = CMP_STRIDE * HEAD_DIM

    h = x.reshape(T, D)
    for l in range(depth):
        zc, q, kk, vv, zg = _in_projection(h, norm_mix[l][None], _proj_weight(w_in[l]), cos_t, sin_t, S)

        y_conv = _short_conv(zc, conv_w[l], norm_conv_out[l][None], S)

        kvc_in = jnp.stack([kk[:, :KV_W], vv[:, :KV_W]])
        kvc_in = kvc_in.reshape(2, B, S, N_KV_HEADS, HEAD_DIM).transpose(0, 1, 3, 2, 4)
        kvc_in = kvc_in.reshape(2, B * N_KV_HEADS, n_chunks, chunk_w)
        pe = jnp.stack([cmp_pe_k[l], cmp_pe_v[l]]).reshape(2, 1, CMP_BLOCK * HEAD_DIM)
        kvc = _compress(kvc_in, pe, jnp.stack([cmp_w1_k[l], cmp_w1_v[l]]),
                        jnp.stack([cmp_w2_k[l], cmp_w2_v[l]]))

        qt = q.reshape(B, S, ATTN_W).transpose(0, 2, 1)
        kk3 = kk.reshape(B, S, KV_OUT_W)

        def v_t(n):
            vt = vv[:, n * KV_W:(n + 1) * KV_W].reshape(B, S, N_KV_HEADS, HEAD_DIM).transpose(0, 2, 3, 1)
            return jnp.concatenate([vt, jnp.ones((B, N_KV_HEADS, ONES_ROWS, S), BF16)], axis=2)

        vct = kvc[1].reshape(B, N_KV_HEADS, n_chunks, HEAD_DIM).transpose(0, 1, 3, 2)
        zgt = zg[:, :N_GATES].reshape(B, S, N_KV_HEADS, GROUP * N_BRANCH).transpose(0, 2, 3, 1)
        zgt = jnp.pad(zgt, ((0, 0), (0, 0), (0, GATE_ROWS - GROUP * N_BRANCH), (0, 0)))
        y_attn = _attention(qt, zgt, kvc, vct, ovt, kk3, v_t(1), onehot, v_t(2), B, S)

        h = _out_projection(y_conv, y_attn.reshape(T, ATTN_W), norm_attn_out[l][None],
                            w_out[l].astype(BF16), h)

        h = _ffn(h, norm_ffn[l][None], w_gate[l].astype(BF16), w_up[l].astype(BF16),
                 w_down[l].astype(BF16), norm_final[None], l == depth - 1)
    return h.reshape(B, S, D)
```

```python
import functools

import jax
import jax.numpy as jnp
import numpy as np
from jax import lax
from jax.experimental import pallas as pl
from jax.experimental.pallas import tpu as pltpu

F32 = jnp.float32
BF16 = jnp.bfloat16

HEAD_DIM = 128
N_HEADS = 8
N_KV_HEADS = 2
GROUP = N_HEADS // N_KV_HEADS
N_BRANCH = 3
CONV_TAPS = 3
CMP_BLOCK = 32
CMP_STRIDE = 16
SEL_BLOCK = 64
N_SELECT = 16
WINDOW = 512
ATTN_Q = 256
ROPE_THETA = 10000.0
RMS_EPS = 1e-6

CONV_W = 1024
ATTN_W = N_HEADS * HEAD_DIM
KV_W = N_KV_HEADS * HEAD_DIM
N_GATES = N_HEADS * N_BRANCH

V7X_VMEM_BYTES = 64 * 1024 * 1024
V7X_VMEM_BUDGET = V7X_VMEM_BYTES - 6 * 1024 * 1024
LANES = 128
SUBLANES = 8
PROJ_TILE_N = 512

MASK_BIAS = -float(2 ** 30)
LOG2_E = 1.4426950408889634
ONES_ROWS = 16
GATE_ROWS = 16


def _params(semantics, vmem_bytes):
    return pltpu.CompilerParams(dimension_semantics=semantics,
                                vmem_limit_bytes=int(min(vmem_bytes, V7X_VMEM_BUDGET)))


def _rms(x, g):
    return x * lax.rsqrt(jnp.mean(x * x, axis=-1, keepdims=True) + RMS_EPS) * g


def _split3(x):
    hi = x.astype(BF16)
    r1 = x - hi.astype(F32)
    mid = r1.astype(BF16)
    lo = (r1 - mid.astype(F32)).astype(BF16)
    return hi, mid, lo


def _dot(a, b):
    return jnp.dot(a, b, preferred_element_type=F32)


def _dot_f32(a, b):
    a0, a1, a2 = _split3(a)
    b0, b1, b2 = _split3(b)
    return (_dot(a0, b0) + (_dot(a0, b1) + _dot(a1, b0))
            + (_dot(a0, b2) + _dot(a1, b1) + _dot(a2, b0)))


CONV_GROUPS = CONV_W // PROJ_TILE_N
N_CONV_TILES = 3 * CONV_GROUPS
N_Q_TILES = ATTN_W // PROJ_TILE_N
N_K_TILES = 2
N_V_TILES = 2
Q_TILE0 = N_CONV_TILES
K_TILE0 = Q_TILE0 + N_Q_TILES
V_TILE0 = K_TILE0 + N_K_TILES
N_PROJ_TILES = V_TILE0 + N_V_TILES
KV_OUT_W = N_K_TILES * PROJ_TILE_N


def _proj_weight(w):
    o = 3 * CONV_W + ATTN_W
    tn = PROJ_TILE_N

    def col(n):
        return w[:, o + n * KV_W:o + (n + 1) * KV_W]

    def zeros(n):
        return jnp.zeros((w.shape[0], n), w.dtype)

    conv = [w[:, part * CONV_W + c * tn:part * CONV_W + (c + 1) * tn]
            for c in range(CONV_GROUPS) for part in range(3)]
    gates = w[:, o + 6 * KV_W:]
    per_kv = GROUP * N_BRANCH
    gate_cols = []
    for hk in range(N_KV_HEADS):
        gate_cols += [gates[:, hk * per_kv:(hk + 1) * per_kv], zeros(GATE_ROWS - per_kv)]
    parts = conv + [w[:, 3 * CONV_W:o], col(0), col(2), col(4), zeros(KV_W),
                    col(1), col(3), col(5)] + gate_cols + [zeros(KV_W - N_KV_HEADS * GATE_ROWS)]
    return jnp.concatenate(parts, axis=1).astype(BF16)


def _inproj_body(x_ref, g_ref, w_ref, cos_ref, sin_ref, cw_ref,
                 yc_ref, qt_ref, k_ref, v_ref, vt_ref, zgt_ref,
                 a_scr, cb_scr, cc_scr, tail_scr, *, q_scale, blocks_per_seq):
    i = pl.program_id(0)
    j = pl.program_id(1)

    @pl.when(j == 0)
    def _():
        a_scr[...] = _rms(x_ref[...], g_ref[...]).astype(BF16)

    @pl.when((i == 0) & (j == 0))
    def _():
        tail_scr[...] = jnp.zeros(tail_scr.shape, F32)

    z = _dot(a_scr[...], w_ref[...])

    def rope(z4):
        cos = cos_ref[...]
        sin = sin_ref[...]
        heads = []
        for h in range(PROJ_TILE_N // HEAD_DIM):
            zh = z4[:, h * HEAD_DIM:(h + 1) * HEAD_DIM]
            heads.append(zh * cos + pltpu.roll(zh, HEAD_DIM // 2, axis=1) * sin)
        return jnp.concatenate(heads, axis=1)

    is_conv = j < Q_TILE0

    @pl.when(is_conv & (j % 3 == 0))
    def _():
        cb_scr[...] = z

    @pl.when(is_conv & (j % 3 == 1))
    def _():
        cc_scr[...] = z

    @pl.when(is_conv & (j % 3 == 2))
    def _():
        group = j // 3
        u = cc_scr[...] * z
        tail = tail_scr[group]
        tail = jnp.where(i % blocks_per_seq == 0, 0.0, tail)
        row = lax.broadcasted_iota(jnp.int32, u.shape, 0)
        t1 = tail[SUBLANES - 1:SUBLANES]
        t2 = tail[SUBLANES - 2:SUBLANES - 1]
        u1 = jnp.where(row == 0, t1, pltpu.roll(u, 1, axis=0))
        u2 = jnp.where(row == 0, t2, jnp.where(row == 1, t1, pltpu.roll(u, 2, axis=0)))
        w = cw_ref[...]
        yc_ref[...] = cb_scr[...] * (u2 * w[0:1] + u1 * w[1:2] + u * w[2:3])
        tail_scr[group] = u[u.shape[0] - SUBLANES:, :]

    @pl.when((j >= Q_TILE0) & (j < K_TILE0))
    def _():
        qt_ref[0] = (rope(z) * q_scale).T.astype(BF16)

    @pl.when((j >= K_TILE0) & (j < V_TILE0))
    def _():
        k_ref[...] = rope(z).astype(BF16)

    @pl.when(j == V_TILE0)
    def _():
        v_ref[...] = z.astype(BF16)
        vt_ref[0] = z.T.astype(BF16)

    @pl.when(j == V_TILE0 + 1)
    def _():
        zt = z.T
        vt_ref[0] = zt.astype(BF16)
        zgt_ref[0] = zt[KV_W:KV_W + N_KV_HEADS * GATE_ROWS]


def _in_projection(x2, g, w_perm, cos_t, sin_t, conv_w, B, seq):
    T, D = x2.shape
    tn = PROJ_TILE_N
    assert w_perm.shape == (D, N_PROJ_TILES * tn)
    tm = min(1024, seq)
    nb = seq // tm
    GR = N_KV_HEADS * GATE_ROWS
    body = functools.partial(_inproj_body, q_scale=HEAD_DIM ** -0.5 * LOG2_E, blocks_per_seq=nb)
    vmem = 2 * (tm * D * 4 + D * tn * 2 + 2 * tm * HEAD_DIM * 4 + tm * tn * 4 + 4 * tm * tn * 2
                + GR * tm * 4) + tm * D * 2 + 2 * tm * tn * 4 + 6 * tm * tn * 4

    def conv_group(j):
        return jnp.minimum(j // 3, CONV_GROUPS - 1)

    return pl.pallas_call(
        body,
        grid=(T // tm, N_PROJ_TILES),
        in_specs=[
            pl.BlockSpec((tm, D), lambda i, j: (i, 0)),
            pl.BlockSpec((1, D), lambda i, j: (0, 0)),
            pl.BlockSpec((D, tn), lambda i, j: (0, j)),
            pl.BlockSpec((tm, HEAD_DIM), lambda i, j: (i % nb, 0)),
            pl.BlockSpec((tm, HEAD_DIM), lambda i, j: (i % nb, 0)),
            pl.BlockSpec((CONV_TAPS, tn), lambda i, j: (0, conv_group(j))),
        ],
        out_specs=[
            pl.BlockSpec((tm, tn), lambda i, j: (i, conv_group(j))),
            pl.BlockSpec((1, tn, tm), lambda i, j: (i // nb, jnp.clip(j - Q_TILE0, 0, N_Q_TILES - 1), i % nb)),
            pl.BlockSpec((tm, tn), lambda i, j: (i, jnp.clip(j - K_TILE0, 0, N_K_TILES - 1))),
            pl.BlockSpec((tm, tn), lambda i, j: (i, 0)),
            pl.BlockSpec((1, tn, tm), lambda i, j: (i // nb, jnp.clip(j - V_TILE0, 0, N_V_TILES - 1), i % nb)),
            pl.BlockSpec((1, GR, tm), lambda i, j: (i // nb, 0, i % nb)),
        ],
        out_shape=[
            jax.ShapeDtypeStruct((T, CONV_W), F32),
            jax.ShapeDtypeStruct((B, ATTN_W, seq), BF16),
            jax.ShapeDtypeStruct((T, KV_OUT_W), BF16),
            jax.ShapeDtypeStruct((T, tn), BF16),
            jax.ShapeDtypeStruct((B, N_V_TILES * tn, seq), BF16),
            jax.ShapeDtypeStruct((B, GR, seq), F32),
        ],
        scratch_shapes=[pltpu.VMEM((tm, D), BF16), pltpu.VMEM((tm, tn), F32), pltpu.VMEM((tm, tn), F32),
                        pltpu.VMEM((CONV_GROUPS, SUBLANES, tn), F32)],
        compiler_params=_params(("arbitrary", "arbitrary"), vmem),
        name="in_projection",
    )(x2, g, w_perm, cos_t, sin_t, conv_w)


def _compress_body(x_ref, pe_ref, w1_ref, w2_ref, o_ref, ot_ref, *, n_valid):
    x = x_ref[0, 0]
    half = x.shape[1]
    w1 = w1_ref[0]
    w1_hi = w1.astype(BF16)
    w1_lo = (w1 - w1_hi.astype(F32)).astype(BF16)
    first = _dot(x, w1_hi[:half]) + _dot(x, w1_lo[:half])
    second = _dot(x, w1_hi[half:]) + _dot(x, w1_lo[half:])
    n = x.shape[0]
    second = pltpu.roll(second, n - 1, axis=0)
    pe = jnp.broadcast_to(pe_ref[0], (SUBLANES, 2 * half))
    pe_term = _dot_f32(pe, w1)[0:1]
    h = first + second + pe_term
    out = _dot_f32(jax.nn.silu(h), w2_ref[0])
    row = lax.broadcasted_iota(jnp.int32, out.shape, 0)
    out = jnp.where(row < n_valid, out, 0.0)
    o_ref[0, 0] = out
    ot_ref[0, 0] = out.T


def _compress(chunks, pe, w1, w2):
    two, BH, n, width = chunks.shape
    d = HEAD_DIM
    body = functools.partial(_compress_body, n_valid=n - 1)
    vmem = 2 * (n * width * 2 + 2 * width * 4 + 2 * width * d * 4 + d * d * 4 + n * d * 4) \
        + 8 * width * d * 4
    return pl.pallas_call(
        body,
        grid=(two, BH),
        in_specs=[
            pl.BlockSpec((1, 1, n, width), lambda s, b: (s, b, 0, 0)),
            pl.BlockSpec((1, 1, 2 * width), lambda s, b: (s, 0, 0)),
            pl.BlockSpec((1, 2 * width, d), lambda s, b: (s, 0, 0)),
            pl.BlockSpec((1, d, d), lambda s, b: (s, 0, 0)),
        ],
        out_specs=[pl.BlockSpec((1, 1, n, d), lambda s, b: (s, b, 0, 0)),
                   pl.BlockSpec((1, 1, d, n), lambda s, b: (s, b, 0, 0))],
        out_shape=[jax.ShapeDtypeStruct((two, BH, n, d), F32),
                   jax.ShapeDtypeStruct((two, BH, d, n), F32)],
        compiler_params=_params(("parallel", "parallel"), vmem),
        name="compress",
    )(chunks, pe, w1, w2)


def _select_blocks(imp, q0):
    n_sel = imp.shape[0]
    sel_i = lax.broadcasted_iota(jnp.int32, imp.shape, 0)
    sel_f = sel_i.astype(F32)
    cur = (q0 + lax.broadcasted_iota(jnp.int32, imp.shape, 1)) // SEL_BLOCK
    forced = (sel_i == 0) | (sel_i == cur) | (sel_i == cur - 1)
    v = jnp.where(sel_i > cur, -jnp.inf, jnp.where(forced, jnp.inf, imp))
    chosen = jnp.zeros(imp.shape, F32)
    for _ in range(min(N_SELECT, n_sel)):
        top = jnp.max(v, axis=0, keepdims=True)
        first = jnp.min(jnp.where(v == top, sel_f, float(n_sel)), axis=0, keepdims=True)
        pick = sel_f == first
        chosen = jnp.where(pick, 1.0, chosen)
        v = jnp.where(pick, -jnp.inf, v)
    return jnp.where((chosen > 0.0) & (sel_i <= cur), 0.0, MASK_BIAS)


def _attn_body(qt_ref, zgt_ref, kc_ref, vct_ref, ovt_ref, ks_ref, vst_ref, e_ref, kw_ref, vwt_ref,
               y_ref, s_scr, m_scr, acc_scr, oc_scr, ow_scr, *, tk, chains, span):
    q0 = pl.program_id(2) * ATTN_Q
    GQ = GROUP * ATTN_Q
    qt = [qt_ref[0, g * HEAD_DIM:(g + 1) * HEAD_DIM, :] for g in range(GROUP)]
    q_all = jnp.concatenate(qt, axis=1)

    def lane_time(shape):
        return q0 + (lax.broadcasted_iota(jnp.int32, shape, 1) & (ATTN_Q - 1))

    s = _dot(kc_ref[0, 0].astype(BF16), q_all)
    cmp_end = lax.broadcasted_iota(jnp.int32, s.shape, 0) * CMP_STRIDE + (CMP_BLOCK - 1)
    s = jnp.where(cmp_end <= lane_time(s.shape), s, -jnp.inf)
    m = jnp.max(s, axis=0, keepdims=True)
    m = jnp.where(m == -jnp.inf, 0.0, m)
    e = jnp.exp2(s - m)
    p = e / jnp.maximum(jnp.sum(e, axis=0, keepdims=True), 1e-30)
    oc_scr[...] = _dot(vct_ref[0, 0].astype(BF16), p.astype(BF16))
    pg = p[:, 0:ATTN_Q]
    for g in range(1, GROUP):
        pg = pg + p[:, g * ATTN_Q:(g + 1) * ATTN_Q]
    ovt = ovt_ref[...]
    p0, p1, p2 = _split3(pg)
    bias = _select_blocks(_dot(ovt, p0) + _dot(ovt, p1) + _dot(ovt, p2), q0).astype(BF16)

    w_start = pl.multiple_of(jnp.maximum(q0 - WINDOW, 0), ATTN_Q)
    sw = _dot(kw_ref[0, pl.ds(w_start, span), :], q_all)
    pos = w_start + lax.broadcasted_iota(jnp.int32, sw.shape, 0)
    t = lane_time(sw.shape)
    sw = jnp.where((pos <= t) & (pos > t - WINDOW), sw, -jnp.inf)
    ew = jnp.exp2(sw - jnp.max(sw, axis=0, keepdims=True))
    vw = jnp.concatenate([vwt_ref[0, :, pl.ds(w_start, span)], jnp.ones((ONES_ROWS, span), BF16)], axis=0)
    ow = _dot(vw, ew.astype(BF16))
    ow_scr[...] = ow[0:HEAD_DIM] / ow[HEAD_DIM:HEAD_DIM + 1]

    cols = GQ // chains
    per = GROUP // chains
    qx = [jnp.concatenate([jnp.concatenate([qt[g], bias], axis=0) for g in range(c * per, (c + 1) * per)],
                          axis=1) for c in range(chains)]

    def scores(j, slot):
        start = pl.multiple_of(j * tk, tk)
        kx = jnp.concatenate([ks_ref[0, pl.ds(start, tk), :], e_ref[pl.ds(start, tk), :]], axis=1)
        for c in range(chains):
            s_scr[slot, :, c * cols:(c + 1) * cols] = _dot(kx, qx[c])

    def update(j, slot, causal):
        start = pl.multiple_of(j * tk, tk)
        vt = jnp.concatenate([vst_ref[0, :, pl.ds(start, tk)], jnp.ones((ONES_ROWS, tk), BF16)], axis=0)
        for c in range(chains):
            sl = slice(c * cols, (c + 1) * cols)
            sc = s_scr[slot, :, sl]
            if causal:
                key = start + lax.broadcasted_iota(jnp.int32, sc.shape, 0)
                sc = jnp.where(key <= lane_time(sc.shape), sc, -jnp.inf)
            m_old = m_scr[:, sl]
            m_new = jnp.maximum(m_old, jnp.max(sc, axis=0, keepdims=True))
            alpha = jnp.exp2(m_old - m_new)
            pj = jnp.exp2(sc - m_new)
            m_scr[:, sl] = m_new
            acc_scr[:, sl] = alpha * acc_scr[:, sl] + _dot(vt, pj.astype(BF16))

    m_scr[...] = jnp.full(m_scr.shape, -jnp.inf, F32)
    acc_scr[...] = jnp.zeros(acc_scr.shape, F32)
    scores(0, 0)
    last = q0 // tk

    @pl.loop(0, last // 2)
    def _(i):
        scores(2 * i + 1, 1)
        update(2 * i, 0, False)
        scores(2 * i + 2, 0)
        update(2 * i + 1, 1, False)

    @pl.when(last % 2 == 1)
    def _():
        scores(last, 1)
        update(last - 1, 0, False)
        update(last, 1, True)

    @pl.when(last % 2 == 0)
    def _():
        update(last, 0, True)

    o_sel = acc_scr[0:HEAD_DIM, :] / acc_scr[HEAD_DIM:HEAD_DIM + 1, :]
    gates = jax.nn.sigmoid(zgt_ref[0])
    for g in range(GROUP):
        sl = slice(g * ATTN_Q, (g + 1) * ATTN_Q)
        r = g * N_BRANCH
        y = (gates[r:r + 1] * oc_scr[:, sl] + gates[r + 1:r + 2] * o_sel[:, sl]
             + gates[r + 2:r + 3] * ow_scr[:, sl])
        y_ref[0, :, g * HEAD_DIM:(g + 1) * HEAD_DIM] = y.T


def _attention(qt, zgt, kvc, kvct, ovt, kk, vt, onehot, B, seq):
    n_cmp = kvc.shape[2]
    n_sel = ovt.shape[0]
    GW = GROUP * HEAD_DIM
    GQ = GROUP * ATTN_Q
    VR = HEAD_DIM + ONES_ROWS
    GR = GATE_ROWS
    tk = min(512, seq)
    span = WINDOW + ATTN_Q
    assert seq >= span
    body = functools.partial(_attn_body, tk=tk, chains=2, span=span)
    resident = 4 * seq * HEAD_DIM * 2 + seq * n_sel * 2
    vmem = 2 * (GW * ATTN_Q * 2 + GR * ATTN_Q * 4 + 2 * n_cmp * HEAD_DIM * 4 + n_sel * n_cmp * 2
                + resident + ATTN_Q * GW * 4) \
        + (2 * tk + VR + 2 * HEAD_DIM + SUBLANES) * GQ * 4 + 8 * (n_cmp + span) * GQ * 4
    return pl.pallas_call(
        body,
        grid=(B, N_KV_HEADS, seq // ATTN_Q),
        in_specs=[
            pl.BlockSpec((1, GW, ATTN_Q), lambda b, h, i: (b, h, i)),
            pl.BlockSpec((1, GR, ATTN_Q), lambda b, h, i: (b, h, i)),
            pl.BlockSpec((1, 1, n_cmp, HEAD_DIM), lambda b, h, i: (0, b * N_KV_HEADS + h, 0, 0)),
            pl.BlockSpec((1, 1, HEAD_DIM, n_cmp), lambda b, h, i: (1, b * N_KV_HEADS + h, 0, 0)),
            pl.BlockSpec((n_sel, n_cmp), lambda b, h, i: (0, 0)),
            pl.BlockSpec((1, seq, HEAD_DIM), lambda b, h, i: (b, 0, N_KV_HEADS + h)),
            pl.BlockSpec((1, HEAD_DIM, seq), lambda b, h, i: (b, N_KV_HEADS + h, 0)),
            pl.BlockSpec((seq, n_sel), lambda b, h, i: (0, 0)),
            pl.BlockSpec((1, seq, HEAD_DIM), lambda b, h, i: (b, 0, 2 * N_KV_HEADS + h)),
            pl.BlockSpec((1, HEAD_DIM, seq), lambda b, h, i: (b, 2 * N_KV_HEADS + h, 0)),
        ],
        out_specs=pl.BlockSpec((1, ATTN_Q, GW), lambda b, h, i: (b, i, h)),
        out_shape=jax.ShapeDtypeStruct((B, seq, ATTN_W), F32),
        scratch_shapes=[pltpu.VMEM((2, tk, GQ), F32),
                        pltpu.VMEM((1, GQ), F32),
                        pltpu.VMEM((VR, GQ), F32),
                        pltpu.VMEM((HEAD_DIM, GQ), F32),
                        pltpu.VMEM((HEAD_DIM, GQ), F32)],
        compiler_params=_params(("parallel", "parallel", "arbitrary"), vmem),
        name="attention",
    )(qt, zgt, kvc, kvct, ovt, kk, vt, onehot, kk, vt)


def _outproj_body(yc_ref, ya_ref, gc_ref, ga_ref, w_ref, x_ref, h_ref, a_scr):
    @pl.when(pl.program_id(1) == 0)
    def _():
        a_scr[:, 0:CONV_W] = _rms(yc_ref[...], gc_ref[...]).astype(BF16)
        a_scr[:, CONV_W:] = _rms(ya_ref[...], ga_ref[...]).astype(BF16)

    h_ref[...] = x_ref[...] + _dot(a_scr[...], w_ref[...])


def _out_projection(yc, ya, g_conv, g_attn, w, x2):
    T, D = x2.shape
    K = w.shape[0]
    tm = min(1024, T)
    tn = min(512, D)
    vmem = 2 * (tm * CONV_W * 4 + tm * ATTN_W * 4 + K * tn * 2 + 2 * tm * tn * 4) + tm * K * 2 \
        + 3 * tm * ATTN_W * 4
    return pl.pallas_call(
        _outproj_body,
        grid=(T // tm, D // tn),
        in_specs=[
            pl.BlockSpec((tm, CONV_W), lambda i, j: (i, 0)),
            pl.BlockSpec((tm, ATTN_W), lambda i, j: (i, 0)),
            pl.BlockSpec((1, CONV_W), lambda i, j: (0, 0)),
            pl.BlockSpec((1, ATTN_W), lambda i, j: (0, 0)),
            pl.BlockSpec((K, tn), lambda i, j: (0, j)),
            pl.BlockSpec((tm, tn), lambda i, j: (i, j)),
        ],
        out_specs=pl.BlockSpec((tm, tn), lambda i, j: (i, j)),
        out_shape=jax.ShapeDtypeStruct((T, D), F32),
        scratch_shapes=[pltpu.VMEM((tm, K), BF16)],
        compiler_params=_params(("parallel", "arbitrary"), vmem),
        name="out_projection",
    )(yc, ya, g_conv, g_attn, w, x2)


def _ffn_body(h_ref, g_ref, wg_ref, wu_ref, wd_ref, gf_ref, o_ref, a_scr, *, final_norm, tn):
    j = pl.program_id(1)

    @pl.when(j == 0)
    def _():
        a_scr[...] = _rms(h_ref[...], g_ref[...]).astype(BF16)

    a = a_scr[...]
    u = (jax.nn.silu(_dot(a, wg_ref[...])) * _dot(a, wu_ref[...])).astype(BF16)
    D = o_ref.shape[1]

    @pl.when(j == 0)
    def _():
        for n in range(D // tn):
            o_ref[:, n * tn:(n + 1) * tn] = _dot(u, wd_ref[:, n * tn:(n + 1) * tn])

    @pl.when(j > 0)
    def _():
        for n in range(D // tn):
            o_ref[:, n * tn:(n + 1) * tn] += _dot(u, wd_ref[:, n * tn:(n + 1) * tn])

    @pl.when(j == pl.num_programs(1) - 1)
    def _():
        out = h_ref[...] + o_ref[...]
        o_ref[...] = _rms(out, gf_ref[...]) if final_norm else out


def _ffn(h, g, wg, wu, wd, g_final, final_norm):
    T, D = h.shape
    F = wg.shape[1]
    tm = min(1024, T)
    tf = 512
    tn = 512
    assert F % tf == 0 and D % tn == 0
    body = functools.partial(_ffn_body, final_norm=final_norm, tn=tn)
    vmem = tm * D * 4 + 2 * (3 * D * tf * 2 + tm * D * 4) + tm * D * 2 + 6 * tm * tf * 4 + 2 * tm * tn * 4
    return pl.pallas_call(
        body,
        grid=(T // tm, F // tf),
        in_specs=[
            pl.BlockSpec((tm, D), lambda i, j: (i, 0), pipeline_mode=pl.Buffered(1)),
            pl.BlockSpec((1, D), lambda i, j: (0, 0)),
            pl.BlockSpec((D, tf), lambda i, j: (0, j)),
            pl.BlockSpec((D, tf), lambda i, j: (0, j)),
            pl.BlockSpec((tf, D), lambda i, j: (j, 0)),
            pl.BlockSpec((1, D), lambda i, j: (0, 0)),
        ],
        out_specs=pl.BlockSpec((tm, D), lambda i, j: (i, 0)),
        out_shape=jax.ShapeDtypeStruct((T, D), F32),
        scratch_shapes=[pltpu.VMEM((tm, D), BF16)],
        compiler_params=_params(("parallel", "arbitrary"), vmem),
        name="ffn",
    )(h, g, wg, wu, wd, g_final)


def _rope_tables(seq):
    half = HEAD_DIM // 2
    inv = 1.0 / (ROPE_THETA ** (jnp.arange(half, dtype=F32) / half))
    ang = jnp.arange(seq).astype(F32)[:, None] * inv[None, :]
    cos = jnp.cos(ang)
    sin = jnp.sin(ang)
    return jnp.concatenate([cos, cos], axis=1), jnp.concatenate([-sin, sin], axis=1)


def _overlap_t(n_sel, n_cmp):
    cs = np.arange(n_cmp)[None, :] * CMP_STRIDE
    ss = np.arange(n_sel)[:, None] * SEL_BLOCK
    return jnp.asarray(((cs < ss + SEL_BLOCK) & (cs + CMP_BLOCK > ss)).astype(np.float32), dtype=BF16)


def _block_onehot(seq, n_sel):
    return jnp.asarray((np.arange(seq)[:, None] // SEL_BLOCK == np.arange(n_sel)[None, :])
                       .astype(np.float32), dtype=BF16)


def kernel(x, norm_mix, w_in, conv_w, cmp_pe_k, cmp_w1_k, cmp_w2_k, cmp_pe_v, cmp_w1_v, cmp_w2_v,
           norm_conv_out, norm_attn_out, w_out, norm_ffn, w_gate, w_up, w_down, norm_final):
    B, S, D = x.shape
    T = B * S
    depth = norm_mix.shape[0]
    assert S % ATTN_Q == 0 and S % CMP_STRIDE == 0
    n_chunks = S // CMP_STRIDE
    n_sel = S // SEL_BLOCK
    cos_t, sin_t = _rope_tables(S)
    ovt = _overlap_t(n_sel, n_chunks)
    onehot = _block_onehot(S, n_sel)
    chunk_w = CMP_STRIDE * HEAD_DIM

    h = x.reshape(T, D)
    for l in range(depth):
        y_conv, qt, kk, vv, vt, zgt = _in_projection(h, norm_mix[l][None], _proj_weight(w_in[l]),
                                                      cos_t, sin_t, conv_w[l], B, S)

        kvc_in = jnp.stack([kk[:, :KV_W], vv[:, :KV_W]])
        kvc_in = kvc_in.reshape(2, B, S, N_KV_HEADS, HEAD_DIM).transpose(0, 1, 3, 2, 4)
        kvc_in = kvc_in.reshape(2, B * N_KV_HEADS, n_chunks, chunk_w)
        pe = jnp.stack([cmp_pe_k[l], cmp_pe_v[l]]).reshape(2, 1, CMP_BLOCK * HEAD_DIM)
        kvc, kvct = _compress(kvc_in, pe, jnp.stack([cmp_w1_k[l], cmp_w1_v[l]]),
                              jnp.stack([cmp_w2_k[l], cmp_w2_v[l]]))

        y_attn = _attention(qt, zgt, kvc, kvct, ovt, kk.reshape(B, S, KV_OUT_W), vt, onehot, B, S)

        h = _out_projection(y_conv, y_attn.reshape(T, ATTN_W), norm_conv_out[l][None],
                            norm_attn_out[l][None], w_out[l].astype(BF16), h)

        h = _ffn(h, norm_ffn[l][None], w_gate[l].astype(BF16), w_up[l].astype(BF16),
                 w_down[l].astype(BF16), norm_final[None], l == depth - 1)
    return h.reshape(B, S, D)
```

```python
import functools

import jax
import jax.numpy as jnp
import numpy as np
from jax import lax
from jax.experimental import pallas as pl
from jax.experimental.pallas import tpu as pltpu

F32 = jnp.float32
BF16 = jnp.bfloat16

HEAD_DIM = 128
N_HEADS = 8
N_KV_HEADS = 2
GROUP = N_HEADS // N_KV_HEADS
N_BRANCH = 3
CONV_TAPS = 3
CMP_BLOCK = 32
CMP_STRIDE = 16
SEL_BLOCK = 64
N_SELECT = 16
N_FORCED = 3
WINDOW = 512
ATTN_Q = 256
ROPE_THETA = 10000.0
RMS_EPS = 1e-6

CONV_W = 1024
ATTN_W = N_HEADS * HEAD_DIM
KV_W = N_KV_HEADS * HEAD_DIM
N_GATES = N_HEADS * N_BRANCH

V7X_VMEM_BYTES = 64 * 1024 * 1024
V7X_VMEM_BUDGET = V7X_VMEM_BYTES - 6 * 1024 * 1024
LANES = 128
SUBLANES = 8
PROJ_TILE_N = 512

MASK_BIAS = -float(2 ** 30)
LOG2_E = 1.4426950408889634
ONES_ROWS = 16


def _params(semantics, vmem_bytes):
    return pltpu.CompilerParams(dimension_semantics=semantics,
                                vmem_limit_bytes=int(min(vmem_bytes, V7X_VMEM_BUDGET)))


def _rms(x, g):
    return x * lax.rsqrt(jnp.mean(x * x, axis=-1, keepdims=True) + RMS_EPS) * g


def _split3(x):
    hi = x.astype(BF16)
    r1 = x - hi.astype(F32)
    mid = r1.astype(BF16)
    lo = (r1 - mid.astype(F32)).astype(BF16)
    return hi, mid, lo


def _dot(a, b):
    return jnp.dot(a, b, preferred_element_type=F32)


def _dot_f32(a, b):
    a0, a1, a2 = _split3(a)
    b0, b1, b2 = _split3(b)
    return (_dot(a0, b0) + (_dot(a0, b1) + _dot(a1, b0))
            + (_dot(a0, b2) + _dot(a1, b1) + _dot(a2, b0)))


CONV_GROUPS = CONV_W // PROJ_TILE_N
N_CONV_TILES = 3 * CONV_GROUPS
N_Q_TILES = ATTN_W // PROJ_TILE_N
N_KV_TILES = 3
Q_TILE0 = N_CONV_TILES
KV_TILE0 = Q_TILE0 + N_Q_TILES
GATE_TILE = KV_TILE0 + N_KV_TILES
N_PROJ_TILES = GATE_TILE + 1
KV_OUT_W = N_KV_TILES * KV_W
assert 2 * KV_W == PROJ_TILE_N and N_GATES % SUBLANES == 0


def _proj_weight(w):
    return jnp.pad(w.astype(BF16), ((0, 0), (0, N_PROJ_TILES * PROJ_TILE_N - w.shape[1])))


def _inproj_body(x_ref, g_ref, w_ref, cos_ref, sin_ref, cw_ref,
                 yc_ref, qt_ref, k_ref, kc_ref, vc_ref, vt_ref, zgt_ref,
                 a_scr, cb_scr, cc_scr, tail_scr, *, q_scale, blocks_per_seq):
    i = pl.program_id(0)
    j = pl.program_id(1)

    @pl.when(j == 0)
    def _():
        a_scr[...] = _rms(x_ref[...], g_ref[...]).astype(BF16)

    @pl.when((i == 0) & (j == 0))
    def _():
        tail_scr[...] = jnp.zeros(tail_scr.shape, F32)

    z = _dot(a_scr[...], w_ref[...])

    def rope(zh):
        cos = cos_ref[...]
        sin = sin_ref[...]
        heads = []
        for h in range(zh.shape[1] // HEAD_DIM):
            z1 = zh[:, h * HEAD_DIM:(h + 1) * HEAD_DIM]
            heads.append(z1 * cos + pltpu.roll(z1, HEAD_DIM // 2, axis=1) * sin)
        return jnp.concatenate(heads, axis=1)

    is_conv = j < Q_TILE0

    @pl.when(is_conv & (j % 3 == 0))
    def _():
        cb_scr[...] = z

    @pl.when(is_conv & (j % 3 == 1))
    def _():
        cc_scr[...] = z

    @pl.when(is_conv & (j % 3 == 2))
    def _():
        group = j // 3
        u = cc_scr[...] * z
        tail = tail_scr[group]
        tail = jnp.where(i % blocks_per_seq == 0, 0.0, tail)
        row = lax.broadcasted_iota(jnp.int32, u.shape, 0)
        t1 = tail[SUBLANES - 1:SUBLANES]
        t2 = tail[SUBLANES - 2:SUBLANES - 1]
        u1 = jnp.where(row == 0, t1, pltpu.roll(u, 1, axis=0))
        u2 = jnp.where(row == 0, t2, jnp.where(row == 1, t1, pltpu.roll(u, 2, axis=0)))
        w = cw_ref[...]
        yc_ref[...] = cb_scr[...] * (u2 * w[0:1] + u1 * w[1:2] + u * w[2:3])
        tail_scr[group] = u[u.shape[0] - SUBLANES:, :]

    @pl.when((j >= Q_TILE0) & (j < KV_TILE0))
    def _():
        qt_ref[0] = (rope(z) * q_scale).T.astype(BF16)

    @pl.when((j >= KV_TILE0) & (j < GATE_TILE))
    def _():
        k_ref[...] = rope(z[:, 0:KV_W]).astype(BF16)
        vt_ref[0] = z[:, KV_W:].T.astype(BF16)

    @pl.when(j == KV_TILE0)
    def _():
        for h in range(N_KV_HEADS):
            kc_ref[0, h] = k_ref[:, h * HEAD_DIM:(h + 1) * HEAD_DIM]
            vc_ref[0, h] = z[:, KV_W + h * HEAD_DIM:KV_W + (h + 1) * HEAD_DIM].astype(BF16)

    @pl.when(j == GATE_TILE)
    def _():
        zgt_ref[0] = z[:, 0:LANES].T[0:N_GATES]


def _in_projection(x2, g, w_perm, cos_t, sin_t, conv_w, B, seq):
    T, D = x2.shape
    tn = PROJ_TILE_N
    assert w_perm.shape == (D, N_PROJ_TILES * tn)
    tm = min(1024, seq)
    nb = seq // tm
    body = functools.partial(_inproj_body, q_scale=HEAD_DIM ** -0.5 * LOG2_E, blocks_per_seq=nb)
    vmem = 2 * (tm * D * 4 + D * tn * 2 + 2 * tm * HEAD_DIM * 4 + tm * tn * 4 + 4 * tm * tn * 2
                + N_GATES * tm * 4) + tm * D * 2 + 2 * tm * tn * 4 + 6 * tm * tn * 4

    def conv_group(j):
        return jnp.minimum(j // 3, CONV_GROUPS - 1)

    def weight_tile(j):
        return jnp.where(j < Q_TILE0, (j % 3) * CONV_GROUPS + j // 3, j)

    def kv_tile(j):
        return jnp.clip(j - KV_TILE0, 0, N_KV_TILES - 1)

    return pl.pallas_call(
        body,
        grid=(T // tm, N_PROJ_TILES),
        in_specs=[
            pl.BlockSpec((tm, D), lambda i, j: (i, 0)),
            pl.BlockSpec((1, D), lambda i, j: (0, 0)),
            pl.BlockSpec((D, tn), lambda i, j: (0, weight_tile(j))),
            pl.BlockSpec((tm, HEAD_DIM), lambda i, j: (i % nb, 0)),
            pl.BlockSpec((tm, HEAD_DIM), lambda i, j: (i % nb, 0)),
            pl.BlockSpec((CONV_TAPS, tn), lambda i, j: (0, conv_group(j))),
        ],
        out_specs=[
            pl.BlockSpec((tm, tn), lambda i, j: (i, conv_group(j))),
            pl.BlockSpec((1, tn, tm), lambda i, j: (i // nb, jnp.clip(j - Q_TILE0, 0, N_Q_TILES - 1), i % nb)),
            pl.BlockSpec((tm, KV_W), lambda i, j: (i, kv_tile(j))),
            pl.BlockSpec((1, N_KV_HEADS, tm, HEAD_DIM), lambda i, j: (i // nb, 0, i % nb, 0)),
            pl.BlockSpec((1, N_KV_HEADS, tm, HEAD_DIM), lambda i, j: (i // nb, 0, i % nb, 0)),
            pl.BlockSpec((1, KV_W, tm), lambda i, j: (i // nb, kv_tile(j), i % nb)),
            pl.BlockSpec((1, N_GATES, tm), lambda i, j: (i // nb, 0, i % nb)),
        ],
        out_shape=[
            jax.ShapeDtypeStruct((T, CONV_W), F32),
            jax.ShapeDtypeStruct((B, ATTN_W, seq), BF16),
            jax.ShapeDtypeStruct((T, KV_OUT_W), BF16),
            jax.ShapeDtypeStruct((B, N_KV_HEADS, seq, HEAD_DIM), BF16),
            jax.ShapeDtypeStruct((B, N_KV_HEADS, seq, HEAD_DIM), BF16),
            jax.ShapeDtypeStruct((B, KV_OUT_W, seq), BF16),
            jax.ShapeDtypeStruct((B, N_GATES, seq), F32),
        ],
        scratch_shapes=[pltpu.VMEM((tm, D), BF16), pltpu.VMEM((tm, tn), F32), pltpu.VMEM((tm, tn), F32),
                        pltpu.VMEM((CONV_GROUPS, SUBLANES, tn), F32)],
        compiler_params=_params(("arbitrary", "arbitrary"), vmem),
        name="in_projection",
    )(x2, g, w_perm, cos_t, sin_t, conv_w)


def _compress_body(x_ref, pe_ref, w1_ref, w2_ref, o_ref, ot_ref, *, n_valid):
    x = x_ref[0, 0]
    half = x.shape[1]
    w1 = w1_ref[0]
    w1_hi = w1.astype(BF16)
    w1_lo = (w1 - w1_hi.astype(F32)).astype(BF16)
    first = _dot(x, w1_hi[:half]) + _dot(x, w1_lo[:half])
    second = _dot(x, w1_hi[half:]) + _dot(x, w1_lo[half:])
    n = x.shape[0]
    second = pltpu.roll(second, n - 1, axis=0)
    pe = jnp.broadcast_to(pe_ref[0], (SUBLANES, 2 * half))
    pe_term = _dot_f32(pe, w1)[0:1]
    h = first + second + pe_term
    out = _dot_f32(jax.nn.silu(h), w2_ref[0])
    row = lax.broadcasted_iota(jnp.int32, out.shape, 0)
    out = jnp.where(row < n_valid, out, 0.0)
    o_ref[0, 0] = out
    ot_ref[0, 0] = out.T


def _compress(chunks, pe, w1, w2):
    two, BH, n, width = chunks.shape
    d = HEAD_DIM
    body = functools.partial(_compress_body, n_valid=n - 1)
    vmem = 2 * (n * width * 2 + 2 * width * 4 + 2 * width * d * 4 + d * d * 4 + n * d * 4) \
        + 8 * width * d * 4
    return pl.pallas_call(
        body,
        grid=(two, BH),
        in_specs=[
            pl.BlockSpec((1, 1, n, width), lambda s, b: (s, b, 0, 0)),
            pl.BlockSpec((1, 1, 2 * width), lambda s, b: (s, 0, 0)),
            pl.BlockSpec((1, 2 * width, d), lambda s, b: (s, 0, 0)),
            pl.BlockSpec((1, d, d), lambda s, b: (s, 0, 0)),
        ],
        out_specs=[pl.BlockSpec((1, 1, n, d), lambda s, b: (s, b, 0, 0)),
                   pl.BlockSpec((1, 1, d, n), lambda s, b: (s, b, 0, 0))],
        out_shape=[jax.ShapeDtypeStruct((two, BH, n, d), F32),
                   jax.ShapeDtypeStruct((two, BH, d, n), F32)],
        compiler_params=_params(("parallel", "parallel"), vmem),
        name="compress",
    )(chunks, pe, w1, w2)


def _select_blocks(imp, q0):
    n_sel = imp.shape[0]
    sel_i = lax.broadcasted_iota(jnp.int32, imp.shape, 0)
    sel_f = sel_i.astype(F32)
    cur = (q0 + lax.broadcasted_iota(jnp.int32, imp.shape, 1)) // SEL_BLOCK
    forced = (sel_i == 0) | (sel_i == cur) | (sel_i == cur - 1)
    v = jnp.where((sel_i > cur) | forced, -jnp.inf, imp)
    chosen = jnp.where(forced, 1.0, 0.0)
    for _ in range(min(N_SELECT, n_sel) - N_FORCED):
        top = jnp.max(v, axis=0, keepdims=True)
        first = jnp.min(jnp.where(v == top, sel_f, float(n_sel)), axis=0, keepdims=True)
        pick = sel_f == first
        chosen = jnp.where(pick, 1.0, chosen)
        v = jnp.where(pick, -jnp.inf, v)
    return jnp.where((chosen > 0.0) & (sel_i <= cur), 0.0, MASK_BIAS)


def _attn_body(qt_ref, zgt_ref, kc_ref, vct_ref, ovt_ref, ks_ref, vst_ref, e_ref, kw_ref, vwt_ref,
               y_ref, s_scr, m_scr, acc_scr, oc_scr, ow_scr, *, tk, chains, span):
    q0 = pl.program_id(2) * ATTN_Q
    GQ = GROUP * ATTN_Q
    qt = [qt_ref[0, g * HEAD_DIM:(g + 1) * HEAD_DIM, :] for g in range(GROUP)]
    q_all = jnp.concatenate(qt, axis=1)

    def lane_time(shape):
        return q0 + (lax.broadcasted_iota(jnp.int32, shape, 1) & (ATTN_Q - 1))

    s = _dot(kc_ref[0, 0].astype(BF16), q_all)
    cmp_end = lax.broadcasted_iota(jnp.int32, s.shape, 0) * CMP_STRIDE + (CMP_BLOCK - 1)
    s = jnp.where(cmp_end <= lane_time(s.shape), s, -jnp.inf)
    m = jnp.max(s, axis=0, keepdims=True)
    m = jnp.where(m == -jnp.inf, 0.0, m)
    e = jnp.exp2(s - m)
    p = e / jnp.maximum(jnp.sum(e, axis=0, keepdims=True), 1e-30)
    oc_scr[...] = _dot(vct_ref[0, 0].astype(BF16), p.astype(BF16))
    pg = p[:, 0:ATTN_Q]
    for g in range(1, GROUP):
        pg = pg + p[:, g * ATTN_Q:(g + 1) * ATTN_Q]
    ovt = ovt_ref[...]
    p0, p1, p2 = _split3(pg)
    bias = _select_blocks(_dot(ovt, p0) + _dot(ovt, p1) + _dot(ovt, p2), q0).astype(BF16)

    w_start = pl.multiple_of(jnp.maximum(q0 - WINDOW, 0), ATTN_Q)
    sw = _dot(kw_ref[0, pl.ds(w_start, span), :], q_all)
    pos = w_start + lax.broadcasted_iota(jnp.int32, sw.shape, 0)
    t = lane_time(sw.shape)
    sw = jnp.where((pos <= t) & (pos > t - WINDOW), sw, -jnp.inf)
    ew = jnp.exp2(sw - jnp.max(sw, axis=0, keepdims=True))
    vw = jnp.concatenate([vwt_ref[0, :, pl.ds(w_start, span)], jnp.ones((ONES_ROWS, span), BF16)], axis=0)
    ow = _dot(vw, ew.astype(BF16))
    ow_scr[...] = ow[0:HEAD_DIM] / ow[HEAD_DIM:HEAD_DIM + 1]

    cols = GQ // chains
    per = GROUP // chains
    qx = [jnp.concatenate([jnp.concatenate([qt[g], bias], axis=0) for g in range(c * per, (c + 1) * per)],
                          axis=1) for c in range(chains)]

    def scores(j, slot):
        start = pl.multiple_of(j * tk, tk)
        kx = jnp.concatenate([ks_ref[0, pl.ds(start, tk), :], e_ref[pl.ds(start, tk), :]], axis=1)
        for c in range(chains):
            s_scr[slot, :, c * cols:(c + 1) * cols] = _dot(kx, qx[c])

    def update(j, slot, causal):
        start = pl.multiple_of(j * tk, tk)
        vt = jnp.concatenate([vst_ref[0, :, pl.ds(start, tk)], jnp.ones((ONES_ROWS, tk), BF16)], axis=0)
        for c in range(chains):
            sl = slice(c * cols, (c + 1) * cols)
            sc = s_scr[slot, :, sl]
            if causal:
                key = start + lax.broadcasted_iota(jnp.int32, sc.shape, 0)
                sc = jnp.where(key <= lane_time(sc.shape), sc, -jnp.inf)
            m_old = m_scr[:, sl]
            m_new = jnp.maximum(m_old, jnp.max(sc, axis=0, keepdims=True))
            alpha = jnp.exp2(m_old - m_new)
            pj = jnp.exp2(sc - m_new)
            m_scr[:, sl] = m_new
            acc_scr[:, sl] = alpha * acc_scr[:, sl] + _dot(vt, pj.astype(BF16))

    m_scr[...] = jnp.full(m_scr.shape, -jnp.inf, F32)
    acc_scr[...] = jnp.zeros(acc_scr.shape, F32)
    scores(0, 0)
    last = q0 // tk

    @pl.loop(0, last // 2)
    def _(i):
        scores(2 * i + 1, 1)
        update(2 * i, 0, False)
        scores(2 * i + 2, 0)
        update(2 * i + 1, 1, False)

    @pl.when(last % 2 == 1)
    def _():
        scores(last, 1)
        update(last - 1, 0, False)
        update(last, 1, True)

    @pl.when(last % 2 == 0)
    def _():
        update(last, 0, True)

    o_sel = acc_scr[0:HEAD_DIM, :] / acc_scr[HEAD_DIM:HEAD_DIM + 1, :]
    gates = jax.nn.sigmoid(zgt_ref[0])
    hk = pl.program_id(1)

    def gate(g, branch):
        row = gates[g * N_BRANCH + branch:g * N_BRANCH + branch + 1]
        for other in range(1, N_KV_HEADS):
            r = (other * GROUP + g) * N_BRANCH + branch
            row = jnp.where(hk == other, gates[r:r + 1], row)
        return row

    for g in range(GROUP):
        sl = slice(g * ATTN_Q, (g + 1) * ATTN_Q)
        y = (gate(g, 0) * oc_scr[:, sl] + gate(g, 1) * o_sel[:, sl]
             + gate(g, 2) * ow_scr[:, sl])
        y_ref[0, :, g * HEAD_DIM:(g + 1) * HEAD_DIM] = y.T


def _attention(qt, zgt, kvc, kvct, ovt, kk, vt, onehot, B, seq):
    n_cmp = kvc.shape[2]
    n_sel = ovt.shape[0]
    GW = GROUP * HEAD_DIM
    GQ = GROUP * ATTN_Q
    VR = HEAD_DIM + ONES_ROWS
    tk = min(512, seq)
    span = WINDOW + ATTN_Q
    assert seq >= span
    body = functools.partial(_attn_body, tk=tk, chains=2, span=span)
    resident = 4 * seq * HEAD_DIM * 2 + seq * n_sel * 2
    vmem = 2 * (GW * ATTN_Q * 2 + N_GATES * ATTN_Q * 4 + 2 * n_cmp * HEAD_DIM * 4 + n_sel * n_cmp * 2
                + resident + ATTN_Q * GW * 4) \
        + (2 * tk + VR + 2 * HEAD_DIM + SUBLANES) * GQ * 4 + 8 * (n_cmp + span) * GQ * 4
    return pl.pallas_call(
        body,
        grid=(B, N_KV_HEADS, seq // ATTN_Q),
        in_specs=[
            pl.BlockSpec((1, GW, ATTN_Q), lambda b, h, i: (b, h, i)),
            pl.BlockSpec((1, N_GATES, ATTN_Q), lambda b, h, i: (b, 0, i)),
            pl.BlockSpec((1, 1, n_cmp, HEAD_DIM), lambda b, h, i: (0, b * N_KV_HEADS + h, 0, 0)),
            pl.BlockSpec((1, 1, HEAD_DIM, n_cmp), lambda b, h, i: (1, b * N_KV_HEADS + h, 0, 0)),
            pl.BlockSpec((n_sel, n_cmp), lambda b, h, i: (0, 0)),
            pl.BlockSpec((1, seq, HEAD_DIM), lambda b, h, i: (b, 0, N_KV_HEADS + h)),
            pl.BlockSpec((1, HEAD_DIM, seq), lambda b, h, i: (b, N_KV_HEADS + h, 0)),
            pl.BlockSpec((seq, n_sel), lambda b, h, i: (0, 0)),
            pl.BlockSpec((1, seq, HEAD_DIM), lambda b, h, i: (b, 0, 2 * N_KV_HEADS + h)),
            pl.BlockSpec((1, HEAD_DIM, seq), lambda b, h, i: (b, 2 * N_KV_HEADS + h, 0)),
        ],
        out_specs=pl.BlockSpec((1, ATTN_Q, GW), lambda b, h, i: (b, i, h)),
        out_shape=jax.ShapeDtypeStruct((B, seq, ATTN_W), F32),
        scratch_shapes=[pltpu.VMEM((2, tk, GQ), F32),
                        pltpu.VMEM((1, GQ), F32),
                        pltpu.VMEM((VR, GQ), F32),
                        pltpu.VMEM((HEAD_DIM, GQ), F32),
                        pltpu.VMEM((HEAD_DIM, GQ), F32)],
        compiler_params=_params(("parallel", "parallel", "arbitrary"), vmem),
        name="attention",
    )(qt, zgt, kvc, kvct, ovt, kk, vt, onehot, kk, vt)


def _outproj_body(yc_ref, ya_ref, gc_ref, ga_ref, w_ref, x_ref, h_ref, a_scr):
    @pl.when(pl.program_id(1) == 0)
    def _():
        a_scr[:, 0:CONV_W] = _rms(yc_ref[...], gc_ref[...]).astype(BF16)
        a_scr[:, CONV_W:] = _rms(ya_ref[...], ga_ref[...]).astype(BF16)

    h_ref[...] = x_ref[...] + _dot(a_scr[...], w_ref[...])


def _out_projection(yc, ya, g_conv, g_attn, w, x2):
    T, D = x2.shape
    K = w.shape[0]
    tm = min(1024, T)
    tn = min(512, D)
    vmem = 2 * (tm * CONV_W * 4 + tm * ATTN_W * 4 + K * tn * 2 + 2 * tm * tn * 4) + tm * K * 2 \
        + 3 * tm * ATTN_W * 4
    return pl.pallas_call(
        _outproj_body,
        grid=(T // tm, D // tn),
        in_specs=[
            pl.BlockSpec((tm, CONV_W), lambda i, j: (i, 0)),
            pl.BlockSpec((tm, ATTN_W), lambda i, j: (i, 0)),
            pl.BlockSpec((1, CONV_W), lambda i, j: (0, 0)),
            pl.BlockSpec((1, ATTN_W), lambda i, j: (0, 0)),
            pl.BlockSpec((K, tn), lambda i, j: (0, j)),
            pl.BlockSpec((tm, tn), lambda i, j: (i, j)),
        ],
        out_specs=pl.BlockSpec((tm, tn), lambda i, j: (i, j)),
        out_shape=jax.ShapeDtypeStruct((T, D), F32),
        scratch_shapes=[pltpu.VMEM((tm, K), BF16)],
        compiler_params=_params(("parallel", "arbitrary"), vmem),
        name="out_projection",
    )(yc, ya, g_conv, g_attn, w, x2)


def _ffn_body(h_ref, g_ref, wg_ref, wu_ref, wd_ref, gf_ref, o_ref, a_scr, *, final_norm, tn):
    j = pl.program_id(1)

    @pl.when(j == 0)
    def _():
        a_scr[...] = _rms(h_ref[...], g_ref[...]).astype(BF16)

    a = a_scr[...]
    u = (jax.nn.silu(_dot(a, wg_ref[...])) * _dot(a, wu_ref[...])).astype(BF16)
    D = o_ref.shape[1]

    @pl.when(j == 0)
    def _():
        for n in range(D // tn):
            o_ref[:, n * tn:(n + 1) * tn] = _dot(u, wd_ref[:, n * tn:(n + 1) * tn])

    @pl.when(j > 0)
    def _():
        for n in range(D // tn):
            o_ref[:, n * tn:(n + 1) * tn] += _dot(u, wd_ref[:, n * tn:(n + 1) * tn])

    @pl.when(j == pl.num_programs(1) - 1)
    def _():
        out = h_ref[...] + o_ref[...]
        o_ref[...] = _rms(out, gf_ref[...]) if final_norm else out


def _ffn(h, g, wg, wu, wd, g_final, final_norm):
    T, D = h.shape
    F = wg.shape[1]
    tm = min(1024, T)
    tf = 512
    tn = 512
    assert F % tf == 0 and D % tn == 0
    body = functools.partial(_ffn_body, final_norm=final_norm, tn=tn)
    vmem = tm * D * 4 + 2 * (3 * D * tf * 2 + tm * D * 4) + tm * D * 2 + 6 * tm * tf * 4 + 2 * tm * tn * 4
    return pl.pallas_call(
        body,
        grid=(T // tm, F // tf),
        in_specs=[
            pl.BlockSpec((tm, D), lambda i, j: (i, 0), pipeline_mode=pl.Buffered(1)),
            pl.BlockSpec((1, D), lambda i, j: (0, 0)),
            pl.BlockSpec((D, tf), lambda i, j: (0, j)),
            pl.BlockSpec((D, tf), lambda i, j: (0, j)),
            pl.BlockSpec((tf, D), lambda i, j: (j, 0)),
            pl.BlockSpec((1, D), lambda i, j: (0, 0)),
        ],
        out_specs=pl.BlockSpec((tm, D), lambda i, j: (i, 0)),
        out_shape=jax.ShapeDtypeStruct((T, D), F32),
        scratch_shapes=[pltpu.VMEM((tm, D), BF16)],
        compiler_params=_params(("parallel", "arbitrary"), vmem),
        name="ffn",
    )(h, g, wg, wu, wd, g_final)


def _rope_tables(seq):
    half = HEAD_DIM // 2
    inv = 1.0 / (ROPE_THETA ** (jnp.arange(half, dtype=F32) / half))
    ang = jnp.arange(seq).astype(F32)[:, None] * inv[None, :]
    cos = jnp.cos(ang)
    sin = jnp.sin(ang)
    return jnp.concatenate([cos, cos], axis=1), jnp.concatenate([-sin, sin], axis=1)


def _overlap_t(n_sel, n_cmp):
    cs = np.arange(n_cmp)[None, :] * CMP_STRIDE
    ss = np.arange(n_sel)[:, None] * SEL_BLOCK
    return jnp.asarray(((cs < ss + SEL_BLOCK) & (cs + CMP_BLOCK > ss)).astype(np.float32), dtype=BF16)


def _block_onehot(seq, n_sel):
    return jnp.asarray((np.arange(seq)[:, None] // SEL_BLOCK == np.arange(n_sel)[None, :])
                       .astype(np.float32), dtype=BF16)


def kernel(x, norm_mix, w_in, conv_w, cmp_pe_k, cmp_w1_k, cmp_w2_k, cmp_pe_v, cmp_w1_v, cmp_w2_v,
           norm_conv_out, norm_attn_out, w_out, norm_ffn, w_gate, w_up, w_down, norm_final):
    B, S, D = x.shape
    T = B * S
    depth = norm_mix.shape[0]
    assert S % ATTN_Q == 0 and S % CMP_STRIDE == 0
    n_chunks = S // CMP_STRIDE
    n_sel = S // SEL_BLOCK
    cos_t, sin_t = _rope_tables(S)
    ovt = _overlap_t(n_sel, n_chunks)
    onehot = _block_onehot(S, n_sel)
    chunk_w = CMP_STRIDE * HEAD_DIM

    h = x.reshape(T, D)
    for l in range(depth):
        y_conv, qt, kk, kc, vc, vt, zgt = _in_projection(h, norm_mix[l][None], _proj_weight(w_in[l]),
                                                          cos_t, sin_t, conv_w[l], B, S)

        kvc_in = jnp.stack([kc, vc]).reshape(2, B * N_KV_HEADS, n_chunks, chunk_w)
        pe = jnp.stack([cmp_pe_k[l], cmp_pe_v[l]]).reshape(2, 1, CMP_BLOCK * HEAD_DIM)
        kvc, kvct = _compress(kvc_in, pe, jnp.stack([cmp_w1_k[l], cmp_w1_v[l]]),
                              jnp.stack([cmp_w2_k[l], cmp_w2_v[l]]))

        y_attn = _attention(qt, zgt, kvc, kvct, ovt, kk.reshape(B, S, KV_OUT_W), vt, onehot, B, S)

        h = _out_projection(y_conv, y_attn.reshape(T, ATTN_W), norm_conv_out[l][None],
                            norm_attn_out[l][None], w_out[l].astype(BF16), h)

        h = _ffn(h, norm_ffn[l][None], w_gate[l].astype(BF16), w_up[l].astype(BF16),
                 w_down[l].astype(BF16), norm_final[None], l == depth - 1)
    return h.reshape(B, S, D)
```

```python
import functools

import jax
import jax.numpy as jnp
import numpy as np
from jax import lax
from jax.experimental import pallas as pl
from jax.experimental.pallas import tpu as pltpu

F32 = jnp.float32
BF16 = jnp.bfloat16

HEAD_DIM = 128
N_HEADS = 8
N_KV_HEADS = 2
GROUP = N_HEADS // N_KV_HEADS
N_BRANCH = 3
CONV_TAPS = 3
CMP_BLOCK = 32
CMP_STRIDE = 16
SEL_BLOCK = 64
N_SELECT = 16
N_FORCED = 3
WINDOW = 512
ATTN_Q = 256
ROPE_THETA = 10000.0
RMS_EPS = 1e-6

CONV_W = 1024
ATTN_W = N_HEADS * HEAD_DIM
KV_W = N_KV_HEADS * HEAD_DIM
N_GATES = N_HEADS * N_BRANCH

V7X_VMEM_BYTES = 64 * 1024 * 1024
V7X_VMEM_BUDGET = V7X_VMEM_BYTES - 6 * 1024 * 1024
LANES = 128
SUBLANES = 8
PROJ_TILE_N = 512

MASK_BIAS = -float(2 ** 30)
LOG2_E = 1.4426950408889634
ONES_ROWS = 16


def _params(semantics, vmem_bytes):
    return pltpu.CompilerParams(dimension_semantics=semantics,
                                vmem_limit_bytes=int(min(vmem_bytes, V7X_VMEM_BUDGET)))


def _rms(x, g):
    return x * lax.rsqrt(jnp.mean(x * x, axis=-1, keepdims=True) + RMS_EPS) * g


def _split3(x):
    hi = x.astype(BF16)
    r1 = x - hi.astype(F32)
    mid = r1.astype(BF16)
    lo = (r1 - mid.astype(F32)).astype(BF16)
    return hi, mid, lo


def _dot(a, b):
    return jnp.dot(a, b, preferred_element_type=F32)


def _dot_f32(a, b):
    a0, a1, a2 = _split3(a)
    b0, b1, b2 = _split3(b)
    return (_dot(a0, b0) + (_dot(a0, b1) + _dot(a1, b0))
            + (_dot(a0, b2) + _dot(a1, b1) + _dot(a2, b0)))


CONV_GROUPS = CONV_W // PROJ_TILE_N
N_CONV_TILES = 3 * CONV_GROUPS
N_Q_TILES = ATTN_W // PROJ_TILE_N
N_KV_TILES = 3
Q_TILE0 = N_CONV_TILES
KV_TILE0 = Q_TILE0 + N_Q_TILES
GATE_TILE = KV_TILE0 + N_KV_TILES
N_PROJ_TILES = GATE_TILE + 1
KV_OUT_W = N_KV_TILES * KV_W
assert 2 * KV_W == PROJ_TILE_N and N_GATES % SUBLANES == 0


def _gate_weight(w):
    return w[:, GATE_TILE * PROJ_TILE_N:].T.astype(BF16)


def _inproj_body(x_ref, g_ref, w_ref, wg_ref, cos_ref, sin_ref, cw_ref,
                 yc_ref, qt_ref, k_ref, kc_ref, vc_ref, vt_ref, zgt_ref,
                 a_scr, cb_scr, cc_scr, tail_scr, *, q_scale, blocks_per_seq):
    i = pl.program_id(0)
    j = pl.program_id(1)

    @pl.when(j == 0)
    def _():
        a_scr[...] = _rms(x_ref[...], g_ref[...]).astype(BF16)

    @pl.when((i == 0) & (j == 0))
    def _():
        tail_scr[...] = jnp.zeros(tail_scr.shape, F32)

    @pl.when(j == GATE_TILE)
    def _():
        zgt_ref[0] = lax.dot_general(wg_ref[...], a_scr[...], (((1,), (1,)), ((), ())),
                                     preferred_element_type=F32)

    @pl.when(j < GATE_TILE)
    def _():
        _inproj_tile(j, i, _dot(a_scr[...], w_ref[...]), cos_ref, sin_ref, cw_ref, yc_ref, qt_ref, k_ref,
                     kc_ref, vc_ref, vt_ref, cb_scr, cc_scr, tail_scr, q_scale, blocks_per_seq)


def _inproj_tile(j, i, z, cos_ref, sin_ref, cw_ref, yc_ref, qt_ref, k_ref, kc_ref, vc_ref, vt_ref,
                 cb_scr, cc_scr, tail_scr, q_scale, blocks_per_seq):
    def rope(zh):
        cos = cos_ref[...]
        sin = sin_ref[...]
        heads = []
        for h in range(zh.shape[1] // HEAD_DIM):
            z1 = zh[:, h * HEAD_DIM:(h + 1) * HEAD_DIM]
            heads.append(z1 * cos + pltpu.roll(z1, HEAD_DIM // 2, axis=1) * sin)
        return jnp.concatenate(heads, axis=1)

    is_conv = j < Q_TILE0

    @pl.when(is_conv & (j % 3 == 0))
    def _():
        cb_scr[...] = z

    @pl.when(is_conv & (j % 3 == 1))
    def _():
        cc_scr[...] = z

    @pl.when(is_conv & (j % 3 == 2))
    def _():
        group = j // 3
        u = cc_scr[...] * z
        tail = tail_scr[group]
        tail = jnp.where(i % blocks_per_seq == 0, 0.0, tail)
        row = lax.broadcasted_iota(jnp.int32, u.shape, 0)
        t1 = tail[SUBLANES - 1:SUBLANES]
        t2 = tail[SUBLANES - 2:SUBLANES - 1]
        u1 = jnp.where(row == 0, t1, pltpu.roll(u, 1, axis=0))
        u2 = jnp.where(row == 0, t2, jnp.where(row == 1, t1, pltpu.roll(u, 2, axis=0)))
        w = cw_ref[...]
        yc_ref[...] = cb_scr[...] * (u2 * w[0:1] + u1 * w[1:2] + u * w[2:3])
        tail_scr[group] = u[u.shape[0] - SUBLANES:, :]

    @pl.when((j >= Q_TILE0) & (j < KV_TILE0))
    def _():
        qt_ref[0] = (rope(z) * q_scale).T.astype(BF16)

    @pl.when((j >= KV_TILE0) & (j < GATE_TILE))
    def _():
        k_ref[...] = rope(z[:, 0:KV_W]).astype(BF16)
        vt_ref[0] = z[:, KV_W:].T.astype(BF16)

    @pl.when(j == KV_TILE0)
    def _():
        for h in range(N_KV_HEADS):
            kc_ref[0, h] = k_ref[:, h * HEAD_DIM:(h + 1) * HEAD_DIM].astype(F32)
            vc_ref[0, h] = z[:, KV_W + h * HEAD_DIM:KV_W + (h + 1) * HEAD_DIM].astype(BF16).astype(F32)


def _in_projection(x2, g, w_proj, w_gates, cos_t, sin_t, conv_w, B, seq):
    T, D = x2.shape
    tn = PROJ_TILE_N
    assert w_proj.shape[1] >= GATE_TILE * tn
    tm = min(1024, seq)
    nb = seq // tm
    body = functools.partial(_inproj_body, q_scale=HEAD_DIM ** -0.5 * LOG2_E, blocks_per_seq=nb)
    vmem = 2 * (tm * D * 4 + D * tn * 2 + 2 * tm * HEAD_DIM * 4 + tm * tn * 4 + 4 * tm * tn * 2
                + N_GATES * tm * 4) + tm * D * 2 + 2 * tm * tn * 4 + 6 * tm * tn * 4

    def conv_group(j):
        return jnp.minimum(j // 3, CONV_GROUPS - 1)

    def weight_tile(j):
        return jnp.where(j < Q_TILE0, (j % 3) * CONV_GROUPS + j // 3, jnp.minimum(j, GATE_TILE - 1))

    def kv_tile(j):
        return jnp.clip(j - KV_TILE0, 0, N_KV_TILES - 1)

    return pl.pallas_call(
        body,
        grid=(T // tm, N_PROJ_TILES),
        in_specs=[
            pl.BlockSpec((tm, D), lambda i, j: (i, 0)),
            pl.BlockSpec((1, D), lambda i, j: (0, 0)),
            pl.BlockSpec((D, tn), lambda i, j: (0, weight_tile(j))),
            pl.BlockSpec((N_GATES, D), lambda i, j: (0, 0)),
            pl.BlockSpec((tm, HEAD_DIM), lambda i, j: (i % nb, 0)),
            pl.BlockSpec((tm, HEAD_DIM), lambda i, j: (i % nb, 0)),
            pl.BlockSpec((CONV_TAPS, tn), lambda i, j: (0, conv_group(j))),
        ],
        out_specs=[
            pl.BlockSpec((tm, tn), lambda i, j: (i, conv_group(j))),
            pl.BlockSpec((1, tn, tm), lambda i, j: (i // nb, jnp.clip(j - Q_TILE0, 0, N_Q_TILES - 1), i % nb)),
            pl.BlockSpec((tm, KV_W), lambda i, j: (i, kv_tile(j))),
            pl.BlockSpec((1, N_KV_HEADS, tm, HEAD_DIM), lambda i, j: (i // nb, 0, i % nb, 0)),
            pl.BlockSpec((1, N_KV_HEADS, tm, HEAD_DIM), lambda i, j: (i // nb, 0, i % nb, 0)),
            pl.BlockSpec((1, KV_W, tm), lambda i, j: (i // nb, kv_tile(j), i % nb)),
            pl.BlockSpec((1, N_GATES, tm), lambda i, j: (i // nb, 0, i % nb)),
        ],
        out_shape=[
            jax.ShapeDtypeStruct((T, CONV_W), F32),
            jax.ShapeDtypeStruct((B, ATTN_W, seq), BF16),
            jax.ShapeDtypeStruct((T, KV_OUT_W), BF16),
            jax.ShapeDtypeStruct((B, N_KV_HEADS, seq, HEAD_DIM), F32),
            jax.ShapeDtypeStruct((B, N_KV_HEADS, seq, HEAD_DIM), F32),
            jax.ShapeDtypeStruct((B, KV_OUT_W, seq), BF16),
            jax.ShapeDtypeStruct((B, N_GATES, seq), F32),
        ],
        scratch_shapes=[pltpu.VMEM((tm, D), BF16), pltpu.VMEM((tm, tn), F32), pltpu.VMEM((tm, tn), F32),
                        pltpu.VMEM((CONV_GROUPS, SUBLANES, tn), F32)],
        compiler_params=_params(("arbitrary", "arbitrary"), vmem),
        name="in_projection",
    )(x2, g, w_proj, w_gates, cos_t, sin_t, conv_w)


def _compress_body(x_ref, pe_ref, w1_ref, w2_ref, o_ref, ot_ref, *, n_valid):
    n = x_ref.shape[2] // CMP_STRIDE
    x = jnp.concatenate([x_ref[0, 0, pl.ds(l, n, stride=CMP_STRIDE), :] for l in range(CMP_STRIDE)],
                        axis=1).astype(BF16)
    half = x.shape[1]
    w1 = w1_ref[0]
    w1_hi = w1.astype(BF16)
    w1_lo = (w1 - w1_hi.astype(F32)).astype(BF16)
    first = _dot(x, w1_hi[:half]) + _dot(x, w1_lo[:half])
    second = _dot(x, w1_hi[half:]) + _dot(x, w1_lo[half:])
    second = pltpu.roll(second, n - 1, axis=0)
    pe = jnp.broadcast_to(pe_ref[0], (SUBLANES, 2 * half))
    pe_term = _dot_f32(pe, w1)[0:1]
    h = first + second + pe_term
    out = _dot_f32(jax.nn.silu(h), w2_ref[0])
    row = lax.broadcasted_iota(jnp.int32, out.shape, 0)
    out = jnp.where(row < n_valid, out, 0.0)
    o_ref[0, 0] = out
    ot_ref[0, 0] = out.T


def _compress(kv, pe, w1, w2):
    two, BH, seq, d = kv.shape
    n = seq // CMP_STRIDE
    width = CMP_STRIDE * d
    body = functools.partial(_compress_body, n_valid=n - 1)
    vmem = 2 * (seq * d * 4 + 2 * width * 4 + 2 * width * d * 4 + d * d * 4 + n * d * 4) \
        + 8 * width * d * 4
    return pl.pallas_call(
        body,
        grid=(two, BH),
        in_specs=[
            pl.BlockSpec((1, 1, seq, d), lambda s, b: (s, b, 0, 0)),
            pl.BlockSpec((1, 1, 2 * width), lambda s, b: (s, 0, 0)),
            pl.BlockSpec((1, 2 * width, d), lambda s, b: (s, 0, 0)),
            pl.BlockSpec((1, d, d), lambda s, b: (s, 0, 0)),
        ],
        out_specs=[pl.BlockSpec((1, 1, n, d), lambda s, b: (s, b, 0, 0)),
                   pl.BlockSpec((1, 1, d, n), lambda s, b: (s, b, 0, 0))],
        out_shape=[jax.ShapeDtypeStruct((two, BH, n, d), F32),
                   jax.ShapeDtypeStruct((two, BH, d, n), F32)],
        compiler_params=_params(("parallel", "parallel"), vmem),
        name="compress",
    )(kv, pe, w1, w2)


def _select_blocks(imp, q0):
    n_sel = imp.shape[0]
    sel_i = lax.broadcasted_iota(jnp.int32, imp.shape, 0)
    sel_f = sel_i.astype(F32)
    cur = (q0 + lax.broadcasted_iota(jnp.int32, imp.shape, 1)) // SEL_BLOCK
    forced = (sel_i == 0) | (sel_i == cur) | (sel_i == cur - 1)
    v = jnp.where((sel_i > cur) | forced, -jnp.inf, imp)
    chosen = jnp.where(forced, 1.0, 0.0)
    for _ in range(min(N_SELECT, n_sel) - N_FORCED):
        top = jnp.max(v, axis=0, keepdims=True)
        first = jnp.min(jnp.where(v == top, sel_f, float(n_sel)), axis=0, keepdims=True)
        pick = sel_f == first
        chosen = jnp.where(pick, 1.0, chosen)
        v = jnp.where(pick, -jnp.inf, v)
    return jnp.where((chosen > 0.0) & (sel_i <= cur), 0.0, MASK_BIAS)


def _attn_body(qt_ref, zgt_ref, kc_ref, vct_ref, ovt_ref, ks_ref, vst_ref, e_ref, kw_ref, vwt_ref, wb_ref,
               y_ref, s_scr, m_scr, acc_scr, oc_scr, ow_scr, *, tk, span):
    q0 = pl.program_id(2) * ATTN_Q
    qt = [qt_ref[0, g * HEAD_DIM:(g + 1) * HEAD_DIM, :] for g in range(GROUP)]
    q_all = jnp.concatenate(qt, axis=1)

    def lane_time(shape):
        return q0 + (lax.broadcasted_iota(jnp.int32, shape, 1) & (ATTN_Q - 1))

    s = _dot(kc_ref[0, 0].astype(BF16), q_all)
    cmp_end = lax.broadcasted_iota(jnp.int32, s.shape, 0) * CMP_STRIDE + (CMP_BLOCK - 1)
    s = jnp.where(cmp_end <= lane_time(s.shape), s, -jnp.inf)
    m = jnp.max(s, axis=0, keepdims=True)
    m = jnp.where(m == -jnp.inf, 0.0, m)
    e = jnp.exp2(s - m)
    p = e / jnp.maximum(jnp.sum(e, axis=0, keepdims=True), 1e-30)
    oc_scr[...] = _dot(vct_ref[0, 0].astype(BF16), p.astype(BF16))
    pg = p[:, 0:ATTN_Q]
    for g in range(1, GROUP):
        pg = pg + p[:, g * ATTN_Q:(g + 1) * ATTN_Q]
    ovt = ovt_ref[...]
    p0, p1, p2 = _split3(pg)
    bias = _select_blocks(_dot(ovt, p0) + _dot(ovt, p1) + _dot(ovt, p2), q0).astype(BF16)

    w_start = pl.multiple_of(jnp.maximum(q0 - WINDOW, 0), ATTN_Q)
    sw = _dot(kw_ref[0, pl.ds(w_start, span), :], q_all)
    sw = sw + jnp.concatenate([wb_ref[0]] * GROUP, axis=1)
    ew = jnp.exp2(sw - jnp.max(sw, axis=0, keepdims=True))
    vw = jnp.concatenate([vwt_ref[0, :, pl.ds(w_start, span)], jnp.ones((ONES_ROWS, span), BF16)], axis=0)
    ow = _dot(vw, ew.astype(BF16))
    ow_scr[...] = ow[0:HEAD_DIM] / ow[HEAD_DIM:HEAD_DIM + 1]

    qx = jnp.concatenate([jnp.concatenate([qt[g], bias], axis=0) for g in range(GROUP)], axis=1)

    def scores(j, slot):
        start = pl.multiple_of(j * tk, tk)
        kx = jnp.concatenate([ks_ref[0, pl.ds(start, tk), :], e_ref[pl.ds(start, tk), :]], axis=1)
        s_scr[slot] = _dot(kx, qx)

    def update(j, slot, causal):
        start = pl.multiple_of(j * tk, tk)
        vt = jnp.concatenate([vst_ref[0, :, pl.ds(start, tk)], jnp.ones((ONES_ROWS, tk), BF16)], axis=0)
        sc = s_scr[slot]
        if causal:
            key = start + lax.broadcasted_iota(jnp.int32, sc.shape, 0)
            sc = jnp.where(key <= lane_time(sc.shape), sc, -jnp.inf)
        m_old = m_scr[...]
        m_new = jnp.maximum(m_old, jnp.max(sc, axis=0, keepdims=True))
        alpha = jnp.exp2(m_old - m_new)
        pj = jnp.exp2(sc - m_new)
        m_scr[...] = m_new
        acc_scr[...] = alpha * acc_scr[...] + _dot(vt, pj.astype(BF16))

    m_scr[...] = jnp.full(m_scr.shape, -jnp.inf, F32)
    acc_scr[...] = jnp.zeros(acc_scr.shape, F32)
    scores(0, 0)
    last = q0 // tk

    @pl.loop(0, last // 2)
    def _(i):
        scores(2 * i + 1, 1)
        update(2 * i, 0, False)
        scores(2 * i + 2, 0)
        update(2 * i + 1, 1, False)

    @pl.when(last % 2 == 1)
    def _():
        scores(last, 1)
        update(last - 1, 0, False)
        update(last, 1, True)

    @pl.when(last % 2 == 0)
    def _():
        update(last, 0, True)

    o_sel = acc_scr[0:HEAD_DIM, :] / acc_scr[HEAD_DIM:HEAD_DIM + 1, :]
    gates = jax.nn.sigmoid(zgt_ref[0])
    hk = pl.program_id(1)

    def gate(g, branch):
        row = gates[g * N_BRANCH + branch:g * N_BRANCH + branch + 1]
        for other in range(1, N_KV_HEADS):
            r = (other * GROUP + g) * N_BRANCH + branch
            row = jnp.where(hk == other, gates[r:r + 1], row)
        return row

    for g in range(GROUP):
        sl = slice(g * ATTN_Q, (g + 1) * ATTN_Q)
        y = (gate(g, 0) * oc_scr[:, sl] + gate(g, 1) * o_sel[:, sl]
             + gate(g, 2) * ow_scr[:, sl])
        y_ref[0, :, g * HEAD_DIM:(g + 1) * HEAD_DIM] = y.T


def _attention(qt, zgt, kvc, kvct, ovt, kk, vt, onehot, B, seq):
    n_cmp = kvc.shape[2]
    n_sel = ovt.shape[0]
    GW = GROUP * HEAD_DIM
    GQ = GROUP * ATTN_Q
    VR = HEAD_DIM + ONES_ROWS
    tk = min(512, seq)
    span = WINDOW + ATTN_Q
    assert seq >= span
    body = functools.partial(_attn_body, tk=tk, span=span)
    resident = 4 * seq * HEAD_DIM * 2 + seq * n_sel * 2
    vmem = 2 * (GW * ATTN_Q * 2 + N_GATES * ATTN_Q * 4 + 2 * n_cmp * HEAD_DIM * 4 + n_sel * n_cmp * 2
                + resident + ATTN_Q * GW * 4) \
        + (2 * tk + VR + 2 * HEAD_DIM + SUBLANES) * GQ * 4 + 8 * (n_cmp + span) * GQ * 4
    return pl.pallas_call(
        body,
        grid=(B, N_KV_HEADS, seq // ATTN_Q),
        in_specs=[
            pl.BlockSpec((1, GW, ATTN_Q), lambda b, h, i: (b, h, i)),
            pl.BlockSpec((1, N_GATES, ATTN_Q), lambda b, h, i: (b, 0, i)),
            pl.BlockSpec((1, 1, n_cmp, HEAD_DIM), lambda b, h, i: (0, b * N_KV_HEADS + h, 0, 0)),
            pl.BlockSpec((1, 1, HEAD_DIM, n_cmp), lambda b, h, i: (1, b * N_KV_HEADS + h, 0, 0)),
            pl.BlockSpec((n_sel, n_cmp), lambda b, h, i: (0, 0)),
            pl.BlockSpec((1, seq, HEAD_DIM), lambda b, h, i: (b, 0, N_KV_HEADS + h)),
            pl.BlockSpec((1, HEAD_DIM, seq), lambda b, h, i: (b, N_KV_HEADS + h, 0)),
            pl.BlockSpec((seq, n_sel), lambda b, h, i: (0, 0)),
            pl.BlockSpec((1, seq, HEAD_DIM), lambda b, h, i: (b, 0, 2 * N_KV_HEADS + h)),
            pl.BlockSpec((1, HEAD_DIM, seq), lambda b, h, i: (b, 2 * N_KV_HEADS + h, 0)),
            pl.BlockSpec((1, span, ATTN_Q), lambda b, h, i: (jnp.minimum(i, WINDOW // ATTN_Q), 0, 0)),
        ],
        out_specs=pl.BlockSpec((1, ATTN_Q, GW), lambda b, h, i: (b, i, h)),
        out_shape=jax.ShapeDtypeStruct((B, seq, ATTN_W), F32),
        scratch_shapes=[pltpu.VMEM((2, tk, GQ), F32),
                        pltpu.VMEM((1, GQ), F32),
                        pltpu.VMEM((VR, GQ), F32),
                        pltpu.VMEM((HEAD_DIM, GQ), F32),
                        pltpu.VMEM((HEAD_DIM, GQ), F32)],
        compiler_params=_params(("parallel", "parallel", "arbitrary"), vmem),
        name="attention",
    )(qt, zgt, kvc, kvct, ovt, kk, vt, onehot, kk, vt, _window_bias(span))


def _outproj_body(yc_ref, ya_ref, gc_ref, ga_ref, w_ref, x_ref, h_ref, a_scr):
    @pl.when(pl.program_id(1) == 0)
    def _():
        a_scr[:, 0:CONV_W] = _rms(yc_ref[...], gc_ref[...]).astype(BF16)
        a_scr[:, CONV_W:] = _rms(ya_ref[...], ga_ref[...]).astype(BF16)

    h_ref[...] = x_ref[...] + _dot(a_scr[...], w_ref[...])


def _out_projection(yc, ya, g_conv, g_attn, w, x2):
    T, D = x2.shape
    K = w.shape[0]
    tm = min(1024, T)
    tn = min(512, D)
    vmem = 2 * (tm * CONV_W * 4 + tm * ATTN_W * 4 + K * tn * 2 + 2 * tm * tn * 4) + tm * K * 2 \
        + 3 * tm * ATTN_W * 4
    return pl.pallas_call(
        _outproj_body,
        grid=(T // tm, D // tn),
        in_specs=[
            pl.BlockSpec((tm, CONV_W), lambda i, j: (i, 0)),
            pl.BlockSpec((tm, ATTN_W), lambda i, j: (i, 0)),
            pl.BlockSpec((1, CONV_W), lambda i, j: (0, 0)),
            pl.BlockSpec((1, ATTN_W), lambda i, j: (0, 0)),
            pl.BlockSpec((K, tn), lambda i, j: (0, j)),
            pl.BlockSpec((tm, tn), lambda i, j: (i, j)),
        ],
        out_specs=pl.BlockSpec((tm, tn), lambda i, j: (i, j)),
        out_shape=jax.ShapeDtypeStruct((T, D), F32),
        scratch_shapes=[pltpu.VMEM((tm, K), BF16)],
        compiler_params=_params(("parallel", "arbitrary"), vmem),
        name="out_projection",
    )(yc, ya, g_conv, g_attn, w, x2)


def _ffn_body(h_ref, g_ref, wg_ref, wu_ref, wd_ref, gf_ref, o_ref, a_scr, *, final_norm, tn):
    j = pl.program_id(1)

    @pl.when(j == 0)
    def _():
        a_scr[...] = _rms(h_ref[...], g_ref[...]).astype(BF16)

    a = a_scr[...]
    u = (jax.nn.silu(_dot(a, wg_ref[...])) * _dot(a, wu_ref[...])).astype(BF16)
    D = o_ref.shape[1]

    @pl.when(j == 0)
    def _():
        for n in range(D // tn):
            o_ref[:, n * tn:(n + 1) * tn] = _dot(u, wd_ref[:, n * tn:(n + 1) * tn])

    @pl.when(j > 0)
    def _():
        for n in range(D // tn):
            o_ref[:, n * tn:(n + 1) * tn] += _dot(u, wd_ref[:, n * tn:(n + 1) * tn])

    @pl.when(j == pl.num_programs(1) - 1)
    def _():
        out = h_ref[...] + o_ref[...]
        o_ref[...] = _rms(out, gf_ref[...]) if final_norm else out


def _ffn(h, g, wg, wu, wd, g_final, final_norm):
    T, D = h.shape
    F = wg.shape[1]
    tm = min(1024, T)
    tf = 512
    tn = 512
    assert F % tf == 0 and D % tn == 0
    body = functools.partial(_ffn_body, final_norm=final_norm, tn=tn)
    vmem = tm * D * 4 + 2 * (3 * D * tf * 2 + tm * D * 4) + tm * D * 2 + 6 * tm * tf * 4 + 2 * tm * tn * 4
    return pl.pallas_call(
        body,
        grid=(T // tm, F // tf),
        in_specs=[
            pl.BlockSpec((tm, D), lambda i, j: (i, 0), pipeline_mode=pl.Buffered(1)),
            pl.BlockSpec((1, D), lambda i, j: (0, 0)),
            pl.BlockSpec((D, tf), lambda i, j: (0, j)),
            pl.BlockSpec((D, tf), lambda i, j: (0, j)),
            pl.BlockSpec((tf, D), lambda i, j: (j, 0)),
            pl.BlockSpec((1, D), lambda i, j: (0, 0)),
        ],
        out_specs=pl.BlockSpec((tm, D), lambda i, j: (i, 0)),
        out_shape=jax.ShapeDtypeStruct((T, D), F32),
        scratch_shapes=[pltpu.VMEM((tm, D), BF16)],
        compiler_params=_params(("parallel", "arbitrary"), vmem),
        name="ffn",
    )(h, g, wg, wu, wd, g_final)


def _rope_tables(seq):
    half = HEAD_DIM // 2
    inv = 1.0 / (ROPE_THETA ** (jnp.arange(half, dtype=F32) / half))
    ang = jnp.arange(seq).astype(F32)[:, None] * inv[None, :]
    cos = jnp.cos(ang)
    sin = jnp.sin(ang)
    return jnp.concatenate([cos, cos], axis=1), jnp.concatenate([-sin, sin], axis=1)


def _overlap_t(n_sel, n_cmp):
    cs = np.arange(n_cmp)[None, :] * CMP_STRIDE
    ss = np.arange(n_sel)[:, None] * SEL_BLOCK
    return jnp.asarray(((cs < ss + SEL_BLOCK) & (cs + CMP_BLOCK > ss)).astype(np.float32), dtype=BF16)


def _window_bias(span):
    r = np.arange(span)[None, :, None]
    c = np.arange(ATTN_Q)[None, None, :]
    off = np.minimum(np.arange(WINDOW // ATTN_Q + 1) * ATTN_Q, WINDOW)[:, None, None]
    valid = (r <= off + c) & (r > off + c - WINDOW)
    return jnp.asarray(np.where(valid, 0.0, -np.inf).astype(np.float32))


def _block_onehot(seq, n_sel):
    return jnp.asarray((np.arange(seq)[:, None] // SEL_BLOCK == np.arange(n_sel)[None, :])
                       .astype(np.float32), dtype=BF16)


def kernel(x, norm_mix, w_in, conv_w, cmp_pe_k, cmp_w1_k, cmp_w2_k, cmp_pe_v, cmp_w1_v, cmp_w2_v,
           norm_conv_out, norm_attn_out, w_out, norm_ffn, w_gate, w_up, w_down, norm_final):
    B, S, D = x.shape
    T = B * S
    depth = norm_mix.shape[0]
    assert S % ATTN_Q == 0 and S % CMP_STRIDE == 0
    n_chunks = S // CMP_STRIDE
    n_sel = S // SEL_BLOCK
    cos_t, sin_t = _rope_tables(S)
    ovt = _overlap_t(n_sel, n_chunks)
    onehot = _block_onehot(S, n_sel)

    h = x.reshape(T, D)
    for l in range(depth):
        y_conv, qt, kk, kc, vc, vt, zgt = _in_projection(h, norm_mix[l][None], w_in[l].astype(BF16),
                                                          _gate_weight(w_in[l]), cos_t, sin_t, conv_w[l], B, S)

        kvc_in = jnp.stack([kc, vc]).reshape(2, B * N_KV_HEADS, S, HEAD_DIM)
        pe = jnp.stack([cmp_pe_k[l], cmp_pe_v[l]]).reshape(2, 1, CMP_BLOCK * HEAD_DIM)
        kvc, kvct = _compress(kvc_in, pe, jnp.stack([cmp_w1_k[l], cmp_w1_v[l]]),
                              jnp.stack([cmp_w2_k[l], cmp_w2_v[l]]))

        y_attn = _attention(qt, zgt, kvc, kvct, ovt, kk.reshape(B, S, KV_OUT_W), vt, onehot, B, S)

        h = _out_projection(y_conv, y_attn.reshape(T, ATTN_W), norm_conv_out[l][None],
                            norm_attn_out[l][None], w_out[l].astype(BF16), h)

        h = _ffn(h, norm_ffn[l][None], w_gate[l].astype(BF16), w_up[l].astype(BF16),
                 w_down[l].astype(BF16), norm_final[None], l == depth - 1)
    return h.reshape(B, S, D)
```

```python
import functools

import jax
import jax.numpy as jnp
import numpy as np
from jax import lax
from jax.experimental import pallas as pl
from jax.experimental.pallas import tpu as pltpu

F32 = jnp.float32
BF16 = jnp.bfloat16

HEAD_DIM = 128
N_HEADS = 8
N_KV_HEADS = 2
GROUP = N_HEADS // N_KV_HEADS
N_BRANCH = 3
CONV_TAPS = 3
CMP_BLOCK = 32
CMP_STRIDE = 16
SEL_BLOCK = 64
N_SELECT = 16
N_FORCED = 3
WINDOW = 512
ATTN_Q = 256
ROPE_THETA = 10000.0
RMS_EPS = 1e-6

CONV_W = 1024
ATTN_W = N_HEADS * HEAD_DIM
KV_W = N_KV_HEADS * HEAD_DIM
N_GATES = N_HEADS * N_BRANCH

V7X_VMEM_BYTES = 64 * 1024 * 1024
V7X_VMEM_BUDGET = V7X_VMEM_BYTES - 6 * 1024 * 1024
LANES = 128
SUBLANES = 8
PROJ_TILE_N = 512
PROJ_CHUNK = 256

MASK_BIAS = -float(2 ** 30)
LOG2_E = 1.4426950408889634
ONES_ROWS = 16


def _params(semantics, vmem_bytes):
    return pltpu.CompilerParams(dimension_semantics=semantics,
                                vmem_limit_bytes=int(min(vmem_bytes, V7X_VMEM_BUDGET)))


def _rms(x, g):
    return x * lax.rsqrt(jnp.mean(x * x, axis=-1, keepdims=True) + RMS_EPS) * g


def _split3(x):
    hi = x.astype(BF16)
    r1 = x - hi.astype(F32)
    mid = r1.astype(BF16)
    lo = (r1 - mid.astype(F32)).astype(BF16)
    return hi, mid, lo


def _dot(a, b):
    return jnp.dot(a, b, preferred_element_type=F32)


def _dot_f32(a, b):
    a0, a1, a2 = _split3(a)
    b0, b1, b2 = _split3(b)
    return (_dot(a0, b0) + (_dot(a0, b1) + _dot(a1, b0))
            + (_dot(a0, b2) + _dot(a1, b1) + _dot(a2, b0)))


CONV_GROUPS = CONV_W // PROJ_TILE_N
N_Q_TILES = ATTN_W // PROJ_TILE_N
N_KV_TILES = 3
Q_STEP0 = CONV_GROUPS
KV_STEP0 = Q_STEP0 + N_Q_TILES
GATE_STEP = KV_STEP0 + N_KV_TILES
N_PROJ_STEPS = GATE_STEP + 1
N_WEIGHT_TILES = 3 * CONV_GROUPS + N_Q_TILES + N_KV_TILES
KV_OUT_W = N_KV_TILES * KV_W
assert 2 * KV_W == PROJ_TILE_N and N_GATES % SUBLANES == 0


def _gate_weight(w):
    return w[:, N_WEIGHT_TILES * PROJ_TILE_N:].T.astype(BF16)


def _inproj_body(x_ref, g_ref, w_ref, wc_ref, wh_ref, wg_ref, cos_ref, sin_ref, cw_ref,
                 yc_ref, qt_ref, k_ref, kc_ref, vc_ref, vt_ref, zgt_ref,
                 a_scr, tail_scr, *, q_scale, blocks_per_seq):
    i = pl.program_id(0)
    j = pl.program_id(1)

    @pl.when(j == 0)
    def _():
        a_scr[...] = _rms(x_ref[...], g_ref[...]).astype(BF16)

    @pl.when((i == 0) & (j == 0))
    def _():
        tail_scr[...] = jnp.zeros(tail_scr.shape, F32)

    @pl.when(j == GATE_STEP)
    def _():
        zgt_ref[0] = lax.dot_general(wg_ref[...], a_scr[...], (((1,), (1,)), ((), ())),
                                     preferred_element_type=F32)

    chunks = [(c, c + PROJ_CHUNK) for c in range(0, PROJ_TILE_N, PROJ_CHUNK)]

    def z_cols(weight_ref, c0, c1):
        return _dot(a_scr[...], weight_ref[:, c0:c1])

    def rope(zh):
        cos = cos_ref[...]
        sin = sin_ref[...]
        heads = []
        for h in range(zh.shape[1] // HEAD_DIM):
            z1 = zh[:, h * HEAD_DIM:(h + 1) * HEAD_DIM]
            heads.append(z1 * cos + pltpu.roll(z1, HEAD_DIM // 2, axis=1) * sin)
        return jnp.concatenate(heads, axis=1)

    @pl.when(j < Q_STEP0)
    def _():
        for c0, c1 in chunks:
            u = z_cols(wc_ref, c0, c1) * z_cols(wh_ref, c0, c1)
            tail = tail_scr[j, :, c0:c1]
            tail = jnp.where(i % blocks_per_seq == 0, 0.0, tail)
            row = lax.broadcasted_iota(jnp.int32, u.shape, 0)
            t1 = tail[SUBLANES - 1:SUBLANES]
            t2 = tail[SUBLANES - 2:SUBLANES - 1]
            u1 = jnp.where(row == 0, t1, pltpu.roll(u, 1, axis=0))
            u2 = jnp.where(row == 0, t2, jnp.where(row == 1, t1, pltpu.roll(u, 2, axis=0)))
            w = cw_ref[:, c0:c1]
            yc_ref[:, c0:c1] = z_cols(w_ref, c0, c1) * (u2 * w[0:1] + u1 * w[1:2] + u * w[2:3])
            tail_scr[j, :, c0:c1] = u[u.shape[0] - SUBLANES:, :]

    @pl.when((j >= Q_STEP0) & (j < KV_STEP0))
    def _():
        for c0, c1 in chunks:
            qt_ref[0, c0:c1, :] = (rope(z_cols(w_ref, c0, c1)) * q_scale).T.astype(BF16)

    @pl.when((j >= KV_STEP0) & (j < GATE_STEP))
    def _():
        k_ref[...] = rope(z_cols(w_ref, 0, KV_W)).astype(BF16)
        zv = z_cols(w_ref, KV_W, 2 * KV_W)
        vt_ref[0] = zv.T.astype(BF16)

        @pl.when(j == KV_STEP0)
        def _():
            for h in range(N_KV_HEADS):
                kc_ref[0, h] = k_ref[:, h * HEAD_DIM:(h + 1) * HEAD_DIM].astype(F32)
                vc_ref[0, h] = zv[:, h * HEAD_DIM:(h + 1) * HEAD_DIM].astype(BF16).astype(F32)


def _in_projection(x2, g, w_proj, w_gates, cos_t, sin_t, conv_w, B, seq):
    T, D = x2.shape
    tn = PROJ_TILE_N
    assert w_proj.shape[1] >= N_WEIGHT_TILES * tn
    tm = min(1024, seq)
    nb = seq // tm
    body = functools.partial(_inproj_body, q_scale=HEAD_DIM ** -0.5 * LOG2_E, blocks_per_seq=nb)
    vmem = 2 * (tm * D * 4 + 3 * D * tn * 2 + 2 * tm * HEAD_DIM * 4 + tm * tn * 4 + 4 * tm * tn * 2
                + N_GATES * tm * 4) + tm * D * 2 + 8 * tm * PROJ_CHUNK * 4

    def conv_group(j):
        return jnp.minimum(j, CONV_GROUPS - 1)

    def main_tile(j):
        return jnp.where(j < Q_STEP0, j, jnp.minimum(j + 2 * CONV_GROUPS, N_WEIGHT_TILES - 1))

    def kv_tile(j):
        return jnp.clip(j - KV_STEP0, 0, N_KV_TILES - 1)

    return pl.pallas_call(
        body,
        grid=(T // tm, N_PROJ_STEPS),
        in_specs=[
            pl.BlockSpec((tm, D), lambda i, j: (i, 0)),
            pl.BlockSpec((1, D), lambda i, j: (0, 0)),
            pl.BlockSpec((D, tn), lambda i, j: (0, main_tile(j))),
            pl.BlockSpec((D, tn), lambda i, j: (0, CONV_GROUPS + conv_group(j))),
            pl.BlockSpec((D, tn), lambda i, j: (0, 2 * CONV_GROUPS + conv_group(j))),
            pl.BlockSpec((N_GATES, D), lambda i, j: (0, 0)),
            pl.BlockSpec((tm, HEAD_DIM), lambda i, j: (i % nb, 0)),
            pl.BlockSpec((tm, HEAD_DIM), lambda i, j: (i % nb, 0)),
            pl.BlockSpec((CONV_TAPS, tn), lambda i, j: (0, conv_group(j))),
        ],
        out_specs=[
            pl.BlockSpec((tm, tn), lambda i, j: (i, conv_group(j))),
            pl.BlockSpec((1, tn, tm), lambda i, j: (i // nb, jnp.clip(j - Q_STEP0, 0, N_Q_TILES - 1), i % nb)),
            pl.BlockSpec((tm, KV_W), lambda i, j: (i, kv_tile(j))),
            pl.BlockSpec((1, N_KV_HEADS, tm, HEAD_DIM), lambda i, j: (i // nb, 0, i % nb, 0)),
            pl.BlockSpec((1, N_KV_HEADS, tm, HEAD_DIM), lambda i, j: (i // nb, 0, i % nb, 0)),
            pl.BlockSpec((1, KV_W, tm), lambda i, j: (i // nb, kv_tile(j), i % nb)),
            pl.BlockSpec((1, N_GATES, tm), lambda i, j: (i // nb, 0, i % nb)),
        ],
        out_shape=[
            jax.ShapeDtypeStruct((T, CONV_W), F32),
            jax.ShapeDtypeStruct((B, ATTN_W, seq), BF16),
            jax.ShapeDtypeStruct((T, KV_OUT_W), BF16),
            jax.ShapeDtypeStruct((B, N_KV_HEADS, seq, HEAD_DIM), F32),
            jax.ShapeDtypeStruct((B, N_KV_HEADS, seq, HEAD_DIM), F32),
            jax.ShapeDtypeStruct((B, KV_OUT_W, seq), BF16),
            jax.ShapeDtypeStruct((B, N_GATES, seq), F32),
        ],
        scratch_shapes=[pltpu.VMEM((tm, D), BF16), pltpu.VMEM((CONV_GROUPS, SUBLANES, tn), F32)],
        compiler_params=_params(("arbitrary", "arbitrary"), vmem),
        name="in_projection",
    )(x2, g, w_proj, w_proj, w_proj, w_gates, cos_t, sin_t, conv_w)


def _compress_body(x_ref, pe_ref, w1_ref, w2_ref, o_ref, ot_ref, *, n_valid):
    n = x_ref.shape[2] // CMP_STRIDE
    x = jnp.concatenate([x_ref[0, 0, pl.ds(l, n, stride=CMP_STRIDE), :] for l in range(CMP_STRIDE)],
                        axis=1).astype(BF16)
    half = x.shape[1]
    w1 = w1_ref[0]
    w1_hi = w1.astype(BF16)
    w1_lo = (w1 - w1_hi.astype(F32)).astype(BF16)
    first = _dot(x, w1_hi[:half]) + _dot(x, w1_lo[:half])
    second = _dot(x, w1_hi[half:]) + _dot(x, w1_lo[half:])
    second = pltpu.roll(second, n - 1, axis=0)
    pe = jnp.broadcast_to(pe_ref[0], (SUBLANES, 2 * half))
    pe_term = _dot_f32(pe, w1)[0:1]
    h = first + second + pe_term
    out = _dot_f32(jax.nn.silu(h), w2_ref[0])
    row = lax.broadcasted_iota(jnp.int32, out.shape, 0)
    out = jnp.where(row < n_valid, out, 0.0)
    o_ref[0, 0] = out
    ot_ref[0, 0] = out.T


def _compress(kv, pe, w1, w2):
    two, BH, seq, d = kv.shape
    n = seq // CMP_STRIDE
    width = CMP_STRIDE * d
    body = functools.partial(_compress_body, n_valid=n - 1)
    vmem = 2 * (seq * d * 4 + 2 * width * 4 + 2 * width * d * 4 + d * d * 4 + n * d * 4) \
        + 8 * width * d * 4
    return pl.pallas_call(
        body,
        grid=(two, BH),
        in_specs=[
            pl.BlockSpec((1, 1, seq, d), lambda s, b: (s, b, 0, 0)),
            pl.BlockSpec((1, 1, 2 * width), lambda s, b: (s, 0, 0)),
            pl.BlockSpec((1, 2 * width, d), lambda s, b: (s, 0, 0)),
            pl.BlockSpec((1, d, d), lambda s, b: (s, 0, 0)),
        ],
        out_specs=[pl.BlockSpec((1, 1, n, d), lambda s, b: (s, b, 0, 0)),
                   pl.BlockSpec((1, 1, d, n), lambda s, b: (s, b, 0, 0))],
        out_shape=[jax.ShapeDtypeStruct((two, BH, n, d), F32),
                   jax.ShapeDtypeStruct((two, BH, d, n), F32)],
        compiler_params=_params(("parallel", "parallel"), vmem),
        name="compress",
    )(kv, pe, w1, w2)


def _select_blocks(imp, q0):
    n_sel = imp.shape[0]
    sel_i = lax.broadcasted_iota(jnp.int32, imp.shape, 0)
    sel_f = sel_i.astype(F32)
    cur = (q0 + lax.broadcasted_iota(jnp.int32, imp.shape, 1)) // SEL_BLOCK
    forced = (sel_i == 0) | (sel_i == cur) | (sel_i == cur - 1)
    v = jnp.where((sel_i > cur) | forced, -jnp.inf, imp)
    chosen = jnp.where(forced, 1.0, 0.0)
    for _ in range(min(N_SELECT, n_sel) - N_FORCED):
        top = jnp.max(v, axis=0, keepdims=True)
        first = jnp.min(jnp.where(v == top, sel_f, float(n_sel)), axis=0, keepdims=True)
        pick = sel_f == first
        chosen = jnp.where(pick, 1.0, chosen)
        v = jnp.where(pick, -jnp.inf, v)
    return jnp.where((chosen > 0.0) & (sel_i <= cur), 0.0, MASK_BIAS)


def _attn_body(qt_ref, zgt_ref, kc_ref, vct_ref, ovt_ref, ks_ref, vst_ref, e_ref, kw_ref, vwt_ref, wb_ref,
               y_ref, s_scr, m_scr, acc_scr, oc_scr, ow_scr, *, tk, span):
    q0 = pl.program_id(2) * ATTN_Q
    qt = [qt_ref[0, g * HEAD_DIM:(g + 1) * HEAD_DIM, :] for g in range(GROUP)]
    q_all = jnp.concatenate(qt, axis=1)

    def lane_time(shape):
        return q0 + (lax.broadcasted_iota(jnp.int32, shape, 1) & (ATTN_Q - 1))

    s = _dot(kc_ref[0, 0].astype(BF16), q_all)
    cmp_end = lax.broadcasted_iota(jnp.int32, s.shape, 0) * CMP_STRIDE + (CMP_BLOCK - 1)
    s = jnp.where(cmp_end <= lane_time(s.shape), s, -jnp.inf)
    m = jnp.max(s, axis=0, keepdims=True)
    m = jnp.where(m == -jnp.inf, 0.0, m)
    e = jnp.exp2(s - m)
    p = e / jnp.maximum(jnp.sum(e, axis=0, keepdims=True), 1e-30)
    oc_scr[...] = _dot(vct_ref[0, 0].astype(BF16), p.astype(BF16))
    pg = p[:, 0:ATTN_Q]
    for g in range(1, GROUP):
        pg = pg + p[:, g * ATTN_Q:(g + 1) * ATTN_Q]
    ovt = ovt_ref[...]
    p0, p1, p2 = _split3(pg)
    bias = _select_blocks(_dot(ovt, p0) + _dot(ovt, p1) + _dot(ovt, p2), q0).astype(BF16)

    w_start = pl.multiple_of(jnp.maximum(q0 - WINDOW, 0), ATTN_Q)
    sw = _dot(kw_ref[0, pl.ds(w_start, span), :], q_all)
    sw = sw + jnp.concatenate([wb_ref[0]] * GROUP, axis=1)
    ew = jnp.exp2(sw - jnp.max(sw, axis=0, keepdims=True))
    vw = jnp.concatenate([vwt_ref[0, :, pl.ds(w_start, span)], jnp.ones((ONES_ROWS, span), BF16)], axis=0)
    ow = _dot(vw, ew.astype(BF16))
    ow_scr[...] = ow[0:HEAD_DIM] / ow[HEAD_DIM:HEAD_DIM + 1]

    qx = jnp.concatenate([jnp.concatenate([qt[g], bias], axis=0) for g in range(GROUP)], axis=1)

    def scores(j, slot):
        start = pl.multiple_of(j * tk, tk)
        kx = jnp.concatenate([ks_ref[0, pl.ds(start, tk), :], e_ref[pl.ds(start, tk), :]], axis=1)
        s_scr[slot] = _dot(kx, qx)

    def update(j, slot, causal):
        start = pl.multiple_of(j * tk, tk)
        vt = jnp.concatenate([vst_ref[0, :, pl.ds(start, tk)], jnp.ones((ONES_ROWS, tk), BF16)], axis=0)
        sc = s_scr[slot]
        if causal:
            key = start + lax.broadcasted_iota(jnp.int32, sc.shape, 0)
            sc = jnp.where(key <= lane_time(sc.shape), sc, -jnp.inf)
        m_old = m_scr[...]
        m_new = jnp.maximum(m_old, jnp.max(sc, axis=0, keepdims=True))
        alpha = jnp.exp2(m_old - m_new)
        pj = jnp.exp2(sc - m_new)
        m_scr[...] = m_new
        acc_scr[...] = alpha * acc_scr[...] + _dot(vt, pj.astype(BF16))

    m_scr[...] = jnp.full(m_scr.shape, -jnp.inf, F32)
    acc_scr[...] = jnp.zeros(acc_scr.shape, F32)
    scores(0, 0)
    last = q0 // tk

    @pl.loop(0, last // 2)
    def _(i):
        scores(2 * i + 1, 1)
        update(2 * i, 0, False)
        scores(2 * i + 2, 0)
        update(2 * i + 1, 1, False)

    @pl.when(last % 2 == 1)
    def _():
        scores(last, 1)
        update(last - 1, 0, False)
        update(last, 1, True)

    @pl.when(last % 2 == 0)
    def _():
        update(last, 0, True)

    o_sel = acc_scr[0:HEAD_DIM, :] / acc_scr[HEAD_DIM:HEAD_DIM + 1, :]
    gates = jax.nn.sigmoid(zgt_ref[0])
    hk = pl.program_id(1)

    def gate(g, branch):
        row = gates[g * N_BRANCH + branch:g * N_BRANCH + branch + 1]
        for other in range(1, N_KV_HEADS):
            r = (other * GROUP + g) * N_BRANCH + branch
            row = jnp.where(hk == other, gates[r:r + 1], row)
        return row

    for g in range(GROUP):
        sl = slice(g * ATTN_Q, (g + 1) * ATTN_Q)
        y = (gate(g, 0) * oc_scr[:, sl] + gate(g, 1) * o_sel[:, sl]
             + gate(g, 2) * ow_scr[:, sl])
        y_ref[0, :, g * HEAD_DIM:(g + 1) * HEAD_DIM] = y.T


def _attention(qt, zgt, kvc, kvct, ovt, kk, vt, onehot, B, seq):
    n_cmp = kvc.shape[2]
    n_sel = ovt.shape[0]
    GW = GROUP * HEAD_DIM
    GQ = GROUP * ATTN_Q
    VR = HEAD_DIM + ONES_ROWS
    tk = min(512, seq)
    span = WINDOW + ATTN_Q
    assert seq >= span
    body = functools.partial(_attn_body, tk=tk, span=span)
    resident = 4 * seq * HEAD_DIM * 2 + seq * n_sel * 2
    vmem = 2 * (GW * ATTN_Q * 2 + N_GATES * ATTN_Q * 4 + 2 * n_cmp * HEAD_DIM * 4 + n_sel * n_cmp * 2
                + resident + ATTN_Q * GW * 4) \
        + (2 * tk + VR + 2 * HEAD_DIM + SUBLANES) * GQ * 4 + 8 * (n_cmp + span) * GQ * 4
    return pl.pallas_call(
        body,
        grid=(B, N_KV_HEADS, seq // ATTN_Q),
        in_specs=[
            pl.BlockSpec((1, GW, ATTN_Q), lambda b, h, i: (b, h, i)),
            pl.BlockSpec((1, N_GATES, ATTN_Q), lambda b, h, i: (b, 0, i)),
            pl.BlockSpec((1, 1, n_cmp, HEAD_DIM), lambda b, h, i: (0, b * N_KV_HEADS + h, 0, 0)),
            pl.BlockSpec((1, 1, HEAD_DIM, n_cmp), lambda b, h, i: (1, b * N_KV_HEADS + h, 0, 0)),
            pl.BlockSpec((n_sel, n_cmp), lambda b, h, i: (0, 0)),
            pl.BlockSpec((1, seq, HEAD_DIM), lambda b, h, i: (b, 0, N_KV_HEADS + h)),
            pl.BlockSpec((1, HEAD_DIM, seq), lambda b, h, i: (b, N_KV_HEADS + h, 0)),
            pl.BlockSpec((seq, n_sel), lambda b, h, i: (0, 0)),
            pl.BlockSpec((1, seq, HEAD_DIM), lambda b, h, i: (b, 0, 2 * N_KV_HEADS + h)),
            pl.BlockSpec((1, HEAD_DIM, seq), lambda b, h, i: (b, 2 * N_KV_HEADS + h, 0)),
            pl.BlockSpec((1, span, ATTN_Q), lambda b, h, i: (jnp.minimum(i, WINDOW // ATTN_Q), 0, 0)),
        ],
        out_specs=pl.BlockSpec((1, ATTN_Q, GW), lambda b, h, i: (b, i, h)),
        out_shape=jax.ShapeDtypeStruct((B, seq, ATTN_W), F32),
        scratch_shapes=[pltpu.VMEM((2, tk, GQ), F32),
                        pltpu.VMEM((1, GQ), F32),
                        pltpu.VMEM((VR, GQ), F32),
                        pltpu.VMEM((HEAD_DIM, GQ), F32),
                        pltpu.VMEM((HEAD_DIM, GQ), F32)],
        compiler_params=_params(("parallel", "parallel", "arbitrary"), vmem),
        name="attention",
    )(qt, zgt, kvc, kvct, ovt, kk, vt, onehot, kk, vt, _window_bias(span))


def _outproj_body(yc_ref, ya_ref, gc_ref, ga_ref, w_ref, x_ref, h_ref, a_scr):
    @pl.when(pl.program_id(1) == 0)
    def _():
        a_scr[:, 0:CONV_W] = _rms(yc_ref[...], gc_ref[...]).astype(BF16)
        a_scr[:, CONV_W:] = _rms(ya_ref[...], ga_ref[...]).astype(BF16)

    h_ref[...] = x_ref[...] + _dot(a_scr[...], w_ref[...])


def _out_projection(yc, ya, g_conv, g_attn, w, x2):
    T, D = x2.shape
    K = w.shape[0]
    tm = min(1024, T)
    tn = min(1024, D)
    vmem = 2 * (tm * CONV_W * 4 + tm * ATTN_W * 4 + K * tn * 2 + 2 * tm * tn * 4) + tm * K * 2 \
        + 3 * tm * ATTN_W * 4
    return pl.pallas_call(
        _outproj_body,
        grid=(T // tm, D // tn),
        in_specs=[
            pl.BlockSpec((tm, CONV_W), lambda i, j: (i, 0)),
            pl.BlockSpec((tm, ATTN_W), lambda i, j: (i, 0)),
            pl.BlockSpec((1, CONV_W), lambda i, j: (0, 0)),
            pl.BlockSpec((1, ATTN_W), lambda i, j: (0, 0)),
            pl.BlockSpec((K, tn), lambda i, j: (0, j)),
            pl.BlockSpec((tm, tn), lambda i, j: (i, j)),
        ],
        out_specs=pl.BlockSpec((tm, tn), lambda i, j: (i, j)),
        out_shape=jax.ShapeDtypeStruct((T, D), F32),
        scratch_shapes=[pltpu.VMEM((tm, K), BF16)],
        compiler_params=_params(("parallel", "arbitrary"), vmem),
        name="out_projection",
    )(yc, ya, g_conv, g_attn, w, x2)


def _ffn_body(h_ref, g_ref, wg_ref, wu_ref, wd_ref, gf_ref, o_ref, a_scr, *, final_norm, tn):
    j = pl.program_id(1)

    @pl.when(j == 0)
    def _():
        a_scr[...] = _rms(h_ref[...], g_ref[...]).astype(BF16)

    a = a_scr[...]
    u = (jax.nn.silu(_dot(a, wg_ref[...])) * _dot(a, wu_ref[...])).astype(BF16)
    D = o_ref.shape[1]

    @pl.when(j == 0)
    def _():
        for n in range(D // tn):
            o_ref[:, n * tn:(n + 1) * tn] = _dot(u, wd_ref[:, n * tn:(n + 1) * tn])

    @pl.when(j > 0)
    def _():
        for n in range(D // tn):
            o_ref[:, n * tn:(n + 1) * tn] += _dot(u, wd_ref[:, n * tn:(n + 1) * tn])

    @pl.when(j == pl.num_programs(1) - 1)
    def _():
        out = h_ref[...] + o_ref[...]
        o_ref[...] = _rms(out, gf_ref[...]) if final_norm else out


def _ffn(h, g, wg, wu, wd, g_final, final_norm):
    T, D = h.shape
    F = wg.shape[1]
    tm = min(1024, T)
    tf = 512
    tn = 512
    assert F % tf == 0 and D % tn == 0
    body = functools.partial(_ffn_body, final_norm=final_norm, tn=tn)
    vmem = tm * D * 4 + 2 * (3 * D * tf * 2 + tm * D * 4) + tm * D * 2 + 6 * tm * tf * 4 + 2 * tm * tn * 4
    return pl.pallas_call(
        body,
        grid=(T // tm, F // tf),
        in_specs=[
            pl.BlockSpec((tm, D), lambda i, j: (i, 0), pipeline_mode=pl.Buffered(1)),
            pl.BlockSpec((1, D), lambda i, j: (0, 0)),
            pl.BlockSpec((D, tf), lambda i, j: (0, j)),
            pl.BlockSpec((D, tf), lambda i, j: (0, j)),
            pl.BlockSpec((tf, D), lambda i, j: (j, 0)),
            pl.BlockSpec((1, D), lambda i, j: (0, 0)),
        ],
        out_specs=pl.BlockSpec((tm, D), lambda i, j: (i, 0)),
        out_shape=jax.ShapeDtypeStruct((T, D), F32),
        scratch_shapes=[pltpu.VMEM((tm, D), BF16)],
        compiler_params=_params(("parallel", "arbitrary"), vmem),
        name="ffn",
    )(h, g, wg, wu, wd, g_final)


def _rope_tables(seq):
    half = HEAD_DIM // 2
    inv = 1.0 / (ROPE_THETA ** (jnp.arange(half, dtype=F32) / half))
    ang = jnp.arange(seq).astype(F32)[:, None] * inv[None, :]
    cos = jnp.cos(ang)
    sin = jnp.sin(ang)
    return jnp.concatenate([cos, cos], axis=1), jnp.concatenate([-sin, sin], axis=1)


def _overlap_t(n_sel, n_cmp):
    cs = np.arange(n_cmp)[None, :] * CMP_STRIDE
    ss = np.arange(n_sel)[:, None] * SEL_BLOCK
    return jnp.asarray(((cs < ss + SEL_BLOCK) & (cs + CMP_BLOCK > ss)).astype(np.float32), dtype=BF16)


def _window_bias(span):
    r = np.arange(span)[None, :, None]
    c = np.arange(ATTN_Q)[None, None, :]
    off = np.minimum(np.arange(WINDOW // ATTN_Q + 1) * ATTN_Q, WINDOW)[:, None, None]
    valid = (r <= off + c) & (r > off + c - WINDOW)
    return jnp.asarray(np.where(valid, 0.0, -np.inf).astype(np.float32))


def _block_onehot(seq, n_sel):
    return jnp.asarray((np.arange(seq)[:, None] // SEL_BLOCK == np.arange(n_sel)[None, :])
                       .astype(np.float32), dtype=BF16)


def kernel(x, norm_mix, w_in, conv_w, cmp_pe_k, cmp_w1_k, cmp_w2_k, cmp_pe_v, cmp_w1_v, cmp_w2_v,
           norm_conv_out, norm_attn_out, w_out, norm_ffn, w_gate, w_up, w_down, norm_final):
    B, S, D = x.shape
    T = B * S
    depth = norm_mix.shape[0]
    assert S % ATTN_Q == 0 and S % CMP_STRIDE == 0
    n_chunks = S // CMP_STRIDE
    n_sel = S // SEL_BLOCK
    cos_t, sin_t = _rope_tables(S)
    ovt = _overlap_t(n_sel, n_chunks)
    onehot = _block_onehot(S, n_sel)

    h = x.reshape(T, D)
    for l in range(depth):
        y_conv, qt, kk, kc, vc, vt, zgt = _in_projection(h, norm_mix[l][None], w_in[l].astype(BF16),
                                                          _gate_weight(w_in[l]), cos_t, sin_t, conv_w[l], B, S)

        kvc_in = jnp.stack([kc, vc]).reshape(2, B * N_KV_HEADS, S, HEAD_DIM)
        pe = jnp.stack([cmp_pe_k[l], cmp_pe_v[l]]).reshape(2, 1, CMP_BLOCK * HEAD_DIM)
        kvc, kvct = _compress(kvc_in, pe, jnp.stack([cmp_w1_k[l], cmp_w1_v[l]]),
                              jnp.stack([cmp_w2_k[l], cmp_w2_v[l]]))

        y_attn = _attention(qt, zgt, kvc, kvct, ovt, kk.reshape(B, S, KV_OUT_W), vt, onehot, B, S)

        h = _out_projection(y_conv, y_attn.reshape(T, ATTN_W), norm_conv_out[l][None],
                            norm_attn_out[l][None], w_out[l].astype(BF16), h)

        h = _ffn(h, norm_ffn[l][None], w_gate[l].astype(BF16), w_up[l].astype(BF16),
                 w_down[l].astype(BF16), norm_final[None], l == depth - 1)
    return h.reshape(B, S, D)
```

```python
import functools

import jax
import jax.numpy as jnp
import numpy as np
from jax import lax
from jax.experimental import pallas as pl
from jax.experimental.pallas import tpu as pltpu

F32 = jnp.float32
BF16 = jnp.bfloat16

HEAD_DIM = 128
N_HEADS = 8
N_KV_HEADS = 2
GROUP = N_HEADS // N_KV_HEADS
N_BRANCH = 3
CONV_TAPS = 3
CMP_BLOCK = 32
CMP_STRIDE = 16
SEL_BLOCK = 64
N_SELECT = 16
N_FORCED = 3
WINDOW = 512
ATTN_Q = 256
ROPE_THETA = 10000.0
RMS_EPS = 1e-6

CONV_W = 1024
ATTN_W = N_HEADS * HEAD_DIM
KV_W = N_KV_HEADS * HEAD_DIM
N_GATES = N_HEADS * N_BRANCH

V7X_VMEM_BYTES = 64 * 1024 * 1024
V7X_VMEM_BUDGET = V7X_VMEM_BYTES - 6 * 1024 * 1024
LANES = 128
SUBLANES = 8
PROJ_TILE_N = 512
PROJ_CHUNK = 256

MASK_BIAS = -float(2 ** 30)
LOG2_E = 1.4426950408889634
ONES_ROWS = 16


def _params(semantics, vmem_bytes):
    return pltpu.CompilerParams(dimension_semantics=semantics,
                                vmem_limit_bytes=int(min(vmem_bytes, V7X_VMEM_BUDGET)))


def _rms(x, g):
    return x * lax.rsqrt(jnp.mean(x * x, axis=-1, keepdims=True) + RMS_EPS) * g


def _split3(x):
    hi = x.astype(BF16)
    r1 = x - hi.astype(F32)
    mid = r1.astype(BF16)
    lo = (r1 - mid.astype(F32)).astype(BF16)
    return hi, mid, lo


def _dot(a, b):
    return jnp.dot(a, b, preferred_element_type=F32)


def _dot_f32(a, b):
    a0, a1, a2 = _split3(a)
    b0, b1, b2 = _split3(b)
    return (_dot(a0, b0) + (_dot(a0, b1) + _dot(a1, b0))
            + (_dot(a0, b2) + _dot(a1, b1) + _dot(a2, b0)))


CONV_GROUPS = CONV_W // PROJ_TILE_N
N_Q_TILES = ATTN_W // PROJ_TILE_N
N_KV_TILES = 3
GATE_STEP = 0
Q_STEP0 = GATE_STEP + 1
KV_STEP0 = Q_STEP0 + N_Q_TILES
CONV_STEP0 = KV_STEP0 + N_KV_TILES
N_PROJ_STEPS = CONV_STEP0 + CONV_GROUPS
N_WEIGHT_TILES = 3 * CONV_GROUPS + N_Q_TILES + N_KV_TILES
KV_OUT_W = N_KV_TILES * KV_W
assert 2 * KV_W == PROJ_TILE_N and N_GATES % SUBLANES == 0


def _gate_weight(w):
    return w[:, N_WEIGHT_TILES * PROJ_TILE_N:].T.astype(BF16)


def _inproj_body(x_ref, g_ref, w_ref, wc_ref, wh_ref, wg_ref, cos_ref, sin_ref, cw_ref,
                 yc_ref, qt_ref, k_ref, kc_ref, vc_ref, vt_ref, zgt_ref,
                 a_scr, tail_scr, *, q_scale, blocks_per_seq):
    i = pl.program_id(0)
    j = pl.program_id(1)

    @pl.when(j == 0)
    def _():
        a_scr[...] = _rms(x_ref[...], g_ref[...]).astype(BF16)

    @pl.when((i == 0) & (j == 0))
    def _():
        tail_scr[...] = jnp.zeros(tail_scr.shape, F32)

    @pl.when(j == GATE_STEP)
    def _():
        zgt_ref[0] = lax.dot_general(wg_ref[...], a_scr[...], (((1,), (1,)), ((), ())),
                                     preferred_element_type=F32)

    chunks = [(c, c + PROJ_CHUNK) for c in range(0, PROJ_TILE_N, PROJ_CHUNK)]

    def z_cols(weight_ref, c0, c1):
        return _dot(a_scr[...], weight_ref[:, c0:c1])

    def rope(zh):
        cos = cos_ref[...]
        sin = sin_ref[...]
        heads = []
        for h in range(zh.shape[1] // HEAD_DIM):
            z1 = zh[:, h * HEAD_DIM:(h + 1) * HEAD_DIM]
            heads.append(z1 * cos + pltpu.roll(z1, HEAD_DIM // 2, axis=1) * sin)
        return jnp.concatenate(heads, axis=1)

    @pl.when(j >= CONV_STEP0)
    def _():
        group = j - CONV_STEP0
        for c0, c1 in chunks:
            u = z_cols(wc_ref, c0, c1) * z_cols(wh_ref, c0, c1)
            tail = tail_scr[group, :, c0:c1]
            tail = jnp.where(i % blocks_per_seq == 0, 0.0, tail)
            row = lax.broadcasted_iota(jnp.int32, u.shape, 0)
            t1 = tail[SUBLANES - 1:SUBLANES]
            t2 = tail[SUBLANES - 2:SUBLANES - 1]
            u1 = jnp.where(row == 0, t1, pltpu.roll(u, 1, axis=0))
            u2 = jnp.where(row == 0, t2, jnp.where(row == 1, t1, pltpu.roll(u, 2, axis=0)))
            w = cw_ref[:, c0:c1]
            yc_ref[:, c0:c1] = z_cols(w_ref, c0, c1) * (u2 * w[0:1] + u1 * w[1:2] + u * w[2:3])
            tail_scr[group, :, c0:c1] = u[u.shape[0] - SUBLANES:, :]

    @pl.when((j >= Q_STEP0) & (j < KV_STEP0))
    def _():
        for c0, c1 in chunks:
            qt_ref[0, c0:c1, :] = (rope(z_cols(w_ref, c0, c1)) * q_scale).T.astype(BF16)

    @pl.when((j >= KV_STEP0) & (j < CONV_STEP0))
    def _():
        k_ref[...] = rope(z_cols(w_ref, 0, KV_W)).astype(BF16)
        zv = z_cols(w_ref, KV_W, 2 * KV_W)
        vt_ref[0] = zv.T.astype(BF16)

        @pl.when(j == KV_STEP0)
        def _():
            for h in range(N_KV_HEADS):
                kc_ref[0, h] = k_ref[:, h * HEAD_DIM:(h + 1) * HEAD_DIM].astype(F32)
                vc_ref[0, h] = zv[:, h * HEAD_DIM:(h + 1) * HEAD_DIM].astype(BF16).astype(F32)


def _in_projection(x2, g, w_proj, w_gates, cos_t, sin_t, conv_w, B, seq):
    T, D = x2.shape
    tn = PROJ_TILE_N
    assert w_proj.shape[1] >= N_WEIGHT_TILES * tn
    tm = min(1024, seq)
    nb = seq // tm
    body = functools.partial(_inproj_body, q_scale=HEAD_DIM ** -0.5 * LOG2_E, blocks_per_seq=nb)
    vmem = 2 * (tm * D * 4 + 3 * D * tn * 2 + 2 * tm * HEAD_DIM * 4 + tm * tn * 4 + 4 * tm * tn * 2
                + N_GATES * tm * 4) + tm * D * 2 + 8 * tm * PROJ_CHUNK * 4

    def conv_group(j):
        return jnp.maximum(j - CONV_STEP0, 0)

    def main_tile(j):
        return jnp.where(j >= CONV_STEP0, j - CONV_STEP0,
                         jnp.maximum(j - Q_STEP0, 0) + 3 * CONV_GROUPS)

    def kv_tile(j):
        return jnp.clip(j - KV_STEP0, 0, N_KV_TILES - 1)

    return pl.pallas_call(
        body,
        grid=(T // tm, N_PROJ_STEPS),
        in_specs=[
            pl.BlockSpec((tm, D), lambda i, j: (i, 0)),
            pl.BlockSpec((1, D), lambda i, j: (0, 0)),
            pl.BlockSpec((D, tn), lambda i, j: (0, main_tile(j))),
            pl.BlockSpec((D, tn), lambda i, j: (0, CONV_GROUPS + conv_group(j))),
            pl.BlockSpec((D, tn), lambda i, j: (0, 2 * CONV_GROUPS + conv_group(j))),
            pl.BlockSpec((N_GATES, D), lambda i, j: (0, 0)),
            pl.BlockSpec((tm, HEAD_DIM), lambda i, j: (i % nb, 0)),
            pl.BlockSpec((tm, HEAD_DIM), lambda i, j: (i % nb, 0)),
            pl.BlockSpec((CONV_TAPS, tn), lambda i, j: (0, conv_group(j))),
        ],
        out_specs=[
            pl.BlockSpec((tm, tn), lambda i, j: (i, conv_group(j))),
            pl.BlockSpec((1, tn, tm), lambda i, j: (i // nb, jnp.clip(j - Q_STEP0, 0, N_Q_TILES - 1), i % nb)),
            pl.BlockSpec((tm, KV_W), lambda i, j: (i, kv_tile(j))),
            pl.BlockSpec((1, N_KV_HEADS, tm, HEAD_DIM), lambda i, j: (i // nb, 0, i % nb, 0)),
            pl.BlockSpec((1, N_KV_HEADS, tm, HEAD_DIM), lambda i, j: (i // nb, 0, i % nb, 0)),
            pl.BlockSpec((1, KV_W, tm), lambda i, j: (i // nb, kv_tile(j), i % nb)),
            pl.BlockSpec((1, N_GATES, tm), lambda i, j: (i // nb, 0, i % nb)),
        ],
        out_shape=[
            jax.ShapeDtypeStruct((T, CONV_W), F32),
            jax.ShapeDtypeStruct((B, ATTN_W, seq), BF16),
            jax.ShapeDtypeStruct((T, KV_OUT_W), BF16),
            jax.ShapeDtypeStruct((B, N_KV_HEADS, seq, HEAD_DIM), F32),
            jax.ShapeDtypeStruct((B, N_KV_HEADS, seq, HEAD_DIM), F32),
            jax.ShapeDtypeStruct((B, KV_OUT_W, seq), BF16),
            jax.ShapeDtypeStruct((B, N_GATES, seq), F32),
        ],
        scratch_shapes=[pltpu.VMEM((tm, D), BF16), pltpu.VMEM((CONV_GROUPS, SUBLANES, tn), F32)],
        compiler_params=_params(("arbitrary", "arbitrary"), vmem),
        name="in_projection",
    )(x2, g, w_proj, w_proj, w_proj, w_gates, cos_t, sin_t, conv_w)


def _compress_body(x_ref, pe_ref, w1_ref, w2_ref, o_ref, ot_ref, *, n_valid):
    n = x_ref.shape[2] // CMP_STRIDE
    x = jnp.concatenate([x_ref[0, 0, pl.ds(l, n, stride=CMP_STRIDE), :] for l in range(CMP_STRIDE)],
                        axis=1).astype(BF16)
    half = x.shape[1]
    w1 = w1_ref[0]
    w1_hi = w1.astype(BF16)
    w1_lo = (w1 - w1_hi.astype(F32)).astype(BF16)
    first = _dot(x, w1_hi[:half]) + _dot(x, w1_lo[:half])
    second = _dot(x, w1_hi[half:]) + _dot(x, w1_lo[half:])
    second = pltpu.roll(second, n - 1, axis=0)
    pe = jnp.broadcast_to(pe_ref[0], (SUBLANES, 2 * half))
    pe_term = _dot_f32(pe, w1)[0:1]
    h = first + second + pe_term
    out = _dot_f32(jax.nn.silu(h), w2_ref[0])
    row = lax.broadcasted_iota(jnp.int32, out.shape, 0)
    out = jnp.where(row < n_valid, out, 0.0)
    o_ref[0, 0] = out
    ot_ref[0, 0] = out.T


def _compress(kv, pe, w1, w2):
    two, BH, seq, d = kv.shape
    n = seq // CMP_STRIDE
    width = CMP_STRIDE * d
    body = functools.partial(_compress_body, n_valid=n - 1)
    vmem = 2 * (seq * d * 4 + 2 * width * 4 + 2 * width * d * 4 + d * d * 4 + n * d * 4) \
        + 8 * width * d * 4
    return pl.pallas_call(
        body,
        grid=(two, BH),
        in_specs=[
            pl.BlockSpec((1, 1, seq, d), lambda s, b: (s, b, 0, 0)),
            pl.BlockSpec((1, 1, 2 * width), lambda s, b: (s, 0, 0)),
            pl.BlockSpec((1, 2 * width, d), lambda s, b: (s, 0, 0)),
            pl.BlockSpec((1, d, d), lambda s, b: (s, 0, 0)),
        ],
        out_specs=[pl.BlockSpec((1, 1, n, d), lambda s, b: (s, b, 0, 0)),
                   pl.BlockSpec((1, 1, d, n), lambda s, b: (s, b, 0, 0))],
        out_shape=[jax.ShapeDtypeStruct((two, BH, n, d), F32),
                   jax.ShapeDtypeStruct((two, BH, d, n), F32)],
        compiler_params=_params(("parallel", "parallel"), vmem),
        name="compress",
    )(kv, pe, w1, w2)


def _select_blocks(imp, q0):
    n_sel = imp.shape[0]
    sel_i = lax.broadcasted_iota(jnp.int32, imp.shape, 0)
    sel_f = sel_i.astype(F32)
    cur = (q0 + lax.broadcasted_iota(jnp.int32, imp.shape, 1)) // SEL_BLOCK
    forced = (sel_i == 0) | (sel_i == cur) | (sel_i == cur - 1)
    v = jnp.where((sel_i > cur) | forced, -jnp.inf, imp)
    chosen = jnp.where(forced, 1.0, 0.0)
    for _ in range(min(N_SELECT, n_sel) - N_FORCED):
        top = jnp.max(v, axis=0, keepdims=True)
        first = jnp.min(jnp.where(v == top, sel_f, float(n_sel)), axis=0, keepdims=True)
        pick = sel_f == first
        chosen = jnp.where(pick, 1.0, chosen)
        v = jnp.where(pick, -jnp.inf, v)
    return jnp.where((chosen > 0.0) & (sel_i <= cur), 0.0, MASK_BIAS)


def _attn_body(qt_ref, zgt_ref, kc_ref, vct_ref, ovt_ref, ks_ref, vst_ref, e_ref, kw_ref, vwt_ref, wb_ref,
               y_ref, s_scr, m_scr, acc_scr, oc_scr, ow_scr, *, tk, span):
    q0 = pl.program_id(2) * ATTN_Q
    qt = [qt_ref[0, g * HEAD_DIM:(g + 1) * HEAD_DIM, :] for g in range(GROUP)]
    q_all = jnp.concatenate(qt, axis=1)

    def lane_time(shape):
        return q0 + (lax.broadcasted_iota(jnp.int32, shape, 1) & (ATTN_Q - 1))

    s = _dot(kc_ref[0, 0].astype(BF16), q_all)
    cmp_end = lax.broadcasted_iota(jnp.int32, s.shape, 0) * CMP_STRIDE + (CMP_BLOCK - 1)
    s = jnp.where(cmp_end <= lane_time(s.shape), s, -jnp.inf)
    m = jnp.max(s, axis=0, keepdims=True)
    m = jnp.where(m == -jnp.inf, 0.0, m)
    e = jnp.exp2(s - m)
    p = e / jnp.maximum(jnp.sum(e, axis=0, keepdims=True), 1e-30)
    oc_scr[...] = _dot(vct_ref[0, 0].astype(BF16), p.astype(BF16))
    pg = p[:, 0:ATTN_Q]
    for g in range(1, GROUP):
        pg = pg + p[:, g * ATTN_Q:(g + 1) * ATTN_Q]
    ovt = ovt_ref[...]
    p0, p1, p2 = _split3(pg)
    bias = _select_blocks(_dot(ovt, p0) + _dot(ovt, p1) + _dot(ovt, p2), q0).astype(BF16)

    w_start = pl.multiple_of(jnp.maximum(q0 - WINDOW, 0), ATTN_Q)
    sw = _dot(kw_ref[0, pl.ds(w_start, span), :], q_all)
    sw = sw + jnp.concatenate([wb_ref[0]] * GROUP, axis=1)
    ew = jnp.exp2(sw - jnp.max(sw, axis=0, keepdims=True))
    vw = jnp.concatenate([vwt_ref[0, :, pl.ds(w_start, span)], jnp.ones((ONES_ROWS, span), BF16)], axis=0)
    ow = _dot(vw, ew.astype(BF16))
    ow_scr[...] = ow[0:HEAD_DIM] / ow[HEAD_DIM:HEAD_DIM + 1]

    qx = jnp.concatenate([jnp.concatenate([qt[g], bias], axis=0) for g in range(GROUP)], axis=1)

    def scores(j, slot):
        start = pl.multiple_of(j * tk, tk)
        kx = jnp.concatenate([ks_ref[0, pl.ds(start, tk), :], e_ref[pl.ds(start, tk), :]], axis=1)
        s_scr[slot] = _dot(kx, qx)

    def update(j, slot, causal):
        start = pl.multiple_of(j * tk, tk)
        vt = jnp.concatenate([vst_ref[0, :, pl.ds(start, tk)], jnp.ones((ONES_ROWS, tk), BF16)], axis=0)
        sc = s_scr[slot]
        if causal:
            key = start + lax.broadcasted_iota(jnp.int32, sc.shape, 0)
            sc = jnp.where(key <= lane_time(sc.shape), sc, -jnp.inf)
        m_old = m_scr[...]
        m_new = jnp.maximum(m_old, jnp.max(sc, axis=0, keepdims=True))
        alpha = jnp.exp2(m_old - m_new)
        pj = jnp.exp2(sc - m_new)
        m_scr[...] = m_new
        acc_scr[...] = alpha * acc_scr[...] + _dot(vt, pj.astype(BF16))

    m_scr[...] = jnp.full(m_scr.shape, -jnp.inf, F32)
    acc_scr[...] = jnp.zeros(acc_scr.shape, F32)
    scores(0, 0)
    last = q0 // tk

    @pl.loop(0, last // 2)
    def _(i):
        scores(2 * i + 1, 1)
        update(2 * i, 0, False)
        scores(2 * i + 2, 0)
        update(2 * i + 1, 1, False)

    @pl.when(last % 2 == 1)
    def _():
        scores(last, 1)
        update(last - 1, 0, False)
        update(last, 1, True)

    @pl.when(last % 2 == 0)
    def _():
        update(last, 0, True)

    o_sel = acc_scr[0:HEAD_DIM, :] / acc_scr[HEAD_DIM:HEAD_DIM + 1, :]
    gates = jax.nn.sigmoid(zgt_ref[0])
    hk = pl.program_id(1)

    def gate(g, branch):
        row = gates[g * N_BRANCH + branch:g * N_BRANCH + branch + 1]
        for other in range(1, N_KV_HEADS):
            r = (other * GROUP + g) * N_BRANCH + branch
            row = jnp.where(hk == other, gates[r:r + 1], row)
        return row

    for g in range(GROUP):
        sl = slice(g * ATTN_Q, (g + 1) * ATTN_Q)
        y = (gate(g, 0) * oc_scr[:, sl] + gate(g, 1) * o_sel[:, sl]
             + gate(g, 2) * ow_scr[:, sl])
        y_ref[0, :, g * HEAD_DIM:(g + 1) * HEAD_DIM] = y.T


def _attention(qt, zgt, kvc, kvct, ovt, kk, vt, onehot, B, seq):
    n_cmp = kvc.shape[2]
    n_sel = ovt.shape[0]
    GW = GROUP * HEAD_DIM
    GQ = GROUP * ATTN_Q
    VR = HEAD_DIM + ONES_ROWS
    tk = min(512, seq)
    span = WINDOW + ATTN_Q
    assert seq >= span
    body = functools.partial(_attn_body, tk=tk, span=span)
    resident = 4 * seq * HEAD_DIM * 2 + seq * n_sel * 2
    vmem = 2 * (GW * ATTN_Q * 2 + N_GATES * ATTN_Q * 4 + 2 * n_cmp * HEAD_DIM * 4 + n_sel * n_cmp * 2
                + resident + ATTN_Q * GW * 4) \
        + (2 * tk + VR + 2 * HEAD_DIM + SUBLANES) * GQ * 4 + 8 * (n_cmp + span) * GQ * 4
    return pl.pallas_call(
        body,
        grid=(B, N_KV_HEADS, seq // ATTN_Q),
        in_specs=[
            pl.BlockSpec((1, GW, ATTN_Q), lambda b, h, i: (b, h, i)),
            pl.BlockSpec((1, N_GATES, ATTN_Q), lambda b, h, i: (b, 0, i)),
            pl.BlockSpec((1, 1, n_cmp, HEAD_DIM), lambda b, h, i: (0, b * N_KV_HEADS + h, 0, 0)),
            pl.BlockSpec((1, 1, HEAD_DIM, n_cmp), lambda b, h, i: (1, b * N_KV_HEADS + h, 0, 0)),
            pl.BlockSpec((n_sel, n_cmp), lambda b, h, i: (0, 0)),
            pl.BlockSpec((1, seq, HEAD_DIM), lambda b, h, i: (b, 0, N_KV_HEADS + h)),
            pl.BlockSpec((1, HEAD_DIM, seq), lambda b, h, i: (b, N_KV_HEADS + h, 0)),
            pl.BlockSpec((seq, n_sel), lambda b, h, i: (0, 0)),
            pl.BlockSpec((1, seq, HEAD_DIM), lambda b, h, i: (b, 0, 2 * N_KV_HEADS + h)),
            pl.BlockSpec((1, HEAD_DIM, seq), lambda b, h, i: (b, 2 * N_KV_HEADS + h, 0)),
            pl.BlockSpec((1, span, ATTN_Q), lambda b, h, i: (jnp.minimum(i, WINDOW // ATTN_Q), 0, 0)),
        ],
        out_specs=pl.BlockSpec((1, ATTN_Q, GW), lambda b, h, i: (b, i, h)),
        out_shape=jax.ShapeDtypeStruct((B, seq, ATTN_W), F32),
        scratch_shapes=[pltpu.VMEM((2, tk, GQ), F32),
                        pltpu.VMEM((1, GQ), F32),
                        pltpu.VMEM((VR, GQ), F32),
                        pltpu.VMEM((HEAD_DIM, GQ), F32),
                        pltpu.VMEM((HEAD_DIM, GQ), F32)],
        compiler_params=_params(("parallel", "parallel", "arbitrary"), vmem),
        name="attention",
    )(qt, zgt, kvc, kvct, ovt, kk, vt, onehot, kk, vt, _window_bias(span))


def _outproj_body(yc_ref, ya_ref, gc_ref, ga_ref, w_ref, x_ref, h_ref, a_scr):
    @pl.when(pl.program_id(1) == 0)
    def _():
        a_scr[:, 0:CONV_W] = _rms(yc_ref[...], gc_ref[...]).astype(BF16)
        a_scr[:, CONV_W:] = _rms(ya_ref[...], ga_ref[...]).astype(BF16)

    h_ref[...] = x_ref[...] + _dot(a_scr[...], w_ref[...])


def _out_projection(yc, ya, g_conv, g_attn, w, x2):
    T, D = x2.shape
    K = w.shape[0]
    tm = min(1024, T)
    tn = min(1024, D)
    vmem = 2 * (tm * CONV_W * 4 + tm * ATTN_W * 4 + K * tn * 2 + 2 * tm * tn * 4) + tm * K * 2 \
        + 3 * tm * ATTN_W * 4
    return pl.pallas_call(
        _outproj_body,
        grid=(T // tm, D // tn),
        in_specs=[
            pl.BlockSpec((tm, CONV_W), lambda i, j: (i, 0)),
            pl.BlockSpec((tm, ATTN_W), lambda i, j: (i, 0)),
            pl.BlockSpec((1, CONV_W), lambda i, j: (0, 0)),
            pl.BlockSpec((1, ATTN_W), lambda i, j: (0, 0)),
            pl.BlockSpec((K, tn), lambda i, j: (0, j)),
            pl.BlockSpec((tm, tn), lambda i, j: (i, j)),
        ],
        out_specs=pl.BlockSpec((tm, tn), lambda i, j: (i, j)),
        out_shape=jax.ShapeDtypeStruct((T, D), F32),
        scratch_shapes=[pltpu.VMEM((tm, K), BF16)],
        compiler_params=_params(("parallel", "arbitrary"), vmem),
        name="out_projection",
    )(yc, ya, g_conv, g_attn, w, x2)


def _ffn_body(h_ref, g_ref, wg_ref, wu_ref, wd_ref, gf_ref, o_ref, a_scr, *, final_norm, tn):
    j = pl.program_id(1)

    @pl.when(j == 0)
    def _():
        a_scr[...] = _rms(h_ref[...], g_ref[...]).astype(BF16)

    a = a_scr[...]
    u = (jax.nn.silu(_dot(a, wg_ref[...])) * _dot(a, wu_ref[...])).astype(BF16)
    D = o_ref.shape[1]

    @pl.when(j == 0)
    def _():
        for n in range(D // tn):
            o_ref[:, n * tn:(n + 1) * tn] = _dot(u, wd_ref[:, n * tn:(n + 1) * tn])

    @pl.when(j > 0)
    def _():
        for n in range(D // tn):
            o_ref[:, n * tn:(n + 1) * tn] += _dot(u, wd_ref[:, n * tn:(n + 1) * tn])

    @pl.when(j == pl.num_programs(1) - 1)
    def _():
        out = h_ref[...] + o_ref[...]
        o_ref[...] = _rms(out, gf_ref[...]) if final_norm else out


def _ffn(h, g, wg, wu, wd, g_final, final_norm):
    T, D = h.shape
    F = wg.shape[1]
    tm = min(1024, T)
    tf = 512
    tn = 512
    assert F % tf == 0 and D % tn == 0
    body = functools.partial(_ffn_body, final_norm=final_norm, tn=tn)
    vmem = tm * D * 4 + 2 * (3 * D * tf * 2 + tm * D * 4) + tm * D * 2 + 6 * tm * tf * 4 + 2 * tm * tn * 4
    return pl.pallas_call(
        body,
        grid=(T // tm, F // tf),
        in_specs=[
            pl.BlockSpec((tm, D), lambda i, j: (i, 0), pipeline_mode=pl.Buffered(1)),
            pl.BlockSpec((1, D), lambda i, j: (0, 0)),
            pl.BlockSpec((D, tf), lambda i, j: (0, j)),
            pl.BlockSpec((D, tf), lambda i, j: (0, j)),
            pl.BlockSpec((tf, D), lambda i, j: (j, 0)),
            pl.BlockSpec((1, D), lambda i, j: (0, 0)),
        ],
        out_specs=pl.BlockSpec((tm, D), lambda i, j: (i, 0)),
        out_shape=jax.ShapeDtypeStruct((T, D), F32),
        scratch_shapes=[pltpu.VMEM((tm, D), BF16)],
        compiler_params=_params(("parallel", "arbitrary"), vmem),
        name="ffn",
    )(h, g, wg, wu, wd, g_final)


def _rope_tables(seq):
    half = HEAD_DIM // 2
    inv = 1.0 / (ROPE_THETA ** (jnp.arange(half, dtype=F32) / half))
    ang = jnp.arange(seq).astype(F32)[:, None] * inv[None, :]
    cos = jnp.cos(ang)
    sin = jnp.sin(ang)
    return jnp.concatenate([cos, cos], axis=1), jnp.concatenate([-sin, sin], axis=1)


def _overlap_t(n_sel, n_cmp):
    cs = np.arange(n_cmp)[None, :] * CMP_STRIDE
    ss = np.arange(n_sel)[:, None] * SEL_BLOCK
    return jnp.asarray(((cs < ss + SEL_BLOCK) & (cs + CMP_BLOCK > ss)).astype(np.float32), dtype=BF16)


def _window_bias(span):
    r = np.arange(span)[None, :, None]
    c = np.arange(ATTN_Q)[None, None, :]
    off = np.minimum(np.arange(WINDOW // ATTN_Q + 1) * ATTN_Q, WINDOW)[:, None, None]
    valid = (r <= off + c) & (r > off + c - WINDOW)
    return jnp.asarray(np.where(valid, 0.0, -np.inf).astype(np.float32))


def _block_onehot(seq, n_sel):
    return jnp.asarray((np.arange(seq)[:, None] // SEL_BLOCK == np.arange(n_sel)[None, :])
                       .astype(np.float32), dtype=BF16)


def kernel(x, norm_mix, w_in, conv_w, cmp_pe_k, cmp_w1_k, cmp_w2_k, cmp_pe_v, cmp_w1_v, cmp_w2_v,
           norm_conv_out, norm_attn_out, w_out, norm_ffn, w_gate, w_up, w_down, norm_final):
    B, S, D = x.shape
    T = B * S
    depth = norm_mix.shape[0]
    assert S % ATTN_Q == 0 and S % CMP_STRIDE == 0
    n_chunks = S // CMP_STRIDE
    n_sel = S // SEL_BLOCK
    cos_t, sin_t = _rope_tables(S)
    ovt = _overlap_t(n_sel, n_chunks)
    onehot = _block_onehot(S, n_sel)

    h = x.reshape(T, D)
    for l in range(depth):
        y_conv, qt, kk, kc, vc, vt, zgt = _in_projection(h, norm_mix[l][None], w_in[l].astype(BF16),
                                                          _gate_weight(w_in[l]), cos_t, sin_t, conv_w[l], B, S)

        kvc_in = jnp.stack([kc, vc]).reshape(2, B * N_KV_HEADS, S, HEAD_DIM)
        pe = jnp.stack([cmp_pe_k[l], cmp_pe_v[l]]).reshape(2, 1, CMP_BLOCK * HEAD_DIM)
        kvc, kvct = _compress(kvc_in, pe, jnp.stack([cmp_w1_k[l], cmp_w1_v[l]]),
                              jnp.stack([cmp_w2_k[l], cmp_w2_v[l]]))

        y_attn = _attention(qt, zgt, kvc, kvct, ovt, kk.reshape(B, S, KV_OUT_W), vt, onehot, B, S)

        h = _out_projection(y_conv, y_attn.reshape(T, ATTN_W), norm_conv_out[l][None],
                            norm_attn_out[l][None], w_out[l].astype(BF16), h)

        h = _ffn(h, norm_ffn[l][None], w_gate[l].astype(BF16), w_up[l].astype(BF16),
                 w_down[l].astype(BF16), norm_final[None], l == depth - 1)
    return h.reshape(B, S, D)
```

```python
import functools

import jax
import jax.numpy as jnp
import numpy as np
from jax import lax
from jax.experimental import pallas as pl
from jax.experimental.pallas import tpu as pltpu

F32 = jnp.float32
BF16 = jnp.bfloat16

HEAD_DIM = 128
N_HEADS = 8
N_KV_HEADS = 2
GROUP = N_HEADS // N_KV_HEADS
N_BRANCH = 3
CONV_TAPS = 3
CMP_BLOCK = 32
CMP_STRIDE = 16
SEL_BLOCK = 64
N_SELECT = 16
N_FORCED = 3
WINDOW = 512
ATTN_Q = 256
ROPE_THETA = 10000.0
RMS_EPS = 1e-6

CONV_W = 1024
ATTN_W = N_HEADS * HEAD_DIM
KV_W = N_KV_HEADS * HEAD_DIM
N_GATES = N_HEADS * N_BRANCH

V7X_VMEM_BYTES = 64 * 1024 * 1024
V7X_VMEM_BUDGET = V7X_VMEM_BYTES - 6 * 1024 * 1024
LANES = 128
SUBLANES = 8
PROJ_TILE_N = 512
PROJ_CHUNK = 256
FFN_EPILOGUE_ROWS = 256

MASK_BIAS = -float(2 ** 30)
LOG2_E = 1.4426950408889634
ONES_ROWS = 16


def _params(semantics, vmem_bytes):
    return pltpu.CompilerParams(dimension_semantics=semantics,
                                vmem_limit_bytes=int(min(vmem_bytes, V7X_VMEM_BUDGET)))


def _rms(x, g):
    return x * lax.rsqrt(jnp.mean(x * x, axis=-1, keepdims=True) + RMS_EPS) * g


def _split3(x):
    hi = x.astype(BF16)
    r1 = x - hi.astype(F32)
    mid = r1.astype(BF16)
    lo = (r1 - mid.astype(F32)).astype(BF16)
    return hi, mid, lo


def _dot(a, b):
    return jnp.dot(a, b, preferred_element_type=F32)


def _dot_f32(a, b):
    a0, a1, a2 = _split3(a)
    b0, b1, b2 = _split3(b)
    return (_dot(a0, b0) + (_dot(a0, b1) + _dot(a1, b0))
            + (_dot(a0, b2) + _dot(a1, b1) + _dot(a2, b0)))


CONV_GROUPS = CONV_W // PROJ_TILE_N
N_Q_TILES = ATTN_W // PROJ_TILE_N
N_KV_TILES = 3
GATE_STEP = 0
Q_STEP0 = GATE_STEP + 1
KV_STEP0 = Q_STEP0 + N_Q_TILES
CONV_STEP0 = KV_STEP0 + N_KV_TILES
N_PROJ_STEPS = CONV_STEP0 + CONV_GROUPS
N_WEIGHT_TILES = 3 * CONV_GROUPS + N_Q_TILES + N_KV_TILES
KV_OUT_W = N_KV_TILES * KV_W
assert 2 * KV_W == PROJ_TILE_N and N_GATES % SUBLANES == 0


def _gate_weight(w):
    return w[:, N_WEIGHT_TILES * PROJ_TILE_N:].T.astype(BF16)


def _inproj_body(x_ref, g_ref, w_ref, wc_ref, wh_ref, wg_ref, cos_ref, sin_ref, cw_ref,
                 yc_ref, qt_ref, k_ref, kc_ref, vc_ref, vt_ref, zgt_ref,
                 a_scr, tail_scr, *, q_scale, blocks_per_seq):
    i = pl.program_id(0)
    j = pl.program_id(1)

    @pl.when(j == 0)
    def _():
        a_scr[...] = _rms(x_ref[...], g_ref[...]).astype(BF16)

    @pl.when((i == 0) & (j == 0))
    def _():
        tail_scr[...] = jnp.zeros(tail_scr.shape, F32)

    @pl.when(j == GATE_STEP)
    def _():
        zgt_ref[0] = lax.dot_general(wg_ref[...], a_scr[...], (((1,), (1,)), ((), ())),
                                     preferred_element_type=F32)

    chunks = [(c, c + PROJ_CHUNK) for c in range(0, PROJ_TILE_N, PROJ_CHUNK)]

    def z_cols(weight_ref, c0, c1):
        return _dot(a_scr[...], weight_ref[:, c0:c1])

    def rope(zh):
        cos = cos_ref[...]
        sin = sin_ref[...]
        heads = []
        for h in range(zh.shape[1] // HEAD_DIM):
            z1 = zh[:, h * HEAD_DIM:(h + 1) * HEAD_DIM]
            heads.append(z1 * cos + pltpu.roll(z1, HEAD_DIM // 2, axis=1) * sin)
        return jnp.concatenate(heads, axis=1)

    @pl.when(j >= CONV_STEP0)
    def _():
        group = j - CONV_STEP0
        for c0, c1 in chunks:
            u = z_cols(wc_ref, c0, c1) * z_cols(wh_ref, c0, c1)
            tail = tail_scr[group, :, c0:c1]
            tail = jnp.where(i % blocks_per_seq == 0, 0.0, tail)
            row = lax.broadcasted_iota(jnp.int32, u.shape, 0)
            t1 = tail[SUBLANES - 1:SUBLANES]
            t2 = tail[SUBLANES - 2:SUBLANES - 1]
            u1 = jnp.where(row == 0, t1, pltpu.roll(u, 1, axis=0))
            u2 = jnp.where(row == 0, t2, jnp.where(row == 1, t1, pltpu.roll(u, 2, axis=0)))
            w = cw_ref[:, c0:c1]
            yc_ref[:, c0:c1] = z_cols(w_ref, c0, c1) * (u2 * w[0:1] + u1 * w[1:2] + u * w[2:3])
            tail_scr[group, :, c0:c1] = u[u.shape[0] - SUBLANES:, :]

    @pl.when((j >= Q_STEP0) & (j < KV_STEP0))
    def _():
        for c0, c1 in chunks:
            qt_ref[0, c0:c1, :] = (rope(z_cols(w_ref, c0, c1)) * q_scale).T.astype(BF16)

    @pl.when((j >= KV_STEP0) & (j < CONV_STEP0))
    def _():
        k_ref[...] = rope(z_cols(w_ref, 0, KV_W)).astype(BF16)
        zv = z_cols(w_ref, KV_W, 2 * KV_W)
        vt_ref[0] = zv.T.astype(BF16)

        @pl.when(j == KV_STEP0)
        def _():
            for h in range(N_KV_HEADS):
                kc_ref[0, h] = k_ref[:, h * HEAD_DIM:(h + 1) * HEAD_DIM].astype(F32)
                vc_ref[0, h] = zv[:, h * HEAD_DIM:(h + 1) * HEAD_DIM].astype(BF16).astype(F32)


def _in_projection(x2, g, w_proj, w_gates, cos_t, sin_t, conv_w, B, seq):
    T, D = x2.shape
    tn = PROJ_TILE_N
    assert w_proj.shape[1] >= N_WEIGHT_TILES * tn
    tm = min(1024, seq)
    nb = seq // tm
    body = functools.partial(_inproj_body, q_scale=HEAD_DIM ** -0.5 * LOG2_E, blocks_per_seq=nb)
    vmem = 2 * (tm * D * 4 + 3 * D * tn * 2 + 2 * tm * HEAD_DIM * 4 + tm * tn * 4 + 4 * tm * tn * 2
                + N_GATES * tm * 4) + tm * D * 2 + 8 * tm * PROJ_CHUNK * 4

    def conv_group(j):
        return jnp.maximum(j - CONV_STEP0, 0)

    def main_tile(j):
        return jnp.where(j >= CONV_STEP0, j - CONV_STEP0,
                         jnp.maximum(j - Q_STEP0, 0) + 3 * CONV_GROUPS)

    def kv_tile(j):
        return jnp.clip(j - KV_STEP0, 0, N_KV_TILES - 1)

    return pl.pallas_call(
        body,
        grid=(T // tm, N_PROJ_STEPS),
        in_specs=[
            pl.BlockSpec((tm, D), lambda i, j: (i, 0)),
            pl.BlockSpec((1, D), lambda i, j: (0, 0)),
            pl.BlockSpec((D, tn), lambda i, j: (0, main_tile(j))),
            pl.BlockSpec((D, tn), lambda i, j: (0, CONV_GROUPS + conv_group(j))),
            pl.BlockSpec((D, tn), lambda i, j: (0, 2 * CONV_GROUPS + conv_group(j))),
            pl.BlockSpec((N_GATES, D), lambda i, j: (0, 0)),
            pl.BlockSpec((tm, HEAD_DIM), lambda i, j: (i % nb, 0)),
            pl.BlockSpec((tm, HEAD_DIM), lambda i, j: (i % nb, 0)),
            pl.BlockSpec((CONV_TAPS, tn), lambda i, j: (0, conv_group(j))),
        ],
        out_specs=[
            pl.BlockSpec((tm, tn), lambda i, j: (i, conv_group(j))),
            pl.BlockSpec((1, tn, tm), lambda i, j: (i // nb, jnp.clip(j - Q_STEP0, 0, N_Q_TILES - 1), i % nb)),
            pl.BlockSpec((tm, KV_W), lambda i, j: (i, kv_tile(j))),
            pl.BlockSpec((1, N_KV_HEADS, tm, HEAD_DIM), lambda i, j: (i // nb, 0, i % nb, 0)),
            pl.BlockSpec((1, N_KV_HEADS, tm, HEAD_DIM), lambda i, j: (i // nb, 0, i % nb, 0)),
            pl.BlockSpec((1, KV_W, tm), lambda i, j: (i // nb, kv_tile(j), i % nb)),
            pl.BlockSpec((1, N_GATES, tm), lambda i, j: (i // nb, 0, i % nb)),
        ],
        out_shape=[
            jax.ShapeDtypeStruct((T, CONV_W), F32),
            jax.ShapeDtypeStruct((B, ATTN_W, seq), BF16),
            jax.ShapeDtypeStruct((T, KV_OUT_W), BF16),
            jax.ShapeDtypeStruct((B, N_KV_HEADS, seq, HEAD_DIM), F32),
            jax.ShapeDtypeStruct((B, N_KV_HEADS, seq, HEAD_DIM), F32),
            jax.ShapeDtypeStruct((B, KV_OUT_W, seq), BF16),
            jax.ShapeDtypeStruct((B, N_GATES, seq), F32),
        ],
        scratch_shapes=[pltpu.VMEM((tm, D), BF16), pltpu.VMEM((CONV_GROUPS, SUBLANES, tn), F32)],
        compiler_params=_params(("arbitrary", "arbitrary"), vmem),
        name="in_projection",
    )(x2, g, w_proj, w_proj, w_proj, w_gates, cos_t, sin_t, conv_w)


def _compress_body(x_ref, pe_ref, w1_ref, w2_ref, o_ref, ot_ref, *, n_valid):
    n = x_ref.shape[2] // CMP_STRIDE
    x = jnp.concatenate([x_ref[0, 0, pl.ds(l, n, stride=CMP_STRIDE), :] for l in range(CMP_STRIDE)],
                        axis=1).astype(BF16)
    half = x.shape[1]
    w1 = w1_ref[0]
    w1_hi = w1.astype(BF16)
    w1_lo = (w1 - w1_hi.astype(F32)).astype(BF16)
    first = _dot(x, w1_hi[:half]) + _dot(x, w1_lo[:half])
    second = _dot(x, w1_hi[half:]) + _dot(x, w1_lo[half:])
    second = pltpu.roll(second, n - 1, axis=0)
    pe = jnp.broadcast_to(pe_ref[0], (SUBLANES, 2 * half))
    pe_term = _dot_f32(pe, w1)[0:1]
    h = first + second + pe_term
    out = _dot_f32(jax.nn.silu(h), w2_ref[0])
    row = lax.broadcasted_iota(jnp.int32, out.shape, 0)
    out = jnp.where(row < n_valid, out, 0.0)
    o_ref[0, 0] = out
    ot_ref[0, 0] = out.T


def _compress(kv, pe, w1, w2):
    two, BH, seq, d = kv.shape
    n = seq // CMP_STRIDE
    width = CMP_STRIDE * d
    body = functools.partial(_compress_body, n_valid=n - 1)
    vmem = 2 * (seq * d * 4 + 2 * width * 4 + 2 * width * d * 4 + d * d * 4 + n * d * 4) \
        + 8 * width * d * 4
    return pl.pallas_call(
        body,
        grid=(two, BH),
        in_specs=[
            pl.BlockSpec((1, 1, seq, d), lambda s, b: (s, b, 0, 0)),
            pl.BlockSpec((1, 1, 2 * width), lambda s, b: (s, 0, 0)),
            pl.BlockSpec((1, 2 * width, d), lambda s, b: (s, 0, 0)),
            pl.BlockSpec((1, d, d), lambda s, b: (s, 0, 0)),
        ],
        out_specs=[pl.BlockSpec((1, 1, n, d), lambda s, b: (s, b, 0, 0)),
                   pl.BlockSpec((1, 1, d, n), lambda s, b: (s, b, 0, 0))],
        out_shape=[jax.ShapeDtypeStruct((two, BH, n, d), F32),
                   jax.ShapeDtypeStruct((two, BH, d, n), F32)],
        compiler_params=_params(("parallel", "parallel"), vmem),
        name="compress",
    )(kv, pe, w1, w2)


def _select_blocks(imp, q0):
    n_sel = imp.shape[0]
    sel_i = lax.broadcasted_iota(jnp.int32, imp.shape, 0)
    sel_f = sel_i.astype(F32)
    cur = (q0 + lax.broadcasted_iota(jnp.int32, imp.shape, 1)) // SEL_BLOCK
    forced = (sel_i == 0) | (sel_i == cur) | (sel_i == cur - 1)
    v = jnp.where((sel_i > cur) | forced, -jnp.inf, imp)
    chosen = jnp.where(forced, 1.0, 0.0)
    for _ in range(min(N_SELECT, n_sel) - N_FORCED):
        top = jnp.max(v, axis=0, keepdims=True)
        first = jnp.min(jnp.where(v == top, sel_f, float(n_sel)), axis=0, keepdims=True)
        pick = sel_f == first
        chosen = jnp.where(pick, 1.0, chosen)
        v = jnp.where(pick, -jnp.inf, v)
    return jnp.where((chosen > 0.0) & (sel_i <= cur), 0.0, MASK_BIAS)


def _attn_body(qt_ref, zgt_ref, kc_ref, vct_ref, ovt_ref, ks_ref, vst_ref, e_ref, kw_ref, vwt_ref, wb_ref,
               y_ref, s_scr, m_scr, acc_scr, oc_scr, ow_scr, *, tk, span):
    q0 = pl.program_id(2) * ATTN_Q
    qt = [qt_ref[0, g * HEAD_DIM:(g + 1) * HEAD_DIM, :] for g in range(GROUP)]
    q_all = jnp.concatenate(qt, axis=1)

    def lane_time(shape):
        return q0 + (lax.broadcasted_iota(jnp.int32, shape, 1) & (ATTN_Q - 1))

    s = _dot(kc_ref[0, 0].astype(BF16), q_all)
    cmp_end = lax.broadcasted_iota(jnp.int32, s.shape, 0) * CMP_STRIDE + (CMP_BLOCK - 1)
    s = jnp.where(cmp_end <= lane_time(s.shape), s, -jnp.inf)
    m = jnp.max(s, axis=0, keepdims=True)
    m = jnp.where(m == -jnp.inf, 0.0, m)
    e = jnp.exp2(s - m)
    p = e / jnp.maximum(jnp.sum(e, axis=0, keepdims=True), 1e-30)
    oc_scr[...] = _dot(vct_ref[0, 0].astype(BF16), p.astype(BF16))
    pg = p[:, 0:ATTN_Q]
    for g in range(1, GROUP):
        pg = pg + p[:, g * ATTN_Q:(g + 1) * ATTN_Q]
    ovt = ovt_ref[...]
    p0, p1, p2 = _split3(pg)
    bias = _select_blocks(_dot(ovt, p0) + _dot(ovt, p1) + _dot(ovt, p2), q0).astype(BF16)

    w_start = pl.multiple_of(jnp.maximum(q0 - WINDOW, 0), ATTN_Q)
    sw = _dot(kw_ref[0, pl.ds(w_start, span), :], q_all)
    sw = sw + jnp.concatenate([wb_ref[0]] * GROUP, axis=1)
    ew = jnp.exp2(sw - jnp.max(sw, axis=0, keepdims=True))
    vw = jnp.concatenate([vwt_ref[0, :, pl.ds(w_start, span)], jnp.ones((ONES_ROWS, span), BF16)], axis=0)
    ow = _dot(vw, ew.astype(BF16))
    ow_scr[...] = ow[0:HEAD_DIM] / ow[HEAD_DIM:HEAD_DIM + 1]

    qx = jnp.concatenate([jnp.concatenate([qt[g], bias], axis=0) for g in range(GROUP)], axis=1)

    def scores(j, slot):
        start = pl.multiple_of(j * tk, tk)
        kx = jnp.concatenate([ks_ref[0, pl.ds(start, tk), :], e_ref[pl.ds(start, tk), :]], axis=1)
        s_scr[slot] = _dot(kx, qx)

    def update(j, slot, causal):
        start = pl.multiple_of(j * tk, tk)
        vt = jnp.concatenate([vst_ref[0, :, pl.ds(start, tk)], jnp.ones((ONES_ROWS, tk), BF16)], axis=0)
        sc = s_scr[slot]
        if causal:
            key = start + lax.broadcasted_iota(jnp.int32, sc.shape, 0)
            sc = jnp.where(key <= lane_time(sc.shape), sc, -jnp.inf)
        m_old = m_scr[...]
        m_new = jnp.maximum(m_old, jnp.max(sc, axis=0, keepdims=True))
        alpha = jnp.exp2(m_old - m_new)
        pj = jnp.exp2(sc - m_new)
        m_scr[...] = m_new
        acc_scr[...] = alpha * acc_scr[...] + _dot(vt, pj.astype(BF16))

    m_scr[...] = jnp.full(m_scr.shape, -jnp.inf, F32)
    acc_scr[...] = jnp.zeros(acc_scr.shape, F32)
    scores(0, 0)
    last = q0 // tk

    @pl.loop(0, last // 2)
    def _(i):
        scores(2 * i + 1, 1)
        update(2 * i, 0, False)
        scores(2 * i + 2, 0)
        update(2 * i + 1, 1, False)

    @pl.when(last % 2 == 1)
    def _():
        scores(last, 1)
        update(last - 1, 0, False)
        update(last, 1, True)

    @pl.when(last % 2 == 0)
    def _():
        update(last, 0, True)

    o_sel = acc_scr[0:HEAD_DIM, :] / acc_scr[HEAD_DIM:HEAD_DIM + 1, :]
    gates = jax.nn.sigmoid(zgt_ref[0])
    hk = pl.program_id(1)

    def gate(g, branch):
        row = gates[g * N_BRANCH + branch:g * N_BRANCH + branch + 1]
        for other in range(1, N_KV_HEADS):
            r = (other * GROUP + g) * N_BRANCH + branch
            row = jnp.where(hk == other, gates[r:r + 1], row)
        return row

    for g in range(GROUP):
        sl = slice(g * ATTN_Q, (g + 1) * ATTN_Q)
        y = (gate(g, 0) * oc_scr[:, sl] + gate(g, 1) * o_sel[:, sl]
             + gate(g, 2) * ow_scr[:, sl])
        y_ref[0, :, g * HEAD_DIM:(g + 1) * HEAD_DIM] = y.T


def _attention(qt, zgt, kvc, kvct, ovt, kk, vt, onehot, B, seq):
    n_cmp = kvc.shape[2]
    n_sel = ovt.shape[0]
    GW = GROUP * HEAD_DIM
    GQ = GROUP * ATTN_Q
    VR = HEAD_DIM + ONES_ROWS
    tk = min(512, seq)
    span = WINDOW + ATTN_Q
    assert seq >= span
    body = functools.partial(_attn_body, tk=tk, span=span)
    resident = 4 * seq * HEAD_DIM * 2 + seq * n_sel * 2
    vmem = 2 * (GW * ATTN_Q * 2 + N_GATES * ATTN_Q * 4 + 2 * n_cmp * HEAD_DIM * 4 + n_sel * n_cmp * 2
                + resident + ATTN_Q * GW * 4) \
        + (2 * tk + VR + 2 * HEAD_DIM + SUBLANES) * GQ * 4 + 8 * (n_cmp + span) * GQ * 4
    return pl.pallas_call(
        body,
        grid=(B, N_KV_HEADS, seq // ATTN_Q),
        in_specs=[
            pl.BlockSpec((1, GW, ATTN_Q), lambda b, h, i: (b, h, i)),
            pl.BlockSpec((1, N_GATES, ATTN_Q), lambda b, h, i: (b, 0, i)),
            pl.BlockSpec((1, 1, n_cmp, HEAD_DIM), lambda b, h, i: (0, b * N_KV_HEADS + h, 0, 0)),
            pl.BlockSpec((1, 1, HEAD_DIM, n_cmp), lambda b, h, i: (1, b * N_KV_HEADS + h, 0, 0)),
            pl.BlockSpec((n_sel, n_cmp), lambda b, h, i: (0, 0)),
            pl.BlockSpec((1, seq, HEAD_DIM), lambda b, h, i: (b, 0, N_KV_HEADS + h)),
            pl.BlockSpec((1, HEAD_DIM, seq), lambda b, h, i: (b, N_KV_HEADS + h, 0)),
            pl.BlockSpec((seq, n_sel), lambda b, h, i: (0, 0)),
            pl.BlockSpec((1, seq, HEAD_DIM), lambda b, h, i: (b, 0, 2 * N_KV_HEADS + h)),
            pl.BlockSpec((1, HEAD_DIM, seq), lambda b, h, i: (b, 2 * N_KV_HEADS + h, 0)),
            pl.BlockSpec((1, span, ATTN_Q), lambda b, h, i: (jnp.minimum(i, WINDOW // ATTN_Q), 0, 0)),
        ],
        out_specs=pl.BlockSpec((1, ATTN_Q, GW), lambda b, h, i: (b, i, h)),
        out_shape=jax.ShapeDtypeStruct((B, seq, ATTN_W), F32),
        scratch_shapes=[pltpu.VMEM((2, tk, GQ), F32),
                        pltpu.VMEM((1, GQ), F32),
                        pltpu.VMEM((VR, GQ), F32),
                        pltpu.VMEM((HEAD_DIM, GQ), F32),
                        pltpu.VMEM((HEAD_DIM, GQ), F32)],
        compiler_params=_params(("parallel", "parallel", "arbitrary"), vmem),
        name="attention",
    )(qt, zgt, kvc, kvct, ovt, kk, vt, onehot, kk, vt, _window_bias(span))


def _outproj_body(yc_ref, ya_ref, gc_ref, ga_ref, w_ref, x_ref, h_ref, a_scr):
    @pl.when(pl.program_id(1) == 0)
    def _():
        a_scr[:, 0:CONV_W] = _rms(yc_ref[...], gc_ref[...]).astype(BF16)
        a_scr[:, CONV_W:] = _rms(ya_ref[...], ga_ref[...]).astype(BF16)

    h_ref[...] = x_ref[...] + _dot(a_scr[...], w_ref[...])


def _out_projection(yc, ya, g_conv, g_attn, w, x2):
    T, D = x2.shape
    K = w.shape[0]
    tm = min(1024, T)
    tn = min(1024, D)
    vmem = 2 * (tm * CONV_W * 4 + tm * ATTN_W * 4 + K * tn * 2 + 2 * tm * tn * 4) + tm * K * 2 \
        + 3 * tm * ATTN_W * 4
    return pl.pallas_call(
        _outproj_body,
        grid=(T // tm, D // tn),
        in_specs=[
            pl.BlockSpec((tm, CONV_W), lambda i, j: (i, 0)),
            pl.BlockSpec((tm, ATTN_W), lambda i, j: (i, 0)),
            pl.BlockSpec((1, CONV_W), lambda i, j: (0, 0)),
            pl.BlockSpec((1, ATTN_W), lambda i, j: (0, 0)),
            pl.BlockSpec((K, tn), lambda i, j: (0, j)),
            pl.BlockSpec((tm, tn), lambda i, j: (i, j)),
        ],
        out_specs=pl.BlockSpec((tm, tn), lambda i, j: (i, j)),
        out_shape=jax.ShapeDtypeStruct((T, D), F32),
        scratch_shapes=[pltpu.VMEM((tm, K), BF16)],
        compiler_params=_params(("parallel", "arbitrary"), vmem),
        name="out_projection",
    )(yc, ya, g_conv, g_attn, w, x2)


def _ffn_body(h_ref, g_ref, wg_ref, wu_ref, wd_ref, gf_ref, o_ref, a_scr, *, final_norm, tn):
    j = pl.program_id(1)

    @pl.when(j == 0)
    def _():
        @pl.loop(0, h_ref.shape[0] // FFN_EPILOGUE_ROWS)
        def _(r):
            rows = pl.ds(pl.multiple_of(r * FFN_EPILOGUE_ROWS, FFN_EPILOGUE_ROWS), FFN_EPILOGUE_ROWS)
            a_scr[rows, :] = _rms(h_ref[rows, :], g_ref[...]).astype(BF16)

    a = a_scr[...]
    u = jnp.concatenate(
        [(jax.nn.silu(_dot(a, wg_ref[:, c:c + PROJ_CHUNK])) * _dot(a, wu_ref[:, c:c + PROJ_CHUNK])).astype(BF16)
         for c in range(0, wg_ref.shape[1], PROJ_CHUNK)], axis=1)
    D = o_ref.shape[1]

    @pl.when(j == 0)
    def _():
        for n in range(D // tn):
            o_ref[:, n * tn:(n + 1) * tn] = _dot(u, wd_ref[:, n * tn:(n + 1) * tn])

    @pl.when(j > 0)
    def _():
        for n in range(D // tn):
            o_ref[:, n * tn:(n + 1) * tn] += _dot(u, wd_ref[:, n * tn:(n + 1) * tn])

    @pl.when(j == pl.num_programs(1) - 1)
    def _():
        @pl.loop(0, o_ref.shape[0] // FFN_EPILOGUE_ROWS)
        def _(r):
            rows = pl.ds(pl.multiple_of(r * FFN_EPILOGUE_ROWS, FFN_EPILOGUE_ROWS), FFN_EPILOGUE_ROWS)
            out = h_ref[rows, :] + o_ref[rows, :]
            o_ref[rows, :] = _rms(out, gf_ref[...]) if final_norm else out


def _ffn(h, g, wg, wu, wd, g_final, final_norm):
    T, D = h.shape
    F = wg.shape[1]
    tm = min(1024, T)
    tf = 512
    tn = PROJ_CHUNK
    assert F % tf == 0 and D % tn == 0
    body = functools.partial(_ffn_body, final_norm=final_norm, tn=tn)
    vmem = 2 * (3 * D * tf * 2 + 2 * tm * D * 4) + tm * D * 2 + 6 * tm * tf * 4 + 2 * tm * tn * 4
    return pl.pallas_call(
        body,
        grid=(T // tm, F // tf),
        in_specs=[
            pl.BlockSpec((tm, D), lambda i, j: (i, 0)),
            pl.BlockSpec((1, D), lambda i, j: (0, 0)),
            pl.BlockSpec((D, tf), lambda i, j: (0, j)),
            pl.BlockSpec((D, tf), lambda i, j: (0, j)),
            pl.BlockSpec((tf, D), lambda i, j: (j, 0)),
            pl.BlockSpec((1, D), lambda i, j: (0, 0)),
        ],
        out_specs=pl.BlockSpec((tm, D), lambda i, j: (i, 0)),
        out_shape=jax.ShapeDtypeStruct((T, D), F32),
        scratch_shapes=[pltpu.VMEM((tm, D), BF16)],
        compiler_params=_params(("parallel", "arbitrary"), vmem),
        name="ffn",
    )(h, g, wg, wu, wd, g_final)


def _rope_tables(seq):
    half = HEAD_DIM // 2
    inv = 1.0 / (ROPE_THETA ** (jnp.arange(half, dtype=F32) / half))
    ang = jnp.arange(seq).astype(F32)[:, None] * inv[None, :]
    cos = jnp.cos(ang)
    sin = jnp.sin(ang)
    return jnp.concatenate([cos, cos], axis=1), jnp.concatenate([-sin, sin], axis=1)


def _overlap_t(n_sel, n_cmp):
    cs = np.arange(n_cmp)[None, :] * CMP_STRIDE
    ss = np.arange(n_sel)[:, None] * SEL_BLOCK
    return jnp.asarray(((cs < ss + SEL_BLOCK) & (cs + CMP_BLOCK > ss)).astype(np.float32), dtype=BF16)


def _window_bias(span):
    r = np.arange(span)[None, :, None]
    c = np.arange(ATTN_Q)[None, None, :]
    off = np.minimum(np.arange(WINDOW // ATTN_Q + 1) * ATTN_Q, WINDOW)[:, None, None]
    valid = (r <= off + c) & (r > off + c - WINDOW)
    return jnp.asarray(np.where(valid, 0.0, -np.inf).astype(np.float32))


def _block_onehot(seq, n_sel):
    return jnp.asarray((np.arange(seq)[:, None] // SEL_BLOCK == np.arange(n_sel)[None, :])
                       .astype(np.float32), dtype=BF16)


def kernel(x, norm_mix, w_in, conv_w, cmp_pe_k, cmp_w1_k, cmp_w2_k, cmp_pe_v, cmp_w1_v, cmp_w2_v,
           norm_conv_out, norm_attn_out, w_out, norm_ffn, w_gate, w_up, w_down, norm_final):
    B, S, D = x.shape
    T = B * S
    depth = norm_mix.shape[0]
    assert S % ATTN_Q == 0 and S % CMP_STRIDE == 0
    n_chunks = S // CMP_STRIDE
    n_sel = S // SEL_BLOCK
    cos_t, sin_t = _rope_tables(S)
    ovt = _overlap_t(n_sel, n_chunks)
    onehot = _block_onehot(S, n_sel)

    h = x.reshape(T, D)
    for l in range(depth):
        y_conv, qt, kk, kc, vc, vt, zgt = _in_projection(h, norm_mix[l][None], w_in[l].astype(BF16),
                                                          _gate_weight(w_in[l]), cos_t, sin_t, conv_w[l], B, S)

        kvc_in = jnp.stack([kc, vc]).reshape(2, B * N_KV_HEADS, S, HEAD_DIM)
        pe = jnp.stack([cmp_pe_k[l], cmp_pe_v[l]]).reshape(2, 1, CMP_BLOCK * HEAD_DIM)
        kvc, kvct = _compress(kvc_in, pe, jnp.stack([cmp_w1_k[l], cmp_w1_v[l]]),
                              jnp.stack([cmp_w2_k[l], cmp_w2_v[l]]))

        y_attn = _attention(qt, zgt, kvc, kvct, ovt, kk.reshape(B, S, KV_OUT_W), vt, onehot, B, S)

        h = _out_projection(y_conv, y_attn.reshape(T, ATTN_W), norm_conv_out[l][None],
                            norm_attn_out[l][None], w_out[l].astype(BF16), h)

        h = _ffn(h, norm_ffn[l][None], w_gate[l].astype(BF16), w_up[l].astype(BF16),
                 w_down[l].astype(BF16), norm_final[None], l == depth - 1)
    return h.reshape(B, S, D)
```

```python
import functools

import jax
import jax.numpy as jnp
import numpy as np
from jax import lax
from jax.experimental import pallas as pl
from jax.experimental.pallas import tpu as pltpu

F32 = jnp.float32
BF16 = jnp.bfloat16

HEAD_DIM = 128
N_HEADS = 8
N_KV_HEADS = 2
GROUP = N_HEADS // N_KV_HEADS
N_BRANCH = 3
CONV_TAPS = 3
CMP_BLOCK = 32
CMP_STRIDE = 16
SEL_BLOCK = 64
N_SELECT = 16
N_FORCED = 3
WINDOW = 512
ATTN_Q = 256
ROPE_THETA = 10000.0
RMS_EPS = 1e-6

CONV_W = 1024
ATTN_W = N_HEADS * HEAD_DIM
KV_W = N_KV_HEADS * HEAD_DIM
N_GATES = N_HEADS * N_BRANCH

V7X_VMEM_BYTES = 64 * 1024 * 1024
V7X_VMEM_BUDGET = V7X_VMEM_BYTES - 6 * 1024 * 1024
LANES = 128
SUBLANES = 8
PROJ_TILE_N = 512
PROJ_CHUNK = 256
FFN_EPILOGUE_ROWS = 256

MASK_BIAS = -float(2 ** 30)
LOG2_E = 1.4426950408889634
ONES_ROWS = 16


def _params(semantics, vmem_bytes):
    return pltpu.CompilerParams(dimension_semantics=semantics,
                                vmem_limit_bytes=int(min(vmem_bytes, V7X_VMEM_BUDGET)))


def _rms(x, g):
    return x * lax.rsqrt(jnp.mean(x * x, axis=-1, keepdims=True) + RMS_EPS) * g


def _split3(x):
    hi = x.astype(BF16)
    r1 = x - hi.astype(F32)
    mid = r1.astype(BF16)
    lo = (r1 - mid.astype(F32)).astype(BF16)
    return hi, mid, lo


def _dot(a, b):
    return jnp.dot(a, b, preferred_element_type=F32)


def _dot_f32(a, b):
    a0, a1, a2 = _split3(a)
    b0, b1, b2 = _split3(b)
    return (_dot(a0, b0) + (_dot(a0, b1) + _dot(a1, b0))
            + (_dot(a0, b2) + _dot(a1, b1) + _dot(a2, b0)))


CONV_GROUPS = CONV_W // PROJ_TILE_N
N_Q_TILES = ATTN_W // PROJ_TILE_N
N_KV_TILES = 3
GATE_STEP = 0
Q_STEP0 = GATE_STEP + 1
KV_STEP0 = Q_STEP0 + N_Q_TILES
CONV_STEP0 = KV_STEP0 + N_KV_TILES
N_PROJ_STEPS = CONV_STEP0 + CONV_GROUPS
N_WEIGHT_TILES = 3 * CONV_GROUPS + N_Q_TILES + N_KV_TILES
KV_OUT_W = N_KV_TILES * KV_W
assert 2 * KV_W == PROJ_TILE_N and N_GATES % SUBLANES == 0


def _gate_weight(w):
    return w[:, N_WEIGHT_TILES * PROJ_TILE_N:].T.astype(BF16)


def _inproj_body(x_ref, g_ref, w_ref, wc_ref, wh_ref, wg_ref, cos_ref, sin_ref, cw_ref,
                 yc_ref, qt_ref, k_ref, kc_ref, vc_ref, vt_ref, zgt_ref,
                 a_scr, tail_scr, *, q_scale, blocks_per_seq):
    i = pl.program_id(0)
    j = pl.program_id(1)

    @pl.when(j == 0)
    def _():
        a_scr[...] = _rms(x_ref[...], g_ref[...]).astype(BF16)

    @pl.when((i == 0) & (j == 0))
    def _():
        tail_scr[...] = jnp.zeros(tail_scr.shape, F32)

    @pl.when(j == GATE_STEP)
    def _():
        zgt_ref[0] = lax.dot_general(wg_ref[...], a_scr[...], (((1,), (1,)), ((), ())),
                                     preferred_element_type=F32)

    chunks = [(c, c + PROJ_CHUNK) for c in range(0, PROJ_TILE_N, PROJ_CHUNK)]

    def z_cols(weight_ref, c0, c1):
        return _dot(a_scr[...], weight_ref[:, c0:c1])

    def rope(zh):
        cos = cos_ref[...]
        sin = sin_ref[...]
        heads = []
        for h in range(zh.shape[1] // HEAD_DIM):
            z1 = zh[:, h * HEAD_DIM:(h + 1) * HEAD_DIM]
            heads.append(z1 * cos + pltpu.roll(z1, HEAD_DIM // 2, axis=1) * sin)
        return jnp.concatenate(heads, axis=1)

    @pl.when(j >= CONV_STEP0)
    def _():
        group = j - CONV_STEP0
        for c0, c1 in chunks:
            u = z_cols(wc_ref, c0, c1) * z_cols(wh_ref, c0, c1)
            tail = tail_scr[group, :, c0:c1]
            tail = jnp.where(i % blocks_per_seq == 0, 0.0, tail)
            row = lax.broadcasted_iota(jnp.int32, u.shape, 0)
            t1 = tail[SUBLANES - 1:SUBLANES]
            t2 = tail[SUBLANES - 2:SUBLANES - 1]
            u1 = jnp.where(row == 0, t1, pltpu.roll(u, 1, axis=0))
            u2 = jnp.where(row == 0, t2, jnp.where(row == 1, t1, pltpu.roll(u, 2, axis=0)))
            w = cw_ref[:, c0:c1]
            yc_ref[:, c0:c1] = z_cols(w_ref, c0, c1) * (u2 * w[0:1] + u1 * w[1:2] + u * w[2:3])
            tail_scr[group, :, c0:c1] = u[u.shape[0] - SUBLANES:, :]

    @pl.when((j >= Q_STEP0) & (j < KV_STEP0))
    def _():
        for c0, c1 in chunks:
            qt_ref[0, c0:c1, :] = (rope(z_cols(w_ref, c0, c1)) * q_scale).T.astype(BF16)

    @pl.when((j >= KV_STEP0) & (j < CONV_STEP0))
    def _():
        kx = rope(z_cols(w_ref, 0, KV_W)).astype(BF16)
        for h in range(N_KV_HEADS):
            k_ref[0, h] = kx[:, h * HEAD_DIM:(h + 1) * HEAD_DIM]
        zv = z_cols(w_ref, KV_W, 2 * KV_W)
        vt_ref[0] = zv.T.astype(BF16)

        @pl.when(j == KV_STEP0)
        def _():
            for h in range(N_KV_HEADS):
                kc_ref[0, h] = kx[:, h * HEAD_DIM:(h + 1) * HEAD_DIM].astype(F32)
                vc_ref[0, h] = zv[:, h * HEAD_DIM:(h + 1) * HEAD_DIM].astype(BF16).astype(F32)


def _in_projection(x2, g, w_proj, w_gates, cos_t, sin_t, conv_w, B, seq):
    T, D = x2.shape
    tn = PROJ_TILE_N
    assert w_proj.shape[1] >= N_WEIGHT_TILES * tn
    tm = min(1024, seq)
    nb = seq // tm
    body = functools.partial(_inproj_body, q_scale=HEAD_DIM ** -0.5 * LOG2_E, blocks_per_seq=nb)
    vmem = 2 * (tm * D * 4 + 3 * D * tn * 2 + 2 * tm * HEAD_DIM * 4 + tm * tn * 4 + 4 * tm * tn * 2
                + N_GATES * tm * 4) + tm * D * 2 + 8 * tm * PROJ_CHUNK * 4

    def conv_group(j):
        return jnp.maximum(j - CONV_STEP0, 0)

    def main_tile(j):
        return jnp.where(j >= CONV_STEP0, j - CONV_STEP0,
                         jnp.maximum(j - Q_STEP0, 0) + 3 * CONV_GROUPS)

    def kv_tile(j):
        return jnp.clip(j - KV_STEP0, 0, N_KV_TILES - 1)

    return pl.pallas_call(
        body,
        grid=(T // tm, N_PROJ_STEPS),
        in_specs=[
            pl.BlockSpec((tm, D), lambda i, j: (i, 0)),
            pl.BlockSpec((1, D), lambda i, j: (0, 0)),
            pl.BlockSpec((D, tn), lambda i, j: (0, main_tile(j))),
            pl.BlockSpec((D, tn), lambda i, j: (0, CONV_GROUPS + conv_group(j))),
            pl.BlockSpec((D, tn), lambda i, j: (0, 2 * CONV_GROUPS + conv_group(j))),
            pl.BlockSpec((N_GATES, D), lambda i, j: (0, 0)),
            pl.BlockSpec((tm, HEAD_DIM), lambda i, j: (i % nb, 0)),
            pl.BlockSpec((tm, HEAD_DIM), lambda i, j: (i % nb, 0)),
            pl.BlockSpec((CONV_TAPS, tn), lambda i, j: (0, conv_group(j))),
        ],
        out_specs=[
            pl.BlockSpec((tm, tn), lambda i, j: (i, conv_group(j))),
            pl.BlockSpec((1, tn, tm), lambda i, j: (i // nb, jnp.clip(j - Q_STEP0, 0, N_Q_TILES - 1), i % nb)),
            pl.BlockSpec((1, N_KV_HEADS, tm, HEAD_DIM), lambda i, j: (i // nb, kv_tile(j), i % nb, 0)),
            pl.BlockSpec((1, N_KV_HEADS, tm, HEAD_DIM), lambda i, j: (i // nb, 0, i % nb, 0)),
            pl.BlockSpec((1, N_KV_HEADS, tm, HEAD_DIM), lambda i, j: (i // nb, 0, i % nb, 0)),
            pl.BlockSpec((1, KV_W, tm), lambda i, j: (i // nb, kv_tile(j), i % nb)),
            pl.BlockSpec((1, N_GATES, tm), lambda i, j: (i // nb, 0, i % nb)),
        ],
        out_shape=[
            jax.ShapeDtypeStruct((T, CONV_W), F32),
            jax.ShapeDtypeStruct((B, ATTN_W, seq), BF16),
            jax.ShapeDtypeStruct((B, N_KV_TILES * N_KV_HEADS, seq, HEAD_DIM), BF16),
            jax.ShapeDtypeStruct((B, N_KV_HEADS, seq, HEAD_DIM), F32),
            jax.ShapeDtypeStruct((B, N_KV_HEADS, seq, HEAD_DIM), F32),
            jax.ShapeDtypeStruct((B, KV_OUT_W, seq), BF16),
            jax.ShapeDtypeStruct((B, N_GATES, seq), F32),
        ],
        scratch_shapes=[pltpu.VMEM((tm, D), BF16), pltpu.VMEM((CONV_GROUPS, SUBLANES, tn), F32)],
        compiler_params=_params(("arbitrary", "arbitrary"), vmem),
        name="in_projection",
    )(x2, g, w_proj, w_proj, w_proj, w_gates, cos_t, sin_t, conv_w)


def _compress_body(x_ref, pe_ref, w1_ref, w2_ref, o_ref, ot_ref, *, n_valid):
    n = x_ref.shape[2] // CMP_STRIDE
    x = jnp.concatenate([x_ref[0, 0, pl.ds(l, n, stride=CMP_STRIDE), :] for l in range(CMP_STRIDE)],
                        axis=1).astype(BF16)
    half = x.shape[1]
    w1 = w1_ref[0]
    w1_hi = w1.astype(BF16)
    w1_lo = (w1 - w1_hi.astype(F32)).astype(BF16)
    first = _dot(x, w1_hi[:half]) + _dot(x, w1_lo[:half])
    second = _dot(x, w1_hi[half:]) + _dot(x, w1_lo[half:])
    second = pltpu.roll(second, n - 1, axis=0)
    pe = jnp.broadcast_to(pe_ref[0], (SUBLANES, 2 * half))
    pe_term = _dot_f32(pe, w1)[0:1]
    h = first + second + pe_term
    out = _dot_f32(jax.nn.silu(h), w2_ref[0])
    row = lax.broadcasted_iota(jnp.int32, out.shape, 0)
    out = jnp.where(row < n_valid, out, 0.0)
    o_ref[0, 0] = out
    ot_ref[0, 0] = out.T


def _compress(kv, pe, w1, w2):
    two, BH, seq, d = kv.shape
    n = seq // CMP_STRIDE
    width = CMP_STRIDE * d
    body = functools.partial(_compress_body, n_valid=n - 1)
    vmem = 2 * (seq * d * 4 + 2 * width * 4 + 2 * width * d * 4 + d * d * 4 + n * d * 4) \
        + 8 * width * d * 4
    return pl.pallas_call(
        body,
        grid=(two, BH),
        in_specs=[
            pl.BlockSpec((1, 1, seq, d), lambda s, b: (s, b, 0, 0)),
            pl.BlockSpec((1, 1, 2 * width), lambda s, b: (s, 0, 0)),
            pl.BlockSpec((1, 2 * width, d), lambda s, b: (s, 0, 0)),
            pl.BlockSpec((1, d, d), lambda s, b: (s, 0, 0)),
        ],
        out_specs=[pl.BlockSpec((1, 1, n, d), lambda s, b: (s, b, 0, 0)),
                   pl.BlockSpec((1, 1, d, n), lambda s, b: (s, b, 0, 0))],
        out_shape=[jax.ShapeDtypeStruct((two, BH, n, d), F32),
                   jax.ShapeDtypeStruct((two, BH, d, n), F32)],
        compiler_params=_params(("parallel", "parallel"), vmem),
        name="compress",
    )(kv, pe, w1, w2)


def _select_blocks(imp, q0):
    n_sel = imp.shape[0]
    sel_i = lax.broadcasted_iota(jnp.int32, imp.shape, 0)
    sel_f = sel_i.astype(F32)
    cur = (q0 + lax.broadcasted_iota(jnp.int32, imp.shape, 1)) // SEL_BLOCK
    forced = (sel_i == 0) | (sel_i == cur) | (sel_i == cur - 1)
    v = jnp.where((sel_i > cur) | forced, -jnp.inf, imp)
    chosen = jnp.where(forced, 1.0, 0.0)
    for _ in range(min(N_SELECT, n_sel) - N_FORCED):
        top = jnp.max(v, axis=0, keepdims=True)
        first = jnp.min(jnp.where(v == top, sel_f, float(n_sel)), axis=0, keepdims=True)
        pick = sel_f == first
        chosen = jnp.where(pick, 1.0, chosen)
        v = jnp.where(pick, -jnp.inf, v)
    return jnp.where((chosen > 0.0) & (sel_i <= cur), 0.0, MASK_BIAS)


def _attn_body(qt_ref, zgt_ref, kc_ref, vct_ref, ovt_ref, ks_ref, vst_ref, e_ref, kw_ref, vwt_ref, wb_ref,
               y_ref, s_scr, m_scr, acc_scr, oc_scr, ow_scr, *, tk, span):
    q0 = pl.program_id(2) * ATTN_Q
    qt = [qt_ref[0, g * HEAD_DIM:(g + 1) * HEAD_DIM, :] for g in range(GROUP)]
    q_all = jnp.concatenate(qt, axis=1)

    def lane_time(shape):
        return q0 + (lax.broadcasted_iota(jnp.int32, shape, 1) & (ATTN_Q - 1))

    s = _dot(kc_ref[0, 0].astype(BF16), q_all)
    cmp_end = lax.broadcasted_iota(jnp.int32, s.shape, 0) * CMP_STRIDE + (CMP_BLOCK - 1)
    s = jnp.where(cmp_end <= lane_time(s.shape), s, -jnp.inf)
    m = jnp.max(s, axis=0, keepdims=True)
    m = jnp.where(m == -jnp.inf, 0.0, m)
    e = jnp.exp2(s - m)
    p = e / jnp.maximum(jnp.sum(e, axis=0, keepdims=True), 1e-30)
    oc_scr[...] = _dot(vct_ref[0, 0].astype(BF16), p.astype(BF16))
    pg = p[:, 0:ATTN_Q]
    for g in range(1, GROUP):
        pg = pg + p[:, g * ATTN_Q:(g + 1) * ATTN_Q]
    ovt = ovt_ref[...]
    p0, p1, p2 = _split3(pg)
    bias = _select_blocks(_dot(ovt, p0) + _dot(ovt, p1) + _dot(ovt, p2), q0).astype(BF16)

    w_start = pl.multiple_of(jnp.maximum(q0 - WINDOW, 0), ATTN_Q)
    sw = _dot(kw_ref[0, 0, pl.ds(w_start, span), :], q_all)
    sw = sw + jnp.concatenate([wb_ref[0]] * GROUP, axis=1)
    ew = jnp.exp2(sw - jnp.max(sw, axis=0, keepdims=True))
    vw = jnp.concatenate([vwt_ref[0, :, pl.ds(w_start, span)], jnp.ones((ONES_ROWS, span), BF16)], axis=0)
    ow = _dot(vw, ew.astype(BF16))
    ow_scr[...] = ow[0:HEAD_DIM] / ow[HEAD_DIM:HEAD_DIM + 1]

    qx = jnp.concatenate([jnp.concatenate([qt[g], bias], axis=0) for g in range(GROUP)], axis=1)

    def scores(j, slot):
        start = pl.multiple_of(j * tk, tk)
        kx = jnp.concatenate([ks_ref[0, 0, pl.ds(start, tk), :], e_ref[pl.ds(start, tk), :]], axis=1)
        s_scr[slot] = _dot(kx, qx)

    def update(j, slot, causal):
        start = pl.multiple_of(j * tk, tk)
        vt = jnp.concatenate([vst_ref[0, :, pl.ds(start, tk)], jnp.ones((ONES_ROWS, tk), BF16)], axis=0)
        sc = s_scr[slot]
        if causal:
            key = start + lax.broadcasted_iota(jnp.int32, sc.shape, 0)
            sc = jnp.where(key <= lane_time(sc.shape), sc, -jnp.inf)
        m_old = m_scr[...]
        m_new = jnp.maximum(m_old, jnp.max(sc, axis=0, keepdims=True))
        alpha = jnp.exp2(m_old - m_new)
        pj = jnp.exp2(sc - m_new)
        m_scr[...] = m_new
        acc_scr[...] = alpha * acc_scr[...] + _dot(vt, pj.astype(BF16))

    m_scr[...] = jnp.full(m_scr.shape, -jnp.inf, F32)
    acc_scr[...] = jnp.zeros(acc_scr.shape, F32)
    scores(0, 0)
    last = q0 // tk

    @pl.loop(0, last // 2)
    def _(i):
        scores(2 * i + 1, 1)
        update(2 * i, 0, False)
        scores(2 * i + 2, 0)
        update(2 * i + 1, 1, False)

    @pl.when(last % 2 == 1)
    def _():
        scores(last, 1)
        update(last - 1, 0, False)
        update(last, 1, True)

    @pl.when(last % 2 == 0)
    def _():
        update(last, 0, True)

    o_sel = acc_scr[0:HEAD_DIM, :] / acc_scr[HEAD_DIM:HEAD_DIM + 1, :]
    gates = jax.nn.sigmoid(zgt_ref[0])
    hk = pl.program_id(1)

    def gate(g, branch):
        row = gates[g * N_BRANCH + branch:g * N_BRANCH + branch + 1]
        for other in range(1, N_KV_HEADS):
            r = (other * GROUP + g) * N_BRANCH + branch
            row = jnp.where(hk == other, gates[r:r + 1], row)
        return row

    for g in range(GROUP):
        sl = slice(g * ATTN_Q, (g + 1) * ATTN_Q)
        y = (gate(g, 0) * oc_scr[:, sl] + gate(g, 1) * o_sel[:, sl]
             + gate(g, 2) * ow_scr[:, sl])
        y_ref[0, :, g * HEAD_DIM:(g + 1) * HEAD_DIM] = y.T


def _attention(qt, zgt, kvc, kvct, ovt, kk, vt, onehot, B, seq):
    n_cmp = kvc.shape[2]
    n_sel = ovt.shape[0]
    GW = GROUP * HEAD_DIM
    GQ = GROUP * ATTN_Q
    VR = HEAD_DIM + ONES_ROWS
    tk = min(512, seq)
    span = WINDOW + ATTN_Q
    assert seq >= span
    body = functools.partial(_attn_body, tk=tk, span=span)
    resident = 4 * seq * HEAD_DIM * 2 + seq * n_sel * 2
    vmem = 2 * (GW * ATTN_Q * 2 + N_GATES * ATTN_Q * 4 + 2 * n_cmp * HEAD_DIM * 4 + n_sel * n_cmp * 2
                + resident + ATTN_Q * GW * 4) \
        + (2 * tk + VR + 2 * HEAD_DIM + SUBLANES) * GQ * 4 + 8 * (n_cmp + span) * GQ * 4
    return pl.pallas_call(
        body,
        grid=(B, N_KV_HEADS, seq // ATTN_Q),
        in_specs=[
            pl.BlockSpec((1, GW, ATTN_Q), lambda b, h, i: (b, h, i)),
            pl.BlockSpec((1, N_GATES, ATTN_Q), lambda b, h, i: (b, 0, i)),
            pl.BlockSpec((1, 1, n_cmp, HEAD_DIM), lambda b, h, i: (0, b * N_KV_HEADS + h, 0, 0)),
            pl.BlockSpec((1, 1, HEAD_DIM, n_cmp), lambda b, h, i: (1, b * N_KV_HEADS + h, 0, 0)),
            pl.BlockSpec((n_sel, n_cmp), lambda b, h, i: (0, 0)),
            pl.BlockSpec((1, 1, seq, HEAD_DIM), lambda b, h, i: (b, N_KV_HEADS + h, 0, 0)),
            pl.BlockSpec((1, HEAD_DIM, seq), lambda b, h, i: (b, N_KV_HEADS + h, 0)),
            pl.BlockSpec((seq, n_sel), lambda b, h, i: (0, 0)),
            pl.BlockSpec((1, 1, seq, HEAD_DIM), lambda b, h, i: (b, 2 * N_KV_HEADS + h, 0, 0)),
            pl.BlockSpec((1, HEAD_DIM, seq), lambda b, h, i: (b, 2 * N_KV_HEADS + h, 0)),
            pl.BlockSpec((1, span, ATTN_Q), lambda b, h, i: (jnp.minimum(i, WINDOW // ATTN_Q), 0, 0)),
        ],
        out_specs=pl.BlockSpec((1, ATTN_Q, GW), lambda b, h, i: (b, i, h)),
        out_shape=jax.ShapeDtypeStruct((B, seq, ATTN_W), F32),
        scratch_shapes=[pltpu.VMEM((2, tk, GQ), F32),
                        pltpu.VMEM((1, GQ), F32),
                        pltpu.VMEM((VR, GQ), F32),
                        pltpu.VMEM((HEAD_DIM, GQ), F32),
                        pltpu.VMEM((HEAD_DIM, GQ), F32)],
        compiler_params=_params(("parallel", "parallel", "arbitrary"), vmem),
        name="attention",
    )(qt, zgt, kvc, kvct, ovt, kk, vt, onehot, kk, vt, _window_bias(span))


def _outproj_body(yc_ref, ya_ref, gc_ref, ga_ref, w_ref, x_ref, h_ref, a_scr):
    @pl.when(pl.program_id(1) == 0)
    def _():
        a_scr[:, 0:CONV_W] = _rms(yc_ref[...], gc_ref[...]).astype(BF16)
        a_scr[:, CONV_W:] = _rms(ya_ref[...], ga_ref[...]).astype(BF16)

    h_ref[...] = x_ref[...] + _dot(a_scr[...], w_ref[...])


def _out_projection(yc, ya, g_conv, g_attn, w, x2):
    T, D = x2.shape
    K = w.shape[0]
    tm = min(1024, T)
    tn = min(1024, D)
    vmem = 2 * (tm * CONV_W * 4 + tm * ATTN_W * 4 + K * tn * 2 + 2 * tm * tn * 4) + tm * K * 2 \
        + 3 * tm * ATTN_W * 4
    return pl.pallas_call(
        _outproj_body,
        grid=(T // tm, D // tn),
        in_specs=[
            pl.BlockSpec((tm, CONV_W), lambda i, j: (i, 0)),
            pl.BlockSpec((tm, ATTN_W), lambda i, j: (i, 0)),
            pl.BlockSpec((1, CONV_W), lambda i, j: (0, 0)),
            pl.BlockSpec((1, ATTN_W), lambda i, j: (0, 0)),
            pl.BlockSpec((K, tn), lambda i, j: (0, j)),
            pl.BlockSpec((tm, tn), lambda i, j: (i, j)),
        ],
        out_specs=pl.BlockSpec((tm, tn), lambda i, j: (i, j)),
        out_shape=jax.ShapeDtypeStruct((T, D), F32),
        scratch_shapes=[pltpu.VMEM((tm, K), BF16)],
        compiler_params=_params(("parallel", "arbitrary"), vmem),
        name="out_projection",
    )(yc, ya, g_conv, g_attn, w, x2)


def _ffn_body(h_ref, g_ref, wg_ref, wu_ref, wd_ref, gf_ref, o_ref, a_scr, *, final_norm, tn):
    j = pl.program_id(1)

    @pl.when(j == 0)
    def _():
        @pl.loop(0, h_ref.shape[0] // FFN_EPILOGUE_ROWS)
        def _(r):
            rows = pl.ds(pl.multiple_of(r * FFN_EPILOGUE_ROWS, FFN_EPILOGUE_ROWS), FFN_EPILOGUE_ROWS)
            a_scr[rows, :] = _rms(h_ref[rows, :], g_ref[...]).astype(BF16)

    a = a_scr[...]
    u = jnp.concatenate(
        [(jax.nn.silu(_dot(a, wg_ref[:, c:c + PROJ_CHUNK])) * _dot(a, wu_ref[:, c:c + PROJ_CHUNK])).astype(BF16)
         for c in range(0, wg_ref.shape[1], PROJ_CHUNK)], axis=1)
    D = o_ref.shape[1]

    @pl.when(j == 0)
    def _():
        for n in range(D // tn):
            o_ref[:, n * tn:(n + 1) * tn] = _dot(u, wd_ref[:, n * tn:(n + 1) * tn])

    @pl.when(j > 0)
    def _():
        for n in range(D // tn):
            o_ref[:, n * tn:(n + 1) * tn] += _dot(u, wd_ref[:, n * tn:(n + 1) * tn])

    @pl.when(j == pl.num_programs(1) - 1)
    def _():
        @pl.loop(0, o_ref.shape[0] // FFN_EPILOGUE_ROWS)
        def _(r):
            rows = pl.ds(pl.multiple_of(r * FFN_EPILOGUE_ROWS, FFN_EPILOGUE_ROWS), FFN_EPILOGUE_ROWS)
            out = h_ref[rows, :] + o_ref[rows, :]
            o_ref[rows, :] = _rms(out, gf_ref[...]) if final_norm else out


def _ffn(h, g, wg, wu, wd, g_final, final_norm):
    T, D = h.shape
    F = wg.shape[1]
    tm = min(1024, T)
    tf = 512
    tn = PROJ_CHUNK
    assert F % tf == 0 and D % tn == 0
    body = functools.partial(_ffn_body, final_norm=final_norm, tn=tn)
    vmem = 2 * (3 * D * tf * 2 + 2 * tm * D * 4) + tm * D * 2 + 6 * tm * tf * 4 + 2 * tm * tn * 4
    return pl.pallas_call(
        body,
        grid=(T // tm, F // tf),
        in_specs=[
            pl.BlockSpec((tm, D), lambda i, j: (i, 0)),
            pl.BlockSpec((1, D), lambda i, j: (0, 0)),
            pl.BlockSpec((D, tf), lambda i, j: (0, j)),
            pl.BlockSpec((D, tf), lambda i, j: (0, j)),
            pl.BlockSpec((tf, D), lambda i, j: (j, 0)),
            pl.BlockSpec((1, D), lambda i, j: (0, 0)),
        ],
        out_specs=pl.BlockSpec((tm, D), lambda i, j: (i, 0)),
        out_shape=jax.ShapeDtypeStruct((T, D), F32),
        scratch_shapes=[pltpu.VMEM((tm, D), BF16)],
        compiler_params=_params(("parallel", "arbitrary"), vmem),
        name="ffn",
    )(h, g, wg, wu, wd, g_final)


def _rope_tables(seq):
    half = HEAD_DIM // 2
    inv = 1.0 / (ROPE_THETA ** (jnp.arange(half, dtype=F32) / half))
    ang = jnp.arange(seq).astype(F32)[:, None] * inv[None, :]
    cos = jnp.cos(ang)
    sin = jnp.sin(ang)
    return jnp.concatenate([cos, cos], axis=1), jnp.concatenate([-sin, sin], axis=1)


def _overlap_t(n_sel, n_cmp):
    cs = np.arange(n_cmp)[None, :] * CMP_STRIDE
    ss = np.arange(n_sel)[:, None] * SEL_BLOCK
    return jnp.asarray(((cs < ss + SEL_BLOCK) & (cs + CMP_BLOCK > ss)).astype(np.float32), dtype=BF16)


def _window_bias(span):
    r = np.arange(span)[None, :, None]
    c = np.arange(ATTN_Q)[None, None, :]
    off = np.minimum(np.arange(WINDOW // ATTN_Q + 1) * ATTN_Q, WINDOW)[:, None, None]
    valid = (r <= off + c) & (r > off + c - WINDOW)
    return jnp.asarray(np.where(valid, 0.0, -np.inf).astype(np.float32))


def _block_onehot(seq, n_sel):
    return jnp.asarray((np.arange(seq)[:, None] // SEL_BLOCK == np.arange(n_sel)[None, :])
                       .astype(np.float32), dtype=BF16)


def kernel(x, norm_mix, w_in, conv_w, cmp_pe_k, cmp_w1_k, cmp_w2_k, cmp_pe_v, cmp_w1_v, cmp_w2_v,
           norm_conv_out, norm_attn_out, w_out, norm_ffn, w_gate, w_up, w_down, norm_final):
    B, S, D = x.shape
    T = B * S
    depth = norm_mix.shape[0]
    assert S % ATTN_Q == 0 and S % CMP_STRIDE == 0
    n_chunks = S // CMP_STRIDE
    n_sel = S // SEL_BLOCK
    cos_t, sin_t = _rope_tables(S)
    ovt = _overlap_t(n_sel, n_chunks)
    onehot = _block_onehot(S, n_sel)

    h = x.reshape(T, D)
    for l in range(depth):
        y_conv, qt, kk, kc, vc, vt, zgt = _in_projection(h, norm_mix[l][None], w_in[l].astype(BF16),
                                                          _gate_weight(w_in[l]), cos_t, sin_t, conv_w[l], B, S)

        kvc_in = jnp.stack([kc, vc]).reshape(2, B * N_KV_HEADS, S, HEAD_DIM)
        pe = jnp.stack([cmp_pe_k[l], cmp_pe_v[l]]).reshape(2, 1, CMP_BLOCK * HEAD_DIM)
        kvc, kvct = _compress(kvc_in, pe, jnp.stack([cmp_w1_k[l], cmp_w1_v[l]]),
                              jnp.stack([cmp_w2_k[l], cmp_w2_v[l]]))

        y_attn = _attention(qt, zgt, kvc, kvct, ovt, kk, vt, onehot, B, S)

        h = _out_projection(y_conv, y_attn.reshape(T, ATTN_W), norm_conv_out[l][None],
                            norm_attn_out[l][None], w_out[l].astype(BF16), h)

        h = _ffn(h, norm_ffn[l][None], w_gate[l].astype(BF16), w_up[l].astype(BF16),
                 w_down[l].astype(BF16), norm_final[None], l == depth - 1)
    return h.reshape(B, S, D)
```

```python
import functools

import jax
import jax.numpy as jnp
import numpy as np
from jax import lax
from jax.experimental import pallas as pl
from jax.experimental.pallas import tpu as pltpu

F32 = jnp.float32
BF16 = jnp.bfloat16

HEAD_DIM = 128
N_HEADS = 8
N_KV_HEADS = 2
GROUP = N_HEADS // N_KV_HEADS
N_BRANCH = 3
CONV_TAPS = 3
CMP_BLOCK = 32
CMP_STRIDE = 16
SEL_BLOCK = 64
N_SELECT = 16
N_FORCED = 3
WINDOW = 512
ATTN_Q = 256
ROPE_THETA = 10000.0
RMS_EPS = 1e-6

CONV_W = 1024
ATTN_W = N_HEADS * HEAD_DIM
KV_W = N_KV_HEADS * HEAD_DIM
N_GATES = N_HEADS * N_BRANCH

V7X_VMEM_BYTES = 64 * 1024 * 1024
V7X_VMEM_BUDGET = V7X_VMEM_BYTES - 6 * 1024 * 1024
LANES = 128
SUBLANES = 8
PROJ_TILE_N = 512
PROJ_CHUNK = 256
FFN_EPILOGUE_ROWS = 256

MASK_BIAS = -float(2 ** 30)
LOG2_E = 1.4426950408889634
ONES_ROWS = 16


def _params(semantics, vmem_bytes):
    return pltpu.CompilerParams(dimension_semantics=semantics,
                                vmem_limit_bytes=int(min(vmem_bytes, V7X_VMEM_BUDGET)))


def _rms(x, g):
    return x * lax.rsqrt(jnp.mean(x * x, axis=-1, keepdims=True) + RMS_EPS) * g


def _split3(x):
    hi = x.astype(BF16)
    r1 = x - hi.astype(F32)
    mid = r1.astype(BF16)
    lo = (r1 - mid.astype(F32)).astype(BF16)
    return hi, mid, lo


def _dot(a, b):
    return jnp.dot(a, b, preferred_element_type=F32)


def _dot_f32(a, b):
    a0, a1, a2 = _split3(a)
    b0, b1, b2 = _split3(b)
    return (_dot(a0, b0) + (_dot(a0, b1) + _dot(a1, b0))
            + (_dot(a0, b2) + _dot(a1, b1) + _dot(a2, b0)))


CONV_GROUPS = CONV_W // PROJ_TILE_N
N_Q_TILES = ATTN_W // PROJ_TILE_N
N_KV_TILES = 3
GATE_STEP = 0
Q_STEP0 = GATE_STEP + 1
KV_STEP0 = Q_STEP0 + N_Q_TILES
CONV_STEP0 = KV_STEP0 + N_KV_TILES
N_PROJ_STEPS = CONV_STEP0 + CONV_GROUPS
N_WEIGHT_TILES = 3 * CONV_GROUPS + N_Q_TILES + N_KV_TILES
KV_OUT_W = N_KV_TILES * KV_W
assert 2 * KV_W == PROJ_TILE_N and N_GATES % SUBLANES == 0


def _gate_weight(w):
    return w[:, N_WEIGHT_TILES * PROJ_TILE_N:].T.astype(BF16)


def _inproj_body(*refs, q_scale, blocks_per_seq, side_blocks):
    n_side = len(side_blocks)
    x_ref, g_ref, w_ref, wc_ref, wh_ref, wg_ref, cos_ref, sin_ref, cw_ref = refs[:9]
    side_in = refs[9:9 + n_side]
    yc_ref, qt_ref, k_ref, kvc_ref, vt_ref, zgt_ref = refs[9 + n_side:15 + n_side]
    side_out = refs[15 + n_side:15 + 2 * n_side]
    a_scr, tail_scr = refs[15 + 2 * n_side:]
    i = pl.program_id(0)
    j = pl.program_id(1)

    step = i * N_PROJ_STEPS + j
    for src, dst, n_blocks in zip(side_in, side_out, side_blocks):
        @pl.when(step < n_blocks)
        def _(src=src, dst=dst):
            dst[...] = src[...].astype(BF16)

    @pl.when(j == 0)
    def _():
        a_scr[...] = _rms(x_ref[...], g_ref[...]).astype(BF16)

    @pl.when((i == 0) & (j == 0))
    def _():
        tail_scr[...] = jnp.zeros(tail_scr.shape, F32)

    @pl.when(j == GATE_STEP)
    def _():
        zgt_ref[0] = lax.dot_general(wg_ref[...], a_scr[...], (((1,), (1,)), ((), ())),
                                     preferred_element_type=F32)

    chunks = [(c, c + PROJ_CHUNK) for c in range(0, PROJ_TILE_N, PROJ_CHUNK)]

    def z_cols(weight_ref, c0, c1):
        return _dot(a_scr[...], weight_ref[:, c0:c1])

    def rope(zh):
        cos = cos_ref[...]
        sin = sin_ref[...]
        heads = []
        for h in range(zh.shape[1] // HEAD_DIM):
            z1 = zh[:, h * HEAD_DIM:(h + 1) * HEAD_DIM]
            heads.append(z1 * cos + pltpu.roll(z1, HEAD_DIM // 2, axis=1) * sin)
        return jnp.concatenate(heads, axis=1)

    @pl.when(j >= CONV_STEP0)
    def _():
        group = j - CONV_STEP0
        for c0, c1 in chunks:
            u = z_cols(wc_ref, c0, c1) * z_cols(wh_ref, c0, c1)
            tail = tail_scr[group, :, c0:c1]
            tail = jnp.where(i % blocks_per_seq == 0, 0.0, tail)
            row = lax.broadcasted_iota(jnp.int32, u.shape, 0)
            t1 = tail[SUBLANES - 1:SUBLANES]
            t2 = tail[SUBLANES - 2:SUBLANES - 1]
            u1 = jnp.where(row == 0, t1, pltpu.roll(u, 1, axis=0))
            u2 = jnp.where(row == 0, t2, jnp.where(row == 1, t1, pltpu.roll(u, 2, axis=0)))
            w = cw_ref[:, c0:c1]
            yc_ref[:, c0:c1] = z_cols(w_ref, c0, c1) * (u2 * w[0:1] + u1 * w[1:2] + u * w[2:3])
            tail_scr[group, :, c0:c1] = u[u.shape[0] - SUBLANES:, :]

    @pl.when((j >= Q_STEP0) & (j < KV_STEP0))
    def _():
        for c0, c1 in chunks:
            qt_ref[0, c0:c1, :] = (rope(z_cols(w_ref, c0, c1)) * q_scale).T.astype(BF16)

    @pl.when((j >= KV_STEP0) & (j < CONV_STEP0))
    def _():
        kx = rope(z_cols(w_ref, 0, KV_W)).astype(BF16)
        for h in range(N_KV_HEADS):
            k_ref[0, h] = kx[:, h * HEAD_DIM:(h + 1) * HEAD_DIM]
        zv = z_cols(w_ref, KV_W, 2 * KV_W)
        vt_ref[0] = zv.T.astype(BF16)

        @pl.when(j == KV_STEP0)
        def _():
            for h in range(N_KV_HEADS):
                kvc_ref[0, 0, h] = kx[:, h * HEAD_DIM:(h + 1) * HEAD_DIM].astype(F32)
                kvc_ref[1, 0, h] = zv[:, h * HEAD_DIM:(h + 1) * HEAD_DIM].astype(BF16).astype(F32)


def _side_rows(n_rows, n_steps):
    for rows in range(2 * SUBLANES, n_rows + 1, 2 * SUBLANES):
        if n_rows % rows == 0 and n_rows // rows <= n_steps:
            return rows
    raise ValueError((n_rows, n_steps))


def _in_projection(x2, g, w_proj, w_gates, cos_t, sin_t, conv_w, side, B, seq):
    T, D = x2.shape
    tn = PROJ_TILE_N
    assert w_proj.shape[1] >= N_WEIGHT_TILES * tn
    tm = min(1024, seq)
    nb = seq // tm
    n_steps = (T // tm) * N_PROJ_STEPS
    side_rows = [_side_rows(w.shape[0], n_steps) for w in side]
    side_blocks = [w.shape[0] // r for w, r in zip(side, side_rows)]
    side_specs = [pl.BlockSpec((r, w.shape[1]),
                               lambda i, j, n=n: (jnp.minimum(i * N_PROJ_STEPS + j, n - 1), 0))
                  for w, r, n in zip(side, side_rows, side_blocks)]
    body = functools.partial(_inproj_body, q_scale=HEAD_DIM ** -0.5 * LOG2_E, blocks_per_seq=nb,
                             side_blocks=tuple(side_blocks))
    vmem = 2 * (tm * D * 4 + 3 * D * tn * 2 + 2 * tm * HEAD_DIM * 4 + tm * tn * 4 + 4 * tm * tn * 2
                + N_GATES * tm * 4) + tm * D * 2 + 8 * tm * PROJ_CHUNK * 4 \
        + 2 * sum(r * w.shape[1] * 6 for w, r in zip(side, side_rows))

    def conv_group(j):
        return jnp.maximum(j - CONV_STEP0, 0)

    def main_tile(j):
        return jnp.where(j >= CONV_STEP0, j - CONV_STEP0,
                         jnp.maximum(j - Q_STEP0, 0) + 3 * CONV_GROUPS)

    def kv_tile(j):
        return jnp.clip(j - KV_STEP0, 0, N_KV_TILES - 1)

    return pl.pallas_call(
        body,
        grid=(T // tm, N_PROJ_STEPS),
        in_specs=[
            pl.BlockSpec((tm, D), lambda i, j: (i, 0)),
            pl.BlockSpec((1, D), lambda i, j: (0, 0)),
            pl.BlockSpec((D, tn), lambda i, j: (0, main_tile(j))),
            pl.BlockSpec((D, tn), lambda i, j: (0, CONV_GROUPS + conv_group(j))),
            pl.BlockSpec((D, tn), lambda i, j: (0, 2 * CONV_GROUPS + conv_group(j))),
            pl.BlockSpec((N_GATES, D), lambda i, j: (0, 0)),
            pl.BlockSpec((tm, HEAD_DIM), lambda i, j: (i % nb, 0)),
            pl.BlockSpec((tm, HEAD_DIM), lambda i, j: (i % nb, 0)),
            pl.BlockSpec((CONV_TAPS, tn), lambda i, j: (0, conv_group(j))),
        ] + side_specs,
        out_specs=[
            pl.BlockSpec((tm, tn), lambda i, j: (i, conv_group(j))),
            pl.BlockSpec((1, tn, tm), lambda i, j: (i // nb, jnp.clip(j - Q_STEP0, 0, N_Q_TILES - 1), i % nb)),
            pl.BlockSpec((1, N_KV_HEADS, tm, HEAD_DIM), lambda i, j: (i // nb, kv_tile(j), i % nb, 0)),
            pl.BlockSpec((2, 1, N_KV_HEADS, tm, HEAD_DIM), lambda i, j: (0, i // nb, 0, i % nb, 0)),
            pl.BlockSpec((1, KV_W, tm), lambda i, j: (i // nb, kv_tile(j), i % nb)),
            pl.BlockSpec((1, N_GATES, tm), lambda i, j: (i // nb, 0, i % nb)),
        ] + side_specs,
        out_shape=[
            jax.ShapeDtypeStruct((T, CONV_W), F32),
            jax.ShapeDtypeStruct((B, ATTN_W, seq), BF16),
            jax.ShapeDtypeStruct((B, N_KV_TILES * N_KV_HEADS, seq, HEAD_DIM), BF16),
            jax.ShapeDtypeStruct((2, B, N_KV_HEADS, seq, HEAD_DIM), F32),
            jax.ShapeDtypeStruct((B, KV_OUT_W, seq), BF16),
            jax.ShapeDtypeStruct((B, N_GATES, seq), F32),
        ] + [jax.ShapeDtypeStruct(w.shape, BF16) for w in side],
        scratch_shapes=[pltpu.VMEM((tm, D), BF16), pltpu.VMEM((CONV_GROUPS, SUBLANES, tn), F32)],
        compiler_params=_params(("arbitrary", "arbitrary"), vmem),
        name="in_projection",
    )(x2, g, w_proj, w_proj, w_proj, w_gates, cos_t, sin_t, conv_w, *side)


def _compress_body(x_ref, pe_ref, w1_ref, w2_ref, o_ref, ot_ref, *, n_valid):
    n = x_ref.shape[2] // CMP_STRIDE
    x = jnp.concatenate([x_ref[0, 0, pl.ds(l, n, stride=CMP_STRIDE), :] for l in range(CMP_STRIDE)],
                        axis=1).astype(BF16)
    half = x.shape[1]
    w1 = w1_ref[0]
    w1_hi = w1.astype(BF16)
    w1_lo = (w1 - w1_hi.astype(F32)).astype(BF16)
    first = _dot(x, w1_hi[:half]) + _dot(x, w1_lo[:half])
    second = _dot(x, w1_hi[half:]) + _dot(x, w1_lo[half:])
    second = pltpu.roll(second, n - 1, axis=0)
    pe = jnp.broadcast_to(pe_ref[0], (SUBLANES, 2 * half))
    pe_term = _dot_f32(pe, w1)[0:1]
    h = first + second + pe_term
    out = _dot_f32(jax.nn.silu(h), w2_ref[0])
    row = lax.broadcasted_iota(jnp.int32, out.shape, 0)
    out = jnp.where(row < n_valid, out, 0.0)
    o_ref[0, 0] = out
    ot_ref[0, 0] = out.T


def _compress(kv, pe, w1, w2):
    two, BH, seq, d = kv.shape
    n = seq // CMP_STRIDE
    width = CMP_STRIDE * d
    body = functools.partial(_compress_body, n_valid=n - 1)
    vmem = 2 * (seq * d * 4 + 2 * width * 4 + 2 * width * d * 4 + d * d * 4 + n * d * 4) \
        + 8 * width * d * 4
    return pl.pallas_call(
        body,
        grid=(two, BH),
        in_specs=[
            pl.BlockSpec((1, 1, seq, d), lambda s, b: (s, b, 0, 0)),
            pl.BlockSpec((1, 1, 2 * width), lambda s, b: (s, 0, 0)),
            pl.BlockSpec((1, 2 * width, d), lambda s, b: (s, 0, 0)),
            pl.BlockSpec((1, d, d), lambda s, b: (s, 0, 0)),
        ],
        out_specs=[pl.BlockSpec((1, 1, n, d), lambda s, b: (s, b, 0, 0)),
                   pl.BlockSpec((1, 1, d, n), lambda s, b: (s, b, 0, 0))],
        out_shape=[jax.ShapeDtypeStruct((two, BH, n, d), F32),
                   jax.ShapeDtypeStruct((two, BH, d, n), F32)],
        compiler_params=_params(("parallel", "parallel"), vmem),
        name="compress",
    )(kv, pe, w1, w2)


def _select_blocks(imp, q0):
    n_sel = imp.shape[0]
    sel_i = lax.broadcasted_iota(jnp.int32, imp.shape, 0)
    sel_f = sel_i.astype(F32)
    cur = (q0 + lax.broadcasted_iota(jnp.int32, imp.shape, 1)) // SEL_BLOCK
    forced = (sel_i == 0) | (sel_i == cur) | (sel_i == cur - 1)
    v = jnp.where((sel_i > cur) | forced, -jnp.inf, imp)
    chosen = jnp.where(forced, 1.0, 0.0)
    for _ in range(min(N_SELECT, n_sel) - N_FORCED):
        top = jnp.max(v, axis=0, keepdims=True)
        first = jnp.min(jnp.where(v == top, sel_f, float(n_sel)), axis=0, keepdims=True)
        pick = sel_f == first
        chosen = jnp.where(pick, 1.0, chosen)
        v = jnp.where(pick, -jnp.inf, v)
    return jnp.where((chosen > 0.0) & (sel_i <= cur), 0.0, MASK_BIAS)


def _attn_body(qt_ref, zgt_ref, kc_ref, vct_ref, ovt_ref, ks_ref, vst_ref, e_ref, kw_ref, vwt_ref, wb_ref,
               y_ref, s_scr, m_scr, acc_scr, oc_scr, ow_scr, *, tk, span):
    q0 = pl.program_id(2) * ATTN_Q
    qt = [qt_ref[0, g * HEAD_DIM:(g + 1) * HEAD_DIM, :] for g in range(GROUP)]
    q_all = jnp.concatenate(qt, axis=1)

    def lane_time(shape):
        return q0 + (lax.broadcasted_iota(jnp.int32, shape, 1) & (ATTN_Q - 1))

    s = _dot(kc_ref[0, 0].astype(BF16), q_all)
    cmp_end = lax.broadcasted_iota(jnp.int32, s.shape, 0) * CMP_STRIDE + (CMP_BLOCK - 1)
    s = jnp.where(cmp_end <= lane_time(s.shape), s, -jnp.inf)
    m = jnp.max(s, axis=0, keepdims=True)
    m = jnp.where(m == -jnp.inf, 0.0, m)
    e = jnp.exp2(s - m)
    p = e / jnp.maximum(jnp.sum(e, axis=0, keepdims=True), 1e-30)
    oc_scr[...] = _dot(vct_ref[0, 0].astype(BF16), p.astype(BF16))
    pg = p[:, 0:ATTN_Q]
    for g in range(1, GROUP):
        pg = pg + p[:, g * ATTN_Q:(g + 1) * ATTN_Q]
    ovt = ovt_ref[...]
    p0, p1, p2 = _split3(pg)
    bias = _select_blocks(_dot(ovt, p0) + _dot(ovt, p1) + _dot(ovt, p2), q0).astype(BF16)

    w_start = pl.multiple_of(jnp.maximum(q0 - WINDOW, 0), ATTN_Q)
    sw = _dot(kw_ref[0, 0, pl.ds(w_start, span), :], q_all)
    sw = sw + jnp.concatenate([wb_ref[0]] * GROUP, axis=1)
    ew = jnp.exp2(sw - jnp.max(sw, axis=0, keepdims=True))
    vw = jnp.concatenate([vwt_ref[0, :, pl.ds(w_start, span)], jnp.ones((ONES_ROWS, span), BF16)], axis=0)
    ow = _dot(vw, ew.astype(BF16))
    ow_scr[...] = ow[0:HEAD_DIM] / ow[HEAD_DIM:HEAD_DIM + 1]

    qx = jnp.concatenate([jnp.concatenate([qt[g], bias], axis=0) for g in range(GROUP)], axis=1)

    def scores(j, slot):
        start = pl.multiple_of(j * tk, tk)
        kx = jnp.concatenate([ks_ref[0, 0, pl.ds(start, tk), :], e_ref[pl.ds(start, tk), :]], axis=1)
        s_scr[slot] = _dot(kx, qx)

    def update(j, slot, causal):
        start = pl.multiple_of(j * tk, tk)
        vt = jnp.concatenate([vst_ref[0, :, pl.ds(start, tk)], jnp.ones((ONES_ROWS, tk), BF16)], axis=0)
        sc = s_scr[slot]
        if causal:
            key = start + lax.broadcasted_iota(jnp.int32, sc.shape, 0)
            sc = jnp.where(key <= lane_time(sc.shape), sc, -jnp.inf)
        m_old = m_scr[...]
        m_new = jnp.maximum(m_old, jnp.max(sc, axis=0, keepdims=True))
        alpha = jnp.exp2(m_old - m_new)
        pj = jnp.exp2(sc - m_new)
        m_scr[...] = m_new
        acc_scr[...] = alpha * acc_scr[...] + _dot(vt, pj.astype(BF16))

    m_scr[...] = jnp.full(m_scr.shape, -jnp.inf, F32)
    acc_scr[...] = jnp.zeros(acc_scr.shape, F32)
    scores(0, 0)
    last = q0 // tk

    @pl.loop(0, last // 2)
    def _(i):
        scores(2 * i + 1, 1)
        update(2 * i, 0, False)
        scores(2 * i + 2, 0)
        update(2 * i + 1, 1, False)

    @pl.when(last % 2 == 1)
    def _():
        scores(last, 1)
        update(last - 1, 0, False)
        update(last, 1, True)

    @pl.when(last % 2 == 0)
    def _():
        update(last, 0, True)

    o_sel = acc_scr[0:HEAD_DIM, :] / acc_scr[HEAD_DIM:HEAD_DIM + 1, :]
    gates = jax.nn.sigmoid(zgt_ref[0])
    hk = pl.program_id(1)

    def gate(g, branch):
        row = gates[g * N_BRANCH + branch:g * N_BRANCH + branch + 1]
        for other in range(1, N_KV_HEADS):
            r = (other * GROUP + g) * N_BRANCH + branch
            row = jnp.where(hk == other, gates[r:r + 1], row)
        return row

    for g in range(GROUP):
        sl = slice(g * ATTN_Q, (g + 1) * ATTN_Q)
        y = (gate(g, 0) * oc_scr[:, sl] + gate(g, 1) * o_sel[:, sl]
             + gate(g, 2) * ow_scr[:, sl])
        y_ref[0, :, g * HEAD_DIM:(g + 1) * HEAD_DIM] = y.T


def _attention(qt, zgt, kvc, kvct, ovt, kk, vt, onehot, B, seq):
    n_cmp = kvc.shape[2]
    n_sel = ovt.shape[0]
    GW = GROUP * HEAD_DIM
    GQ = GROUP * ATTN_Q
    VR = HEAD_DIM + ONES_ROWS
    tk = min(512, seq)
    span = WINDOW + ATTN_Q
    assert seq >= span
    body = functools.partial(_attn_body, tk=tk, span=span)
    resident = 4 * seq * HEAD_DIM * 2 + seq * n_sel * 2
    vmem = 2 * (GW * ATTN_Q * 2 + N_GATES * ATTN_Q * 4 + 2 * n_cmp * HEAD_DIM * 4 + n_sel * n_cmp * 2
                + resident + ATTN_Q * GW * 4) \
        + (2 * tk + VR + 2 * HEAD_DIM + SUBLANES) * GQ * 4 + 8 * (n_cmp + span) * GQ * 4
    return pl.pallas_call(
        body,
        grid=(B, N_KV_HEADS, seq // ATTN_Q),
        in_specs=[
            pl.BlockSpec((1, GW, ATTN_Q), lambda b, h, i: (b, h, i)),
            pl.BlockSpec((1, N_GATES, ATTN_Q), lambda b, h, i: (b, 0, i)),
            pl.BlockSpec((1, 1, n_cmp, HEAD_DIM), lambda b, h, i: (0, b * N_KV_HEADS + h, 0, 0)),
            pl.BlockSpec((1, 1, HEAD_DIM, n_cmp), lambda b, h, i: (1, b * N_KV_HEADS + h, 0, 0)),
            pl.BlockSpec((n_sel, n_cmp), lambda b, h, i: (0, 0)),
            pl.BlockSpec((1, 1, seq, HEAD_DIM), lambda b, h, i: (b, N_KV_HEADS + h, 0, 0)),
            pl.BlockSpec((1, HEAD_DIM, seq), lambda b, h, i: (b, N_KV_HEADS + h, 0)),
            pl.BlockSpec((seq, n_sel), lambda b, h, i: (0, 0)),
            pl.BlockSpec((1, 1, seq, HEAD_DIM), lambda b, h, i: (b, 2 * N_KV_HEADS + h, 0, 0)),
            pl.BlockSpec((1, HEAD_DIM, seq), lambda b, h, i: (b, 2 * N_KV_HEADS + h, 0)),
            pl.BlockSpec((1, span, ATTN_Q), lambda b, h, i: (jnp.minimum(i, WINDOW // ATTN_Q), 0, 0)),
        ],
        out_specs=pl.BlockSpec((1, ATTN_Q, GW), lambda b, h, i: (b, i, h)),
        out_shape=jax.ShapeDtypeStruct((B, seq, ATTN_W), F32),
        scratch_shapes=[pltpu.VMEM((2, tk, GQ), F32),
                        pltpu.VMEM((1, GQ), F32),
                        pltpu.VMEM((VR, GQ), F32),
                        pltpu.VMEM((HEAD_DIM, GQ), F32),
                        pltpu.VMEM((HEAD_DIM, GQ), F32)],
        compiler_params=_params(("parallel", "parallel", "arbitrary"), vmem),
        name="attention",
    )(qt, zgt, kvc, kvct, ovt, kk, vt, onehot, kk, vt, _window_bias(span))


def _outproj_body(yc_ref, ya_ref, gc_ref, ga_ref, w_ref, x_ref, h_ref, a_scr):
    @pl.when(pl.program_id(1) == 0)
    def _():
        a_scr[:, 0:CONV_W] = _rms(yc_ref[...], gc_ref[...]).astype(BF16)
        a_scr[:, CONV_W:] = _rms(ya_ref[...], ga_ref[...]).astype(BF16)

    h_ref[...] = x_ref[...] + _dot(a_scr[...], w_ref[...])


def _out_projection(yc, ya, g_conv, g_attn, w, x2):
    T, D = x2.shape
    K = w.shape[0]
    tm = min(1024, T)
    tn = min(1024, D)
    vmem = 2 * (tm * CONV_W * 4 + tm * ATTN_W * 4 + K * tn * 2 + 2 * tm * tn * 4) + tm * K * 2 \
        + 3 * tm * ATTN_W * 4
    return pl.pallas_call(
        _outproj_body,
        grid=(T // tm, D // tn),
        in_specs=[
            pl.BlockSpec((tm, CONV_W), lambda i, j: (i, 0)),
            pl.BlockSpec((tm, ATTN_W), lambda i, j: (i, 0)),
            pl.BlockSpec((1, CONV_W), lambda i, j: (0, 0)),
            pl.BlockSpec((1, ATTN_W), lambda i, j: (0, 0)),
            pl.BlockSpec((K, tn), lambda i, j: (0, j)),
            pl.BlockSpec((tm, tn), lambda i, j: (i, j)),
        ],
        out_specs=pl.BlockSpec((tm, tn), lambda i, j: (i, j)),
        out_shape=jax.ShapeDtypeStruct((T, D), F32),
        scratch_shapes=[pltpu.VMEM((tm, K), BF16)],
        compiler_params=_params(("parallel", "arbitrary"), vmem),
        name="out_projection",
    )(yc, ya, g_conv, g_attn, w, x2)


def _ffn_body(h_ref, g_ref, wg_ref, wu_ref, wd_ref, gf_ref, o_ref, a_scr, *, final_norm, tn):
    j = pl.program_id(1)

    @pl.when(j == 0)
    def _():
        @pl.loop(0, h_ref.shape[0] // FFN_EPILOGUE_ROWS)
        def _(r):
            rows = pl.ds(pl.multiple_of(r * FFN_EPILOGUE_ROWS, FFN_EPILOGUE_ROWS), FFN_EPILOGUE_ROWS)
            a_scr[rows, :] = _rms(h_ref[rows, :], g_ref[...]).astype(BF16)

    a = a_scr[...]
    u = jnp.concatenate(
        [(jax.nn.silu(_dot(a, wg_ref[:, c:c + PROJ_CHUNK])) * _dot(a, wu_ref[:, c:c + PROJ_CHUNK])).astype(BF16)
         for c in range(0, wg_ref.shape[1], PROJ_CHUNK)], axis=1)
    D = o_ref.shape[1]

    @pl.when(j == 0)
    def _():
        for n in range(D // tn):
            o_ref[:, n * tn:(n + 1) * tn] = _dot(u, wd_ref[:, n * tn:(n + 1) * tn])

    @pl.when(j > 0)
    def _():
        for n in range(D // tn):
            o_ref[:, n * tn:(n + 1) * tn] += _dot(u, wd_ref[:, n * tn:(n + 1) * tn])

    @pl.when(j == pl.num_programs(1) - 1)
    def _():
        @pl.loop(0, o_ref.shape[0] // FFN_EPILOGUE_ROWS)
        def _(r):
            rows = pl.ds(pl.multiple_of(r * FFN_EPILOGUE_ROWS, FFN_EPILOGUE_ROWS), FFN_EPILOGUE_ROWS)
            out = h_ref[rows, :] + o_ref[rows, :]
            o_ref[rows, :] = _rms(out, gf_ref[...]) if final_norm else out


def _ffn(h, g, wg, wu, wd, g_final, final_norm):
    T, D = h.shape
    F = wg.shape[1]
    tm = min(1024, T)
    tf = 512
    tn = PROJ_CHUNK
    assert F % tf == 0 and D % tn == 0
    body = functools.partial(_ffn_body, final_norm=final_norm, tn=tn)
    vmem = 2 * (3 * D * tf * 2 + 2 * tm * D * 4) + tm * D * 2 + 6 * tm * tf * 4 + 2 * tm * tn * 4
    return pl.pallas_call(
        body,
        grid=(T // tm, F // tf),
        in_specs=[
            pl.BlockSpec((tm, D), lambda i, j: (i, 0)),
            pl.BlockSpec((1, D), lambda i, j: (0, 0)),
            pl.BlockSpec((D, tf), lambda i, j: (0, j)),
            pl.BlockSpec((D, tf), lambda i, j: (0, j)),
            pl.BlockSpec((tf, D), lambda i, j: (j, 0)),
            pl.BlockSpec((1, D), lambda i, j: (0, 0)),
        ],
        out_specs=pl.BlockSpec((tm, D), lambda i, j: (i, 0)),
        out_shape=jax.ShapeDtypeStruct((T, D), F32),
        scratch_shapes=[pltpu.VMEM((tm, D), BF16)],
        compiler_params=_params(("parallel", "arbitrary"), vmem),
        name="ffn",
    )(h, g, wg, wu, wd, g_final)


def _rope_tables(seq):
    half = HEAD_DIM // 2
    inv = 1.0 / (ROPE_THETA ** (jnp.arange(half, dtype=F32) / half))
    ang = jnp.arange(seq).astype(F32)[:, None] * inv[None, :]
    cos = jnp.cos(ang)
    sin = jnp.sin(ang)
    return jnp.concatenate([cos, cos], axis=1), jnp.concatenate([-sin, sin], axis=1)


def _overlap_t(n_sel, n_cmp):
    cs = np.arange(n_cmp)[None, :] * CMP_STRIDE
    ss = np.arange(n_sel)[:, None] * SEL_BLOCK
    return jnp.asarray(((cs < ss + SEL_BLOCK) & (cs + CMP_BLOCK > ss)).astype(np.float32), dtype=BF16)


def _window_bias(span):
    r = np.arange(span)[None, :, None]
    c = np.arange(ATTN_Q)[None, None, :]
    off = np.minimum(np.arange(WINDOW // ATTN_Q + 1) * ATTN_Q, WINDOW)[:, None, None]
    valid = (r <= off + c) & (r > off + c - WINDOW)
    return jnp.asarray(np.where(valid, 0.0, -np.inf).astype(np.float32))


def _block_onehot(seq, n_sel):
    return jnp.asarray((np.arange(seq)[:, None] // SEL_BLOCK == np.arange(n_sel)[None, :])
                       .astype(np.float32), dtype=BF16)


def kernel(x, norm_mix, w_in, conv_w, cmp_pe_k, cmp_w1_k, cmp_w2_k, cmp_pe_v, cmp_w1_v, cmp_w2_v,
           norm_conv_out, norm_attn_out, w_out, norm_ffn, w_gate, w_up, w_down, norm_final):
    B, S, D = x.shape
    T = B * S
    depth = norm_mix.shape[0]
    assert S % ATTN_Q == 0 and S % CMP_STRIDE == 0
    n_chunks = S // CMP_STRIDE
    n_sel = S // SEL_BLOCK
    cos_t, sin_t = _rope_tables(S)
    ovt = _overlap_t(n_sel, n_chunks)
    onehot = _block_onehot(S, n_sel)

    h = x.reshape(T, D)
    for l in range(depth):
        y_conv, qt, kk, kvc_in, vt, zgt, w_out_b, w_gate_b, w_up_b, w_down_b = _in_projection(
            h, norm_mix[l][None], w_in[l].astype(BF16), _gate_weight(w_in[l]), cos_t, sin_t, conv_w[l],
            [w_out[l], w_gate[l], w_up[l], w_down[l]], B, S)
        kvc_in = kvc_in.reshape(2, B * N_KV_HEADS, S, HEAD_DIM)
        pe = jnp.stack([cmp_pe_k[l], cmp_pe_v[l]]).reshape(2, 1, CMP_BLOCK * HEAD_DIM)
        kvc, kvct = _compress(kvc_in, pe, jnp.stack([cmp_w1_k[l], cmp_w1_v[l]]),
                              jnp.stack([cmp_w2_k[l], cmp_w2_v[l]]))

        y_attn = _attention(qt, zgt, kvc, kvct, ovt, kk, vt, onehot, B, S)

        h = _out_projection(y_conv, y_attn.reshape(T, ATTN_W), norm_conv_out[l][None],
                            norm_attn_out[l][None], w_out_b, h)

        h = _ffn(h, norm_ffn[l][None], w_gate_b, w_up_b, w_down_b, norm_final[None], l == depth - 1)
    return h.reshape(B, S, D)
```

```python
import functools

import jax
import jax.numpy as jnp
import numpy as np
from jax import lax
from jax.experimental import pallas as pl
from jax.experimental.pallas import tpu as pltpu

F32 = jnp.float32
BF16 = jnp.bfloat16

HEAD_DIM = 128
N_HEADS = 8
N_KV_HEADS = 2
GROUP = N_HEADS // N_KV_HEADS
N_BRANCH = 3
CONV_TAPS = 3
CMP_BLOCK = 32
CMP_STRIDE = 16
SEL_BLOCK = 64
N_SELECT = 16
N_FORCED = 3
WINDOW = 512
ATTN_Q = 256
ROPE_THETA = 10000.0
ROPE_SPLIT = 128
RMS_EPS = 1e-6

CONV_W = 1024
ATTN_W = N_HEADS * HEAD_DIM
KV_W = N_KV_HEADS * HEAD_DIM
N_GATES = N_HEADS * N_BRANCH

V7X_VMEM_BYTES = 64 * 1024 * 1024
V7X_VMEM_BUDGET = V7X_VMEM_BYTES - 6 * 1024 * 1024
LANES = 128
SUBLANES = 8
PROJ_TILE_N = 512
PROJ_CHUNK = 256
FFN_EPILOGUE_ROWS = 256

MASK_BIAS = -float(2 ** 30)
LOG2_E = 1.4426950408889634
ONES_ROWS = 16


def _params(semantics, vmem_bytes):
    return pltpu.CompilerParams(dimension_semantics=semantics,
                                vmem_limit_bytes=int(min(vmem_bytes, V7X_VMEM_BUDGET)))


def _rms(x, g):
    return x * lax.rsqrt(jnp.mean(x * x, axis=-1, keepdims=True) + RMS_EPS) * g


def _split3(x):
    hi = x.astype(BF16)
    r1 = x - hi.astype(F32)
    mid = r1.astype(BF16)
    lo = (r1 - mid.astype(F32)).astype(BF16)
    return hi, mid, lo


def _dot(a, b):
    return jnp.dot(a, b, preferred_element_type=F32)


def _dot_f32(a, b):
    a0, a1, a2 = _split3(a)
    b0, b1, b2 = _split3(b)
    return (_dot(a0, b0) + (_dot(a0, b1) + _dot(a1, b0))
            + (_dot(a0, b2) + _dot(a1, b1) + _dot(a2, b0)))


CONV_GROUPS = CONV_W // PROJ_TILE_N
N_Q_TILES = ATTN_W // PROJ_TILE_N
N_KV_TILES = 3
GATE_STEP = 0
Q_STEP0 = GATE_STEP + 1
KV_STEP0 = Q_STEP0 + N_Q_TILES
CONV_STEP0 = KV_STEP0 + N_KV_TILES
N_PROJ_STEPS = CONV_STEP0 + CONV_GROUPS
N_WEIGHT_TILES = 3 * CONV_GROUPS + N_Q_TILES + N_KV_TILES
KV_OUT_W = N_KV_TILES * KV_W
assert 2 * KV_W == PROJ_TILE_N and N_GATES % SUBLANES == 0


def _gate_weight(w):
    gates = w[:, N_WEIGHT_TILES * PROJ_TILE_N:].astype(BF16)
    return jnp.pad(gates, ((0, 0), (0, LANES - gates.shape[1])))


def _inproj_body(*refs, q_scale, blocks_per_seq, side_blocks):
    n_side = len(side_blocks)
    x_ref, g_ref, w_ref, wc_ref, wh_ref, wg_ref, cos_ref, sin_ref, cw_ref = refs[:9]
    side_in = refs[9:9 + n_side]
    yc_ref, qt_ref, k_ref, kvc_ref, vt_ref, zgt_ref = refs[9 + n_side:15 + n_side]
    side_out = refs[15 + n_side:15 + 2 * n_side]
    a_scr, tail_scr, zg_scr = refs[15 + 2 * n_side:]
    i = pl.program_id(0)
    j = pl.program_id(1)

    step = i * N_PROJ_STEPS + j
    for src, dst, n_blocks in zip(side_in, side_out, side_blocks):
        @pl.when(step < n_blocks)
        def _(src=src, dst=dst):
            dst[...] = src[...].astype(BF16)

    @pl.when(j == 0)
    def _():
        a_scr[...] = _rms(x_ref[...], g_ref[...]).astype(BF16)

    @pl.when((i == 0) & (j == 0))
    def _():
        tail_scr[...] = jnp.zeros(tail_scr.shape, F32)

    @pl.when(j == GATE_STEP)
    def _():
        zg_scr[...] = _dot(a_scr[...], wg_ref[...])
        zgt_ref[0] = zg_scr[...].T[0:N_GATES]

    chunks = [(c, c + PROJ_CHUNK) for c in range(0, PROJ_TILE_N, PROJ_CHUNK)]

    def z_cols(weight_ref, c0, c1):
        return _dot(a_scr[...], weight_ref[:, c0:c1])

    def rope(zh):
        cos = cos_ref[...]
        sin = sin_ref[...]
        heads = []
        for h in range(zh.shape[1] // HEAD_DIM):
            z1 = zh[:, h * HEAD_DIM:(h + 1) * HEAD_DIM]
            heads.append(z1 * cos + pltpu.roll(z1, HEAD_DIM // 2, axis=1) * sin)
        return jnp.concatenate(heads, axis=1)

    @pl.when(j >= CONV_STEP0)
    def _():
        group = j - CONV_STEP0
        for c0, c1 in chunks:
            u = z_cols(wc_ref, c0, c1) * z_cols(wh_ref, c0, c1)
            tail = tail_scr[group, :, c0:c1]
            tail = jnp.where(i % blocks_per_seq == 0, 0.0, tail)
            row = lax.broadcasted_iota(jnp.int32, u.shape, 0)
            t1 = tail[SUBLANES - 1:SUBLANES]
            t2 = tail[SUBLANES - 2:SUBLANES - 1]
            u1 = jnp.where(row == 0, t1, pltpu.roll(u, 1, axis=0))
            u2 = jnp.where(row == 0, t2, jnp.where(row == 1, t1, pltpu.roll(u, 2, axis=0)))
            w = cw_ref[:, c0:c1]
            yc_ref[:, c0:c1] = z_cols(w_ref, c0, c1) * (u2 * w[0:1] + u1 * w[1:2] + u * w[2:3])
            tail_scr[group, :, c0:c1] = u[u.shape[0] - SUBLANES:, :]

    @pl.when((j >= Q_STEP0) & (j < KV_STEP0))
    def _():
        for c0, c1 in chunks:
            qt_ref[0, c0:c1, :] = (rope(z_cols(w_ref, c0, c1)) * q_scale).T.astype(BF16)

    @pl.when((j >= KV_STEP0) & (j < CONV_STEP0))
    def _():
        kx = rope(z_cols(w_ref, 0, KV_W)).astype(BF16)
        for h in range(N_KV_HEADS):
            k_ref[0, h] = kx[:, h * HEAD_DIM:(h + 1) * HEAD_DIM]
        zv = z_cols(w_ref, KV_W, 2 * KV_W)
        vt_ref[0] = zv.T.astype(BF16)

        @pl.when(j == KV_STEP0)
        def _():
            for h in range(N_KV_HEADS):
                kvc_ref[0, 0, h] = kx[:, h * HEAD_DIM:(h + 1) * HEAD_DIM].astype(F32)
                kvc_ref[1, 0, h] = zv[:, h * HEAD_DIM:(h + 1) * HEAD_DIM].astype(BF16).astype(F32)


def _side_rows(n_rows, n_steps):
    for rows in range(2 * SUBLANES, n_rows + 1, 2 * SUBLANES):
        if n_rows % rows == 0 and n_rows // rows <= n_steps:
            return rows
    raise ValueError((n_rows, n_steps))


def _in_projection(x2, g, w_proj, w_gates, cos_t, sin_t, conv_w, side, B, seq):
    T, D = x2.shape
    tn = PROJ_TILE_N
    assert w_proj.shape[1] >= N_WEIGHT_TILES * tn
    tm = min(1024, seq)
    nb = seq // tm
    n_steps = (T // tm) * N_PROJ_STEPS
    side_rows = [_side_rows(w.shape[0], n_steps) for w in side]
    side_blocks = [w.shape[0] // r for w, r in zip(side, side_rows)]
    side_specs = [pl.BlockSpec((r, w.shape[1]),
                               lambda i, j, n=n: (jnp.minimum(i * N_PROJ_STEPS + j, n - 1), 0))
                  for w, r, n in zip(side, side_rows, side_blocks)]
    body = functools.partial(_inproj_body, q_scale=HEAD_DIM ** -0.5 * LOG2_E, blocks_per_seq=nb,
                             side_blocks=tuple(side_blocks))
    vmem = 2 * (tm * D * 4 + 3 * D * tn * 2 + 2 * tm * HEAD_DIM * 4 + tm * tn * 4 + 4 * tm * tn * 2
                + N_GATES * tm * 4) + tm * D * 2 + 8 * tm * PROJ_CHUNK * 4 \
        + 2 * sum(r * w.shape[1] * 6 for w, r in zip(side, side_rows))

    def conv_group(j):
        return jnp.maximum(j - CONV_STEP0, 0)

    def main_tile(j):
        return jnp.where(j >= CONV_STEP0, j - CONV_STEP0,
                         jnp.maximum(j - Q_STEP0, 0) + 3 * CONV_GROUPS)

    def kv_tile(j):
        return jnp.clip(j - KV_STEP0, 0, N_KV_TILES - 1)

    return pl.pallas_call(
        body,
        grid=(T // tm, N_PROJ_STEPS),
        in_specs=[
            pl.BlockSpec((tm, D), lambda i, j: (i, 0)),
            pl.BlockSpec((1, D), lambda i, j: (0, 0)),
            pl.BlockSpec((D, tn), lambda i, j: (0, main_tile(j))),
            pl.BlockSpec((D, tn), lambda i, j: (0, CONV_GROUPS + conv_group(j))),
            pl.BlockSpec((D, tn), lambda i, j: (0, 2 * CONV_GROUPS + conv_group(j))),
            pl.BlockSpec((D, LANES), lambda i, j: (0, 0)),
            pl.BlockSpec((tm, HEAD_DIM), lambda i, j: (i % nb, 0)),
            pl.BlockSpec((tm, HEAD_DIM), lambda i, j: (i % nb, 0)),
            pl.BlockSpec((CONV_TAPS, tn), lambda i, j: (0, conv_group(j))),
        ] + side_specs,
        out_specs=[
            pl.BlockSpec((tm, tn), lambda i, j: (i, conv_group(j))),
            pl.BlockSpec((1, tn, tm), lambda i, j: (i // nb, jnp.clip(j - Q_STEP0, 0, N_Q_TILES - 1), i % nb)),
            pl.BlockSpec((1, N_KV_HEADS, tm, HEAD_DIM), lambda i, j: (i // nb, kv_tile(j), i % nb, 0)),
            pl.BlockSpec((2, 1, N_KV_HEADS, tm, HEAD_DIM), lambda i, j: (0, i // nb, 0, i % nb, 0)),
            pl.BlockSpec((1, KV_W, tm), lambda i, j: (i // nb, kv_tile(j), i % nb)),
            pl.BlockSpec((1, N_GATES, tm), lambda i, j: (i // nb, 0, i % nb)),
        ] + side_specs,
        out_shape=[
            jax.ShapeDtypeStruct((T, CONV_W), F32),
            jax.ShapeDtypeStruct((B, ATTN_W, seq), BF16),
            jax.ShapeDtypeStruct((B, N_KV_TILES * N_KV_HEADS, seq, HEAD_DIM), BF16),
            jax.ShapeDtypeStruct((2, B, N_KV_HEADS, seq, HEAD_DIM), F32),
            jax.ShapeDtypeStruct((B, KV_OUT_W, seq), BF16),
            jax.ShapeDtypeStruct((B, N_GATES, seq), F32),
        ] + [jax.ShapeDtypeStruct(w.shape, BF16) for w in side],
        scratch_shapes=[pltpu.VMEM((tm, D), BF16), pltpu.VMEM((CONV_GROUPS, SUBLANES, tn), F32),
                        pltpu.VMEM((tm, LANES), F32)],
        compiler_params=_params(("arbitrary", "arbitrary"), vmem),
        name="in_projection",
    )(x2, g, w_proj, w_proj, w_proj, w_gates, cos_t, sin_t, conv_w, *side)


def _compress_body(x_ref, pe_ref, w1_ref, w2_ref, o_ref, ot_ref, *, n_valid):
    n = x_ref.shape[2] // CMP_STRIDE
    x = jnp.concatenate([x_ref[0, 0, pl.ds(l, n, stride=CMP_STRIDE), :] for l in range(CMP_STRIDE)],
                        axis=1).astype(BF16)
    half = x.shape[1]
    w1 = w1_ref[0]
    w1_hi = w1.astype(BF16)
    w1_lo = (w1 - w1_hi.astype(F32)).astype(BF16)
    first = _dot(x, w1_hi[:half]) + _dot(x, w1_lo[:half])
    second = _dot(x, w1_hi[half:]) + _dot(x, w1_lo[half:])
    second = pltpu.roll(second, n - 1, axis=0)
    pe = jnp.broadcast_to(pe_ref[0], (SUBLANES, 2 * half))
    pe_term = _dot_f32(pe, w1)[0:1]
    h = first + second + pe_term
    out = _dot_f32(jax.nn.silu(h), w2_ref[0])
    row = lax.broadcasted_iota(jnp.int32, out.shape, 0)
    out = jnp.where(row < n_valid, out, 0.0)
    o_ref[0, 0] = out
    ot_ref[0, 0] = out.T


def _compress(kv, pe, w1, w2):
    two, BH, seq, d = kv.shape
    n = seq // CMP_STRIDE
    width = CMP_STRIDE * d
    body = functools.partial(_compress_body, n_valid=n - 1)
    vmem = 2 * (seq * d * 4 + 2 * width * 4 + 2 * width * d * 4 + d * d * 4 + n * d * 4) \
        + 8 * width * d * 4
    return pl.pallas_call(
        body,
        grid=(two, BH),
        in_specs=[
            pl.BlockSpec((1, 1, seq, d), lambda s, b: (s, b, 0, 0)),
            pl.BlockSpec((1, 1, 2 * width), lambda s, b: (s, 0, 0)),
            pl.BlockSpec((1, 2 * width, d), lambda s, b: (s, 0, 0)),
            pl.BlockSpec((1, d, d), lambda s, b: (s, 0, 0)),
        ],
        out_specs=[pl.BlockSpec((1, 1, n, d), lambda s, b: (s, b, 0, 0)),
                   pl.BlockSpec((1, 1, d, n), lambda s, b: (s, b, 0, 0))],
        out_shape=[jax.ShapeDtypeStruct((two, BH, n, d), F32),
                   jax.ShapeDtypeStruct((two, BH, d, n), F32)],
        compiler_params=_params(("parallel", "parallel"), vmem),
        name="compress",
    )(kv, pe, w1, w2)


def _select_blocks(imp, q0):
    n_sel = imp.shape[0]
    sel_i = lax.broadcasted_iota(jnp.int32, imp.shape, 0)
    sel_f = sel_i.astype(F32)
    cur = (q0 + lax.broadcasted_iota(jnp.int32, imp.shape, 1)) // SEL_BLOCK
    forced = (sel_i == 0) | (sel_i == cur) | (sel_i == cur - 1)
    v = jnp.where((sel_i > cur) | forced, -jnp.inf, imp)
    chosen = jnp.where(forced, 1.0, 0.0)
    for _ in range(min(N_SELECT, n_sel) - N_FORCED):
        top = jnp.max(v, axis=0, keepdims=True)
        first = jnp.min(jnp.where(v == top, sel_f, float(n_sel)), axis=0, keepdims=True)
        pick = sel_f == first
        chosen = jnp.where(pick, 1.0, chosen)
        v = jnp.where(pick, -jnp.inf, v)
    return jnp.where((chosen > 0.0) & (sel_i <= cur), 0.0, MASK_BIAS)


def _attn_body(qt_ref, zgt_ref, kc_ref, vct_ref, ovt_ref, ks_ref, vst_ref, e_ref, kw_ref, vwt_ref, wb_ref,
               y_ref, s_scr, m_scr, acc_scr, oc_scr, ow_scr, *, tk, span):
    q0 = pl.program_id(2) * ATTN_Q
    qt = [qt_ref[0, g * HEAD_DIM:(g + 1) * HEAD_DIM, :] for g in range(GROUP)]
    q_all = jnp.concatenate(qt, axis=1)

    def lane_time(shape):
        return q0 + (lax.broadcasted_iota(jnp.int32, shape, 1) & (ATTN_Q - 1))

    s = _dot(kc_ref[0, 0].astype(BF16), q_all)
    cmp_end = lax.broadcasted_iota(jnp.int32, s.shape, 0) * CMP_STRIDE + (CMP_BLOCK - 1)
    s = jnp.where(cmp_end <= lane_time(s.shape), s, -jnp.inf)
    m = jnp.max(s, axis=0, keepdims=True)
    m = jnp.where(m == -jnp.inf, 0.0, m)
    e = jnp.exp2(s - m)
    p = e / jnp.maximum(jnp.sum(e, axis=0, keepdims=True), 1e-30)
    oc_scr[...] = _dot(vct_ref[0, 0].astype(BF16), p.astype(BF16))
    pg = p[:, 0:ATTN_Q]
    for g in range(1, GROUP):
        pg = pg + p[:, g * ATTN_Q:(g + 1) * ATTN_Q]
    ovt = ovt_ref[...]
    p0, p1, p2 = _split3(pg)
    bias = _select_blocks(_dot(ovt, p0) + _dot(ovt, p1) + _dot(ovt, p2), q0).astype(BF16)

    w_start = pl.multiple_of(jnp.maximum(q0 - WINDOW, 0), ATTN_Q)
    sw = _dot(kw_ref[0, 0, pl.ds(w_start, span), :], q_all)
    sw = sw + jnp.concatenate([wb_ref[0]] * GROUP, axis=1)
    ew = jnp.exp2(sw - jnp.max(sw, axis=0, keepdims=True))
    vw = jnp.concatenate([vwt_ref[0, :, pl.ds(w_start, span)], jnp.ones((ONES_ROWS, span), BF16)], axis=0)
    ow = _dot(vw, ew.astype(BF16))
    ow_scr[...] = ow[0:HEAD_DIM] / ow[HEAD_DIM:HEAD_DIM + 1]

    qx = jnp.concatenate([jnp.concatenate([qt[g], bias], axis=0) for g in range(GROUP)], axis=1)

    def scores(j, slot):
        start = pl.multiple_of(j * tk, tk)
        kx = jnp.concatenate([ks_ref[0, 0, pl.ds(start, tk), :], e_ref[pl.ds(start, tk), :]], axis=1)
        s_scr[slot] = _dot(kx, qx)

    def update(j, slot, causal):
        start = pl.multiple_of(j * tk, tk)
        vt = jnp.concatenate([vst_ref[0, :, pl.ds(start, tk)], jnp.ones((ONES_ROWS, tk), BF16)], axis=0)
        sc = s_scr[slot]
        if causal:
            key = start + lax.broadcasted_iota(jnp.int32, sc.shape, 0)
            sc = jnp.where(key <= lane_time(sc.shape), sc, -jnp.inf)
        m_old = m_scr[...]
        m_new = jnp.maximum(m_old, jnp.max(sc, axis=0, keepdims=True))
        alpha = jnp.exp2(m_old - m_new)
        pj = jnp.exp2(sc - m_new)
        m_scr[...] = m_new
        acc_scr[...] = alpha * acc_scr[...] + _dot(vt, pj.astype(BF16))

    m_scr[...] = jnp.full(m_scr.shape, -jnp.inf, F32)
    acc_scr[...] = jnp.zeros(acc_scr.shape, F32)
    scores(0, 0)
    last = q0 // tk

    @pl.loop(0, last // 2)
    def _(i):
        scores(2 * i + 1, 1)
        update(2 * i, 0, False)
        scores(2 * i + 2, 0)
        update(2 * i + 1, 1, False)

    @pl.when(last % 2 == 1)
    def _():
        scores(last, 1)
        update(last - 1, 0, False)
        update(last, 1, True)

    @pl.when(last % 2 == 0)
    def _():
        update(last, 0, True)

    o_sel = acc_scr[0:HEAD_DIM, :] / acc_scr[HEAD_DIM:HEAD_DIM + 1, :]
    gates = jax.nn.sigmoid(zgt_ref[0])
    hk = pl.program_id(1)

    def gate(g, branch):
        row = gates[g * N_BRANCH + branch:g * N_BRANCH + branch + 1]
        for other in range(1, N_KV_HEADS):
            r = (other * GROUP + g) * N_BRANCH + branch
            row = jnp.where(hk == other, gates[r:r + 1], row)
        return row

    for g in range(GROUP):
        sl = slice(g * ATTN_Q, (g + 1) * ATTN_Q)
        y = (gate(g, 0) * oc_scr[:, sl] + gate(g, 1) * o_sel[:, sl]
             + gate(g, 2) * ow_scr[:, sl])
        y_ref[0, :, g * HEAD_DIM:(g + 1) * HEAD_DIM] = y.T


def _attention(qt, zgt, kvc, kvct, ovt, kk, vt, onehot, B, seq):
    n_cmp = kvc.shape[2]
    n_sel = ovt.shape[0]
    GW = GROUP * HEAD_DIM
    GQ = GROUP * ATTN_Q
    VR = HEAD_DIM + ONES_ROWS
    tk = min(512, seq)
    span = WINDOW + ATTN_Q
    assert seq >= span
    body = functools.partial(_attn_body, tk=tk, span=span)
    resident = 4 * seq * HEAD_DIM * 2 + seq * n_sel * 2
    vmem = 2 * (GW * ATTN_Q * 2 + N_GATES * ATTN_Q * 4 + 2 * n_cmp * HEAD_DIM * 4 + n_sel * n_cmp * 2
                + resident + ATTN_Q * GW * 4) \
        + (2 * tk + VR + 2 * HEAD_DIM + SUBLANES) * GQ * 4 + 8 * (n_cmp + span) * GQ * 4
    return pl.pallas_call(
        body,
        grid=(B, N_KV_HEADS, seq // ATTN_Q),
        in_specs=[
            pl.BlockSpec((1, GW, ATTN_Q), lambda b, h, i: (b, h, i)),
            pl.BlockSpec((1, N_GATES, ATTN_Q), lambda b, h, i: (b, 0, i)),
            pl.BlockSpec((1, 1, n_cmp, HEAD_DIM), lambda b, h, i: (0, b * N_KV_HEADS + h, 0, 0)),
            pl.BlockSpec((1, 1, HEAD_DIM, n_cmp), lambda b, h, i: (1, b * N_KV_HEADS + h, 0, 0)),
            pl.BlockSpec((n_sel, n_cmp), lambda b, h, i: (0, 0)),
            pl.BlockSpec((1, 1, seq, HEAD_DIM), lambda b, h, i: (b, N_KV_HEADS + h, 0, 0)),
            pl.BlockSpec((1, HEAD_DIM, seq), lambda b, h, i: (b, N_KV_HEADS + h, 0)),
            pl.BlockSpec((seq, n_sel), lambda b, h, i: (0, 0)),
            pl.BlockSpec((1, 1, seq, HEAD_DIM), lambda b, h, i: (b, 2 * N_KV_HEADS + h, 0, 0)),
            pl.BlockSpec((1, HEAD_DIM, seq), lambda b, h, i: (b, 2 * N_KV_HEADS + h, 0)),
            pl.BlockSpec((1, span, ATTN_Q), lambda b, h, i: (jnp.minimum(i, WINDOW // ATTN_Q), 0, 0)),
        ],
        out_specs=pl.BlockSpec((1, ATTN_Q, GW), lambda b, h, i: (b, i, h)),
        out_shape=jax.ShapeDtypeStruct((B, seq, ATTN_W), F32),
        scratch_shapes=[pltpu.VMEM((2, tk, GQ), F32),
                        pltpu.VMEM((1, GQ), F32),
                        pltpu.VMEM((VR, GQ), F32),
                        pltpu.VMEM((HEAD_DIM, GQ), F32),
                        pltpu.VMEM((HEAD_DIM, GQ), F32)],
        compiler_params=_params(("parallel", "parallel", "arbitrary"), vmem),
        name="attention",
    )(qt, zgt, kvc, kvct, ovt, kk, vt, onehot, kk, vt, _window_bias(span))


def _outproj_body(yc_ref, ya_ref, gc_ref, ga_ref, w_ref, x_ref, h_ref, a_scr):
    @pl.when(pl.program_id(1) == 0)
    def _():
        a_scr[:, 0:CONV_W] = _rms(yc_ref[...], gc_ref[...]).astype(BF16)
        a_scr[:, CONV_W:] = _rms(ya_ref[...], ga_ref[...]).astype(BF16)

    h_ref[...] = x_ref[...] + _dot(a_scr[...], w_ref[...])


def _out_projection(yc, ya, g_conv, g_attn, w, x2):
    T, D = x2.shape
    K = w.shape[0]
    tm = min(1024, T)
    tn = min(1024, D)
    vmem = 2 * (tm * CONV_W * 4 + tm * ATTN_W * 4 + K * tn * 2 + 2 * tm * tn * 4) + tm * K * 2 \
        + 3 * tm * ATTN_W * 4
    return pl.pallas_call(
        _outproj_body,
        grid=(T // tm, D // tn),
        in_specs=[
            pl.BlockSpec((tm, CONV_W), lambda i, j: (i, 0)),
            pl.BlockSpec((tm, ATTN_W), lambda i, j: (i, 0)),
            pl.BlockSpec((1, CONV_W), lambda i, j: (0, 0)),
            pl.BlockSpec((1, ATTN_W), lambda i, j: (0, 0)),
            pl.BlockSpec((K, tn), lambda i, j: (0, j)),
            pl.BlockSpec((tm, tn), lambda i, j: (i, j)),
        ],
        out_specs=pl.BlockSpec((tm, tn), lambda i, j: (i, j)),
        out_shape=jax.ShapeDtypeStruct((T, D), F32),
        scratch_shapes=[pltpu.VMEM((tm, K), BF16)],
        compiler_params=_params(("parallel", "arbitrary"), vmem),
        name="out_projection",
    )(yc, ya, g_conv, g_attn, w, x2)


def _ffn_body(h_ref, g_ref, wg_ref, wu_ref, wd_ref, gf_ref, o_ref, a_scr, *, final_norm, tn):
    j = pl.program_id(1)

    @pl.when(j == 0)
    def _():
        @pl.loop(0, h_ref.shape[0] // FFN_EPILOGUE_ROWS)
        def _(r):
            rows = pl.ds(pl.multiple_of(r * FFN_EPILOGUE_ROWS, FFN_EPILOGUE_ROWS), FFN_EPILOGUE_ROWS)
            a_scr[rows, :] = _rms(h_ref[rows, :], g_ref[...]).astype(BF16)

    a = a_scr[...]
    u = jnp.concatenate(
        [(jax.nn.silu(_dot(a, wg_ref[:, c:c + PROJ_CHUNK])) * _dot(a, wu_ref[:, c:c + PROJ_CHUNK])).astype(BF16)
         for c in range(0, wg_ref.shape[1], PROJ_CHUNK)], axis=1)
    D = o_ref.shape[1]

    @pl.when(j == 0)
    def _():
        for n in range(D // tn):
            o_ref[:, n * tn:(n + 1) * tn] = _dot(u, wd_ref[:, n * tn:(n + 1) * tn])

    @pl.when(j > 0)
    def _():
        for n in range(D // tn):
            o_ref[:, n * tn:(n + 1) * tn] += _dot(u, wd_ref[:, n * tn:(n + 1) * tn])

    @pl.when(j == pl.num_programs(1) - 1)
    def _():
        @pl.loop(0, o_ref.shape[0] // FFN_EPILOGUE_ROWS)
        def _(r):
            rows = pl.ds(pl.multiple_of(r * FFN_EPILOGUE_ROWS, FFN_EPILOGUE_ROWS), FFN_EPILOGUE_ROWS)
            out = h_ref[rows, :] + o_ref[rows, :]
            o_ref[rows, :] = _rms(out, gf_ref[...]) if final_norm else out


def _ffn(h, g, wg, wu, wd, g_final, final_norm):
    T, D = h.shape
    F = wg.shape[1]
    tm = min(1024, T)
    tf = 512
    tn = PROJ_CHUNK
    assert F % tf == 0 and D % tn == 0
    body = functools.partial(_ffn_body, final_norm=final_norm, tn=tn)
    vmem = 2 * (3 * D * tf * 2 + 2 * tm * D * 4) + tm * D * 2 + 6 * tm * tf * 4 + 2 * tm * tn * 4
    return pl.pallas_call(
        body,
        grid=(T // tm, F // tf),
        in_specs=[
            pl.BlockSpec((tm, D), lambda i, j: (i, 0)),
            pl.BlockSpec((1, D), lambda i, j: (0, 0)),
            pl.BlockSpec((D, tf), lambda i, j: (0, j)),
            pl.BlockSpec((D, tf), lambda i, j: (0, j)),
            pl.BlockSpec((tf, D), lambda i, j: (j, 0)),
            pl.BlockSpec((1, D), lambda i, j: (0, 0)),
        ],
        out_specs=pl.BlockSpec((tm, D), lambda i, j: (i, 0)),
        out_shape=jax.ShapeDtypeStruct((T, D), F32),
        scratch_shapes=[pltpu.VMEM((tm, D), BF16)],
        compiler_params=_params(("parallel", "arbitrary"), vmem),
        name="ffn",
    )(h, g, wg, wu, wd, g_final)


def _rope_tables(seq):
    half = HEAD_DIM // 2
    inv = 1.0 / (ROPE_THETA ** (jnp.arange(half, dtype=F32) / half))
    lo = jnp.arange(ROPE_SPLIT).astype(F32)[:, None] * inv[None, :]
    hi = (jnp.arange(seq // ROPE_SPLIT) * ROPE_SPLIT).astype(F32)[:, None] * inv[None, :]
    cos_lo, sin_lo = jnp.cos(lo)[None], jnp.sin(lo)[None]
    cos_hi, sin_hi = jnp.cos(hi)[:, None, :], jnp.sin(hi)[:, None, :]
    cos = (cos_hi * cos_lo - sin_hi * sin_lo).reshape(seq, half)
    sin = (sin_hi * cos_lo + cos_hi * sin_lo).reshape(seq, half)
    return jnp.concatenate([cos, cos], axis=1), jnp.concatenate([-sin, sin], axis=1)


def _overlap_t(n_sel, n_cmp):
    cs = np.arange(n_cmp)[None, :] * CMP_STRIDE
    ss = np.arange(n_sel)[:, None] * SEL_BLOCK
    return jnp.asarray(((cs < ss + SEL_BLOCK) & (cs + CMP_BLOCK > ss)).astype(np.float32), dtype=BF16)


def _window_bias(span):
    r = np.arange(span)[None, :, None]
    c = np.arange(ATTN_Q)[None, None, :]
    off = np.minimum(np.arange(WINDOW // ATTN_Q + 1) * ATTN_Q, WINDOW)[:, None, None]
    valid = (r <= off + c) & (r > off + c - WINDOW)
    return jnp.asarray(np.where(valid, 0.0, -np.inf).astype(np.float32))


def _block_onehot(seq, n_sel):
    return jnp.asarray((np.arange(seq)[:, None] // SEL_BLOCK == np.arange(n_sel)[None, :])
                       .astype(np.float32), dtype=BF16)


def kernel(x, norm_mix, w_in, conv_w, cmp_pe_k, cmp_w1_k, cmp_w2_k, cmp_pe_v, cmp_w1_v, cmp_w2_v,
           norm_conv_out, norm_attn_out, w_out, norm_ffn, w_gate, w_up, w_down, norm_final):
    B, S, D = x.shape
    T = B * S
    depth = norm_mix.shape[0]
    assert S % ATTN_Q == 0 and S % CMP_STRIDE == 0 and S % ROPE_SPLIT == 0
    n_chunks = S // CMP_STRIDE
    n_sel = S // SEL_BLOCK
    cos_t, sin_t = _rope_tables(S)
    ovt = _overlap_t(n_sel, n_chunks)
    onehot = _block_onehot(S, n_sel)

    h = x.reshape(T, D)
    for l in range(depth):
        y_conv, qt, kk, kvc_in, vt, zgt, w_out_b, w_gate_b, w_up_b, w_down_b = _in_projection(
            h, norm_mix[l][None], w_in[l].astype(BF16), _gate_weight(w_in[l]), cos_t, sin_t, conv_w[l],
            [w_out[l], w_gate[l], w_up[l], w_down[l]], B, S)
        kvc_in = kvc_in.reshape(2, B * N_KV_HEADS, S, HEAD_DIM)
        pe = jnp.stack([cmp_pe_k[l], cmp_pe_v[l]]).reshape(2, 1, CMP_BLOCK * HEAD_DIM)
        kvc, kvct = _compress(kvc_in, pe, jnp.stack([cmp_w1_k[l], cmp_w1_v[l]]),
                              jnp.stack([cmp_w2_k[l], cmp_w2_v[l]]))

        y_attn = _attention(qt, zgt, kvc, kvct, ovt, kk, vt, onehot, B, S)

        h = _out_projection(y_conv, y_attn.reshape(T, ATTN_W), norm_conv_out[l][None],
                            norm_attn_out[l][None], w_out_b, h)

        h = _ffn(h, norm_ffn[l][None], w_gate_b, w_up_b, w_down_b, norm_final[None], l == depth - 1)
    return h.reshape(B, S, D)
```

```python
import functools

import jax
import jax.numpy as jnp
import numpy as np
from jax import lax
from jax.experimental import pallas as pl
from jax.experimental.pallas import tpu as pltpu

F32 = jnp.float32
BF16 = jnp.bfloat16

HEAD_DIM = 128
N_HEADS = 8
N_KV_HEADS = 2
GROUP = N_HEADS // N_KV_HEADS
N_BRANCH = 3
CONV_TAPS = 3
CMP_BLOCK = 32
CMP_STRIDE = 16
SEL_BLOCK = 64
N_SELECT = 16
N_FORCED = 3
WINDOW = 512
ATTN_Q = 256
ROPE_THETA = 10000.0
ROPE_SPLIT = 128
RMS_EPS = 1e-6

CONV_W = 1024
ATTN_W = N_HEADS * HEAD_DIM
KV_W = N_KV_HEADS * HEAD_DIM
N_GATES = N_HEADS * N_BRANCH

V7X_VMEM_BYTES = 64 * 1024 * 1024
V7X_VMEM_BUDGET = V7X_VMEM_BYTES - 6 * 1024 * 1024
LANES = 128
SUBLANES = 8
PROJ_TILE_N = 512
PROJ_CHUNK = 256
FFN_EPILOGUE_ROWS = 256

MASK_BIAS = -float(2 ** 30)
LOG2_E = 1.4426950408889634
ONES_ROWS = 16


def _params(semantics, vmem_bytes):
    return pltpu.CompilerParams(dimension_semantics=semantics,
                                vmem_limit_bytes=int(min(vmem_bytes, V7X_VMEM_BUDGET)))


def _rms(x, g):
    return x * lax.rsqrt(jnp.mean(x * x, axis=-1, keepdims=True) + RMS_EPS) * g


def _split3(x):
    hi = x.astype(BF16)
    r1 = x - hi.astype(F32)
    mid = r1.astype(BF16)
    lo = (r1 - mid.astype(F32)).astype(BF16)
    return hi, mid, lo


def _dot(a, b):
    return jnp.dot(a, b, preferred_element_type=F32)


def _dot_f32(a, b):
    a0, a1, a2 = _split3(a)
    b0, b1, b2 = _split3(b)
    return (_dot(a0, b0) + (_dot(a0, b1) + _dot(a1, b0))
            + (_dot(a0, b2) + _dot(a1, b1) + _dot(a2, b0)))


CONV_GROUPS = CONV_W // PROJ_TILE_N
N_Q_TILES = ATTN_W // PROJ_TILE_N
N_KV_TILES = 3
GATE_STEP = 0
Q_STEP0 = GATE_STEP + 1
KV_STEP0 = Q_STEP0 + N_Q_TILES
CONV_STEP0 = KV_STEP0 + N_KV_TILES
N_PROJ_STEPS = CONV_STEP0 + CONV_GROUPS
N_WEIGHT_TILES = 3 * CONV_GROUPS + N_Q_TILES + N_KV_TILES
KV_OUT_W = N_KV_TILES * KV_W
assert 2 * KV_W == PROJ_TILE_N and N_GATES % SUBLANES == 0


def _gate_weight(w):
    gates = w[:, N_WEIGHT_TILES * PROJ_TILE_N:].astype(BF16)
    return jnp.pad(gates, ((0, 0), (0, LANES - gates.shape[1])))


def _inproj_body(*refs, q_scale, blocks_per_seq, side_blocks):
    n_side = len(side_blocks)
    x_ref, g_ref, w_ref, wc_ref, wh_ref, wg_ref, cos_ref, sin_ref, cw_ref = refs[:9]
    side_in = refs[9:9 + n_side]
    yc_ref, qt_ref, k_ref, kvc_ref, vt_ref, zgt_ref = refs[9 + n_side:15 + n_side]
    side_out = refs[15 + n_side:15 + 2 * n_side]
    a_scr, tail_scr, zg_scr = refs[15 + 2 * n_side:]
    i = pl.program_id(0)
    j = pl.program_id(1)

    step = i * N_PROJ_STEPS + j
    for src, dst, n_blocks in zip(side_in, side_out, side_blocks):
        @pl.when(step < n_blocks)
        def _(src=src, dst=dst):
            dst[...] = src[...].astype(BF16)

    @pl.when(j == 0)
    def _():
        a_scr[...] = _rms(x_ref[...], g_ref[...]).astype(BF16)

    @pl.when((i == 0) & (j == 0))
    def _():
        tail_scr[...] = jnp.zeros(tail_scr.shape, F32)

    @pl.when(j == GATE_STEP)
    def _():
        zg_scr[...] = _dot(a_scr[...], wg_ref[...])
        zgt_ref[0] = zg_scr[...].T[0:N_GATES]

    chunks = [(c, c + PROJ_CHUNK) for c in range(0, PROJ_TILE_N, PROJ_CHUNK)]

    def z_cols(weight_ref, c0, c1):
        return _dot(a_scr[...], weight_ref[:, c0:c1])

    def rope(zh):
        cos = cos_ref[...]
        sin = sin_ref[...]
        heads = []
        for h in range(zh.shape[1] // HEAD_DIM):
            z1 = zh[:, h * HEAD_DIM:(h + 1) * HEAD_DIM]
            heads.append(z1 * cos + pltpu.roll(z1, HEAD_DIM // 2, axis=1) * sin)
        return jnp.concatenate(heads, axis=1)

    @pl.when(j >= CONV_STEP0)
    def _():
        group = j - CONV_STEP0
        for c0, c1 in chunks:
            u = z_cols(wc_ref, c0, c1) * z_cols(wh_ref, c0, c1)
            tail = tail_scr[group, :, c0:c1]
            tail = jnp.where(i % blocks_per_seq == 0, 0.0, tail)
            row = lax.broadcasted_iota(jnp.int32, u.shape, 0)
            t1 = tail[SUBLANES - 1:SUBLANES]
            t2 = tail[SUBLANES - 2:SUBLANES - 1]
            u1 = jnp.where(row == 0, t1, pltpu.roll(u, 1, axis=0))
            u2 = jnp.where(row == 0, t2, jnp.where(row == 1, t1, pltpu.roll(u, 2, axis=0)))
            w = cw_ref[:, c0:c1]
            yc_ref[:, c0:c1] = z_cols(w_ref, c0, c1) * (u2 * w[0:1] + u1 * w[1:2] + u * w[2:3])
            tail_scr[group, :, c0:c1] = u[u.shape[0] - SUBLANES:, :]

    @pl.when((j >= Q_STEP0) & (j < KV_STEP0))
    def _():
        for c0, c1 in chunks:
            qt_ref[0, c0:c1, :] = (rope(z_cols(w_ref, c0, c1)) * q_scale).T.astype(BF16)

    @pl.when((j >= KV_STEP0) & (j < CONV_STEP0))
    def _():
        kx = rope(z_cols(w_ref, 0, KV_W)).astype(BF16)
        for h in range(N_KV_HEADS):
            k_ref[0, h] = kx[:, h * HEAD_DIM:(h + 1) * HEAD_DIM]
        zv = z_cols(w_ref, KV_W, 2 * KV_W)
        vt_ref[0] = zv.T.astype(BF16)

        @pl.when(j == KV_STEP0)
        def _():
            for h in range(N_KV_HEADS):
                kvc_ref[0, 0, h] = kx[:, h * HEAD_DIM:(h + 1) * HEAD_DIM].astype(F32)
                kvc_ref[1, 0, h] = zv[:, h * HEAD_DIM:(h + 1) * HEAD_DIM].astype(BF16).astype(F32)


def _side_rows(n_rows, n_steps):
    for rows in range(2 * SUBLANES, n_rows + 1, 2 * SUBLANES):
        if n_rows % rows == 0 and n_rows // rows <= n_steps:
            return rows
    raise ValueError((n_rows, n_steps))


def _in_projection(x2, g, w_proj, w_gates, cos_t, sin_t, conv_w, side, B, seq):
    T, D = x2.shape
    tn = PROJ_TILE_N
    assert w_proj.shape[1] >= N_WEIGHT_TILES * tn
    tm = min(1024, seq)
    nb = seq // tm
    n_steps = (T // tm) * N_PROJ_STEPS
    side_rows = [_side_rows(w.shape[0], n_steps) for w in side]
    side_blocks = [w.shape[0] // r for w, r in zip(side, side_rows)]
    side_specs = [pl.BlockSpec((r, w.shape[1]),
                               lambda i, j, n=n: (jnp.minimum(i * N_PROJ_STEPS + j, n - 1), 0))
                  for w, r, n in zip(side, side_rows, side_blocks)]
    body = functools.partial(_inproj_body, q_scale=HEAD_DIM ** -0.5 * LOG2_E, blocks_per_seq=nb,
                             side_blocks=tuple(side_blocks))
    vmem = 2 * (tm * D * 4 + 3 * D * tn * 2 + 2 * tm * HEAD_DIM * 4 + tm * tn * 4 + 4 * tm * tn * 2
                + N_GATES * tm * 4) + tm * D * 2 + 8 * tm * PROJ_CHUNK * 4 \
        + 2 * sum(r * w.shape[1] * 6 for w, r in zip(side, side_rows))

    def conv_group(j):
        return jnp.maximum(j - CONV_STEP0, 0)

    def main_tile(j):
        return jnp.where(j >= CONV_STEP0, j - CONV_STEP0,
                         jnp.maximum(j - Q_STEP0, 0) + 3 * CONV_GROUPS)

    def kv_tile(j):
        return jnp.clip(j - KV_STEP0, 0, N_KV_TILES - 1)

    return pl.pallas_call(
        body,
        grid=(T // tm, N_PROJ_STEPS),
        in_specs=[
            pl.BlockSpec((tm, D), lambda i, j: (i, 0)),
            pl.BlockSpec((1, D), lambda i, j: (0, 0)),
            pl.BlockSpec((D, tn), lambda i, j: (0, main_tile(j))),
            pl.BlockSpec((D, tn), lambda i, j: (0, CONV_GROUPS + conv_group(j))),
            pl.BlockSpec((D, tn), lambda i, j: (0, 2 * CONV_GROUPS + conv_group(j))),
            pl.BlockSpec((D, LANES), lambda i, j: (0, 0)),
            pl.BlockSpec((tm, HEAD_DIM), lambda i, j: (i % nb, 0)),
            pl.BlockSpec((tm, HEAD_DIM), lambda i, j: (i % nb, 0)),
            pl.BlockSpec((CONV_TAPS, tn), lambda i, j: (0, conv_group(j))),
        ] + side_specs,
        out_specs=[
            pl.BlockSpec((tm, tn), lambda i, j: (i, conv_group(j))),
            pl.BlockSpec((1, tn, tm), lambda i, j: (i // nb, jnp.clip(j - Q_STEP0, 0, N_Q_TILES - 1), i % nb)),
            pl.BlockSpec((1, N_KV_HEADS, tm, HEAD_DIM), lambda i, j: (i // nb, kv_tile(j), i % nb, 0)),
            pl.BlockSpec((2, 1, N_KV_HEADS, tm, HEAD_DIM), lambda i, j: (0, i // nb, 0, i % nb, 0)),
            pl.BlockSpec((1, KV_W, tm), lambda i, j: (i // nb, kv_tile(j), i % nb)),
            pl.BlockSpec((1, N_GATES, tm), lambda i, j: (i // nb, 0, i % nb)),
        ] + side_specs,
        out_shape=[
            jax.ShapeDtypeStruct((T, CONV_W), F32),
            jax.ShapeDtypeStruct((B, ATTN_W, seq), BF16),
            jax.ShapeDtypeStruct((B, N_KV_TILES * N_KV_HEADS, seq, HEAD_DIM), BF16),
            jax.ShapeDtypeStruct((2, B, N_KV_HEADS, seq, HEAD_DIM), F32),
            jax.ShapeDtypeStruct((B, KV_OUT_W, seq), BF16),
            jax.ShapeDtypeStruct((B, N_GATES, seq), F32),
        ] + [jax.ShapeDtypeStruct(w.shape, BF16) for w in side],
        scratch_shapes=[pltpu.VMEM((tm, D), BF16), pltpu.VMEM((CONV_GROUPS, SUBLANES, tn), F32),
                        pltpu.VMEM((tm, LANES), F32)],
        compiler_params=_params(("arbitrary", "arbitrary"), vmem),
        name="in_projection",
    )(x2, g, w_proj, w_proj, w_proj, w_gates, cos_t, sin_t, conv_w, *side)


def _compress_body(x_ref, pe_ref, w1_ref, w2_ref, o_ref, ot_ref, *, n_valid):
    n = x_ref.shape[2] // CMP_STRIDE
    x = jnp.concatenate([x_ref[0, 0, pl.ds(l, n, stride=CMP_STRIDE), :] for l in range(CMP_STRIDE)],
                        axis=1).astype(BF16)
    half = x.shape[1]
    w1 = w1_ref[0]
    w1_hi = w1.astype(BF16)
    w1_lo = (w1 - w1_hi.astype(F32)).astype(BF16)
    first = _dot(x, w1_hi[:half]) + _dot(x, w1_lo[:half])
    second = _dot(x, w1_hi[half:]) + _dot(x, w1_lo[half:])
    second = pltpu.roll(second, n - 1, axis=0)
    pe = jnp.broadcast_to(pe_ref[0], (SUBLANES, 2 * half))
    pe_term = _dot_f32(pe, w1)[0:1]
    h = first + second + pe_term
    out = _dot_f32(jax.nn.silu(h), w2_ref[0])
    row = lax.broadcasted_iota(jnp.int32, out.shape, 0)
    out = jnp.where(row < n_valid, out, 0.0)
    o_ref[0, 0] = out
    ot_ref[0, 0] = out.T


def _compress(kv, pe, w1, w2):
    two, BH, seq, d = kv.shape
    n = seq // CMP_STRIDE
    width = CMP_STRIDE * d
    body = functools.partial(_compress_body, n_valid=n - 1)
    vmem = 2 * (seq * d * 4 + 2 * width * 4 + 2 * width * d * 4 + d * d * 4 + n * d * 4) \
        + 8 * width * d * 4
    return pl.pallas_call(
        body,
        grid=(two, BH),
        in_specs=[
            pl.BlockSpec((1, 1, seq, d), lambda s, b: (s, b, 0, 0)),
            pl.BlockSpec((1, 1, 2 * width), lambda s, b: (s, 0, 0)),
            pl.BlockSpec((1, 2 * width, d), lambda s, b: (s, 0, 0)),
            pl.BlockSpec((1, d, d), lambda s, b: (s, 0, 0)),
        ],
        out_specs=[pl.BlockSpec((1, 1, n, d), lambda s, b: (s, b, 0, 0)),
                   pl.BlockSpec((1, 1, d, n), lambda s, b: (s, b, 0, 0))],
        out_shape=[jax.ShapeDtypeStruct((two, BH, n, d), F32),
                   jax.ShapeDtypeStruct((two, BH, d, n), F32)],
        compiler_params=_params(("parallel", "parallel"), vmem),
        name="compress",
    )(kv, pe, w1, w2)


def _select_blocks(imp, q0):
    n_sel = imp.shape[0]
    sel_i = lax.broadcasted_iota(jnp.int32, imp.shape, 0)
    sel_f = sel_i.astype(F32)
    cur = (q0 + lax.broadcasted_iota(jnp.int32, imp.shape, 1)) // SEL_BLOCK
    forced = (sel_i == 0) | (sel_i == cur) | (sel_i == cur - 1)
    v = jnp.where((sel_i > cur) | forced, -jnp.inf, imp)
    chosen = jnp.where(forced, 1.0, 0.0)
    for _ in range(min(N_SELECT, n_sel) - N_FORCED):
        top = jnp.max(v, axis=0, keepdims=True)
        first = jnp.min(jnp.where(v == top, sel_f, float(n_sel)), axis=0, keepdims=True)
        pick = sel_f == first
        chosen = jnp.where(pick, 1.0, chosen)
        v = jnp.where(pick, -jnp.inf, v)
    return jnp.where((chosen > 0.0) & (sel_i <= cur), 0.0, MASK_BIAS)


def _attn_body(qt_ref, zgt_ref, kc_ref, vct_ref, ovt_ref, ks_ref, vst_ref, e_ref, kw_ref, vwt_ref, wb_ref,
               y_ref, s_scr, m_scr, l_scr, acc_scr, oc_scr, ow_scr, *, tk, span):
    q0 = pl.program_id(2) * ATTN_Q
    qt = [qt_ref[0, g * HEAD_DIM:(g + 1) * HEAD_DIM, :] for g in range(GROUP)]
    q_all = jnp.concatenate(qt, axis=1)

    def lane_time(shape):
        return q0 + (lax.broadcasted_iota(jnp.int32, shape, 1) & (ATTN_Q - 1))

    s = _dot(kc_ref[0, 0].astype(BF16), q_all)
    cmp_end = lax.broadcasted_iota(jnp.int32, s.shape, 0) * CMP_STRIDE + (CMP_BLOCK - 1)
    s = jnp.where(cmp_end <= lane_time(s.shape), s, -jnp.inf)
    m = jnp.max(s, axis=0, keepdims=True)
    m = jnp.where(m == -jnp.inf, 0.0, m)
    e = jnp.exp2(s - m)
    p = e / jnp.maximum(jnp.sum(e, axis=0, keepdims=True), 1e-30)
    oc_scr[...] = _dot(vct_ref[0, 0].astype(BF16), p.astype(BF16))
    pg = p[:, 0:ATTN_Q]
    for g in range(1, GROUP):
        pg = pg + p[:, g * ATTN_Q:(g + 1) * ATTN_Q]
    ovt = ovt_ref[...]
    p0, p1, p2 = _split3(pg)
    bias = _select_blocks(_dot(ovt, p0) + _dot(ovt, p1) + _dot(ovt, p2), q0).astype(BF16)

    w_start = pl.multiple_of(jnp.maximum(q0 - WINDOW, 0), ATTN_Q)
    sw = _dot(kw_ref[0, 0, pl.ds(w_start, span), :], q_all)
    sw = sw + jnp.concatenate([wb_ref[0]] * GROUP, axis=1)
    ew = jnp.exp2(sw - jnp.max(sw, axis=0, keepdims=True))
    vw = jnp.concatenate([vwt_ref[0, :, pl.ds(w_start, span)], jnp.ones((ONES_ROWS, span), BF16)], axis=0)
    ow = _dot(vw, ew.astype(BF16))
    ow_scr[...] = ow[0:HEAD_DIM] / ow[HEAD_DIM:HEAD_DIM + 1]

    qx = jnp.concatenate([jnp.concatenate([qt[g], bias], axis=0) for g in range(GROUP)], axis=1)

    def scores(j, slot):
        start = pl.multiple_of(j * tk, tk)
        kx = jnp.concatenate([ks_ref[0, 0, pl.ds(start, tk), :], e_ref[pl.ds(start, tk), :]], axis=1)
        s_scr[slot] = _dot(kx, qx)

    def update(j, slot, causal):
        start = pl.multiple_of(j * tk, tk)
        vt = vst_ref[0, :, pl.ds(start, tk)]
        sc = s_scr[slot]
        if causal:
            key = start + lax.broadcasted_iota(jnp.int32, sc.shape, 0)
            sc = jnp.where(key <= lane_time(sc.shape), sc, -jnp.inf)
        m_old = m_scr[...]
        m_new = jnp.maximum(m_old, jnp.max(sc, axis=0, keepdims=True))
        alpha = jnp.exp2(m_old - m_new)
        pj = jnp.exp2(sc - m_new)
        m_scr[...] = m_new
        l_scr[...] = alpha * l_scr[...] + jnp.sum(pj, axis=0, keepdims=True)
        acc_scr[...] = alpha * acc_scr[...] + _dot(vt, pj.astype(BF16))

    m_scr[...] = jnp.full(m_scr.shape, -jnp.inf, F32)
    l_scr[...] = jnp.zeros(l_scr.shape, F32)
    acc_scr[...] = jnp.zeros(acc_scr.shape, F32)
    scores(0, 0)
    last = q0 // tk

    @pl.loop(0, last // 2)
    def _(i):
        scores(2 * i + 1, 1)
        update(2 * i, 0, False)
        scores(2 * i + 2, 0)
        update(2 * i + 1, 1, False)

    @pl.when(last % 2 == 1)
    def _():
        scores(last, 1)
        update(last - 1, 0, False)
        update(last, 1, True)

    @pl.when(last % 2 == 0)
    def _():
        update(last, 0, True)

    o_sel = acc_scr[...] / l_scr[...]
    gates = jax.nn.sigmoid(zgt_ref[0])
    hk = pl.program_id(1)

    def gate(g, branch):
        row = gates[g * N_BRANCH + branch:g * N_BRANCH + branch + 1]
        for other in range(1, N_KV_HEADS):
            r = (other * GROUP + g) * N_BRANCH + branch
            row = jnp.where(hk == other, gates[r:r + 1], row)
        return row

    for g in range(GROUP):
        sl = slice(g * ATTN_Q, (g + 1) * ATTN_Q)
        y = (gate(g, 0) * oc_scr[:, sl] + gate(g, 1) * o_sel[:, sl]
             + gate(g, 2) * ow_scr[:, sl])
        y_ref[0, :, g * HEAD_DIM:(g + 1) * HEAD_DIM] = y.T


def _attention(qt, zgt, kvc, kvct, ovt, kk, vt, onehot, B, seq):
    n_cmp = kvc.shape[2]
    n_sel = ovt.shape[0]
    GW = GROUP * HEAD_DIM
    GQ = GROUP * ATTN_Q
    VR = HEAD_DIM + ONES_ROWS
    tk = min(512, seq)
    span = WINDOW + ATTN_Q
    assert seq >= span
    body = functools.partial(_attn_body, tk=tk, span=span)
    resident = 4 * seq * HEAD_DIM * 2 + seq * n_sel * 2
    vmem = 2 * (GW * ATTN_Q * 2 + N_GATES * ATTN_Q * 4 + 2 * n_cmp * HEAD_DIM * 4 + n_sel * n_cmp * 2
                + resident + ATTN_Q * GW * 4) \
        + (2 * tk + VR + 2 * HEAD_DIM + SUBLANES) * GQ * 4 + 8 * (n_cmp + span) * GQ * 4
    return pl.pallas_call(
        body,
        grid=(B, N_KV_HEADS, seq // ATTN_Q),
        in_specs=[
            pl.BlockSpec((1, GW, ATTN_Q), lambda b, h, i: (b, h, i)),
            pl.BlockSpec((1, N_GATES, ATTN_Q), lambda b, h, i: (b, 0, i)),
            pl.BlockSpec((1, 1, n_cmp, HEAD_DIM), lambda b, h, i: (0, b * N_KV_HEADS + h, 0, 0)),
            pl.BlockSpec((1, 1, HEAD_DIM, n_cmp), lambda b, h, i: (1, b * N_KV_HEADS + h, 0, 0)),
            pl.BlockSpec((n_sel, n_cmp), lambda b, h, i: (0, 0)),
            pl.BlockSpec((1, 1, seq, HEAD_DIM), lambda b, h, i: (b, N_KV_HEADS + h, 0, 0)),
            pl.BlockSpec((1, HEAD_DIM, seq), lambda b, h, i: (b, N_KV_HEADS + h, 0)),
            pl.BlockSpec((seq, n_sel), lambda b, h, i: (0, 0)),
            pl.BlockSpec((1, 1, seq, HEAD_DIM), lambda b, h, i: (b, 2 * N_KV_HEADS + h, 0, 0)),
            pl.BlockSpec((1, HEAD_DIM, seq), lambda b, h, i: (b, 2 * N_KV_HEADS + h, 0)),
            pl.BlockSpec((1, span, ATTN_Q), lambda b, h, i: (jnp.minimum(i, WINDOW // ATTN_Q), 0, 0)),
        ],
        out_specs=pl.BlockSpec((1, ATTN_Q, GW), lambda b, h, i: (b, i, h)),
        out_shape=jax.ShapeDtypeStruct((B, seq, ATTN_W), F32),
        scratch_shapes=[pltpu.VMEM((2, tk, GQ), F32),
                        pltpu.VMEM((1, GQ), F32),
                        pltpu.VMEM((1, GQ), F32),
                        pltpu.VMEM((HEAD_DIM, GQ), F32),
                        pltpu.VMEM((HEAD_DIM, GQ), F32),
                        pltpu.VMEM((HEAD_DIM, GQ), F32)],
        compiler_params=_params(("parallel", "parallel", "arbitrary"), vmem),
        name="attention",
    )(qt, zgt, kvc, kvct, ovt, kk, vt, onehot, kk, vt, _window_bias(span))


def _outproj_body(yc_ref, ya_ref, gc_ref, ga_ref, w_ref, x_ref, h_ref, a_scr):
    @pl.when(pl.program_id(1) == 0)
    def _():
        a_scr[:, 0:CONV_W] = _rms(yc_ref[...], gc_ref[...]).astype(BF16)
        a_scr[:, CONV_W:] = _rms(ya_ref[...], ga_ref[...]).astype(BF16)

    h_ref[...] = x_ref[...] + _dot(a_scr[...], w_ref[...])


def _out_projection(yc, ya, g_conv, g_attn, w, x2):
    T, D = x2.shape
    K = w.shape[0]
    tm = min(1024, T)
    tn = min(1024, D)
    vmem = 2 * (tm * CONV_W * 4 + tm * ATTN_W * 4 + K * tn * 2 + 2 * tm * tn * 4) + tm * K * 2 \
        + 3 * tm * ATTN_W * 4
    return pl.pallas_call(
        _outproj_body,
        grid=(T // tm, D // tn),
        in_specs=[
            pl.BlockSpec((tm, CONV_W), lambda i, j: (i, 0)),
            pl.BlockSpec((tm, ATTN_W), lambda i, j: (i, 0)),
            pl.BlockSpec((1, CONV_W), lambda i, j: (0, 0)),
            pl.BlockSpec((1, ATTN_W), lambda i, j: (0, 0)),
            pl.BlockSpec((K, tn), lambda i, j: (0, j)),
            pl.BlockSpec((tm, tn), lambda i, j: (i, j)),
        ],
        out_specs=pl.BlockSpec((tm, tn), lambda i, j: (i, j)),
        out_shape=jax.ShapeDtypeStruct((T, D), F32),
        scratch_shapes=[pltpu.VMEM((tm, K), BF16)],
        compiler_params=_params(("parallel", "arbitrary"), vmem),
        name="out_projection",
    )(yc, ya, g_conv, g_attn, w, x2)


def _ffn_body(h_ref, g_ref, wg_ref, wu_ref, wd_ref, gf_ref, o_ref, a_scr, *, final_norm, tn):
    j = pl.program_id(1)

    @pl.when(j == 0)
    def _():
        @pl.loop(0, h_ref.shape[0] // FFN_EPILOGUE_ROWS)
        def _(r):
            rows = pl.ds(pl.multiple_of(r * FFN_EPILOGUE_ROWS, FFN_EPILOGUE_ROWS), FFN_EPILOGUE_ROWS)
            a_scr[rows, :] = _rms(h_ref[rows, :], g_ref[...]).astype(BF16)

    a = a_scr[...]
    u = jnp.concatenate(
        [(jax.nn.silu(_dot(a, wg_ref[:, c:c + PROJ_CHUNK])) * _dot(a, wu_ref[:, c:c + PROJ_CHUNK])).astype(BF16)
         for c in range(0, wg_ref.shape[1], PROJ_CHUNK)], axis=1)
    D = o_ref.shape[1]

    @pl.when(j == 0)
    def _():
        for n in range(D // tn):
            o_ref[:, n * tn:(n + 1) * tn] = _dot(u, wd_ref[:, n * tn:(n + 1) * tn])

    @pl.when(j > 0)
    def _():
        for n in range(D // tn):
            o_ref[:, n * tn:(n + 1) * tn] += _dot(u, wd_ref[:, n * tn:(n + 1) * tn])

    @pl.when(j == pl.num_programs(1) - 1)
    def _():
        @pl.loop(0, o_ref.shape[0] // FFN_EPILOGUE_ROWS)
        def _(r):
            rows = pl.ds(pl.multiple_of(r * FFN_EPILOGUE_ROWS, FFN_EPILOGUE_ROWS), FFN_EPILOGUE_ROWS)
            out = h_ref[rows, :] + o_ref[rows, :]
            o_ref[rows, :] = _rms(out, gf_ref[...]) if final_norm else out


def _ffn(h, g, wg, wu, wd, g_final, final_norm):
    T, D = h.shape
    F = wg.shape[1]
    tm = min(1024, T)
    tf = 512
    tn = PROJ_CHUNK
    assert F % tf == 0 and D % tn == 0
    body = functools.partial(_ffn_body, final_norm=final_norm, tn=tn)
    vmem = 2 * (3 * D * tf * 2 + 2 * tm * D * 4) + tm * D * 2 + 6 * tm * tf * 4 + 2 * tm * tn * 4
    return pl.pallas_call(
        body,
        grid=(T // tm, F // tf),
        in_specs=[
            pl.BlockSpec((tm, D), lambda i, j: (i, 0)),
            pl.BlockSpec((1, D), lambda i, j: (0, 0)),
            pl.BlockSpec((D, tf), lambda i, j: (0, j)),
            pl.BlockSpec((D, tf), lambda i, j: (0, j)),
            pl.BlockSpec((tf, D), lambda i, j: (j, 0)),
            pl.BlockSpec((1, D), lambda i, j: (0, 0)),
        ],
        out_specs=pl.BlockSpec((tm, D), lambda i, j: (i, 0)),
        out_shape=jax.ShapeDtypeStruct((T, D), F32),
        scratch_shapes=[pltpu.VMEM((tm, D), BF16)],
        compiler_params=_params(("parallel", "arbitrary"), vmem),
        name="ffn",
    )(h, g, wg, wu, wd, g_final)


def _rope_tables(seq):
    half = HEAD_DIM // 2
    inv = 1.0 / (ROPE_THETA ** (jnp.arange(half, dtype=F32) / half))
    lo = jnp.arange(ROPE_SPLIT).astype(F32)[:, None] * inv[None, :]
    hi = (jnp.arange(seq // ROPE_SPLIT) * ROPE_SPLIT).astype(F32)[:, None] * inv[None, :]
    cos_lo, sin_lo = jnp.cos(lo)[None], jnp.sin(lo)[None]
    cos_hi, sin_hi = jnp.cos(hi)[:, None, :], jnp.sin(hi)[:, None, :]
    cos = (cos_hi * cos_lo - sin_hi * sin_lo).reshape(seq, half)
    sin = (sin_hi * cos_lo + cos_hi * sin_lo).reshape(seq, half)
    return jnp.concatenate([cos, cos], axis=1), jnp.concatenate([-sin, sin], axis=1)


def _overlap_t(n_sel, n_cmp):
    cs = np.arange(n_cmp)[None, :] * CMP_STRIDE
    ss = np.arange(n_sel)[:, None] * SEL_BLOCK
    return jnp.asarray(((cs < ss + SEL_BLOCK) & (cs + CMP_BLOCK > ss)).astype(np.float32), dtype=BF16)


def _window_bias(span):
    r = np.arange(span)[None, :, None]
    c = np.arange(ATTN_Q)[None, None, :]
    off = np.minimum(np.arange(WINDOW // ATTN_Q + 1) * ATTN_Q, WINDOW)[:, None, None]
    valid = (r <= off + c) & (r > off + c - WINDOW)
    return jnp.asarray(np.where(valid, 0.0, -np.inf).astype(np.float32))


def _block_onehot(seq, n_sel):
    return jnp.asarray((np.arange(seq)[:, None] // SEL_BLOCK == np.arange(n_sel)[None, :])
                       .astype(np.float32), dtype=BF16)


def kernel(x, norm_mix, w_in, conv_w, cmp_pe_k, cmp_w1_k, cmp_w2_k, cmp_pe_v, cmp_w1_v, cmp_w2_v,
           norm_conv_out, norm_attn_out, w_out, norm_ffn, w_gate, w_up, w_down, norm_final):
    B, S, D = x.shape
    T = B * S
    depth = norm_mix.shape[0]
    assert S % ATTN_Q == 0 and S % CMP_STRIDE == 0 and S % ROPE_SPLIT == 0
    n_chunks = S // CMP_STRIDE
    n_sel = S // SEL_BLOCK
    cos_t, sin_t = _rope_tables(S)
    ovt = _overlap_t(n_sel, n_chunks)
    onehot = _block_onehot(S, n_sel)

    h = x.reshape(T, D)
    for l in range(depth):
        y_conv, qt, kk, kvc_in, vt, zgt, w_out_b, w_gate_b, w_up_b, w_down_b = _in_projection(
            h, norm_mix[l][None], w_in[l].astype(BF16), _gate_weight(w_in[l]), cos_t, sin_t, conv_w[l],
            [w_out[l], w_gate[l], w_up[l], w_down[l]], B, S)
        kvc_in = kvc_in.reshape(2, B * N_KV_HEADS, S, HEAD_DIM)
        pe = jnp.stack([cmp_pe_k[l], cmp_pe_v[l]]).reshape(2, 1, CMP_BLOCK * HEAD_DIM)
        kvc, kvct = _compress(kvc_in, pe, jnp.stack([cmp_w1_k[l], cmp_w1_v[l]]),
                              jnp.stack([cmp_w2_k[l], cmp_w2_v[l]]))

        y_attn = _attention(qt, zgt, kvc, kvct, ovt, kk, vt, onehot, B, S)

        h = _out_projection(y_conv, y_attn.reshape(T, ATTN_W), norm_conv_out[l][None],
                            norm_attn_out[l][None], w_out_b, h)

        h = _ffn(h, norm_ffn[l][None], w_gate_b, w_up_b, w_down_b, norm_final[None], l == depth - 1)
    return h.reshape(B, S, D)
```

```python
import functools

import jax
import jax.numpy as jnp
import numpy as np
from jax import lax
from jax.experimental import pallas as pl
from jax.experimental.pallas import tpu as pltpu

F32 = jnp.float32
BF16 = jnp.bfloat16

HEAD_DIM = 128
N_HEADS = 8
N_KV_HEADS = 2
GROUP = N_HEADS // N_KV_HEADS
N_BRANCH = 3
CONV_TAPS = 3
CMP_BLOCK = 32
CMP_STRIDE = 16
SEL_BLOCK = 64
N_SELECT = 16
N_FORCED = 3
WINDOW = 512
ATTN_Q = 256
ROPE_THETA = 10000.0
ROPE_SPLIT = 128
RMS_EPS = 1e-6

CONV_W = 1024
ATTN_W = N_HEADS * HEAD_DIM
KV_W = N_KV_HEADS * HEAD_DIM
N_GATES = N_HEADS * N_BRANCH

V7X_VMEM_BYTES = 64 * 1024 * 1024
V7X_VMEM_BUDGET = V7X_VMEM_BYTES - 6 * 1024 * 1024
LANES = 128
SUBLANES = 8
PROJ_TILE_N = 512
PROJ_CHUNK = 256
FFN_EPILOGUE_ROWS = 256

MASK_BIAS = -float(2 ** 30)
LOG2_E = 1.4426950408889634
ONES_ROWS = 16


def _params(semantics, vmem_bytes):
    return pltpu.CompilerParams(dimension_semantics=semantics,
                                vmem_limit_bytes=int(min(vmem_bytes, V7X_VMEM_BUDGET)))


def _rms(x, g):
    return x * lax.rsqrt(jnp.mean(x * x, axis=-1, keepdims=True) + RMS_EPS) * g


def _split3(x):
    hi = x.astype(BF16)
    r1 = x - hi.astype(F32)
    mid = r1.astype(BF16)
    lo = (r1 - mid.astype(F32)).astype(BF16)
    return hi, mid, lo


def _dot(a, b):
    return jnp.dot(a, b, preferred_element_type=F32)


def _dot_f32(a, b):
    a0, a1, a2 = _split3(a)
    b0, b1, b2 = _split3(b)
    return (_dot(a0, b0) + (_dot(a0, b1) + _dot(a1, b0))
            + (_dot(a0, b2) + _dot(a1, b1) + _dot(a2, b0)))


CONV_GROUPS = CONV_W // PROJ_TILE_N
N_Q_TILES = ATTN_W // PROJ_TILE_N
N_KV_TILES = 3
GATE_STEP = 0
Q_STEP0 = GATE_STEP + 1
KV_STEP0 = Q_STEP0 + N_Q_TILES
CONV_STEP0 = KV_STEP0 + N_KV_TILES
N_PROJ_STEPS = CONV_STEP0 + CONV_GROUPS
N_WEIGHT_TILES = 3 * CONV_GROUPS + N_Q_TILES + N_KV_TILES
KV_OUT_W = N_KV_TILES * KV_W
assert 2 * KV_W == PROJ_TILE_N and N_GATES % SUBLANES == 0


def _gate_weight(w):
    gates = w[:, N_WEIGHT_TILES * PROJ_TILE_N:].astype(BF16)
    return jnp.pad(gates, ((0, 0), (0, LANES - gates.shape[1])))


def _inproj_body(*refs, q_scale, blocks_per_seq, side_blocks):
    n_side = len(side_blocks)
    x_ref, g_ref, w_ref, wc_ref, wh_ref, wg_ref, cos_ref, sin_ref, cw_ref = refs[:9]
    side_in = refs[9:9 + n_side]
    yc_ref, qt_ref, k_ref, kvc_ref, vt_ref, zgt_ref = refs[9 + n_side:15 + n_side]
    side_out = refs[15 + n_side:15 + 2 * n_side]
    a_scr, tail_scr, zg_scr = refs[15 + 2 * n_side:]
    i = pl.program_id(0)
    j = pl.program_id(1)

    step = i * N_PROJ_STEPS + j
    for src, dst, n_blocks in zip(side_in, side_out, side_blocks):
        @pl.when(step < n_blocks)
        def _(src=src, dst=dst):
            dst[...] = src[...].astype(BF16)

    @pl.when(j == 0)
    def _():
        a_scr[...] = _rms(x_ref[...], g_ref[...]).astype(BF16)

    @pl.when((i == 0) & (j == 0))
    def _():
        tail_scr[...] = jnp.zeros(tail_scr.shape, F32)

    @pl.when(j == GATE_STEP)
    def _():
        zg_scr[...] = _dot(a_scr[...], wg_ref[...])
        zgt_ref[0] = zg_scr[...].T[0:N_GATES]

    chunks = [(c, c + PROJ_CHUNK) for c in range(0, PROJ_TILE_N, PROJ_CHUNK)]

    def z_cols(weight_ref, c0, c1):
        return _dot(a_scr[...], weight_ref[:, c0:c1])

    def rope(zh):
        cos = cos_ref[...]
        sin = sin_ref[...]
        heads = []
        for h in range(zh.shape[1] // HEAD_DIM):
            z1 = zh[:, h * HEAD_DIM:(h + 1) * HEAD_DIM]
            heads.append(z1 * cos + pltpu.roll(z1, HEAD_DIM // 2, axis=1) * sin)
        return jnp.concatenate(heads, axis=1)

    @pl.when(j >= CONV_STEP0)
    def _():
        group = j - CONV_STEP0
        for c0, c1 in chunks:
            u = z_cols(wc_ref, c0, c1) * z_cols(wh_ref, c0, c1)
            tail = tail_scr[group, :, c0:c1]
            tail = jnp.where(i % blocks_per_seq == 0, 0.0, tail)
            row = lax.broadcasted_iota(jnp.int32, u.shape, 0)
            t1 = tail[SUBLANES - 1:SUBLANES]
            t2 = tail[SUBLANES - 2:SUBLANES - 1]
            u1 = jnp.where(row == 0, t1, pltpu.roll(u, 1, axis=0))
            u2 = jnp.where(row == 0, t2, jnp.where(row == 1, t1, pltpu.roll(u, 2, axis=0)))
            w = cw_ref[:, c0:c1]
            yc_ref[:, c0:c1] = z_cols(w_ref, c0, c1) * (u2 * w[0:1] + u1 * w[1:2] + u * w[2:3])
            tail_scr[group, :, c0:c1] = u[u.shape[0] - SUBLANES:, :]

    @pl.when((j >= Q_STEP0) & (j < KV_STEP0))
    def _():
        for c0, c1 in chunks:
            qt_ref[0, c0:c1, :] = (rope(z_cols(w_ref, c0, c1)) * q_scale).T.astype(BF16)

    @pl.when((j >= KV_STEP0) & (j < CONV_STEP0))
    def _():
        kx = rope(z_cols(w_ref, 0, KV_W)).astype(BF16)
        for h in range(N_KV_HEADS):
            k_ref[0, h] = kx[:, h * HEAD_DIM:(h + 1) * HEAD_DIM]
        zv = z_cols(w_ref, KV_W, 2 * KV_W)
        vt_ref[0] = zv.T.astype(BF16)

        @pl.when(j == KV_STEP0)
        def _():
            for h in range(N_KV_HEADS):
                kvc_ref[0, 0, h] = kx[:, h * HEAD_DIM:(h + 1) * HEAD_DIM].astype(F32)
                kvc_ref[1, 0, h] = zv[:, h * HEAD_DIM:(h + 1) * HEAD_DIM].astype(BF16).astype(F32)


def _side_rows(n_rows, n_steps):
    for rows in range(2 * SUBLANES, n_rows + 1, 2 * SUBLANES):
        if n_rows % rows == 0 and n_rows // rows <= n_steps:
            return rows
    raise ValueError((n_rows, n_steps))


def _in_projection(x2, g, w_proj, w_gates, cos_t, sin_t, conv_w, side, B, seq):
    T, D = x2.shape
    tn = PROJ_TILE_N
    assert w_proj.shape[1] >= N_WEIGHT_TILES * tn
    tm = min(1024, seq)
    nb = seq // tm
    n_steps = (T // tm) * N_PROJ_STEPS
    side_rows = [_side_rows(w.shape[0], n_steps) for w in side]
    side_blocks = [w.shape[0] // r for w, r in zip(side, side_rows)]
    side_specs = [pl.BlockSpec((r, w.shape[1]),
                               lambda i, j, n=n: (jnp.minimum(i * N_PROJ_STEPS + j, n - 1), 0))
                  for w, r, n in zip(side, side_rows, side_blocks)]
    body = functools.partial(_inproj_body, q_scale=HEAD_DIM ** -0.5 * LOG2_E, blocks_per_seq=nb,
                             side_blocks=tuple(side_blocks))
    vmem = 2 * (tm * D * 4 + 3 * D * tn * 2 + 2 * tm * HEAD_DIM * 4 + tm * tn * 4 + 4 * tm * tn * 2
                + N_GATES * tm * 4) + tm * D * 2 + 8 * tm * PROJ_CHUNK * 4 \
        + 2 * sum(r * w.shape[1] * 6 for w, r in zip(side, side_rows))

    def conv_group(j):
        return jnp.maximum(j - CONV_STEP0, 0)

    def main_tile(j):
        return jnp.where(j >= CONV_STEP0, j - CONV_STEP0,
                         jnp.maximum(j - Q_STEP0, 0) + 3 * CONV_GROUPS)

    def kv_tile(j):
        return jnp.clip(j - KV_STEP0, 0, N_KV_TILES - 1)

    return pl.pallas_call(
        body,
        grid=(T // tm, N_PROJ_STEPS),
        in_specs=[
            pl.BlockSpec((tm, D), lambda i, j: (i, 0)),
            pl.BlockSpec((1, D), lambda i, j: (0, 0)),
            pl.BlockSpec((D, tn), lambda i, j: (0, main_tile(j))),
            pl.BlockSpec((D, tn), lambda i, j: (0, CONV_GROUPS + conv_group(j))),
            pl.BlockSpec((D, tn), lambda i, j: (0, 2 * CONV_GROUPS + conv_group(j))),
            pl.BlockSpec((D, LANES), lambda i, j: (0, 0)),
            pl.BlockSpec((tm, HEAD_DIM), lambda i, j: (i % nb, 0)),
            pl.BlockSpec((tm, HEAD_DIM), lambda i, j: (i % nb, 0)),
            pl.BlockSpec((CONV_TAPS, tn), lambda i, j: (0, conv_group(j))),
        ] + side_specs,
        out_specs=[
            pl.BlockSpec((tm, tn), lambda i, j: (i, conv_group(j))),
            pl.BlockSpec((1, tn, tm), lambda i, j: (i // nb, jnp.clip(j - Q_STEP0, 0, N_Q_TILES - 1), i % nb)),
            pl.BlockSpec((1, N_KV_HEADS, tm, HEAD_DIM), lambda i, j: (i // nb, kv_tile(j), i % nb, 0)),
            pl.BlockSpec((2, 1, N_KV_HEADS, tm, HEAD_DIM), lambda i, j: (0, i // nb, 0, i % nb, 0)),
            pl.BlockSpec((1, KV_W, tm), lambda i, j: (i // nb, kv_tile(j), i % nb)),
            pl.BlockSpec((1, N_GATES, tm), lambda i, j: (i // nb, 0, i % nb)),
        ] + side_specs,
        out_shape=[
            jax.ShapeDtypeStruct((T, CONV_W), F32),
            jax.ShapeDtypeStruct((B, ATTN_W, seq), BF16),
            jax.ShapeDtypeStruct((B, N_KV_TILES * N_KV_HEADS, seq, HEAD_DIM), BF16),
            jax.ShapeDtypeStruct((2, B, N_KV_HEADS, seq, HEAD_DIM), F32),
            jax.ShapeDtypeStruct((B, KV_OUT_W, seq), BF16),
            jax.ShapeDtypeStruct((B, N_GATES, seq), F32),
        ] + [jax.ShapeDtypeStruct(w.shape, BF16) for w in side],
        scratch_shapes=[pltpu.VMEM((tm, D), BF16), pltpu.VMEM((CONV_GROUPS, SUBLANES, tn), F32),
                        pltpu.VMEM((tm, LANES), F32)],
        compiler_params=_params(("arbitrary", "arbitrary"), vmem),
        name="in_projection",
    )(x2, g, w_proj, w_proj, w_proj, w_gates, cos_t, sin_t, conv_w, *side)


def _compress_body(x_ref, pe_ref, w1_ref, w2_ref, o_ref, ot_ref, *, n_valid):
    n = x_ref.shape[2] // CMP_STRIDE
    x = jnp.concatenate([x_ref[0, 0, pl.ds(l, n, stride=CMP_STRIDE), :] for l in range(CMP_STRIDE)],
                        axis=1).astype(BF16)
    half = x.shape[1]
    w1 = w1_ref[0]
    w1_hi = w1.astype(BF16)
    w1_lo = (w1 - w1_hi.astype(F32)).astype(BF16)
    first = _dot(x, w1_hi[:half]) + _dot(x, w1_lo[:half])
    second = _dot(x, w1_hi[half:]) + _dot(x, w1_lo[half:])
    second = pltpu.roll(second, n - 1, axis=0)
    pe = jnp.broadcast_to(pe_ref[0], (SUBLANES, 2 * half))
    pe_term = _dot_f32(pe, w1)[0:1]
    h = first + second + pe_term
    out = _dot_f32(jax.nn.silu(h), w2_ref[0])
    row = lax.broadcasted_iota(jnp.int32, out.shape, 0)
    out = jnp.where(row < n_valid, out, 0.0)
    o_ref[0, 0] = out
    ot_ref[0, 0] = out.T


def _compress(kv, pe, w1, w2):
    two, BH, seq, d = kv.shape
    n = seq // CMP_STRIDE
    width = CMP_STRIDE * d
    body = functools.partial(_compress_body, n_valid=n - 1)
    vmem = 2 * (seq * d * 4 + 2 * width * 4 + 2 * width * d * 4 + d * d * 4 + n * d * 4) \
        + 8 * width * d * 4
    return pl.pallas_call(
        body,
        grid=(two, BH),
        in_specs=[
            pl.BlockSpec((1, 1, seq, d), lambda s, b: (s, b, 0, 0)),
            pl.BlockSpec((1, 1, 2 * width), lambda s, b: (s, 0, 0)),
            pl.BlockSpec((1, 2 * width, d), lambda s, b: (s, 0, 0)),
            pl.BlockSpec((1, d, d), lambda s, b: (s, 0, 0)),
        ],
        out_specs=[pl.BlockSpec((1, 1, n, d), lambda s, b: (s, b, 0, 0)),
                   pl.BlockSpec((1, 1, d, n), lambda s, b: (s, b, 0, 0))],
        out_shape=[jax.ShapeDtypeStruct((two, BH, n, d), F32),
                   jax.ShapeDtypeStruct((two, BH, d, n), F32)],
        compiler_params=_params(("parallel", "parallel"), vmem),
        name="compress",
    )(kv, pe, w1, w2)


def _select_blocks(imp, q0):
    n_sel = imp.shape[0]
    sel_i = lax.broadcasted_iota(jnp.int32, imp.shape, 0)
    sel_f = sel_i.astype(F32)
    cur = (q0 + lax.broadcasted_iota(jnp.int32, imp.shape, 1)) // SEL_BLOCK
    forced = (sel_i == 0) | (sel_i == cur) | (sel_i == cur - 1)
    v = jnp.where((sel_i > cur) | forced, -jnp.inf, imp)
    chosen = jnp.where(forced, 1.0, 0.0)
    for _ in range(min(N_SELECT, n_sel) - N_FORCED):
        top = jnp.max(v, axis=0, keepdims=True)
        first = jnp.min(jnp.where(v == top, sel_f, float(n_sel)), axis=0, keepdims=True)
        pick = sel_f == first
        chosen = jnp.where(pick, 1.0, chosen)
        v = jnp.where(pick, -jnp.inf, v)
    return jnp.where((chosen > 0.0) & (sel_i <= cur), 0.0, MASK_BIAS)


def _attn_body(qt_ref, zgt_ref, kc_ref, vct_ref, ovt_ref, ks_ref, vst_ref, e_ref, kw_ref, vwt_ref, wb_ref,
               y_ref, s_scr, m_scr, acc_scr, oc_scr, ow_scr, *, tk, span):
    q0 = pl.program_id(2) * ATTN_Q
    qt = [qt_ref[0, g * HEAD_DIM:(g + 1) * HEAD_DIM, :] for g in range(GROUP)]
    q_all = jnp.concatenate(qt, axis=1)

    def lane_time(shape):
        return q0 + (lax.broadcasted_iota(jnp.int32, shape, 1) & (ATTN_Q - 1))

    s = _dot(kc_ref[0, 0].astype(BF16), q_all)
    cmp_end = lax.broadcasted_iota(jnp.int32, s.shape, 0) * CMP_STRIDE + (CMP_BLOCK - 1)
    s = jnp.where(cmp_end <= lane_time(s.shape), s, -jnp.inf)
    m = jnp.max(s, axis=0, keepdims=True)
    m = jnp.where(m == -jnp.inf, 0.0, m)
    e = jnp.exp2(s - m)
    p = e / jnp.maximum(jnp.sum(e, axis=0, keepdims=True), 1e-30)
    oc_scr[...] = _dot(vct_ref[0, 0].astype(BF16), p.astype(BF16))
    pg = p[:, 0:ATTN_Q]
    for g in range(1, GROUP):
        pg = pg + p[:, g * ATTN_Q:(g + 1) * ATTN_Q]
    ovt = ovt_ref[...]
    p0, p1, p2 = _split3(pg)
    bias = _select_blocks(_dot(ovt, p0) + _dot(ovt, p1) + _dot(ovt, p2), q0).astype(BF16)

    w_start = pl.multiple_of(jnp.maximum(q0 - WINDOW, 0), ATTN_Q)
    sw = _dot(kw_ref[0, 0, pl.ds(w_start, span), :], q_all)
    sw = sw + jnp.concatenate([wb_ref[0]] * GROUP, axis=1)
    ew = jnp.exp2(sw - jnp.max(sw, axis=0, keepdims=True))
    vw = jnp.concatenate([vwt_ref[0, :, pl.ds(w_start, span)], jnp.ones((ONES_ROWS, span), BF16)], axis=0)
    ow = _dot(vw, ew.astype(BF16))
    ow_scr[...] = ow[0:HEAD_DIM] / ow[HEAD_DIM:HEAD_DIM + 1]

    qx = jnp.concatenate([jnp.concatenate([qt[g], bias], axis=0) for g in range(GROUP)], axis=1)

    def scores(j, slot):
        start = pl.multiple_of(j * tk, tk)
        kx = jnp.concatenate([ks_ref[0, 0, pl.ds(start, tk), :], e_ref[pl.ds(start, tk), :]], axis=1)
        s_scr[slot] = _dot(kx, qx)

    def update(j, slot, causal):
        start = pl.multiple_of(j * tk, tk)
        vt = jnp.concatenate([vst_ref[0, :, pl.ds(start, tk)], jnp.ones((ONES_ROWS, tk), BF16)], axis=0)
        sc = s_scr[slot]
        if causal:
            key = start + lax.broadcasted_iota(jnp.int32, sc.shape, 0)
            sc = jnp.where(key <= lane_time(sc.shape), sc, -jnp.inf)
        m_old = m_scr[...]
        m_new = jnp.maximum(m_old, jnp.max(sc, axis=0, keepdims=True))
        alpha = jnp.exp2(m_old - m_new)
        pj = jnp.exp2(sc - m_new)
        m_scr[...] = m_new
        acc_scr[...] = alpha * acc_scr[...] + _dot(vt, pj.astype(BF16))

    m_scr[...] = jnp.full(m_scr.shape, -jnp.inf, F32)
    acc_scr[...] = jnp.zeros(acc_scr.shape, F32)
    scores(0, 0)
    last = q0 // tk

    @pl.loop(0, last // 2)
    def _(i):
        scores(2 * i + 1, 1)
        update(2 * i, 0, False)
        scores(2 * i + 2, 0)
        update(2 * i + 1, 1, False)

    @pl.when(last % 2 == 1)
    def _():
        scores(last, 1)
        update(last - 1, 0, False)
        update(last, 1, True)

    @pl.when(last % 2 == 0)
    def _():
        update(last, 0, True)

    o_sel = acc_scr[0:HEAD_DIM, :] / acc_scr[HEAD_DIM:HEAD_DIM + 1, :]
    gates = jax.nn.sigmoid(zgt_ref[0])
    hk = pl.program_id(1)

    def gate(g, branch):
        row = gates[g * N_BRANCH + branch:g * N_BRANCH + branch + 1]
        for other in range(1, N_KV_HEADS):
            r = (other * GROUP + g) * N_BRANCH + branch
            row = jnp.where(hk == other, gates[r:r + 1], row)
        return row

    for g in range(GROUP):
        sl = slice(g * ATTN_Q, (g + 1) * ATTN_Q)
        y = (gate(g, 0) * oc_scr[:, sl] + gate(g, 1) * o_sel[:, sl]
             + gate(g, 2) * ow_scr[:, sl])
        y_ref[0, :, g * HEAD_DIM:(g + 1) * HEAD_DIM] = y.T


def _attention(qt, zgt, kvc, kvct, ovt, kk, vt, onehot, B, seq):
    n_cmp = kvc.shape[2]
    n_sel = ovt.shape[0]
    GW = GROUP * HEAD_DIM
    GQ = GROUP * ATTN_Q
    VR = HEAD_DIM + ONES_ROWS
    tk = min(512, seq)
    span = WINDOW + ATTN_Q
    assert seq >= span
    body = functools.partial(_attn_body, tk=tk, span=span)
    resident = 4 * seq * HEAD_DIM * 2 + seq * n_sel * 2
    vmem = 2 * (GW * ATTN_Q * 2 + N_GATES * ATTN_Q * 4 + 2 * n_cmp * HEAD_DIM * 4 + n_sel * n_cmp * 2
                + resident + ATTN_Q * GW * 4) \
        + (2 * tk + VR + 2 * HEAD_DIM + SUBLANES) * GQ * 4 + 8 * (n_cmp + span) * GQ * 4
    return pl.pallas_call(
        body,
        grid=(B, N_KV_HEADS, seq // ATTN_Q),
        in_specs=[
            pl.BlockSpec((1, GW, ATTN_Q), lambda b, h, i: (b, h, i)),
            pl.BlockSpec((1, N_GATES, ATTN_Q), lambda b, h, i: (b, 0, i)),
            pl.BlockSpec((1, 1, n_cmp, HEAD_DIM), lambda b, h, i: (0, b * N_KV_HEADS + h, 0, 0)),
            pl.BlockSpec((1, 1, HEAD_DIM, n_cmp), lambda b, h, i: (1, b * N_KV_HEADS + h, 0, 0)),
            pl.BlockSpec((n_sel, n_cmp), lambda b, h, i: (0, 0)),
            pl.BlockSpec((1, 1, seq, HEAD_DIM), lambda b, h, i: (b, N_KV_HEADS + h, 0, 0)),
            pl.BlockSpec((1, HEAD_DIM, seq), lambda b, h, i: (b, N_KV_HEADS + h, 0)),
            pl.BlockSpec((seq, n_sel), lambda b, h, i: (0, 0)),
            pl.BlockSpec((1, 1, seq, HEAD_DIM), lambda b, h, i: (b, 2 * N_KV_HEADS + h, 0, 0)),
            pl.BlockSpec((1, HEAD_DIM, seq), lambda b, h, i: (b, 2 * N_KV_HEADS + h, 0)),
            pl.BlockSpec((1, span, ATTN_Q), lambda b, h, i: (jnp.minimum(i, WINDOW // ATTN_Q), 0, 0)),
        ],
        out_specs=pl.BlockSpec((1, ATTN_Q, GW), lambda b, h, i: (b, i, h)),
        out_shape=jax.ShapeDtypeStruct((B, seq, ATTN_W), F32),
        scratch_shapes=[pltpu.VMEM((2, tk, GQ), F32),
                        pltpu.VMEM((1, GQ), F32),
                        pltpu.VMEM((VR, GQ), F32),
                        pltpu.VMEM((HEAD_DIM, GQ), F32),
                        pltpu.VMEM((HEAD_DIM, GQ), F32)],
        compiler_params=_params(("parallel", "parallel", "arbitrary"), vmem),
        name="attention",
    )(qt, zgt, kvc, kvct, ovt, kk, vt, onehot, kk, vt, _window_bias(span))


def _outproj_body(yc_ref, ya_ref, gc_ref, ga_ref, w_ref, x_ref, h_ref, a_scr):
    @pl.when(pl.program_id(1) == 0)
    def _():
        a_scr[:, 0:CONV_W] = _rms(yc_ref[...], gc_ref[...]).astype(BF16)
        a_scr[:, CONV_W:] = _rms(ya_ref[...], ga_ref[...]).astype(BF16)

    h_ref[...] = x_ref[...] + _dot(a_scr[...], w_ref[...])


def _out_projection(yc, ya, g_conv, g_attn, w, x2):
    T, D = x2.shape
    K = w.shape[0]
    tm = min(1024, T)
    tn = min(1024, D)
    vmem = 2 * (tm * CONV_W * 4 + tm * ATTN_W * 4 + K * tn * 2 + 2 * tm * tn * 4) + tm * K * 2 \
        + 3 * tm * ATTN_W * 4
    return pl.pallas_call(
        _outproj_body,
        grid=(T // tm, D // tn),
        in_specs=[
            pl.BlockSpec((tm, CONV_W), lambda i, j: (i, 0)),
            pl.BlockSpec((tm, ATTN_W), lambda i, j: (i, 0)),
            pl.BlockSpec((1, CONV_W), lambda i, j: (0, 0)),
            pl.BlockSpec((1, ATTN_W), lambda i, j: (0, 0)),
            pl.BlockSpec((K, tn), lambda i, j: (0, j)),
            pl.BlockSpec((tm, tn), lambda i, j: (i, j)),
        ],
        out_specs=pl.BlockSpec((tm, tn), lambda i, j: (i, j)),
        out_shape=jax.ShapeDtypeStruct((T, D), F32),
        scratch_shapes=[pltpu.VMEM((tm, K), BF16)],
        compiler_params=_params(("parallel", "arbitrary"), vmem),
        name="out_projection",
    )(yc, ya, g_conv, g_attn, w, x2)


def _ffn_body(h_ref, g_ref, wg_ref, wu_ref, wd_ref, gf_ref, o_ref, a_scr, *, final_norm, tn):
    j = pl.program_id(1)

    @pl.when(j == 0)
    def _():
        @pl.loop(0, h_ref.shape[0] // FFN_EPILOGUE_ROWS)
        def _(r):
            rows = pl.ds(pl.multiple_of(r * FFN_EPILOGUE_ROWS, FFN_EPILOGUE_ROWS), FFN_EPILOGUE_ROWS)
            a_scr[rows, :] = _rms(h_ref[rows, :], g_ref[...]).astype(BF16)

    a = a_scr[...]
    u = jnp.concatenate(
        [(jax.nn.silu(_dot(a, wg_ref[:, c:c + PROJ_CHUNK])) * _dot(a, wu_ref[:, c:c + PROJ_CHUNK])).astype(BF16)
         for c in range(0, wg_ref.shape[1], PROJ_CHUNK)], axis=1)
    D = o_ref.shape[1]

    @pl.when(j == 0)
    def _():
        for n in range(D // tn):
            o_ref[:, n * tn:(n + 1) * tn] = _dot(u, wd_ref[:, n * tn:(n + 1) * tn])

    @pl.when(j > 0)
    def _():
        for n in range(D // tn):
            o_ref[:, n * tn:(n + 1) * tn] += _dot(u, wd_ref[:, n * tn:(n + 1) * tn])

    @pl.when(j == pl.num_programs(1) - 1)
    def _():
        @pl.loop(0, o_ref.shape[0] // FFN_EPILOGUE_ROWS)
        def _(r):
            rows = pl.ds(pl.multiple_of(r * FFN_EPILOGUE_ROWS, FFN_EPILOGUE_ROWS), FFN_EPILOGUE_ROWS)
            out = h_ref[rows, :] + o_ref[rows, :]
            o_ref[rows, :] = _rms(out, gf_ref[...]) if final_norm else out


def _ffn(h, g, wg, wu, wd, g_final, final_norm):
    T, D = h.shape
    F = wg.shape[1]
    tm = min(1024, T)
    tf = 512
    tn = PROJ_CHUNK
    assert F % tf == 0 and D % tn == 0
    body = functools.partial(_ffn_body, final_norm=final_norm, tn=tn)
    vmem = 2 * (3 * D * tf * 2 + 2 * tm * D * 4) + tm * D * 2 + 6 * tm * tf * 4 + 2 * tm * tn * 4
    return pl.pallas_call(
        body,
        grid=(T // tm, F // tf),
        in_specs=[
            pl.BlockSpec((tm, D), lambda i, j: (i, 0)),
            pl.BlockSpec((1, D), lambda i, j: (0, 0)),
            pl.BlockSpec((D, tf), lambda i, j: (0, j)),
            pl.BlockSpec((D, tf), lambda i, j: (0, j)),
            pl.BlockSpec((tf, D), lambda i, j: (j, 0)),
            pl.BlockSpec((1, D), lambda i, j: (0, 0)),
        ],
        out_specs=pl.BlockSpec((tm, D), lambda i, j: (i, 0)),
        out_shape=jax.ShapeDtypeStruct((T, D), F32),
        scratch_shapes=[pltpu.VMEM((tm, D), BF16)],
        compiler_params=_params(("parallel", "arbitrary"), vmem),
        name="ffn",
    )(h, g, wg, wu, wd, g_final)


def _rope_tables(seq):
    half = HEAD_DIM // 2
    inv = 1.0 / (ROPE_THETA ** (jnp.arange(half, dtype=F32) / half))
    inv = jnp.concatenate([inv, inv])
    sign = jnp.concatenate([-jnp.ones((half,), F32), jnp.ones((half,), F32)])
    lo = jnp.arange(ROPE_SPLIT).astype(F32)[:, None] * inv[None, :]
    hi = (jnp.arange(seq // ROPE_SPLIT) * ROPE_SPLIT).astype(F32)[:, None] * inv[None, :]
    cos_lo, sin_lo = jnp.cos(lo)[None], jnp.sin(lo)[None]
    cos_hi, sin_hi = jnp.cos(hi)[:, None, :], jnp.sin(hi)[:, None, :]
    cos = (cos_hi * cos_lo - sin_hi * sin_lo).reshape(seq, HEAD_DIM)
    sin = ((sin_hi * cos_lo + cos_hi * sin_lo) * sign).reshape(seq, HEAD_DIM)
    return cos, sin


def _overlap_t(n_sel, n_cmp):
    cs = np.arange(n_cmp)[None, :] * CMP_STRIDE
    ss = np.arange(n_sel)[:, None] * SEL_BLOCK
    return jnp.asarray(((cs < ss + SEL_BLOCK) & (cs + CMP_BLOCK > ss)).astype(np.float32), dtype=BF16)


def _window_bias(span):
    r = np.arange(span)[None, :, None]
    c = np.arange(ATTN_Q)[None, None, :]
    off = np.minimum(np.arange(WINDOW // ATTN_Q + 1) * ATTN_Q, WINDOW)[:, None, None]
    valid = (r <= off + c) & (r > off + c - WINDOW)
    return jnp.asarray(np.where(valid, 0.0, -np.inf).astype(np.float32))


def _block_onehot(seq, n_sel):
    return jnp.asarray((np.arange(seq)[:, None] // SEL_BLOCK == np.arange(n_sel)[None, :])
                       .astype(np.float32), dtype=BF16)


def kernel(x, norm_mix, w_in, conv_w, cmp_pe_k, cmp_w1_k, cmp_w2_k, cmp_pe_v, cmp_w1_v, cmp_w2_v,
           norm_conv_out, norm_attn_out, w_out, norm_ffn, w_gate, w_up, w_down, norm_final):
    B, S, D = x.shape
    T = B * S
    depth = norm_mix.shape[0]
    assert S % ATTN_Q == 0 and S % CMP_STRIDE == 0 and S % ROPE_SPLIT == 0
    n_chunks = S // CMP_STRIDE
    n_sel = S // SEL_BLOCK
    cos_t, sin_t = _rope_tables(S)
    ovt = _overlap_t(n_sel, n_chunks)
    onehot = _block_onehot(S, n_sel)

    h = x.reshape(T, D)
    for l in range(depth):
        y_conv, qt, kk, kvc_in, vt, zgt, w_out_b, w_gate_b, w_up_b, w_down_b = _in_projection(
            h, norm_mix[l][None], w_in[l].astype(BF16), _gate_weight(w_in[l]), cos_t, sin_t, conv_w[l],
            [w_out[l], w_gate[l], w_up[l], w_down[l]], B, S)
        kvc_in = kvc_in.reshape(2, B * N_KV_HEADS, S, HEAD_DIM)
        pe = jnp.stack([cmp_pe_k[l], cmp_pe_v[l]]).reshape(2, 1, CMP_BLOCK * HEAD_DIM)
        kvc, kvct = _compress(kvc_in, pe, jnp.stack([cmp_w1_k[l], cmp_w1_v[l]]),
                              jnp.stack([cmp_w2_k[l], cmp_w2_v[l]]))

        y_attn = _attention(qt, zgt, kvc, kvct, ovt, kk, vt, onehot, B, S)

        h = _out_projection(y_conv, y_attn.reshape(T, ATTN_W), norm_conv_out[l][None],
                            norm_attn_out[l][None], w_out_b, h)

        h = _ffn(h, norm_ffn[l][None], w_gate_b, w_up_b, w_down_b, norm_final[None], l == depth - 1)
    return h.reshape(B, S, D)
```

```python
import functools

import jax
import jax.numpy as jnp
import numpy as np
from jax import lax
from jax.experimental import pallas as pl
from jax.experimental.pallas import tpu as pltpu

F32 = jnp.float32
BF16 = jnp.bfloat16

HEAD_DIM = 128
N_HEADS = 8
N_KV_HEADS = 2
GROUP = N_HEADS // N_KV_HEADS
N_BRANCH = 3
CONV_TAPS = 3
CMP_BLOCK = 32
CMP_STRIDE = 16
SEL_BLOCK = 64
N_SELECT = 16
N_FORCED = 3
WINDOW = 512
ATTN_Q = 256
ROPE_THETA = 10000.0
ROPE_SPLIT = 128
RMS_EPS = 1e-6

CONV_W = 1024
ATTN_W = N_HEADS * HEAD_DIM
KV_W = N_KV_HEADS * HEAD_DIM
N_GATES = N_HEADS * N_BRANCH

V7X_VMEM_BYTES = 64 * 1024 * 1024
V7X_VMEM_BUDGET = V7X_VMEM_BYTES - 6 * 1024 * 1024
LANES = 128
SUBLANES = 8
PROJ_TILE_N = 512
PROJ_CHUNK = 256
FFN_EPILOGUE_ROWS = 256

MASK_BIAS = -float(2 ** 30)
LOG2_E = 1.4426950408889634
ONES_ROWS = 16


def _params(semantics, vmem_bytes):
    return pltpu.CompilerParams(dimension_semantics=semantics,
                                vmem_limit_bytes=int(min(vmem_bytes, V7X_VMEM_BUDGET)))


def _rms(x, g):
    return x * lax.rsqrt(jnp.mean(x * x, axis=-1, keepdims=True) + RMS_EPS) * g


def _split3(x):
    hi = x.astype(BF16)
    r1 = x - hi.astype(F32)
    mid = r1.astype(BF16)
    lo = (r1 - mid.astype(F32)).astype(BF16)
    return hi, mid, lo


def _dot(a, b):
    return jnp.dot(a, b, preferred_element_type=F32)


def _dot_f32(a, b):
    a0, a1, a2 = _split3(a)
    b0, b1, b2 = _split3(b)
    return (_dot(a0, b0) + (_dot(a0, b1) + _dot(a1, b0))
            + (_dot(a0, b2) + _dot(a1, b1) + _dot(a2, b0)))


CONV_GROUPS = CONV_W // PROJ_TILE_N
N_Q_TILES = ATTN_W // PROJ_TILE_N
N_KV_TILES = 3
GATE_STEP = 0
Q_STEP0 = GATE_STEP + 1
KV_STEP0 = Q_STEP0 + N_Q_TILES
CONV_STEP0 = KV_STEP0 + N_KV_TILES
N_PROJ_STEPS = CONV_STEP0 + CONV_GROUPS
N_WEIGHT_TILES = 3 * CONV_GROUPS + N_Q_TILES + N_KV_TILES
KV_OUT_W = N_KV_TILES * KV_W
assert 2 * KV_W == PROJ_TILE_N and N_GATES % SUBLANES == 0


def _gate_weight(w):
    gates = w[:, N_WEIGHT_TILES * PROJ_TILE_N:].astype(BF16)
    return jnp.pad(gates, ((0, 0), (0, LANES - gates.shape[1])))


def _inproj_body(*refs, q_scale, blocks_per_seq, side_blocks):
    n_side = len(side_blocks)
    x_ref, g_ref, w_ref, wc_ref, wh_ref, wg_ref, cos_ref, sin_ref, cw_ref = refs[:9]
    side_in = refs[9:9 + n_side]
    yc_ref, qt_ref, k_ref, kvc_ref, vt_ref, zgt_ref = refs[9 + n_side:15 + n_side]
    side_out = refs[15 + n_side:15 + 2 * n_side]
    a_scr, tail_scr, zg_scr = refs[15 + 2 * n_side:]
    i = pl.program_id(0)
    j = pl.program_id(1)

    step = i * N_PROJ_STEPS + j
    for src, dst, n_blocks in zip(side_in, side_out, side_blocks):
        @pl.when(step < n_blocks)
        def _(src=src, dst=dst):
            dst[...] = src[...].astype(BF16)

    @pl.when(j == 0)
    def _():
        a_scr[...] = _rms(x_ref[...], g_ref[...]).astype(BF16)

    @pl.when((i == 0) & (j == 0))
    def _():
        tail_scr[...] = jnp.zeros(tail_scr.shape, F32)

    @pl.when(j == GATE_STEP)
    def _():
        zg_scr[...] = _dot(a_scr[...], wg_ref[...])
        zgt_ref[0] = zg_scr[...].T[0:N_GATES]

    chunks = [(c, c + PROJ_CHUNK) for c in range(0, PROJ_TILE_N, PROJ_CHUNK)]

    def z_cols(weight_ref, c0, c1):
        return _dot(a_scr[...], weight_ref[:, c0:c1])

    def rope(zh):
        cos = cos_ref[...]
        sin = sin_ref[...]
        heads = []
        for h in range(zh.shape[1] // HEAD_DIM):
            z1 = zh[:, h * HEAD_DIM:(h + 1) * HEAD_DIM]
            heads.append(z1 * cos + pltpu.roll(z1, HEAD_DIM // 2, axis=1) * sin)
        return jnp.concatenate(heads, axis=1)

    @pl.when(j >= CONV_STEP0)
    def _():
        group = j - CONV_STEP0
        for c0, c1 in chunks:
            u = z_cols(wc_ref, c0, c1) * z_cols(wh_ref, c0, c1)
            tail = tail_scr[group, :, c0:c1]
            tail = jnp.where(i % blocks_per_seq == 0, 0.0, tail)
            row = lax.broadcasted_iota(jnp.int32, u.shape, 0)
            t1 = tail[SUBLANES - 1:SUBLANES]
            t2 = tail[SUBLANES - 2:SUBLANES - 1]
            u1 = jnp.where(row == 0, t1, pltpu.roll(u, 1, axis=0))
            u2 = jnp.where(row == 0, t2, jnp.where(row == 1, t1, pltpu.roll(u, 2, axis=0)))
            w = cw_ref[:, c0:c1]
            yc_ref[:, c0:c1] = z_cols(w_ref, c0, c1) * (u2 * w[0:1] + u1 * w[1:2] + u * w[2:3])
            tail_scr[group, :, c0:c1] = u[u.shape[0] - SUBLANES:, :]

    @pl.when((j >= Q_STEP0) & (j < KV_STEP0))
    def _():
        for c0, c1 in chunks:
            qt_ref[0, c0:c1, :] = (rope(z_cols(w_ref, c0, c1)) * q_scale).T.astype(BF16)

    @pl.when((j >= KV_STEP0) & (j < CONV_STEP0))
    def _():
        kx = rope(z_cols(w_ref, 0, KV_W)).astype(BF16)
        for h in range(N_KV_HEADS):
            k_ref[0, h] = kx[:, h * HEAD_DIM:(h + 1) * HEAD_DIM]
        zv = z_cols(w_ref, KV_W, 2 * KV_W)
        vt_ref[0] = zv.T.astype(BF16)

        @pl.when(j == KV_STEP0)
        def _():
            for h in range(N_KV_HEADS):
                kvc_ref[0, 0, h] = kx[:, h * HEAD_DIM:(h + 1) * HEAD_DIM].astype(F32)
                kvc_ref[1, 0, h] = zv[:, h * HEAD_DIM:(h + 1) * HEAD_DIM].astype(BF16).astype(F32)


def _side_rows(n_rows, n_steps):
    for rows in range(2 * SUBLANES, n_rows + 1, 2 * SUBLANES):
        if n_rows % rows == 0 and n_rows // rows <= n_steps:
            return rows
    raise ValueError((n_rows, n_steps))


def _in_projection(x2, g, w_proj, w_gates, cos_t, sin_t, conv_w, side, B, seq):
    T, D = x2.shape
    tn = PROJ_TILE_N
    assert w_proj.shape[1] >= N_WEIGHT_TILES * tn
    tm = min(1024, seq)
    nb = seq // tm
    n_steps = (T // tm) * N_PROJ_STEPS
    side_rows = [_side_rows(w.shape[0], n_steps) for w in side]
    side_blocks = [w.shape[0] // r for w, r in zip(side, side_rows)]
    side_specs = [pl.BlockSpec((r, w.shape[1]),
                               lambda i, j, n=n: (jnp.minimum(i * N_PROJ_STEPS + j, n - 1), 0))
                  for w, r, n in zip(side, side_rows, side_blocks)]
    body = functools.partial(_inproj_body, q_scale=HEAD_DIM ** -0.5 * LOG2_E, blocks_per_seq=nb,
                             side_blocks=tuple(side_blocks))
    vmem = 2 * (tm * D * 4 + 3 * D * tn * 2 + 2 * tm * HEAD_DIM * 4 + tm * tn * 4 + 4 * tm * tn * 2
                + N_GATES * tm * 4) + tm * D * 2 + 8 * tm * PROJ_CHUNK * 4 \
        + 2 * sum(r * w.shape[1] * 6 for w, r in zip(side, side_rows))

    def conv_group(j):
        return jnp.maximum(j - CONV_STEP0, 0)

    def main_tile(j):
        return jnp.where(j >= CONV_STEP0, j - CONV_STEP0,
                         jnp.maximum(j - Q_STEP0, 0) + 3 * CONV_GROUPS)

    def kv_tile(j):
        return jnp.clip(j - KV_STEP0, 0, N_KV_TILES - 1)

    return pl.pallas_call(
        body,
        grid=(T // tm, N_PROJ_STEPS),
        in_specs=[
            pl.BlockSpec((tm, D), lambda i, j: (i, 0)),
            pl.BlockSpec((1, D), lambda i, j: (0, 0)),
            pl.BlockSpec((D, tn), lambda i, j: (0, main_tile(j))),
            pl.BlockSpec((D, tn), lambda i, j: (0, CONV_GROUPS + conv_group(j))),
            pl.BlockSpec((D, tn), lambda i, j: (0, 2 * CONV_GROUPS + conv_group(j))),
            pl.BlockSpec((D, LANES), lambda i, j: (0, 0)),
            pl.BlockSpec((tm, HEAD_DIM), lambda i, j: (i % nb, 0)),
            pl.BlockSpec((tm, HEAD_DIM), lambda i, j: (i % nb, 0)),
            pl.BlockSpec((CONV_TAPS, tn), lambda i, j: (0, conv_group(j))),
        ] + side_specs,
        out_specs=[
            pl.BlockSpec((tm, tn), lambda i, j: (i, conv_group(j))),
            pl.BlockSpec((1, tn, tm), lambda i, j: (i // nb, jnp.clip(j - Q_STEP0, 0, N_Q_TILES - 1), i % nb)),
            pl.BlockSpec((1, N_KV_HEADS, tm, HEAD_DIM), lambda i, j: (i // nb, kv_tile(j), i % nb, 0)),
            pl.BlockSpec((2, 1, N_KV_HEADS, tm, HEAD_DIM), lambda i, j: (0, i // nb, 0, i % nb, 0)),
            pl.BlockSpec((1, KV_W, tm), lambda i, j: (i // nb, kv_tile(j), i % nb)),
            pl.BlockSpec((1, N_GATES, tm), lambda i, j: (i // nb, 0, i % nb)),
        ] + side_specs,
        out_shape=[
            jax.ShapeDtypeStruct((T, CONV_W), F32),
            jax.ShapeDtypeStruct((B, ATTN_W, seq), BF16),
            jax.ShapeDtypeStruct((B, N_KV_TILES * N_KV_HEADS, seq, HEAD_DIM), BF16),
            jax.ShapeDtypeStruct((2, B, N_KV_HEADS, seq, HEAD_DIM), F32),
            jax.ShapeDtypeStruct((B, KV_OUT_W, seq), BF16),
            jax.ShapeDtypeStruct((B, N_GATES, seq), F32),
        ] + [jax.ShapeDtypeStruct(w.shape, BF16) for w in side],
        scratch_shapes=[pltpu.VMEM((tm, D), BF16), pltpu.VMEM((CONV_GROUPS, SUBLANES, tn), F32),
                        pltpu.VMEM((tm, LANES), F32)],
        compiler_params=_params(("arbitrary", "arbitrary"), vmem),
        name="in_projection",
    )(x2, g, w_proj, w_proj, w_proj, w_gates, cos_t, sin_t, conv_w, *side)


def _compress_body(x_ref, pe_ref, w1_ref, w2_ref, o_ref, ot_ref, *, n_valid):
    n = x_ref.shape[2] // CMP_STRIDE
    x = jnp.concatenate([x_ref[0, 0, pl.ds(l, n, stride=CMP_STRIDE), :] for l in range(CMP_STRIDE)],
                        axis=1).astype(BF16)
    half = x.shape[1]
    w1 = w1_ref[0]
    w1_hi = w1.astype(BF16)
    w1_lo = (w1 - w1_hi.astype(F32)).astype(BF16)
    first = _dot(x, w1_hi[:half]) + _dot(x, w1_lo[:half])
    second = _dot(x, w1_hi[half:]) + _dot(x, w1_lo[half:])
    second = pltpu.roll(second, n - 1, axis=0)
    pe = jnp.broadcast_to(pe_ref[0], (SUBLANES, 2 * half))
    pe_term = _dot_f32(pe, w1)[0:1]
    h = first + second + pe_term
    out = _dot_f32(jax.nn.silu(h), w2_ref[0])
    row = lax.broadcasted_iota(jnp.int32, out.shape, 0)
    out = jnp.where(row < n_valid, out, 0.0)
    o_ref[0, 0] = out
    ot_ref[0, 0] = out.T


def _compress(kv, pe, w1, w2):
    two, BH, seq, d = kv.shape
    n = seq // CMP_STRIDE
    width = CMP_STRIDE * d
    body = functools.partial(_compress_body, n_valid=n - 1)
    vmem = 2 * (seq * d * 4 + 2 * width * 4 + 2 * width * d * 4 + d * d * 4 + n * d * 4) \
        + 8 * width * d * 4
    return pl.pallas_call(
        body,
        grid=(two, BH),
        in_specs=[
            pl.BlockSpec((1, 1, seq, d), lambda s, b: (s, b, 0, 0)),
            pl.BlockSpec((1, 1, 2 * width), lambda s, b: (s, 0, 0)),
            pl.BlockSpec((1, 2 * width, d), lambda s, b: (s, 0, 0)),
            pl.BlockSpec((1, d, d), lambda s, b: (s, 0, 0)),
        ],
        out_specs=[pl.BlockSpec((1, 1, n, d), lambda s, b: (s, b, 0, 0)),
                   pl.BlockSpec((1, 1, d, n), lambda s, b: (s, b, 0, 0))],
        out_shape=[jax.ShapeDtypeStruct((two, BH, n, d), F32),
                   jax.ShapeDtypeStruct((two, BH, d, n), F32)],
        compiler_params=_params(("parallel", "parallel"), vmem),
        name="compress",
    )(kv, pe, w1, w2)


def _select_blocks(imp, q0):
    n_sel = imp.shape[0]
    sel_i = lax.broadcasted_iota(jnp.int32, imp.shape, 0)
    sel_f = sel_i.astype(F32)
    cur = (q0 + lax.broadcasted_iota(jnp.int32, imp.shape, 1)) // SEL_BLOCK
    forced = (sel_i == 0) | (sel_i == cur) | (sel_i == cur - 1)
    v = jnp.where((sel_i > cur) | forced, -jnp.inf, imp)
    chosen = jnp.where(forced, 1.0, 0.0)
    for _ in range(min(N_SELECT, n_sel) - N_FORCED):
        top = jnp.max(v, axis=0, keepdims=True)
        first = jnp.min(jnp.where(v == top, sel_f, float(n_sel)), axis=0, keepdims=True)
        pick = sel_f == first
        chosen = jnp.where(pick, 1.0, chosen)
        v = jnp.where(pick, -jnp.inf, v)
    return jnp.where((chosen > 0.0) & (sel_i <= cur), 0.0, MASK_BIAS)


def _attn_body(qt_ref, zgt_ref, kc_ref, vct_ref, ovt_ref, ks_ref, vst_ref, e_ref, kw_ref, vwt_ref, wb_ref,
               y_ref, s_scr, m_scr, acc_scr, oc_scr, ow_scr, *, tk, span):
    q0 = pl.program_id(2) * ATTN_Q
    qt = [qt_ref[0, g * HEAD_DIM:(g + 1) * HEAD_DIM, :] for g in range(GROUP)]
    q_all = jnp.concatenate(qt, axis=1)

    def lane_time(shape):
        return q0 + (lax.broadcasted_iota(jnp.int32, shape, 1) & (ATTN_Q - 1))

    s = _dot(kc_ref[0, 0].astype(BF16), q_all)
    cmp_end = lax.broadcasted_iota(jnp.int32, s.shape, 0) * CMP_STRIDE + (CMP_BLOCK - 1)
    s = jnp.where(cmp_end <= lane_time(s.shape), s, -jnp.inf)
    m = jnp.max(s, axis=0, keepdims=True)
    m = jnp.where(m == -jnp.inf, 0.0, m)
    e = jnp.exp2(s - m)
    p = e / jnp.maximum(jnp.sum(e, axis=0, keepdims=True), 1e-30)
    oc_scr[...] = _dot(vct_ref[0, 0].astype(BF16), p.astype(BF16))
    pg = p[:, 0:ATTN_Q]
    for g in range(1, GROUP):
        pg = pg + p[:, g * ATTN_Q:(g + 1) * ATTN_Q]
    ovt = ovt_ref[...]
    p0, p1, p2 = _split3(pg)
    bias = _select_blocks(_dot(ovt, p0) + _dot(ovt, p1) + _dot(ovt, p2), q0).astype(BF16)

    w_start = pl.multiple_of(jnp.maximum(q0 - WINDOW, 0), ATTN_Q)
    sw = _dot(kw_ref[0, 0, pl.ds(w_start, span), :], q_all)
    sw = sw + jnp.concatenate([wb_ref[0]] * GROUP, axis=1)
    ew = jnp.exp2(sw - jnp.max(sw, axis=0, keepdims=True))
    vw = jnp.concatenate([vwt_ref[0, :, pl.ds(w_start, span)], jnp.ones((ONES_ROWS, span), BF16)], axis=0)
    ow = _dot(vw, ew.astype(BF16))
    ow_scr[...] = ow[0:HEAD_DIM] / ow[HEAD_DIM:HEAD_DIM + 1]

    qx = jnp.concatenate([jnp.concatenate([qt[g], bias], axis=0) for g in range(GROUP)], axis=1)

    def scores(j, slot):
        start = pl.multiple_of(j * tk, tk)
        kx = jnp.concatenate([ks_ref[0, 0, pl.ds(start, tk), :], e_ref[pl.ds(start, tk), :]], axis=1)
        s_scr[slot] = _dot(kx, qx)

    def update(j, slot, causal):
        start = pl.multiple_of(j * tk, tk)
        vt = jnp.concatenate([vst_ref[0, :, pl.ds(start, tk)], jnp.ones((ONES_ROWS, tk), BF16)], axis=0)
        sc = s_scr[slot]
        if causal:
            key = start + lax.broadcasted_iota(jnp.int32, sc.shape, 0)
            sc = jnp.where(key <= lane_time(sc.shape), sc, -jnp.inf)
        m_old = m_scr[...]
        m_new = jnp.maximum(m_old, jnp.max(sc, axis=0, keepdims=True))
        alpha = jnp.exp2(m_old - m_new)
        pj = jnp.exp2(sc - m_new)
        m_scr[...] = m_new
        acc_scr[...] = alpha * acc_scr[...] + _dot(vt, pj.astype(BF16))

    m_scr[...] = jnp.full(m_scr.shape, -jnp.inf, F32)
    acc_scr[...] = jnp.zeros(acc_scr.shape, F32)
    scores(0, 0)
    last = q0 // tk

    @pl.loop(0, last // 2)
    def _(i):
        scores(2 * i + 1, 1)
        update(2 * i, 0, False)
        scores(2 * i + 2, 0)
        update(2 * i + 1, 1, False)

    @pl.when(last % 2 == 1)
    def _():
        scores(last, 1)
        update(last - 1, 0, False)
        update(last, 1, True)

    @pl.when(last % 2 == 0)
    def _():
        update(last, 0, True)

    o_sel = acc_scr[0:HEAD_DIM, :] / acc_scr[HEAD_DIM:HEAD_DIM + 1, :]
    gates = jax.nn.sigmoid(zgt_ref[0])
    hk = pl.program_id(1)

    def gate(g, branch):
        row = gates[g * N_BRANCH + branch:g * N_BRANCH + branch + 1]
        for other in range(1, N_KV_HEADS):
            r = (other * GROUP + g) * N_BRANCH + branch
            row = jnp.where(hk == other, gates[r:r + 1], row)
        return row

    for g in range(GROUP):
        sl = slice(g * ATTN_Q, (g + 1) * ATTN_Q)
        y = (gate(g, 0) * oc_scr[:, sl] + gate(g, 1) * o_sel[:, sl]
             + gate(g, 2) * ow_scr[:, sl])
        y_ref[0, :, g * HEAD_DIM:(g + 1) * HEAD_DIM] = y.T


def _attention(qt, zgt, kvc, kvct, ovt, kk, vt, onehot, B, seq):
    n_cmp = kvc.shape[2]
    n_sel = ovt.shape[0]
    GW = GROUP * HEAD_DIM
    GQ = GROUP * ATTN_Q
    VR = HEAD_DIM + ONES_ROWS
    tk = min(512, seq)
    span = WINDOW + ATTN_Q
    assert seq >= span
    body = functools.partial(_attn_body, tk=tk, span=span)
    resident = 4 * seq * HEAD_DIM * 2 + seq * n_sel * 2
    vmem = 2 * (GW * ATTN_Q * 2 + N_GATES * ATTN_Q * 4 + 2 * n_cmp * HEAD_DIM * 4 + n_sel * n_cmp * 2
                + resident + ATTN_Q * GW * 4) \
        + (2 * tk + VR + 2 * HEAD_DIM + SUBLANES) * GQ * 4 + 8 * (n_cmp + span) * GQ * 4
    return pl.pallas_call(
        body,
        grid=(B, N_KV_HEADS, seq // ATTN_Q),
        in_specs=[
            pl.BlockSpec((1, GW, ATTN_Q), lambda b, h, i: (b, h, i)),
            pl.BlockSpec((1, N_GATES, ATTN_Q), lambda b, h, i: (b, 0, i)),
            pl.BlockSpec((1, 1, n_cmp, HEAD_DIM), lambda b, h, i: (0, b * N_KV_HEADS + h, 0, 0)),
            pl.BlockSpec((1, 1, HEAD_DIM, n_cmp), lambda b, h, i: (1, b * N_KV_HEADS + h, 0, 0)),
            pl.BlockSpec((n_sel, n_cmp), lambda b, h, i: (0, 0)),
            pl.BlockSpec((1, 1, seq, HEAD_DIM), lambda b, h, i: (b, N_KV_HEADS + h, 0, 0)),
            pl.BlockSpec((1, HEAD_DIM, seq), lambda b, h, i: (b, N_KV_HEADS + h, 0)),
            pl.BlockSpec((seq, n_sel), lambda b, h, i: (0, 0)),
            pl.BlockSpec((1, 1, seq, HEAD_DIM), lambda b, h, i: (b, 2 * N_KV_HEADS + h, 0, 0)),
            pl.BlockSpec((1, HEAD_DIM, seq), lambda b, h, i: (b, 2 * N_KV_HEADS + h, 0)),
            pl.BlockSpec((1, span, ATTN_Q), lambda b, h, i: (jnp.minimum(i, WINDOW // ATTN_Q), 0, 0)),
        ],
        out_specs=pl.BlockSpec((1, ATTN_Q, GW), lambda b, h, i: (b, i, h)),
        out_shape=jax.ShapeDtypeStruct((B, seq, ATTN_W), F32),
        scratch_shapes=[pltpu.VMEM((2, tk, GQ), F32),
                        pltpu.VMEM((1, GQ), F32),
                        pltpu.VMEM((VR, GQ), F32),
                        pltpu.VMEM((HEAD_DIM, GQ), F32),
                        pltpu.VMEM((HEAD_DIM, GQ), F32)],
        compiler_params=_params(("parallel", "parallel", "arbitrary"), vmem),
        name="attention",
    )(qt, zgt, kvc, kvct, ovt, kk, vt, onehot, kk, vt, _window_bias(span))


def _outproj_body(yc_ref, ya_ref, gc_ref, ga_ref, w_ref, x_ref, h_ref, a_scr):
    a_scr[:, 0:CONV_W] = _rms(yc_ref[...], gc_ref[...]).astype(BF16)
    a_scr[:, CONV_W:] = _rms(ya_ref[...], ga_ref[...]).astype(BF16)
    for c in range(0, h_ref.shape[1], PROJ_TILE_N):
        cols = slice(c, c + PROJ_TILE_N)
        h_ref[:, cols] = x_ref[:, cols] + _dot(a_scr[...], w_ref[:, cols])


def _out_projection(yc, ya, g_conv, g_attn, w, x2):
    T, D = x2.shape
    K = w.shape[0]
    tm = min(512, T)
    assert D % PROJ_TILE_N == 0
    vmem = 2 * (tm * CONV_W * 4 + tm * ATTN_W * 4 + 2 * tm * D * 4) + K * D * 2 + tm * K * 2 \
        + 2 * tm * ATTN_W * 4 + 2 * tm * PROJ_TILE_N * 4
    return pl.pallas_call(
        _outproj_body,
        grid=(T // tm,),
        in_specs=[
            pl.BlockSpec((tm, CONV_W), lambda i: (i, 0)),
            pl.BlockSpec((tm, ATTN_W), lambda i: (i, 0)),
            pl.BlockSpec((1, CONV_W), lambda i: (0, 0)),
            pl.BlockSpec((1, ATTN_W), lambda i: (0, 0)),
            pl.BlockSpec((K, D), lambda i: (0, 0), pipeline_mode=pl.Buffered(1)),
            pl.BlockSpec((tm, D), lambda i: (i, 0)),
        ],
        out_specs=pl.BlockSpec((tm, D), lambda i: (i, 0)),
        out_shape=jax.ShapeDtypeStruct((T, D), F32),
        scratch_shapes=[pltpu.VMEM((tm, K), BF16)],
        compiler_params=_params(("parallel",), vmem),
        name="out_projection",
    )(yc, ya, g_conv, g_attn, w, x2)


def _ffn_body(h_ref, g_ref, wg_ref, wu_ref, wd_ref, gf_ref, o_ref, a_scr, *, final_norm, tn):
    j = pl.program_id(1)

    @pl.when(j == 0)
    def _():
        @pl.loop(0, h_ref.shape[0] // FFN_EPILOGUE_ROWS)
        def _(r):
            rows = pl.ds(pl.multiple_of(r * FFN_EPILOGUE_ROWS, FFN_EPILOGUE_ROWS), FFN_EPILOGUE_ROWS)
            a_scr[rows, :] = _rms(h_ref[rows, :], g_ref[...]).astype(BF16)

    a = a_scr[...]
    u = jnp.concatenate(
        [(jax.nn.silu(_dot(a, wg_ref[:, c:c + PROJ_CHUNK])) * _dot(a, wu_ref[:, c:c + PROJ_CHUNK])).astype(BF16)
         for c in range(0, wg_ref.shape[1], PROJ_CHUNK)], axis=1)
    D = o_ref.shape[1]

    @pl.when(j == 0)
    def _():
        for n in range(D // tn):
            o_ref[:, n * tn:(n + 1) * tn] = _dot(u, wd_ref[:, n * tn:(n + 1) * tn])

    @pl.when(j > 0)
    def _():
        for n in range(D // tn):
            o_ref[:, n * tn:(n + 1) * tn] += _dot(u, wd_ref[:, n * tn:(n + 1) * tn])

    @pl.when(j == pl.num_programs(1) - 1)
    def _():
        @pl.loop(0, o_ref.shape[0] // FFN_EPILOGUE_ROWS)
        def _(r):
            rows = pl.ds(pl.multiple_of(r * FFN_EPILOGUE_ROWS, FFN_EPILOGUE_ROWS), FFN_EPILOGUE_ROWS)
            out = h_ref[rows, :] + o_ref[rows, :]
            o_ref[rows, :] = _rms(out, gf_ref[...]) if final_norm else out


def _ffn(h, g, wg, wu, wd, g_final, final_norm):
    T, D = h.shape
    F = wg.shape[1]
    tm = min(1024, T)
    tf = 512
    tn = PROJ_CHUNK
    assert F % tf == 0 and D % tn == 0
    body = functools.partial(_ffn_body, final_norm=final_norm, tn=tn)
    vmem = 2 * (3 * D * tf * 2 + 2 * tm * D * 4) + tm * D * 2 + 6 * tm * tf * 4 + 2 * tm * tn * 4
    return pl.pallas_call(
        body,
        grid=(T // tm, F // tf),
        in_specs=[
            pl.BlockSpec((tm, D), lambda i, j: (i, 0)),
            pl.BlockSpec((1, D), lambda i, j: (0, 0)),
            pl.BlockSpec((D, tf), lambda i, j: (0, j)),
            pl.BlockSpec((D, tf), lambda i, j: (0, j)),
            pl.BlockSpec((tf, D), lambda i, j: (j, 0)),
            pl.BlockSpec((1, D), lambda i, j: (0, 0)),
        ],
        out_specs=pl.BlockSpec((tm, D), lambda i, j: (i, 0)),
        out_shape=jax.ShapeDtypeStruct((T, D), F32),
        scratch_shapes=[pltpu.VMEM((tm, D), BF16)],
        compiler_params=_params(("parallel", "arbitrary"), vmem),
        name="ffn",
    )(h, g, wg, wu, wd, g_final)


def _rope_tables(seq):
    half = HEAD_DIM // 2
    inv = 1.0 / (ROPE_THETA ** (jnp.arange(half, dtype=F32) / half))
    inv = jnp.concatenate([inv, inv])
    sign = jnp.concatenate([-jnp.ones((half,), F32), jnp.ones((half,), F32)])
    lo = jnp.arange(ROPE_SPLIT).astype(F32)[:, None] * inv[None, :]
    hi = (jnp.arange(seq // ROPE_SPLIT) * ROPE_SPLIT).astype(F32)[:, None] * inv[None, :]
    cos_lo, sin_lo = jnp.cos(lo)[None], jnp.sin(lo)[None]
    cos_hi, sin_hi = jnp.cos(hi)[:, None, :], jnp.sin(hi)[:, None, :]
    cos = (cos_hi * cos_lo - sin_hi * sin_lo).reshape(seq, HEAD_DIM)
    sin = ((sin_hi * cos_lo + cos_hi * sin_lo) * sign).reshape(seq, HEAD_DIM)
    return cos, sin


def _overlap_t(n_sel, n_cmp):
    cs = np.arange(n_cmp)[None, :] * CMP_STRIDE
    ss = np.arange(n_sel)[:, None] * SEL_BLOCK
    return jnp.asarray(((cs < ss + SEL_BLOCK) & (cs + CMP_BLOCK > ss)).astype(np.float32), dtype=BF16)


def _window_bias(span):
    r = np.arange(span)[None, :, None]
    c = np.arange(ATTN_Q)[None, None, :]
    off = np.minimum(np.arange(WINDOW // ATTN_Q + 1) * ATTN_Q, WINDOW)[:, None, None]
    valid = (r <= off + c) & (r > off + c - WINDOW)
    return jnp.asarray(np.where(valid, 0.0, -np.inf).astype(np.float32))


def _block_onehot(seq, n_sel):
    return jnp.asarray((np.arange(seq)[:, None] // SEL_BLOCK == np.arange(n_sel)[None, :])
                       .astype(np.float32), dtype=BF16)


def kernel(x, norm_mix, w_in, conv_w, cmp_pe_k, cmp_w1_k, cmp_w2_k, cmp_pe_v, cmp_w1_v, cmp_w2_v,
           norm_conv_out, norm_attn_out, w_out, norm_ffn, w_gate, w_up, w_down, norm_final):
    B, S, D = x.shape
    T = B * S
    depth = norm_mix.shape[0]
    assert S % ATTN_Q == 0 and S % CMP_STRIDE == 0 and S % ROPE_SPLIT == 0
    n_chunks = S // CMP_STRIDE
    n_sel = S // SEL_BLOCK
    cos_t, sin_t = _rope_tables(S)
    ovt = _overlap_t(n_sel, n_chunks)
    onehot = _block_onehot(S, n_sel)

    h = x.reshape(T, D)
    for l in range(depth):
        w_tiles = w_in[l][:, :N_WEIGHT_TILES * PROJ_TILE_N].astype(BF16)
        y_conv, qt, kk, kvc_in, vt, zgt, w_out_b, w_gate_b, w_up_b, w_down_b = _in_projection(
            h, norm_mix[l][None], w_tiles, _gate_weight(w_in[l]), cos_t, sin_t, conv_w[l],
            [w_out[l], w_gate[l], w_up[l], w_down[l]], B, S)
        kvc_in = kvc_in.reshape(2, B * N_KV_HEADS, S, HEAD_DIM)
        pe = jnp.stack([cmp_pe_k[l], cmp_pe_v[l]]).reshape(2, 1, CMP_BLOCK * HEAD_DIM)
        kvc, kvct = _compress(kvc_in, pe, jnp.stack([cmp_w1_k[l], cmp_w1_v[l]]),
                              jnp.stack([cmp_w2_k[l], cmp_w2_v[l]]))

        y_attn = _attention(qt, zgt, kvc, kvct, ovt, kk, vt, onehot, B, S)

        h = _out_projection(y_conv, y_attn.reshape(T, ATTN_W), norm_conv_out[l][None],
                            norm_attn_out[l][None], w_out_b, h)

        h = _ffn(h, norm_ffn[l][None], w_gate_b, w_up_b, w_down_b, norm_final[None], l == depth - 1)
    return h.reshape(B, S, D)
```

```python
import functools

import jax
import jax.numpy as jnp
import numpy as np
from jax import lax
from jax.experimental import pallas as pl
from jax.experimental.pallas import tpu as pltpu

F32 = jnp.float32
BF16 = jnp.bfloat16

HEAD_DIM = 128
N_HEADS = 8
N_KV_HEADS = 2
GROUP = N_HEADS // N_KV_HEADS
N_BRANCH = 3
CONV_TAPS = 3
CMP_BLOCK = 32
CMP_STRIDE = 16
SEL_BLOCK = 64
N_SELECT = 16
N_FORCED = 3
WINDOW = 512
ATTN_Q = 256
ROPE_THETA = 10000.0
ROPE_SPLIT = 128
RMS_EPS = 1e-6

CONV_W = 1024
ATTN_W = N_HEADS * HEAD_DIM
KV_W = N_KV_HEADS * HEAD_DIM
N_GATES = N_HEADS * N_BRANCH

V7X_VMEM_BYTES = 64 * 1024 * 1024
V7X_VMEM_BUDGET = V7X_VMEM_BYTES - 5 * 1024 * 1024
LANES = 128
SUBLANES = 8
PROJ_TILE_N = 512
PROJ_CHUNK = 256
NORM_ROWS = 256

MASK_BIAS = -float(2 ** 30)
LOG2_E = 1.4426950408889634
ONES_ROWS = 16


def _params(semantics, vmem_bytes):
    return pltpu.CompilerParams(dimension_semantics=semantics,
                                vmem_limit_bytes=int(min(vmem_bytes, V7X_VMEM_BUDGET)))


def _rms(x, g):
    return x * lax.rsqrt(jnp.mean(x * x, axis=-1, keepdims=True) + RMS_EPS) * g


def _split3(x):
    hi = x.astype(BF16)
    r1 = x - hi.astype(F32)
    mid = r1.astype(BF16)
    lo = (r1 - mid.astype(F32)).astype(BF16)
    return hi, mid, lo


def _dot(a, b):
    return jnp.dot(a, b, preferred_element_type=F32)


def _dot_f32(a, b):
    a0, a1, a2 = _split3(a)
    b0, b1, b2 = _split3(b)
    return (_dot(a0, b0) + (_dot(a0, b1) + _dot(a1, b0))
            + (_dot(a0, b2) + _dot(a1, b1) + _dot(a2, b0)))


CONV_GROUPS = CONV_W // PROJ_TILE_N
N_Q_TILES = ATTN_W // PROJ_TILE_N
N_KV_TILES = 3
Q_STEP0 = 0
KV_STEP0 = Q_STEP0 + N_Q_TILES
CONV_STEP0 = KV_STEP0 + N_KV_TILES
N_PROJ_STEPS = CONV_STEP0 + CONV_GROUPS
N_WEIGHT_TILES = 3 * CONV_GROUPS + N_Q_TILES + N_KV_TILES
KV_OUT_W = N_KV_TILES * KV_W
assert 2 * KV_W == PROJ_TILE_N and N_GATES % SUBLANES == 0


def _gate_weight(w):
    gates = w[:, N_WEIGHT_TILES * PROJ_TILE_N:].astype(BF16)
    return jnp.pad(gates, ((0, 0), (0, LANES - gates.shape[1])))


def _inproj_body(*refs, q_scale, blocks_per_seq, side_blocks):
    n_side = len(side_blocks)
    x_ref, g_ref, w_ref, wc_ref, wh_ref, wg_ref, cos_ref, sin_ref, cw_ref = refs[:9]
    side_in = refs[9:9 + n_side]
    yc_ref, qt_ref, k_ref, kvc_ref, vt_ref, zgt_ref = refs[9 + n_side:15 + n_side]
    side_out = refs[15 + n_side:15 + 2 * n_side]
    a_scr, tail_scr, zg_scr = refs[15 + 2 * n_side:]
    i = pl.program_id(0)
    j = pl.program_id(1)

    step = i * N_PROJ_STEPS + j
    for src, dst, n_blocks in zip(side_in, side_out, side_blocks):
        @pl.when(step < n_blocks)
        def _(src=src, dst=dst):
            dst[...] = src[...].astype(BF16)

    @pl.when(j == 0)
    def _():
        @pl.loop(0, x_ref.shape[0] // NORM_ROWS)
        def _(r):
            rows = pl.ds(pl.multiple_of(r * NORM_ROWS, NORM_ROWS), NORM_ROWS)
            a_scr[rows, :] = _rms(x_ref[rows, :], g_ref[...]).astype(BF16)

    @pl.when((i == 0) & (j == 0))
    def _():
        tail_scr[...] = jnp.zeros(tail_scr.shape, F32)

    @pl.when(j == Q_STEP0)
    def _():
        zg_scr[...] = _dot(a_scr[...], wg_ref[...])
        zgt_ref[0] = zg_scr[...].T[0:N_GATES]

    chunks = [(c, c + PROJ_CHUNK) for c in range(0, PROJ_TILE_N, PROJ_CHUNK)]

    def z_cols(weight_ref, c0, c1):
        return _dot(a_scr[...], weight_ref[:, c0:c1])

    def rope(zh):
        cos = cos_ref[...]
        sin = sin_ref[...]
        heads = []
        for h in range(zh.shape[1] // HEAD_DIM):
            z1 = zh[:, h * HEAD_DIM:(h + 1) * HEAD_DIM]
            heads.append(z1 * cos + pltpu.roll(z1, HEAD_DIM // 2, axis=1) * sin)
        return jnp.concatenate(heads, axis=1)

    @pl.when(j >= CONV_STEP0)
    def _():
        group = j - CONV_STEP0
        for c0, c1 in chunks:
            u = z_cols(wc_ref, c0, c1) * z_cols(wh_ref, c0, c1)
            tail = tail_scr[group, :, c0:c1]
            tail = jnp.where(i % blocks_per_seq == 0, 0.0, tail)
            row = lax.broadcasted_iota(jnp.int32, u.shape, 0)
            t1 = tail[SUBLANES - 1:SUBLANES]
            t2 = tail[SUBLANES - 2:SUBLANES - 1]
            u1 = jnp.where(row == 0, t1, pltpu.roll(u, 1, axis=0))
            u2 = jnp.where(row == 0, t2, jnp.where(row == 1, t1, pltpu.roll(u, 2, axis=0)))
            w = cw_ref[:, c0:c1]
            yc_ref[:, c0:c1] = z_cols(w_ref, c0, c1) * (u2 * w[0:1] + u1 * w[1:2] + u * w[2:3])
            tail_scr[group, :, c0:c1] = u[u.shape[0] - SUBLANES:, :]

    @pl.when((j >= Q_STEP0) & (j < KV_STEP0))
    def _():
        for c0, c1 in chunks:
            qt_ref[0, c0:c1, :] = (rope(z_cols(w_ref, c0, c1)) * q_scale).T.astype(BF16)

    @pl.when((j >= KV_STEP0) & (j < CONV_STEP0))
    def _():
        kx = rope(z_cols(w_ref, 0, KV_W)).astype(BF16)
        for h in range(N_KV_HEADS):
            k_ref[0, h] = kx[:, h * HEAD_DIM:(h + 1) * HEAD_DIM]
        zv = z_cols(w_ref, KV_W, 2 * KV_W)
        vt_ref[0] = zv.T.astype(BF16)

        @pl.when(j == KV_STEP0)
        def _():
            for h in range(N_KV_HEADS):
                kvc_ref[0, 0, h] = kx[:, h * HEAD_DIM:(h + 1) * HEAD_DIM].astype(F32)
                kvc_ref[1, 0, h] = zv[:, h * HEAD_DIM:(h + 1) * HEAD_DIM].astype(BF16).astype(F32)


def _side_rows(n_rows, n_steps):
    for rows in range(2 * SUBLANES, n_rows + 1, 2 * SUBLANES):
        if n_rows % rows == 0 and n_rows // rows <= n_steps:
            return rows
    raise ValueError((n_rows, n_steps))


def _in_projection(x2, g, w_proj, w_gates, cos_t, sin_t, conv_w, side, B, seq):
    T, D = x2.shape
    tn = PROJ_TILE_N
    assert w_proj.shape[1] >= N_WEIGHT_TILES * tn
    tm = min(1024, seq)
    nb = seq // tm
    n_steps = (T // tm) * N_PROJ_STEPS
    side_rows = [_side_rows(w.shape[0], n_steps) for w in side]
    side_blocks = [w.shape[0] // r for w, r in zip(side, side_rows)]
    side_specs = [pl.BlockSpec((r, w.shape[1]),
                               lambda i, j, n=n: (jnp.minimum(i * N_PROJ_STEPS + j, n - 1), 0))
                  for w, r, n in zip(side, side_rows, side_blocks)]
    body = functools.partial(_inproj_body, q_scale=HEAD_DIM ** -0.5 * LOG2_E, blocks_per_seq=nb,
                             side_blocks=tuple(side_blocks))
    vmem = 2 * (tm * D * 4 + 3 * D * tn * 2 + 2 * tm * HEAD_DIM * 4 + tm * tn * 4 + 4 * tm * tn * 2
                + N_GATES * tm * 4) + tm * D * 2 + 8 * tm * PROJ_CHUNK * 4 \
        + 2 * sum(r * w.shape[1] * 6 for w, r in zip(side, side_rows))

    def conv_group(j):
        return jnp.maximum(j - CONV_STEP0, 0)

    def main_tile(j):
        return jnp.where(j >= CONV_STEP0, j - CONV_STEP0,
                         jnp.maximum(j - Q_STEP0, 0) + 3 * CONV_GROUPS)

    def kv_tile(j):
        return jnp.clip(j - KV_STEP0, 0, N_KV_TILES - 1)

    return pl.pallas_call(
        body,
        grid=(T // tm, N_PROJ_STEPS),
        in_specs=[
            pl.BlockSpec((tm, D), lambda i, j: (i, 0)),
            pl.BlockSpec((1, D), lambda i, j: (0, 0)),
            pl.BlockSpec((D, tn), lambda i, j: (0, main_tile(j))),
            pl.BlockSpec((D, tn), lambda i, j: (0, CONV_GROUPS + conv_group(j))),
            pl.BlockSpec((D, tn), lambda i, j: (0, 2 * CONV_GROUPS + conv_group(j))),
            pl.BlockSpec((D, LANES), lambda i, j: (0, 0)),
            pl.BlockSpec((tm, HEAD_DIM), lambda i, j: (i % nb, 0)),
            pl.BlockSpec((tm, HEAD_DIM), lambda i, j: (i % nb, 0)),
            pl.BlockSpec((CONV_TAPS, tn), lambda i, j: (0, conv_group(j))),
        ] + side_specs,
        out_specs=[
            pl.BlockSpec((tm, tn), lambda i, j: (i, conv_group(j))),
            pl.BlockSpec((1, tn, tm), lambda i, j: (i // nb, jnp.clip(j - Q_STEP0, 0, N_Q_TILES - 1), i % nb)),
            pl.BlockSpec((1, N_KV_HEADS, tm, HEAD_DIM), lambda i, j: (i // nb, kv_tile(j), i % nb, 0)),
            pl.BlockSpec((2, 1, N_KV_HEADS, tm, HEAD_DIM), lambda i, j: (0, i // nb, 0, i % nb, 0)),
            pl.BlockSpec((1, KV_W, tm), lambda i, j: (i // nb, kv_tile(j), i % nb)),
            pl.BlockSpec((1, N_GATES, tm), lambda i, j: (i // nb, 0, i % nb)),
        ] + side_specs,
        out_shape=[
            jax.ShapeDtypeStruct((T, CONV_W), F32),
            jax.ShapeDtypeStruct((B, ATTN_W, seq), BF16),
            jax.ShapeDtypeStruct((B, N_KV_TILES * N_KV_HEADS, seq, HEAD_DIM), BF16),
            jax.ShapeDtypeStruct((2, B, N_KV_HEADS, seq, HEAD_DIM), F32),
            jax.ShapeDtypeStruct((B, KV_OUT_W, seq), BF16),
            jax.ShapeDtypeStruct((B, N_GATES, seq), F32),
        ] + [jax.ShapeDtypeStruct(w.shape, BF16) for w in side],
        scratch_shapes=[pltpu.VMEM((tm, D), BF16), pltpu.VMEM((CONV_GROUPS, SUBLANES, tn), F32),
                        pltpu.VMEM((tm, LANES), F32)],
        compiler_params=_params(("arbitrary", "arbitrary"), vmem),
        name="in_projection",
    )(x2, g, w_proj, w_proj, w_proj, w_gates, cos_t, sin_t, conv_w, *side)


def _compress_body(x_ref, pe_ref, w1_ref, w2_ref, o_ref, ot_ref, *, n_valid):
    n = x_ref.shape[2] // CMP_STRIDE
    x = jnp.concatenate([x_ref[0, 0, pl.ds(l, n, stride=CMP_STRIDE), :] for l in range(CMP_STRIDE)],
                        axis=1).astype(BF16)
    half = x.shape[1]
    w1 = w1_ref[0]
    w1_hi = w1.astype(BF16)
    w1_lo = (w1 - w1_hi.astype(F32)).astype(BF16)
    first = _dot(x, w1_hi[:half]) + _dot(x, w1_lo[:half])
    second = _dot(x, w1_hi[half:]) + _dot(x, w1_lo[half:])
    second = pltpu.roll(second, n - 1, axis=0)
    pe = jnp.broadcast_to(pe_ref[0], (SUBLANES, 2 * half))
    pe_term = _dot_f32(pe, w1)[0:1]
    h = first + second + pe_term
    out = _dot_f32(jax.nn.silu(h), w2_ref[0])
    row = lax.broadcasted_iota(jnp.int32, out.shape, 0)
    out = jnp.where(row < n_valid, out, 0.0)
    o_ref[0, 0] = out
    ot_ref[0, 0] = out.T


def _compress(kv, pe, w1, w2):
    two, BH, seq, d = kv.shape
    n = seq // CMP_STRIDE
    width = CMP_STRIDE * d
    body = functools.partial(_compress_body, n_valid=n - 1)
    vmem = 2 * (seq * d * 4 + 2 * width * 4 + 2 * width * d * 4 + d * d * 4 + n * d * 4) \
        + 8 * width * d * 4
    return pl.pallas_call(
        body,
        grid=(two, BH),
        in_specs=[
            pl.BlockSpec((1, 1, seq, d), lambda s, b: (s, b, 0, 0)),
            pl.BlockSpec((1, 1, 2 * width), lambda s, b: (s, 0, 0)),
            pl.BlockSpec((1, 2 * width, d), lambda s, b: (s, 0, 0)),
            pl.BlockSpec((1, d, d), lambda s, b: (s, 0, 0)),
        ],
        out_specs=[pl.BlockSpec((1, 1, n, d), lambda s, b: (s, b, 0, 0)),
                   pl.BlockSpec((1, 1, d, n), lambda s, b: (s, b, 0, 0))],
        out_shape=[jax.ShapeDtypeStruct((two, BH, n, d), F32),
                   jax.ShapeDtypeStruct((two, BH, d, n), F32)],
        compiler_params=_params(("parallel", "parallel"), vmem),
        name="compress",
    )(kv, pe, w1, w2)


def _select_blocks(imp, q0):
    n_sel = imp.shape[0]
    sel_i = lax.broadcasted_iota(jnp.int32, imp.shape, 0)
    sel_f = sel_i.astype(F32)
    cur = (q0 + lax.broadcasted_iota(jnp.int32, imp.shape, 1)) // SEL_BLOCK
    forced = (sel_i == 0) | (sel_i == cur) | (sel_i == cur - 1)
    v = jnp.where((sel_i > cur) | forced, -jnp.inf, imp)
    chosen = jnp.where(forced, 1.0, 0.0)
    for _ in range(min(N_SELECT, n_sel) - N_FORCED):
        top = jnp.max(v, axis=0, keepdims=True)
        first = jnp.min(jnp.where(v == top, sel_f, float(n_sel)), axis=0, keepdims=True)
        pick = sel_f == first
        chosen = jnp.where(pick, 1.0, chosen)
        v = jnp.where(pick, -jnp.inf, v)
    return jnp.where((chosen > 0.0) & (sel_i <= cur), 0.0, MASK_BIAS)


def _attn_body(qt_ref, zgt_ref, kc_ref, vct_ref, ovt_ref, ks_ref, vst_ref, e_ref, kw_ref, vwt_ref, wb_ref,
               y_ref, s_scr, m_scr, acc_scr, oc_scr, ow_scr, *, tk, span):
    q0 = pl.program_id(2) * ATTN_Q
    qt = [qt_ref[0, g * HEAD_DIM:(g + 1) * HEAD_DIM, :] for g in range(GROUP)]
    q_all = jnp.concatenate(qt, axis=1)

    def lane_time(shape):
        return q0 + (lax.broadcasted_iota(jnp.int32, shape, 1) & (ATTN_Q - 1))

    s = _dot(kc_ref[0, 0].astype(BF16), q_all)
    cmp_end = lax.broadcasted_iota(jnp.int32, s.shape, 0) * CMP_STRIDE + (CMP_BLOCK - 1)
    s = jnp.where(cmp_end <= lane_time(s.shape), s, -jnp.inf)
    m = jnp.max(s, axis=0, keepdims=True)
    m = jnp.where(m == -jnp.inf, 0.0, m)
    e = jnp.exp2(s - m)
    p = e / jnp.maximum(jnp.sum(e, axis=0, keepdims=True), 1e-30)
    oc_scr[...] = _dot(vct_ref[0, 0].astype(BF16), p.astype(BF16))
    pg = p[:, 0:ATTN_Q]
    for g in range(1, GROUP):
        pg = pg + p[:, g * ATTN_Q:(g + 1) * ATTN_Q]
    ovt = ovt_ref[...]
    p0, p1, p2 = _split3(pg)
    bias = _select_blocks(_dot(ovt, p0) + _dot(ovt, p1) + _dot(ovt, p2), q0).astype(BF16)

    w_start = pl.multiple_of(jnp.maximum(q0 - WINDOW, 0), ATTN_Q)
    sw = _dot(kw_ref[0, 0, pl.ds(w_start, span), :], q_all)
    sw = sw + jnp.concatenate([wb_ref[0]] * GROUP, axis=1)
    ew = jnp.exp2(sw - jnp.max(sw, axis=0, keepdims=True))
    vw = jnp.concatenate([vwt_ref[0, :, pl.ds(w_start, span)], jnp.ones((ONES_ROWS, span), BF16)], axis=0)
    ow = _dot(vw, ew.astype(BF16))
    ow_scr[...] = ow[0:HEAD_DIM] / ow[HEAD_DIM:HEAD_DIM + 1]

    qx = jnp.concatenate([jnp.concatenate([qt[g], bias], axis=0) for g in range(GROUP)], axis=1)

    def scores(j, slot):
        start = pl.multiple_of(j * tk, tk)
        kx = jnp.concatenate([ks_ref[0, 0, pl.ds(start, tk), :], e_ref[pl.ds(start, tk), :]], axis=1)
        s_scr[slot] = _dot(kx, qx)

    def update(j, slot, causal):
        start = pl.multiple_of(j * tk, tk)
        vt = jnp.concatenate([vst_ref[0, :, pl.ds(start, tk)], jnp.ones((ONES_ROWS, tk), BF16)], axis=0)
        sc = s_scr[slot]
        if causal:
            key = start + lax.broadcasted_iota(jnp.int32, sc.shape, 0)
            sc = jnp.where(key <= lane_time(sc.shape), sc, -jnp.inf)
        m_old = m_scr[...]
        m_new = jnp.maximum(m_old, jnp.max(sc, axis=0, keepdims=True))
        alpha = jnp.exp2(m_old - m_new)
        pj = jnp.exp2(sc - m_new)
        m_scr[...] = m_new
        acc_scr[...] = alpha * acc_scr[...] + _dot(vt, pj.astype(BF16))

    m_scr[...] = jnp.full(m_scr.shape, -jnp.inf, F32)
    acc_scr[...] = jnp.zeros(acc_scr.shape, F32)
    scores(0, 0)
    last = q0 // tk

    @pl.loop(0, last // 2)
    def _(i):
        scores(2 * i + 1, 1)
        update(2 * i, 0, False)
        scores(2 * i + 2, 0)
        update(2 * i + 1, 1, False)

    @pl.when(last % 2 == 1)
    def _():
        scores(last, 1)
        update(last - 1, 0, False)
        update(last, 1, True)

    @pl.when(last % 2 == 0)
    def _():
        update(last, 0, True)

    o_sel = acc_scr[0:HEAD_DIM, :] / acc_scr[HEAD_DIM:HEAD_DIM + 1, :]
    gates = jax.nn.sigmoid(zgt_ref[0])
    hk = pl.program_id(1)

    def gate(g, branch):
        row = gates[g * N_BRANCH + branch:g * N_BRANCH + branch + 1]
        for other in range(1, N_KV_HEADS):
            r = (other * GROUP + g) * N_BRANCH + branch
            row = jnp.where(hk == other, gates[r:r + 1], row)
        return row

    for g in range(GROUP):
        sl = slice(g * ATTN_Q, (g + 1) * ATTN_Q)
        y = (gate(g, 0) * oc_scr[:, sl] + gate(g, 1) * o_sel[:, sl]
             + gate(g, 2) * ow_scr[:, sl])
        y_ref[0, :, g * HEAD_DIM:(g + 1) * HEAD_DIM] = y.T


def _attention(qt, zgt, kvc, kvct, ovt, kk, vt, onehot, B, seq):
    n_cmp = kvc.shape[2]
    n_sel = ovt.shape[0]
    GW = GROUP * HEAD_DIM
    GQ = GROUP * ATTN_Q
    VR = HEAD_DIM + ONES_ROWS
    tk = min(512, seq)
    span = WINDOW + ATTN_Q
    assert seq >= span
    body = functools.partial(_attn_body, tk=tk, span=span)
    resident = 4 * seq * HEAD_DIM * 2 + seq * n_sel * 2
    vmem = 2 * (GW * ATTN_Q * 2 + N_GATES * ATTN_Q * 4 + 2 * n_cmp * HEAD_DIM * 4 + n_sel * n_cmp * 2
                + resident + ATTN_Q * GW * 4) \
        + (2 * tk + VR + 2 * HEAD_DIM + SUBLANES) * GQ * 4 + 8 * (n_cmp + span) * GQ * 4
    return pl.pallas_call(
        body,
        grid=(B, N_KV_HEADS, seq // ATTN_Q),
        in_specs=[
            pl.BlockSpec((1, GW, ATTN_Q), lambda b, h, i: (b, h, i)),
            pl.BlockSpec((1, N_GATES, ATTN_Q), lambda b, h, i: (b, 0, i)),
            pl.BlockSpec((1, 1, n_cmp, HEAD_DIM), lambda b, h, i: (0, b * N_KV_HEADS + h, 0, 0)),
            pl.BlockSpec((1, 1, HEAD_DIM, n_cmp), lambda b, h, i: (1, b * N_KV_HEADS + h, 0, 0)),
            pl.BlockSpec((n_sel, n_cmp), lambda b, h, i: (0, 0)),
            pl.BlockSpec((1, 1, seq, HEAD_DIM), lambda b, h, i: (b, N_KV_HEADS + h, 0, 0)),
            pl.BlockSpec((1, HEAD_DIM, seq), lambda b, h, i: (b, N_KV_HEADS + h, 0)),
            pl.BlockSpec((seq, n_sel), lambda b, h, i: (0, 0)),
            pl.BlockSpec((1, 1, seq, HEAD_DIM), lambda b, h, i: (b, 2 * N_KV_HEADS + h, 0, 0)),
            pl.BlockSpec((1, HEAD_DIM, seq), lambda b, h, i: (b, 2 * N_KV_HEADS + h, 0)),
            pl.BlockSpec((1, span, ATTN_Q), lambda b, h, i: (jnp.minimum(i, WINDOW // ATTN_Q), 0, 0)),
        ],
        out_specs=pl.BlockSpec((1, ATTN_Q, GW), lambda b, h, i: (b, i, h)),
        out_shape=jax.ShapeDtypeStruct((B, seq, ATTN_W), F32),
        scratch_shapes=[pltpu.VMEM((2, tk, GQ), F32),
                        pltpu.VMEM((1, GQ), F32),
                        pltpu.VMEM((VR, GQ), F32),
                        pltpu.VMEM((HEAD_DIM, GQ), F32),
                        pltpu.VMEM((HEAD_DIM, GQ), F32)],
        compiler_params=_params(("parallel", "parallel", "arbitrary"), vmem),
        name="attention",
    )(qt, zgt, kvc, kvct, ovt, kk, vt, onehot, kk, vt, _window_bias(span))


def _outproj_body(yc_ref, ya_ref, gc_ref, ga_ref, w_ref, x_ref, h_ref, a_scr):
    a_scr[:, 0:CONV_W] = _rms(yc_ref[...], gc_ref[...]).astype(BF16)
    a_scr[:, CONV_W:] = _rms(ya_ref[...], ga_ref[...]).astype(BF16)
    for c in range(0, h_ref.shape[1], PROJ_TILE_N):
        cols = slice(c, c + PROJ_TILE_N)
        h_ref[:, cols] = x_ref[:, cols] + _dot(a_scr[...], w_ref[:, cols])


def _out_projection(yc, ya, g_conv, g_attn, w, x2):
    T, D = x2.shape
    K = w.shape[0]
    tm = min(512, T)
    assert D % PROJ_TILE_N == 0
    vmem = 2 * (tm * CONV_W * 4 + tm * ATTN_W * 4 + 2 * tm * D * 4) + K * D * 2 + tm * K * 2 \
        + 2 * tm * ATTN_W * 4 + 2 * tm * PROJ_TILE_N * 4
    return pl.pallas_call(
        _outproj_body,
        grid=(T // tm,),
        in_specs=[
            pl.BlockSpec((tm, CONV_W), lambda i: (i, 0)),
            pl.BlockSpec((tm, ATTN_W), lambda i: (i, 0)),
            pl.BlockSpec((1, CONV_W), lambda i: (0, 0)),
            pl.BlockSpec((1, ATTN_W), lambda i: (0, 0)),
            pl.BlockSpec((K, D), lambda i: (0, 0), pipeline_mode=pl.Buffered(1)),
            pl.BlockSpec((tm, D), lambda i: (i, 0)),
        ],
        out_specs=pl.BlockSpec((tm, D), lambda i: (i, 0)),
        out_shape=jax.ShapeDtypeStruct((T, D), F32),
        scratch_shapes=[pltpu.VMEM((tm, K), BF16)],
        compiler_params=_params(("parallel",), vmem),
        name="out_projection",
    )(yc, ya, g_conv, g_attn, w, x2)


def _ffn_body(h_ref, g_ref, wg_ref, wu_ref, wd_ref, gf_ref, o_ref, a_scr, *, final_norm, tn):
    j = pl.program_id(1)

    @pl.when(j == 0)
    def _():
        @pl.loop(0, h_ref.shape[0] // NORM_ROWS)
        def _(r):
            rows = pl.ds(pl.multiple_of(r * NORM_ROWS, NORM_ROWS), NORM_ROWS)
            a_scr[rows, :] = _rms(h_ref[rows, :], g_ref[...]).astype(BF16)

    a = a_scr[...]
    u = jnp.concatenate(
        [(jax.nn.silu(_dot(a, wg_ref[:, c:c + PROJ_CHUNK])) * _dot(a, wu_ref[:, c:c + PROJ_CHUNK])).astype(BF16)
         for c in range(0, wg_ref.shape[1], PROJ_CHUNK)], axis=1)
    D = o_ref.shape[1]

    @pl.when(j == 0)
    def _():
        for n in range(D // tn):
            o_ref[:, n * tn:(n + 1) * tn] = _dot(u, wd_ref[:, n * tn:(n + 1) * tn])

    @pl.when(j > 0)
    def _():
        for n in range(D // tn):
            o_ref[:, n * tn:(n + 1) * tn] += _dot(u, wd_ref[:, n * tn:(n + 1) * tn])

    @pl.when(j == pl.num_programs(1) - 1)
    def _():
        @pl.loop(0, o_ref.shape[0] // NORM_ROWS)
        def _(r):
            rows = pl.ds(pl.multiple_of(r * NORM_ROWS, NORM_ROWS), NORM_ROWS)
            out = h_ref[rows, :] + o_ref[rows, :]
            o_ref[rows, :] = _rms(out, gf_ref[...]) if final_norm else out


def _ffn(h, g, wg, wu, wd, g_final, final_norm):
    T, D = h.shape
    F = wg.shape[1]
    tm = min(1024, T)
    tf = 512
    tn = PROJ_CHUNK
    assert F % tf == 0 and D % tn == 0
    body = functools.partial(_ffn_body, final_norm=final_norm, tn=tn)
    vmem = 2 * (3 * D * tf * 2 + 2 * tm * D * 4) + tm * D * 2 + 6 * tm * tf * 4 + 2 * tm * tn * 4
    return pl.pallas_call(
        body,
        grid=(T // tm, F // tf),
        in_specs=[
            pl.BlockSpec((tm, D), lambda i, j: (i, 0)),
            pl.BlockSpec((1, D), lambda i, j: (0, 0)),
            pl.BlockSpec((D, tf), lambda i, j: (0, j)),
            pl.BlockSpec((D, tf), lambda i, j: (0, j)),
            pl.BlockSpec((tf, D), lambda i, j: (j, 0)),
            pl.BlockSpec((1, D), lambda i, j: (0, 0)),
        ],
        out_specs=pl.BlockSpec((tm, D), lambda i, j: (i, 0)),
        out_shape=jax.ShapeDtypeStruct((T, D), F32),
        scratch_shapes=[pltpu.VMEM((tm, D), BF16)],
        compiler_params=_params(("parallel", "arbitrary"), vmem),
        name="ffn",
    )(h, g, wg, wu, wd, g_final)


def _rope_tables(seq):
    half = HEAD_DIM // 2
    inv = 1.0 / (ROPE_THETA ** (jnp.arange(half, dtype=F32) / half))
    inv = jnp.concatenate([inv, inv])
    sign = jnp.concatenate([-jnp.ones((half,), F32), jnp.ones((half,), F32)])
    lo = jnp.arange(ROPE_SPLIT).astype(F32)[:, None] * inv[None, :]
    hi = (jnp.arange(seq // ROPE_SPLIT) * ROPE_SPLIT).astype(F32)[:, None] * inv[None, :]
    cos_lo, sin_lo = jnp.cos(lo)[None], jnp.sin(lo)[None]
    cos_hi, sin_hi = jnp.cos(hi)[:, None, :], jnp.sin(hi)[:, None, :]
    cos = (cos_hi * cos_lo - sin_hi * sin_lo).reshape(seq, HEAD_DIM)
    sin = ((sin_hi * cos_lo + cos_hi * sin_lo) * sign).reshape(seq, HEAD_DIM)
    return cos, sin


def _overlap_t(n_sel, n_cmp):
    cs = np.arange(n_cmp)[None, :] * CMP_STRIDE
    ss = np.arange(n_sel)[:, None] * SEL_BLOCK
    return jnp.asarray(((cs < ss + SEL_BLOCK) & (cs + CMP_BLOCK > ss)).astype(np.float32), dtype=BF16)


def _window_bias(span):
    r = np.arange(span)[None, :, None]
    c = np.arange(ATTN_Q)[None, None, :]
    off = np.minimum(np.arange(WINDOW // ATTN_Q + 1) * ATTN_Q, WINDOW)[:, None, None]
    valid = (r <= off + c) & (r > off + c - WINDOW)
    return jnp.asarray(np.where(valid, 0.0, -np.inf).astype(np.float32))


def _block_onehot(seq, n_sel):
    return jnp.asarray((np.arange(seq)[:, None] // SEL_BLOCK == np.arange(n_sel)[None, :])
                       .astype(np.float32), dtype=BF16)


def kernel(x, norm_mix, w_in, conv_w, cmp_pe_k, cmp_w1_k, cmp_w2_k, cmp_pe_v, cmp_w1_v, cmp_w2_v,
           norm_conv_out, norm_attn_out, w_out, norm_ffn, w_gate, w_up, w_down, norm_final):
    B, S, D = x.shape
    T = B * S
    depth = norm_mix.shape[0]
    assert S % ATTN_Q == 0 and S % CMP_STRIDE == 0 and S % ROPE_SPLIT == 0
    n_chunks = S // CMP_STRIDE
    n_sel = S // SEL_BLOCK
    cos_t, sin_t = _rope_tables(S)
    ovt = _overlap_t(n_sel, n_chunks)
    onehot = _block_onehot(S, n_sel)

    h = x.reshape(T, D)
    for l in range(depth):
        y_conv, qt, kk, kvc_in, vt, zgt, w_out_b, w_gate_b, w_up_b, w_down_b = _in_projection(
            h, norm_mix[l][None], w_in[l].astype(BF16), _gate_weight(w_in[l]), cos_t, sin_t, conv_w[l],
            [w_out[l], w_gate[l], w_up[l], w_down[l]], B, S)
        kvc_in = kvc_in.reshape(2, B * N_KV_HEADS, S, HEAD_DIM)
        pe = jnp.stack([cmp_pe_k[l], cmp_pe_v[l]]).reshape(2, 1, CMP_BLOCK * HEAD_DIM)
        kvc, kvct = _compress(kvc_in, pe, jnp.stack([cmp_w1_k[l], cmp_w1_v[l]]),
                              jnp.stack([cmp_w2_k[l], cmp_w2_v[l]]))

        y_attn = _attention(qt, zgt, kvc, kvct, ovt, kk, vt, onehot, B, S)

        h = _out_projection(y_conv, y_attn.reshape(T, ATTN_W), norm_conv_out[l][None],
                            norm_attn_out[l][None], w_out_b, h)

        h = _ffn(h, norm_ffn[l][None], w_gate_b, w_up_b, w_down_b, norm_final[None], l == depth - 1)
    return h.reshape(B, S, D)
```

```python
import functools

import jax
import jax.numpy as jnp
import numpy as np
from jax import lax
from jax.experimental import pallas as pl
from jax.experimental.pallas import tpu as pltpu

F32 = jnp.float32
BF16 = jnp.bfloat16

HEAD_DIM = 128
N_HEADS = 8
N_KV_HEADS = 2
GROUP = N_HEADS // N_KV_HEADS
N_BRANCH = 3
CONV_TAPS = 3
CMP_BLOCK = 32
CMP_STRIDE = 16
SEL_BLOCK = 64
N_SELECT = 16
N_FORCED = 3
WINDOW = 512
ATTN_Q = 256
ROPE_THETA = 10000.0
ROPE_SPLIT = 128
RMS_EPS = 1e-6

CONV_W = 1024
ATTN_W = N_HEADS * HEAD_DIM
KV_W = N_KV_HEADS * HEAD_DIM
N_GATES = N_HEADS * N_BRANCH

V7X_VMEM_BYTES = 64 * 1024 * 1024
V7X_VMEM_BUDGET = V7X_VMEM_BYTES - 6 * 1024 * 1024
LANES = 128
SUBLANES = 8
PROJ_TILE_N = 512
PROJ_CHUNK = 256
FFN_EPILOGUE_ROWS = 256

MASK_BIAS = -float(2 ** 30)
LOG2_E = 1.4426950408889634
ONES_ROWS = 16


def _params(semantics, vmem_bytes):
    return pltpu.CompilerParams(dimension_semantics=semantics,
                                vmem_limit_bytes=int(min(vmem_bytes, V7X_VMEM_BUDGET)))


def _rms(x, g):
    return x * lax.rsqrt(jnp.mean(x * x, axis=-1, keepdims=True) + RMS_EPS) * g


def _split3(x):
    hi = x.astype(BF16)
    r1 = x - hi.astype(F32)
    mid = r1.astype(BF16)
    lo = (r1 - mid.astype(F32)).astype(BF16)
    return hi, mid, lo


def _dot(a, b):
    return jnp.dot(a, b, preferred_element_type=F32)


def _dot_f32(a, b):
    a0, a1, a2 = _split3(a)
    b0, b1, b2 = _split3(b)
    return (_dot(a0, b0) + (_dot(a0, b1) + _dot(a1, b0))
            + (_dot(a0, b2) + _dot(a1, b1) + _dot(a2, b0)))


CONV_GROUPS = CONV_W // PROJ_TILE_N
N_Q_TILES = ATTN_W // PROJ_TILE_N
N_KV_TILES = 3
GATE_STEP = 0
Q_STEP0 = GATE_STEP + 1
KV_STEP0 = Q_STEP0 + N_Q_TILES
CONV_STEP0 = KV_STEP0 + N_KV_TILES
N_PROJ_STEPS = CONV_STEP0 + CONV_GROUPS
N_WEIGHT_TILES = 3 * CONV_GROUPS + N_Q_TILES + N_KV_TILES
KV_OUT_W = N_KV_TILES * KV_W
assert 2 * KV_W == PROJ_TILE_N and N_GATES % SUBLANES == 0


def _gate_weight(w):
    gates = w[:, N_WEIGHT_TILES * PROJ_TILE_N:].astype(BF16)
    return jnp.pad(gates, ((0, 0), (0, LANES - gates.shape[1])))


def _inproj_body(*refs, q_scale, blocks_per_seq, side_blocks):
    n_side = len(side_blocks)
    x_ref, g_ref, w_ref, wc_ref, wh_ref, wg_ref, cos_ref, sin_ref, cw_ref = refs[:9]
    side_in = refs[9:9 + n_side]
    yc_ref, qt_ref, k_ref, kvc_ref, vt_ref, zgt_ref = refs[9 + n_side:15 + n_side]
    side_out = refs[15 + n_side:15 + 2 * n_side]
    a_scr, tail_scr, zg_scr = refs[15 + 2 * n_side:]
    p = pl.program_id(0)
    j = pl.program_id(1)
    i = p - 1
    live = p >= 1

    step = p * N_PROJ_STEPS + j
    for src, dst, n_blocks in zip(side_in, side_out, side_blocks):
        @pl.when(step < n_blocks)
        def _(src=src, dst=dst):
            dst[...] = src[...].astype(BF16)

    @pl.when(p < pl.num_programs(0) - 1)
    def _():
        piece = x_ref.shape[0]
        rows = pl.ds(pl.multiple_of(j * piece, piece), piece)
        a_scr[p % 2, rows, :] = _rms(x_ref[...], g_ref[...]).astype(BF16)

    @pl.when((p == 0) & (j == 0))
    def _():
        tail_scr[...] = jnp.zeros(tail_scr.shape, F32)

    def a_block():
        return a_scr[(p + 1) % 2]

    @pl.when(live & (j == GATE_STEP))
    def _():
        zg_scr[...] = _dot(a_block(), wg_ref[...])
        zgt_ref[0] = zg_scr[...].T[0:N_GATES]

    chunks = [(c, c + PROJ_CHUNK) for c in range(0, PROJ_TILE_N, PROJ_CHUNK)]

    def z_cols(weight_ref, c0, c1):
        return _dot(a_block(), weight_ref[:, c0:c1])

    def rope(zh):
        cos = cos_ref[...]
        sin = sin_ref[...]
        heads = []
        for h in range(zh.shape[1] // HEAD_DIM):
            z1 = zh[:, h * HEAD_DIM:(h + 1) * HEAD_DIM]
            heads.append(z1 * cos + pltpu.roll(z1, HEAD_DIM // 2, axis=1) * sin)
        return jnp.concatenate(heads, axis=1)

    @pl.when(live & (j >= CONV_STEP0))
    def _():
        group = j - CONV_STEP0
        for c0, c1 in chunks:
            u = z_cols(wc_ref, c0, c1) * z_cols(wh_ref, c0, c1)
            tail = tail_scr[group, :, c0:c1]
            tail = jnp.where(i % blocks_per_seq == 0, 0.0, tail)
            row = lax.broadcasted_iota(jnp.int32, u.shape, 0)
            t1 = tail[SUBLANES - 1:SUBLANES]
            t2 = tail[SUBLANES - 2:SUBLANES - 1]
            u1 = jnp.where(row == 0, t1, pltpu.roll(u, 1, axis=0))
            u2 = jnp.where(row == 0, t2, jnp.where(row == 1, t1, pltpu.roll(u, 2, axis=0)))
            w = cw_ref[:, c0:c1]
            yc_ref[:, c0:c1] = z_cols(w_ref, c0, c1) * (u2 * w[0:1] + u1 * w[1:2] + u * w[2:3])
            tail_scr[group, :, c0:c1] = u[u.shape[0] - SUBLANES:, :]

    @pl.when(live & (j >= Q_STEP0) & (j < KV_STEP0))
    def _():
        for c0, c1 in chunks:
            qt_ref[0, c0:c1, :] = (rope(z_cols(w_ref, c0, c1)) * q_scale).T.astype(BF16)

    @pl.when(live & (j >= KV_STEP0) & (j < CONV_STEP0))
    def _():
        kx = rope(z_cols(w_ref, 0, KV_W)).astype(BF16)
        for h in range(N_KV_HEADS):
            k_ref[0, h] = kx[:, h * HEAD_DIM:(h + 1) * HEAD_DIM]
        zv = z_cols(w_ref, KV_W, 2 * KV_W)
        vt_ref[0] = zv.T.astype(BF16)

        @pl.when(j == KV_STEP0)
        def _():
            for h in range(N_KV_HEADS):
                kvc_ref[0, 0, h] = kx[:, h * HEAD_DIM:(h + 1) * HEAD_DIM].astype(F32)
                kvc_ref[1, 0, h] = zv[:, h * HEAD_DIM:(h + 1) * HEAD_DIM].astype(BF16).astype(F32)


def _side_rows(n_rows, n_steps):
    for rows in range(2 * SUBLANES, n_rows + 1, 2 * SUBLANES):
        if n_rows % rows == 0 and n_rows // rows <= n_steps:
            return rows
    raise ValueError((n_rows, n_steps))


def _in_projection(x2, g, w_proj, w_gates, cos_t, sin_t, conv_w, side, B, seq):
    T, D = x2.shape
    tn = PROJ_TILE_N
    assert w_proj.shape[1] >= N_WEIGHT_TILES * tn
    tm = min(1024, seq)
    nb = seq // tm
    x_rows = tm // N_PROJ_STEPS
    assert tm % N_PROJ_STEPS == 0 and x_rows % (2 * SUBLANES) == 0
    n_x_blocks = T // x_rows
    n_steps = (T // tm + 1) * N_PROJ_STEPS
    side_rows = [_side_rows(w.shape[0], n_steps) for w in side]
    side_blocks = [w.shape[0] // r for w, r in zip(side, side_rows)]
    side_specs = [pl.BlockSpec((r, w.shape[1]),
                               lambda i, j, n=n: (jnp.minimum(i * N_PROJ_STEPS + j, n - 1), 0))
                  for w, r, n in zip(side, side_rows, side_blocks)]
    body = functools.partial(_inproj_body, q_scale=HEAD_DIM ** -0.5 * LOG2_E, blocks_per_seq=nb,
                             side_blocks=tuple(side_blocks))
    vmem = 2 * (x_rows * D * 4 + 3 * D * tn * 2 + 2 * tm * HEAD_DIM * 4 + tm * tn * 4 + 4 * tm * tn * 2
                + N_GATES * tm * 4) + 2 * tm * D * 2 + 8 * tm * PROJ_CHUNK * 4 \
        + 2 * sum(r * w.shape[1] * 6 for w, r in zip(side, side_rows))

    def at(index):
        return lambda p, j: index(jnp.maximum(p - 1, 0), jnp.where(p == 0, 0, j))

    def conv_group(j):
        return jnp.maximum(j - CONV_STEP0, 0)

    def main_tile(j):
        return jnp.where(j >= CONV_STEP0, j - CONV_STEP0,
                         jnp.maximum(j - Q_STEP0, 0) + 3 * CONV_GROUPS)

    def kv_tile(j):
        return jnp.clip(j - KV_STEP0, 0, N_KV_TILES - 1)

    return pl.pallas_call(
        body,
        grid=(T // tm + 1, N_PROJ_STEPS),
        in_specs=[
            pl.BlockSpec((x_rows, D), lambda p, j: (jnp.minimum(p * N_PROJ_STEPS + j, n_x_blocks - 1), 0)),
            pl.BlockSpec((1, D), lambda p, j: (0, 0)),
            pl.BlockSpec((D, tn), at(lambda i, j: (0, main_tile(j)))),
            pl.BlockSpec((D, tn), at(lambda i, j: (0, CONV_GROUPS + conv_group(j)))),
            pl.BlockSpec((D, tn), at(lambda i, j: (0, 2 * CONV_GROUPS + conv_group(j)))),
            pl.BlockSpec((D, LANES), lambda p, j: (0, 0)),
            pl.BlockSpec((tm, HEAD_DIM), at(lambda i, j: (i % nb, 0))),
            pl.BlockSpec((tm, HEAD_DIM), at(lambda i, j: (i % nb, 0))),
            pl.BlockSpec((CONV_TAPS, tn), at(lambda i, j: (0, conv_group(j)))),
        ] + side_specs,
        out_specs=[
            pl.BlockSpec((tm, tn), at(lambda i, j: (i, conv_group(j)))),
            pl.BlockSpec((1, tn, tm), at(lambda i, j: (i // nb, jnp.clip(j - Q_STEP0, 0, N_Q_TILES - 1), i % nb))),
            pl.BlockSpec((1, N_KV_HEADS, tm, HEAD_DIM), at(lambda i, j: (i // nb, kv_tile(j), i % nb, 0))),
            pl.BlockSpec((2, 1, N_KV_HEADS, tm, HEAD_DIM), at(lambda i, j: (0, i // nb, 0, i % nb, 0))),
            pl.BlockSpec((1, KV_W, tm), at(lambda i, j: (i // nb, kv_tile(j), i % nb))),
            pl.BlockSpec((1, N_GATES, tm), at(lambda i, j: (i // nb, 0, i % nb))),
        ] + side_specs,
        out_shape=[
            jax.ShapeDtypeStruct((T, CONV_W), F32),
            jax.ShapeDtypeStruct((B, ATTN_W, seq), BF16),
            jax.ShapeDtypeStruct((B, N_KV_TILES * N_KV_HEADS, seq, HEAD_DIM), BF16),
            jax.ShapeDtypeStruct((2, B, N_KV_HEADS, seq, HEAD_DIM), F32),
            jax.ShapeDtypeStruct((B, KV_OUT_W, seq), BF16),
            jax.ShapeDtypeStruct((B, N_GATES, seq), F32),
        ] + [jax.ShapeDtypeStruct(w.shape, BF16) for w in side],
        scratch_shapes=[pltpu.VMEM((2, tm, D), BF16), pltpu.VMEM((CONV_GROUPS, SUBLANES, tn), F32),
                        pltpu.VMEM((tm, LANES), F32)],
        compiler_params=_params(("arbitrary", "arbitrary"), vmem),
        name="in_projection",
    )(x2, g, w_proj, w_proj, w_proj, w_gates, cos_t, sin_t, conv_w, *side)


def _compress_body(x_ref, pe_ref, w1_ref, w2_ref, o_ref, ot_ref, *, n_valid):
    n = x_ref.shape[2] // CMP_STRIDE
    x = jnp.concatenate([x_ref[0, 0, pl.ds(l, n, stride=CMP_STRIDE), :] for l in range(CMP_STRIDE)],
                        axis=1).astype(BF16)
    half = x.shape[1]
    w1 = w1_ref[0]
    w1_hi = w1.astype(BF16)
    w1_lo = (w1 - w1_hi.astype(F32)).astype(BF16)
    first = _dot(x, w1_hi[:half]) + _dot(x, w1_lo[:half])
    second = _dot(x, w1_hi[half:]) + _dot(x, w1_lo[half:])
    second = pltpu.roll(second, n - 1, axis=0)
    pe = jnp.broadcast_to(pe_ref[0], (SUBLANES, 2 * half))
    pe_term = _dot_f32(pe, w1)[0:1]
    h = first + second + pe_term
    out = _dot_f32(jax.nn.silu(h), w2_ref[0])
    row = lax.broadcasted_iota(jnp.int32, out.shape, 0)
    out = jnp.where(row < n_valid, out, 0.0)
    o_ref[0, 0] = out
    ot_ref[0, 0] = out.T


def _compress(kv, pe, w1, w2):
    two, BH, seq, d = kv.shape
    n = seq // CMP_STRIDE
    width = CMP_STRIDE * d
    body = functools.partial(_compress_body, n_valid=n - 1)
    vmem = 2 * (seq * d * 4 + 2 * width * 4 + 2 * width * d * 4 + d * d * 4 + n * d * 4) \
        + 8 * width * d * 4
    return pl.pallas_call(
        body,
        grid=(two, BH),
        in_specs=[
            pl.BlockSpec((1, 1, seq, d), lambda s, b: (s, b, 0, 0)),
            pl.BlockSpec((1, 1, 2 * width), lambda s, b: (s, 0, 0)),
            pl.BlockSpec((1, 2 * width, d), lambda s, b: (s, 0, 0)),
            pl.BlockSpec((1, d, d), lambda s, b: (s, 0, 0)),
        ],
        out_specs=[pl.BlockSpec((1, 1, n, d), lambda s, b: (s, b, 0, 0)),
                   pl.BlockSpec((1, 1, d, n), lambda s, b: (s, b, 0, 0))],
        out_shape=[jax.ShapeDtypeStruct((two, BH, n, d), F32),
                   jax.ShapeDtypeStruct((two, BH, d, n), F32)],
        compiler_params=_params(("parallel", "parallel"), vmem),
        name="compress",
    )(kv, pe, w1, w2)


def _select_blocks(imp, q0):
    n_sel = imp.shape[0]
    sel_i = lax.broadcasted_iota(jnp.int32, imp.shape, 0)
    sel_f = sel_i.astype(F32)
    cur = (q0 + lax.broadcasted_iota(jnp.int32, imp.shape, 1)) // SEL_BLOCK
    forced = (sel_i == 0) | (sel_i == cur) | (sel_i == cur - 1)
    v = jnp.where((sel_i > cur) | forced, -jnp.inf, imp)
    chosen = jnp.where(forced, 1.0, 0.0)
    for _ in range(min(N_SELECT, n_sel) - N_FORCED):
        top = jnp.max(v, axis=0, keepdims=True)
        first = jnp.min(jnp.where(v == top, sel_f, float(n_sel)), axis=0, keepdims=True)
        pick = sel_f == first
        chosen = jnp.where(pick, 1.0, chosen)
        v = jnp.where(pick, -jnp.inf, v)
    return jnp.where((chosen > 0.0) & (sel_i <= cur), 0.0, MASK_BIAS)


def _attn_body(qt_ref, zgt_ref, kc_ref, vct_ref, ovt_ref, ks_ref, vst_ref, e_ref, kw_ref, vwt_ref, wb_ref,
               y_ref, s_scr, m_scr, acc_scr, oc_scr, ow_scr, *, tk, span):
    q0 = pl.program_id(2) * ATTN_Q
    qt = [qt_ref[0, g * HEAD_DIM:(g + 1) * HEAD_DIM, :] for g in range(GROUP)]
    q_all = jnp.concatenate(qt, axis=1)

    def lane_time(shape):
        return q0 + (lax.broadcasted_iota(jnp.int32, shape, 1) & (ATTN_Q - 1))

    s = _dot(kc_ref[0, 0].astype(BF16), q_all)
    cmp_end = lax.broadcasted_iota(jnp.int32, s.shape, 0) * CMP_STRIDE + (CMP_BLOCK - 1)
    s = jnp.where(cmp_end <= lane_time(s.shape), s, -jnp.inf)
    m = jnp.max(s, axis=0, keepdims=True)
    m = jnp.where(m == -jnp.inf, 0.0, m)
    e = jnp.exp2(s - m)
    p = e / jnp.maximum(jnp.sum(e, axis=0, keepdims=True), 1e-30)
    oc_scr[...] = _dot(vct_ref[0, 0].astype(BF16), p.astype(BF16))
    pg = p[:, 0:ATTN_Q]
    for g in range(1, GROUP):
        pg = pg + p[:, g * ATTN_Q:(g + 1) * ATTN_Q]
    ovt = ovt_ref[...]
    p0, p1, p2 = _split3(pg)
    bias = _select_blocks(_dot(ovt, p0) + _dot(ovt, p1) + _dot(ovt, p2), q0).astype(BF16)

    w_start = pl.multiple_of(jnp.maximum(q0 - WINDOW, 0), ATTN_Q)
    sw = _dot(kw_ref[0, 0, pl.ds(w_start, span), :], q_all)
    sw = sw + jnp.concatenate([wb_ref[0]] * GROUP, axis=1)
    ew = jnp.exp2(sw - jnp.max(sw, axis=0, keepdims=True))
    vw = jnp.concatenate([vwt_ref[0, :, pl.ds(w_start, span)], jnp.ones((ONES_ROWS, span), BF16)], axis=0)
    ow = _dot(vw, ew.astype(BF16))
    ow_scr[...] = ow[0:HEAD_DIM] / ow[HEAD_DIM:HEAD_DIM + 1]

    qx = jnp.concatenate([jnp.concatenate([qt[g], bias], axis=0) for g in range(GROUP)], axis=1)

    def scores(j, slot):
        start = pl.multiple_of(j * tk, tk)
        kx = jnp.concatenate([ks_ref[0, 0, pl.ds(start, tk), :], e_ref[pl.ds(start, tk), :]], axis=1)
        s_scr[slot] = _dot(kx, qx)

    def update(j, slot, causal):
        start = pl.multiple_of(j * tk, tk)
        vt = jnp.concatenate([vst_ref[0, :, pl.ds(start, tk)], jnp.ones((ONES_ROWS, tk), BF16)], axis=0)
        sc = s_scr[slot]
        if causal:
            key = start + lax.broadcasted_iota(jnp.int32, sc.shape, 0)
            sc = jnp.where(key <= lane_time(sc.shape), sc, -jnp.inf)
        m_old = m_scr[...]
        m_new = jnp.maximum(m_old, jnp.max(sc, axis=0, keepdims=True))
        alpha = jnp.exp2(m_old - m_new)
        pj = jnp.exp2(sc - m_new)
        m_scr[...] = m_new
        acc_scr[...] = alpha * acc_scr[...] + _dot(vt, pj.astype(BF16))

    m_scr[...] = jnp.full(m_scr.shape, -jnp.inf, F32)
    acc_scr[...] = jnp.zeros(acc_scr.shape, F32)
    scores(0, 0)
    last = q0 // tk

    @pl.loop(0, last // 2)
    def _(i):
        scores(2 * i + 1, 1)
        update(2 * i, 0, False)
        scores(2 * i + 2, 0)
        update(2 * i + 1, 1, False)

    @pl.when(last % 2 == 1)
    def _():
        scores(last, 1)
        update(last - 1, 0, False)
        update(last, 1, True)

    @pl.when(last % 2 == 0)
    def _():
        update(last, 0, True)

    o_sel = acc_scr[0:HEAD_DIM, :] / acc_scr[HEAD_DIM:HEAD_DIM + 1, :]
    gates = jax.nn.sigmoid(zgt_ref[0])
    hk = pl.program_id(1)

    def gate(g, branch):
        row = gates[g * N_BRANCH + branch:g * N_BRANCH + branch + 1]
        for other in range(1, N_KV_HEADS):
            r = (other * GROUP + g) * N_BRANCH + branch
            row = jnp.where(hk == other, gates[r:r + 1], row)
        return row

    for g in range(GROUP):
        sl = slice(g * ATTN_Q, (g + 1) * ATTN_Q)
        y = (gate(g, 0) * oc_scr[:, sl] + gate(g, 1) * o_sel[:, sl]
             + gate(g, 2) * ow_scr[:, sl])
        y_ref[0, :, g * HEAD_DIM:(g + 1) * HEAD_DIM] = y.T


def _attention(qt, zgt, kvc, kvct, ovt, kk, vt, onehot, B, seq):
    n_cmp = kvc.shape[2]
    n_sel = ovt.shape[0]
    GW = GROUP * HEAD_DIM
    GQ = GROUP * ATTN_Q
    VR = HEAD_DIM + ONES_ROWS
    tk = min(512, seq)
    span = WINDOW + ATTN_Q
    assert seq >= span
    body = functools.partial(_attn_body, tk=tk, span=span)
    resident = 4 * seq * HEAD_DIM * 2 + seq * n_sel * 2
    vmem = 2 * (GW * ATTN_Q * 2 + N_GATES * ATTN_Q * 4 + 2 * n_cmp * HEAD_DIM * 4 + n_sel * n_cmp * 2
                + resident + ATTN_Q * GW * 4) \
        + (2 * tk + VR + 2 * HEAD_DIM + SUBLANES) * GQ * 4 + 8 * (n_cmp + span) * GQ * 4
    return pl.pallas_call(
        body,
        grid=(B, N_KV_HEADS, seq // ATTN_Q),
        in_specs=[
            pl.BlockSpec((1, GW, ATTN_Q), lambda b, h, i: (b, h, i)),
            pl.BlockSpec((1, N_GATES, ATTN_Q), lambda b, h, i: (b, 0, i)),
            pl.BlockSpec((1, 1, n_cmp, HEAD_DIM), lambda b, h, i: (0, b * N_KV_HEADS + h, 0, 0)),
            pl.BlockSpec((1, 1, HEAD_DIM, n_cmp), lambda b, h, i: (1, b * N_KV_HEADS + h, 0, 0)),
            pl.BlockSpec((n_sel, n_cmp), lambda b, h, i: (0, 0)),
            pl.BlockSpec((1, 1, seq, HEAD_DIM), lambda b, h, i: (b, N_KV_HEADS + h, 0, 0)),
            pl.BlockSpec((1, HEAD_DIM, seq), lambda b, h, i: (b, N_KV_HEADS + h, 0)),
            pl.BlockSpec((seq, n_sel), lambda b, h, i: (0, 0)),
            pl.BlockSpec((1, 1, seq, HEAD_DIM), lambda b, h, i: (b, 2 * N_KV_HEADS + h, 0, 0)),
            pl.BlockSpec((1, HEAD_DIM, seq), lambda b, h, i: (b, 2 * N_KV_HEADS + h, 0)),
            pl.BlockSpec((1, span, ATTN_Q), lambda b, h, i: (jnp.minimum(i, WINDOW // ATTN_Q), 0, 0)),
        ],
        out_specs=pl.BlockSpec((1, ATTN_Q, GW), lambda b, h, i: (b, i, h)),
        out_shape=jax.ShapeDtypeStruct((B, seq, ATTN_W), F32),
        scratch_shapes=[pltpu.VMEM((2, tk, GQ), F32),
                        pltpu.VMEM((1, GQ), F32),
                        pltpu.VMEM((VR, GQ), F32),
                        pltpu.VMEM((HEAD_DIM, GQ), F32),
                        pltpu.VMEM((HEAD_DIM, GQ), F32)],
        compiler_params=_params(("parallel", "parallel", "arbitrary"), vmem),
        name="attention",
    )(qt, zgt, kvc, kvct, ovt, kk, vt, onehot, kk, vt, _window_bias(span))


def _outproj_body(yc_ref, ya_ref, gc_ref, ga_ref, w_ref, x_ref, h_ref, a_scr):
    a_scr[:, 0:CONV_W] = _rms(yc_ref[...], gc_ref[...]).astype(BF16)
    a_scr[:, CONV_W:] = _rms(ya_ref[...], ga_ref[...]).astype(BF16)
    for c in range(0, h_ref.shape[1], PROJ_TILE_N):
        cols = slice(c, c + PROJ_TILE_N)
        h_ref[:, cols] = x_ref[:, cols] + _dot(a_scr[...], w_ref[:, cols])


def _out_projection(yc, ya, g_conv, g_attn, w, x2):
    T, D = x2.shape
    K = w.shape[0]
    tm = min(512, T)
    assert D % PROJ_TILE_N == 0
    vmem = 2 * (tm * CONV_W * 4 + tm * ATTN_W * 4 + 2 * tm * D * 4) + K * D * 2 + tm * K * 2 \
        + 2 * tm * ATTN_W * 4 + 2 * tm * PROJ_TILE_N * 4
    return pl.pallas_call(
        _outproj_body,
        grid=(T // tm,),
        in_specs=[
            pl.BlockSpec((tm, CONV_W), lambda i: (i, 0)),
            pl.BlockSpec((tm, ATTN_W), lambda i: (i, 0)),
            pl.BlockSpec((1, CONV_W), lambda i: (0, 0)),
            pl.BlockSpec((1, ATTN_W), lambda i: (0, 0)),
            pl.BlockSpec((K, D), lambda i: (0, 0), pipeline_mode=pl.Buffered(1)),
            pl.BlockSpec((tm, D), lambda i: (i, 0)),
        ],
        out_specs=pl.BlockSpec((tm, D), lambda i: (i, 0)),
        out_shape=jax.ShapeDtypeStruct((T, D), F32),
        scratch_shapes=[pltpu.VMEM((tm, K), BF16)],
        compiler_params=_params(("parallel",), vmem),
        name="out_projection",
    )(yc, ya, g_conv, g_attn, w, x2)


def _ffn_body(h_ref, g_ref, wg_ref, wu_ref, wd_ref, gf_ref, o_ref, a_scr, *, final_norm, tn):
    j = pl.program_id(1)

    @pl.when(j == 0)
    def _():
        @pl.loop(0, h_ref.shape[0] // FFN_EPILOGUE_ROWS)
        def _(r):
            rows = pl.ds(pl.multiple_of(r * FFN_EPILOGUE_ROWS, FFN_EPILOGUE_ROWS), FFN_EPILOGUE_ROWS)
            a_scr[rows, :] = _rms(h_ref[rows, :], g_ref[...]).astype(BF16)

    a = a_scr[...]
    u = jnp.concatenate(
        [(jax.nn.silu(_dot(a, wg_ref[:, c:c + PROJ_CHUNK])) * _dot(a, wu_ref[:, c:c + PROJ_CHUNK])).astype(BF16)
         for c in range(0, wg_ref.shape[1], PROJ_CHUNK)], axis=1)
    D = o_ref.shape[1]

    @pl.when(j == 0)
    def _():
        for n in range(D // tn):
            o_ref[:, n * tn:(n + 1) * tn] = _dot(u, wd_ref[:, n * tn:(n + 1) * tn])

    @pl.when(j > 0)
    def _():
        for n in range(D // tn):
            o_ref[:, n * tn:(n + 1) * tn] += _dot(u, wd_ref[:, n * tn:(n + 1) * tn])

    @pl.when(j == pl.num_programs(1) - 1)
    def _():
        @pl.loop(0, o_ref.shape[0] // FFN_EPILOGUE_ROWS)
        def _(r):
            rows = pl.ds(pl.multiple_of(r * FFN_EPILOGUE_ROWS, FFN_EPILOGUE_ROWS), FFN_EPILOGUE_ROWS)
            out = h_ref[rows, :] + o_ref[rows, :]
            o_ref[rows, :] = _rms(out, gf_ref[...]) if final_norm else out


def _ffn(h, g, wg, wu, wd, g_final, final_norm):
    T, D = h.shape
    F = wg.shape[1]
    tm = min(1024, T)
    tf = 512
    tn = PROJ_CHUNK
    assert F % tf == 0 and D % tn == 0
    body = functools.partial(_ffn_body, final_norm=final_norm, tn=tn)
    vmem = 2 * (3 * D * tf * 2 + 2 * tm * D * 4) + tm * D * 2 + 6 * tm * tf * 4 + 2 * tm * tn * 4
    return pl.pallas_call(
        body,
        grid=(T // tm, F // tf),
        in_specs=[
            pl.BlockSpec((tm, D), lambda i, j: (i, 0)),
            pl.BlockSpec((1, D), lambda i, j: (0, 0)),
            pl.BlockSpec((D, tf), lambda i, j: (0, j)),
            pl.BlockSpec((D, tf), lambda i, j: (0, j)),
            pl.BlockSpec((tf, D), lambda i, j: (j, 0)),
            pl.BlockSpec((1, D), lambda i, j: (0, 0)),
        ],
        out_specs=pl.BlockSpec((tm, D), lambda i, j: (i, 0)),
        out_shape=jax.ShapeDtypeStruct((T, D), F32),
        scratch_shapes=[pltpu.VMEM((tm, D), BF16)],
        compiler_params=_params(("parallel", "arbitrary"), vmem),
        name="ffn",
    )(h, g, wg, wu, wd, g_final)


def _rope_tables(seq):
    half = HEAD_DIM // 2
    inv = 1.0 / (ROPE_THETA ** (jnp.arange(half, dtype=F32) / half))
    inv = jnp.concatenate([inv, inv])
    sign = jnp.concatenate([-jnp.ones((half,), F32), jnp.ones((half,), F32)])
    lo = jnp.arange(ROPE_SPLIT).astype(F32)[:, None] * inv[None, :]
    hi = (jnp.arange(seq // ROPE_SPLIT) * ROPE_SPLIT).astype(F32)[:, None] * inv[None, :]
    cos_lo, sin_lo = jnp.cos(lo)[None], jnp.sin(lo)[None]
    cos_hi, sin_hi = jnp.cos(hi)[:, None, :], jnp.sin(hi)[:, None, :]
    cos = (cos_hi * cos_lo - sin_hi * sin_lo).reshape(seq, HEAD_DIM)
    sin = ((sin_hi * cos_lo + cos_hi * sin_lo) * sign).reshape(seq, HEAD_DIM)
    return cos, sin


def _overlap_t(n_sel, n_cmp):
    cs = np.arange(n_cmp)[None, :] * CMP_STRIDE
    ss = np.arange(n_sel)[:, None] * SEL_BLOCK
    return jnp.asarray(((cs < ss + SEL_BLOCK) & (cs + CMP_BLOCK > ss)).astype(np.float32), dtype=BF16)


def _window_bias(span):
    r = np.arange(span)[None, :, None]
    c = np.arange(ATTN_Q)[None, None, :]
    off = np.minimum(np.arange(WINDOW // ATTN_Q + 1) * ATTN_Q, WINDOW)[:, None, None]
    valid = (r <= off + c) & (r > off + c - WINDOW)
    return jnp.asarray(np.where(valid, 0.0, -np.inf).astype(np.float32))


def _block_onehot(seq, n_sel):
    return jnp.asarray((np.arange(seq)[:, None] // SEL_BLOCK == np.arange(n_sel)[None, :])
                       .astype(np.float32), dtype=BF16)


def kernel(x, norm_mix, w_in, conv_w, cmp_pe_k, cmp_w1_k, cmp_w2_k, cmp_pe_v, cmp_w1_v, cmp_w2_v,
           norm_conv_out, norm_attn_out, w_out, norm_ffn, w_gate, w_up, w_down, norm_final):
    B, S, D = x.shape
    T = B * S
    depth = norm_mix.shape[0]
    assert S % ATTN_Q == 0 and S % CMP_STRIDE == 0 and S % ROPE_SPLIT == 0
    n_chunks = S // CMP_STRIDE
    n_sel = S // SEL_BLOCK
    cos_t, sin_t = _rope_tables(S)
    ovt = _overlap_t(n_sel, n_chunks)
    onehot = _block_onehot(S, n_sel)

    h = x.reshape(T, D)
    for l in range(depth):
        y_conv, qt, kk, kvc_in, vt, zgt, w_out_b, w_gate_b, w_up_b, w_down_b = _in_projection(
            h, norm_mix[l][None], w_in[l].astype(BF16), _gate_weight(w_in[l]), cos_t, sin_t, conv_w[l],
            [w_out[l], w_gate[l], w_up[l], w_down[l]], B, S)
        kvc_in = kvc_in.reshape(2, B * N_KV_HEADS, S, HEAD_DIM)
        pe = jnp.stack([cmp_pe_k[l], cmp_pe_v[l]]).reshape(2, 1, CMP_BLOCK * HEAD_DIM)
        kvc, kvct = _compress(kvc_in, pe, jnp.stack([cmp_w1_k[l], cmp_w1_v[l]]),
                              jnp.stack([cmp_w2_k[l], cmp_w2_v[l]]))

        y_attn = _attention(qt, zgt, kvc, kvct, ovt, kk, vt, onehot, B, S)

        h = _out_projection(y_conv, y_attn.reshape(T, ATTN_W), norm_conv_out[l][None],
                            norm_attn_out[l][None], w_out_b, h)

        h = _ffn(h, norm_ffn[l][None], w_gate_b, w_up_b, w_down_b, norm_final[None], l == depth - 1)
    return h.reshape(B, S, D)
```

```python
import functools

import jax
import jax.numpy as jnp
import numpy as np
from jax import lax
from jax.experimental import pallas as pl
from jax.experimental.pallas import tpu as pltpu

F32 = jnp.float32
BF16 = jnp.bfloat16

HEAD_DIM = 128
N_HEADS = 8
N_KV_HEADS = 2
GROUP = N_HEADS // N_KV_HEADS
N_BRANCH = 3
CONV_TAPS = 3
CMP_BLOCK = 32
CMP_STRIDE = 16
SEL_BLOCK = 64
N_SELECT = 16
N_FORCED = 3
WINDOW = 512
ATTN_Q = 256
ROPE_THETA = 10000.0
ROPE_SPLIT = 128
RMS_EPS = 1e-6

CONV_W = 1024
ATTN_W = N_HEADS * HEAD_DIM
KV_W = N_KV_HEADS * HEAD_DIM
N_GATES = N_HEADS * N_BRANCH

V7X_VMEM_BYTES = 64 * 1024 * 1024
V7X_VMEM_BUDGET = V7X_VMEM_BYTES - 6 * 1024 * 1024
LANES = 128
SUBLANES = 8
PROJ_TILE_N = 512
PROJ_CHUNK = 256
FFN_EPILOGUE_ROWS = 256

MASK_BIAS = -float(2 ** 30)
LOG2_E = 1.4426950408889634
ONES_ROWS = 16


def _params(semantics, vmem_bytes):
    return pltpu.CompilerParams(dimension_semantics=semantics,
                                vmem_limit_bytes=int(min(vmem_bytes, V7X_VMEM_BUDGET)))


def _rms(x, g):
    return x * lax.rsqrt(jnp.mean(x * x, axis=-1, keepdims=True) + RMS_EPS) * g


def _split3(x):
    hi = x.astype(BF16)
    r1 = x - hi.astype(F32)
    mid = r1.astype(BF16)
    lo = (r1 - mid.astype(F32)).astype(BF16)
    return hi, mid, lo


def _dot(a, b):
    return jnp.dot(a, b, preferred_element_type=F32)


def _dot_f32(a, b):
    a0, a1, a2 = _split3(a)
    b0, b1, b2 = _split3(b)
    return (_dot(a0, b0) + (_dot(a0, b1) + _dot(a1, b0))
            + (_dot(a0, b2) + _dot(a1, b1) + _dot(a2, b0)))


CONV_GROUPS = CONV_W // PROJ_TILE_N
N_Q_TILES = ATTN_W // PROJ_TILE_N
N_KV_TILES = 3
GATE_STEP = 0
Q_STEP0 = GATE_STEP + 1
KV_STEP0 = Q_STEP0 + N_Q_TILES
CONV_STEP0 = KV_STEP0 + N_KV_TILES
N_PROJ_STEPS = CONV_STEP0 + CONV_GROUPS
N_WEIGHT_TILES = 3 * CONV_GROUPS + N_Q_TILES + N_KV_TILES
KV_OUT_W = N_KV_TILES * KV_W
assert 2 * KV_W == PROJ_TILE_N and N_GATES % SUBLANES == 0


def _gate_weight(w):
    gates = w[:, N_WEIGHT_TILES * PROJ_TILE_N:].astype(BF16)
    return jnp.pad(gates, ((0, 0), (0, LANES - gates.shape[1])))


def _inproj_body(*refs, q_scale, blocks_per_seq, side_blocks):
    n_side = len(side_blocks)
    x_ref, g_ref, w_ref, wc_ref, wh_ref, wg_ref, cos_ref, sin_ref, cw_ref = refs[:9]
    side_in = refs[9:9 + n_side]
    yc_ref, qt_ref, k_ref, kvc_ref, vt_ref, zgt_ref = refs[9 + n_side:15 + n_side]
    side_out = refs[15 + n_side:15 + 2 * n_side]
    a_scr, tail_scr, zg_scr = refs[15 + 2 * n_side:]
    p = pl.program_id(0)
    j = pl.program_id(1)
    i = p - 1
    live = p >= 1

    step = p * N_PROJ_STEPS + j
    for src, dst, n_blocks in zip(side_in, side_out, side_blocks):
        @pl.when(step < n_blocks)
        def _(src=src, dst=dst):
            dst[...] = src[...].astype(BF16)

    @pl.when(live & (j == 0))
    def _():
        a_scr[0] = a_scr[1]

    @pl.when(p < pl.num_programs(0) - 1)
    def _():
        piece = x_ref.shape[0]
        rows = pl.ds(pl.multiple_of(j * piece, piece), piece)
        a_scr[1, rows, :] = _rms(x_ref[...], g_ref[...]).astype(BF16)

    @pl.when((p == 0) & (j == 0))
    def _():
        tail_scr[...] = jnp.zeros(tail_scr.shape, F32)

    def a_block():
        return a_scr[0]

    @pl.when(live & (j == GATE_STEP))
    def _():
        zg_scr[...] = _dot(a_block(), wg_ref[...])
        zgt_ref[0] = zg_scr[...].T[0:N_GATES]

    chunks = [(c, c + PROJ_CHUNK) for c in range(0, PROJ_TILE_N, PROJ_CHUNK)]

    def z_cols(weight_ref, c0, c1):
        return _dot(a_block(), weight_ref[:, c0:c1])

    def rope(zh):
        cos = cos_ref[...]
        sin = sin_ref[...]
        heads = []
        for h in range(zh.shape[1] // HEAD_DIM):
            z1 = zh[:, h * HEAD_DIM:(h + 1) * HEAD_DIM]
            heads.append(z1 * cos + pltpu.roll(z1, HEAD_DIM // 2, axis=1) * sin)
        return jnp.concatenate(heads, axis=1)

    @pl.when(live & (j >= CONV_STEP0))
    def _():
        group = j - CONV_STEP0
        for c0, c1 in chunks:
            u = z_cols(wc_ref, c0, c1) * z_cols(wh_ref, c0, c1)
            tail = tail_scr[group, :, c0:c1]
            tail = jnp.where(i % blocks_per_seq == 0, 0.0, tail)
            row = lax.broadcasted_iota(jnp.int32, u.shape, 0)
            t1 = tail[SUBLANES - 1:SUBLANES]
            t2 = tail[SUBLANES - 2:SUBLANES - 1]
            u1 = jnp.where(row == 0, t1, pltpu.roll(u, 1, axis=0))
            u2 = jnp.where(row == 0, t2, jnp.where(row == 1, t1, pltpu.roll(u, 2, axis=0)))
            w = cw_ref[:, c0:c1]
            yc_ref[:, c0:c1] = z_cols(w_ref, c0, c1) * (u2 * w[0:1] + u1 * w[1:2] + u * w[2:3])
            tail_scr[group, :, c0:c1] = u[u.shape[0] - SUBLANES:, :]

    @pl.when(live & (j >= Q_STEP0) & (j < KV_STEP0))
    def _():
        for c0, c1 in chunks:
            qt_ref[0, c0:c1, :] = (rope(z_cols(w_ref, c0, c1)) * q_scale).T.astype(BF16)

    @pl.when(live & (j >= KV_STEP0) & (j < CONV_STEP0))
    def _():
        kx = rope(z_cols(w_ref, 0, KV_W)).astype(BF16)
        for h in range(N_KV_HEADS):
            k_ref[0, h] = kx[:, h * HEAD_DIM:(h + 1) * HEAD_DIM]
        zv = z_cols(w_ref, KV_W, 2 * KV_W)
        vt_ref[0] = zv.T.astype(BF16)

        @pl.when(j == KV_STEP0)
        def _():
            for h in range(N_KV_HEADS):
                kvc_ref[0, 0, h] = kx[:, h * HEAD_DIM:(h + 1) * HEAD_DIM].astype(F32)
                kvc_ref[1, 0, h] = zv[:, h * HEAD_DIM:(h + 1) * HEAD_DIM].astype(BF16).astype(F32)


def _side_rows(n_rows, n_steps):
    for rows in range(2 * SUBLANES, n_rows + 1, 2 * SUBLANES):
        if n_rows % rows == 0 and n_rows // rows <= n_steps:
            return rows
    raise ValueError((n_rows, n_steps))


def _in_projection(x2, g, w_proj, w_gates, cos_t, sin_t, conv_w, side, B, seq):
    T, D = x2.shape
    tn = PROJ_TILE_N
    assert w_proj.shape[1] >= N_WEIGHT_TILES * tn
    tm = min(1024, seq)
    nb = seq // tm
    x_rows = tm // N_PROJ_STEPS
    assert tm % N_PROJ_STEPS == 0 and x_rows % (2 * SUBLANES) == 0
    n_x_blocks = T // x_rows
    n_steps = (T // tm + 1) * N_PROJ_STEPS
    side_rows = [_side_rows(w.shape[0], n_steps) for w in side]
    side_blocks = [w.shape[0] // r for w, r in zip(side, side_rows)]
    side_specs = [pl.BlockSpec((r, w.shape[1]),
                               lambda i, j, n=n: (jnp.minimum(i * N_PROJ_STEPS + j, n - 1), 0))
                  for w, r, n in zip(side, side_rows, side_blocks)]
    body = functools.partial(_inproj_body, q_scale=HEAD_DIM ** -0.5 * LOG2_E, blocks_per_seq=nb,
                             side_blocks=tuple(side_blocks))
    vmem = 2 * (x_rows * D * 4 + 3 * D * tn * 2 + 2 * tm * HEAD_DIM * 4 + tm * tn * 4 + 4 * tm * tn * 2
                + N_GATES * tm * 4) + 2 * tm * D * 2 + 8 * tm * PROJ_CHUNK * 4 \
        + 2 * sum(r * w.shape[1] * 6 for w, r in zip(side, side_rows))

    def at(index):
        return lambda p, j: index(jnp.maximum(p - 1, 0), jnp.where(p == 0, 0, j))

    def conv_group(j):
        return jnp.maximum(j - CONV_STEP0, 0)

    def main_tile(j):
        return jnp.where(j >= CONV_STEP0, j - CONV_STEP0,
                         jnp.maximum(j - Q_STEP0, 0) + 3 * CONV_GROUPS)

    def kv_tile(j):
        return jnp.clip(j - KV_STEP0, 0, N_KV_TILES - 1)

    return pl.pallas_call(
        body,
        grid=(T // tm + 1, N_PROJ_STEPS),
        in_specs=[
            pl.BlockSpec((x_rows, D), lambda p, j: (jnp.minimum(p * N_PROJ_STEPS + j, n_x_blocks - 1), 0)),
            pl.BlockSpec((1, D), lambda p, j: (0, 0)),
            pl.BlockSpec((D, tn), at(lambda i, j: (0, main_tile(j)))),
            pl.BlockSpec((D, tn), at(lambda i, j: (0, CONV_GROUPS + conv_group(j)))),
            pl.BlockSpec((D, tn), at(lambda i, j: (0, 2 * CONV_GROUPS + conv_group(j)))),
            pl.BlockSpec((D, LANES), lambda p, j: (0, 0)),
            pl.BlockSpec((tm, HEAD_DIM), at(lambda i, j: (i % nb, 0))),
            pl.BlockSpec((tm, HEAD_DIM), at(lambda i, j: (i % nb, 0))),
            pl.BlockSpec((CONV_TAPS, tn), at(lambda i, j: (0, conv_group(j)))),
        ] + side_specs,
        out_specs=[
            pl.BlockSpec((tm, tn), at(lambda i, j: (i, conv_group(j)))),
            pl.BlockSpec((1, tn, tm), at(lambda i, j: (i // nb, jnp.clip(j - Q_STEP0, 0, N_Q_TILES - 1), i % nb))),
            pl.BlockSpec((1, N_KV_HEADS, tm, HEAD_DIM), at(lambda i, j: (i // nb, kv_tile(j), i % nb, 0))),
            pl.BlockSpec((2, 1, N_KV_HEADS, tm, HEAD_DIM), at(lambda i, j: (0, i // nb, 0, i % nb, 0))),
            pl.BlockSpec((1, KV_W, tm), at(lambda i, j: (i // nb, kv_tile(j), i % nb))),
            pl.BlockSpec((1, N_GATES, tm), at(lambda i, j: (i // nb, 0, i % nb))),
        ] + side_specs,
        out_shape=[
            jax.ShapeDtypeStruct((T, CONV_W), F32),
            jax.ShapeDtypeStruct((B, ATTN_W, seq), BF16),
            jax.ShapeDtypeStruct((B, N_KV_TILES * N_KV_HEADS, seq, HEAD_DIM), BF16),
            jax.ShapeDtypeStruct((2, B, N_KV_HEADS, seq, HEAD_DIM), F32),
            jax.ShapeDtypeStruct((B, KV_OUT_W, seq), BF16),
            jax.ShapeDtypeStruct((B, N_GATES, seq), F32),
        ] + [jax.ShapeDtypeStruct(w.shape, BF16) for w in side],
        scratch_shapes=[pltpu.VMEM((2, tm, D), BF16), pltpu.VMEM((CONV_GROUPS, SUBLANES, tn), F32),
                        pltpu.VMEM((tm, LANES), F32)],
        compiler_params=_params(("arbitrary", "arbitrary"), vmem),
        name="in_projection",
    )(x2, g, w_proj, w_proj, w_proj, w_gates, cos_t, sin_t, conv_w, *side)


def _compress_body(x_ref, pe_ref, w1_ref, w2_ref, o_ref, ot_ref, *, n_valid):
    n = x_ref.shape[2] // CMP_STRIDE
    x = jnp.concatenate([x_ref[0, 0, pl.ds(l, n, stride=CMP_STRIDE), :] for l in range(CMP_STRIDE)],
                        axis=1).astype(BF16)
    half = x.shape[1]
    w1 = w1_ref[0]
    w1_hi = w1.astype(BF16)
    w1_lo = (w1 - w1_hi.astype(F32)).astype(BF16)
    first = _dot(x, w1_hi[:half]) + _dot(x, w1_lo[:half])
    second = _dot(x, w1_hi[half:]) + _dot(x, w1_lo[half:])
    second = pltpu.roll(second, n - 1, axis=0)
    pe = jnp.broadcast_to(pe_ref[0], (SUBLANES, 2 * half))
    pe_term = _dot_f32(pe, w1)[0:1]
    h = first + second + pe_term
    out = _dot_f32(jax.nn.silu(h), w2_ref[0])
    row = lax.broadcasted_iota(jnp.int32, out.shape, 0)
    out = jnp.where(row < n_valid, out, 0.0)
    o_ref[0, 0] = out
    ot_ref[0, 0] = out.T


def _compress(kv, pe, w1, w2):
    two, BH, seq, d = kv.shape
    n = seq // CMP_STRIDE
    width = CMP_STRIDE * d
    body = functools.partial(_compress_body, n_valid=n - 1)
    vmem = 2 * (seq * d * 4 + 2 * width * 4 + 2 * width * d * 4 + d * d * 4 + n * d * 4) \
        + 8 * width * d * 4
    return pl.pallas_call(
        body,
        grid=(two, BH),
        in_specs=[
            pl.BlockSpec((1, 1, seq, d), lambda s, b: (s, b, 0, 0)),
            pl.BlockSpec((1, 1, 2 * width), lambda s, b: (s, 0, 0)),
            pl.BlockSpec((1, 2 * width, d), lambda s, b: (s, 0, 0)),
            pl.BlockSpec((1, d, d), lambda s, b: (s, 0, 0)),
        ],
        out_specs=[pl.BlockSpec((1, 1, n, d), lambda s, b: (s, b, 0, 0)),
                   pl.BlockSpec((1, 1, d, n), lambda s, b: (s, b, 0, 0))],
        out_shape=[jax.ShapeDtypeStruct((two, BH, n, d), F32),
                   jax.ShapeDtypeStruct((two, BH, d, n), F32)],
        compiler_params=_params(("parallel", "parallel"), vmem),
        name="compress",
    )(kv, pe, w1, w2)


def _select_blocks(imp, q0):
    n_sel = imp.shape[0]
    sel_i = lax.broadcasted_iota(jnp.int32, imp.shape, 0)
    sel_f = sel_i.astype(F32)
    cur = (q0 + lax.broadcasted_iota(jnp.int32, imp.shape, 1)) // SEL_BLOCK
    forced = (sel_i == 0) | (sel_i == cur) | (sel_i == cur - 1)
    v = jnp.where((sel_i > cur) | forced, -jnp.inf, imp)
    chosen = jnp.where(forced, 1.0, 0.0)
    for _ in range(min(N_SELECT, n_sel) - N_FORCED):
        top = jnp.max(v, axis=0, keepdims=True)
        first = jnp.min(jnp.where(v == top, sel_f, float(n_sel)), axis=0, keepdims=True)
        pick = sel_f == first
        chosen = jnp.where(pick, 1.0, chosen)
        v = jnp.where(pick, -jnp.inf, v)
    return jnp.where((chosen > 0.0) & (sel_i <= cur), 0.0, MASK_BIAS)


def _attn_body(qt_ref, zgt_ref, kc_ref, vct_ref, ovt_ref, ks_ref, vst_ref, e_ref, kw_ref, vwt_ref, wb_ref,
               y_ref, s_scr, m_scr, acc_scr, oc_scr, ow_scr, *, tk, span):
    q0 = pl.program_id(2) * ATTN_Q
    qt = [qt_ref[0, g * HEAD_DIM:(g + 1) * HEAD_DIM, :] for g in range(GROUP)]
    q_all = jnp.concatenate(qt, axis=1)

    def lane_time(shape):
        return q0 + (lax.broadcasted_iota(jnp.int32, shape, 1) & (ATTN_Q - 1))

    s = _dot(kc_ref[0, 0].astype(BF16), q_all)
    cmp_end = lax.broadcasted_iota(jnp.int32, s.shape, 0) * CMP_STRIDE + (CMP_BLOCK - 1)
    s = jnp.where(cmp_end <= lane_time(s.shape), s, -jnp.inf)
    m = jnp.max(s, axis=0, keepdims=True)
    m = jnp.where(m == -jnp.inf, 0.0, m)
    e = jnp.exp2(s - m)
    p = e / jnp.maximum(jnp.sum(e, axis=0, keepdims=True), 1e-30)
    oc_scr[...] = _dot(vct_ref[0, 0].astype(BF16), p.astype(BF16))
    pg = p[:, 0:ATTN_Q]
    for g in range(1, GROUP):
        pg = pg + p[:, g * ATTN_Q:(g + 1) * ATTN_Q]
    ovt = ovt_ref[...]
    p0, p1, p2 = _split3(pg)
    bias = _select_blocks(_dot(ovt, p0) + _dot(ovt, p1) + _dot(ovt, p2), q0).astype(BF16)

    w_start = pl.multiple_of(jnp.maximum(q0 - WINDOW, 0), ATTN_Q)
    sw = _dot(kw_ref[0, 0, pl.ds(w_start, span), :], q_all)
    sw = sw + jnp.concatenate([wb_ref[0]] * GROUP, axis=1)
    ew = jnp.exp2(sw - jnp.max(sw, axis=0, keepdims=True))
    vw = jnp.concatenate([vwt_ref[0, :, pl.ds(w_start, span)], jnp.ones((ONES_ROWS, span), BF16)], axis=0)
    ow = _dot(vw, ew.astype(BF16))
    ow_scr[...] = ow[0:HEAD_DIM] / ow[HEAD_DIM:HEAD_DIM + 1]

    qx = jnp.concatenate([jnp.concatenate([qt[g], bias], axis=0) for g in range(GROUP)], axis=1)

    def scores(j, slot):
        start = pl.multiple_of(j * tk, tk)
        kx = jnp.concatenate([ks_ref[0, 0, pl.ds(start, tk), :], e_ref[pl.ds(start, tk), :]], axis=1)
        s_scr[slot] = _dot(kx, qx)

    def update(j, slot, causal):
        start = pl.multiple_of(j * tk, tk)
        vt = jnp.concatenate([vst_ref[0, :, pl.ds(start, tk)], jnp.ones((ONES_ROWS, tk), BF16)], axis=0)
        sc = s_scr[slot]
        if causal:
            key = start + lax.broadcasted_iota(jnp.int32, sc.shape, 0)
            sc = jnp.where(key <= lane_time(sc.shape), sc, -jnp.inf)
        m_old = m_scr[...]
        m_new = jnp.maximum(m_old, jnp.max(sc, axis=0, keepdims=True))
        alpha = jnp.exp2(m_old - m_new)
        pj = jnp.exp2(sc - m_new)
        m_scr[...] = m_new
        acc_scr[...] = alpha * acc_scr[...] + _dot(vt, pj.astype(BF16))

    m_scr[...] = jnp.full(m_scr.shape, -jnp.inf, F32)
    acc_scr[...] = jnp.zeros(acc_scr.shape, F32)
    scores(0, 0)
    last = q0 // tk

    @pl.loop(0, last // 2)
    def _(i):
        scores(2 * i + 1, 1)
        update(2 * i, 0, False)
        scores(2 * i + 2, 0)
        update(2 * i + 1, 1, False)

    @pl.when(last % 2 == 1)
    def _():
        scores(last, 1)
        update(last - 1, 0, False)
        update(last, 1, True)

    @pl.when(last % 2 == 0)
    def _():
        update(last, 0, True)

    o_sel = acc_scr[0:HEAD_DIM, :] / acc_scr[HEAD_DIM:HEAD_DIM + 1, :]
    gates = jax.nn.sigmoid(zgt_ref[0])
    hk = pl.program_id(1)

    def gate(g, branch):
        row = gates[g * N_BRANCH + branch:g * N_BRANCH + branch + 1]
        for other in range(1, N_KV_HEADS):
            r = (other * GROUP + g) * N_BRANCH + branch
            row = jnp.where(hk == other, gates[r:r + 1], row)
        return row

    for g in range(GROUP):
        sl = slice(g * ATTN_Q, (g + 1) * ATTN_Q)
        y = (gate(g, 0) * oc_scr[:, sl] + gate(g, 1) * o_sel[:, sl]
             + gate(g, 2) * ow_scr[:, sl])
        y_ref[0, :, g * HEAD_DIM:(g + 1) * HEAD_DIM] = y.T


def _attention(qt, zgt, kvc, kvct, ovt, kk, vt, onehot, B, seq):
    n_cmp = kvc.shape[2]
    n_sel = ovt.shape[0]
    GW = GROUP * HEAD_DIM
    GQ = GROUP * ATTN_Q
    VR = HEAD_DIM + ONES_ROWS
    tk = min(512, seq)
    span = WINDOW + ATTN_Q
    assert seq >= span
    body = functools.partial(_attn_body, tk=tk, span=span)
    resident = 4 * seq * HEAD_DIM * 2 + seq * n_sel * 2
    vmem = 2 * (GW * ATTN_Q * 2 + N_GATES * ATTN_Q * 4 + 2 * n_cmp * HEAD_DIM * 4 + n_sel * n_cmp * 2
                + resident + ATTN_Q * GW * 4) \
        + (2 * tk + VR + 2 * HEAD_DIM + SUBLANES) * GQ * 4 + 8 * (n_cmp + span) * GQ * 4
    return pl.pallas_call(
        body,
        grid=(B, N_KV_HEADS, seq // ATTN_Q),
        in_specs=[
            pl.BlockSpec((1, GW, ATTN_Q), lambda b, h, i: (b, h, i)),
            pl.BlockSpec((1, N_GATES, ATTN_Q), lambda b, h, i: (b, 0, i)),
            pl.BlockSpec((1, 1, n_cmp, HEAD_DIM), lambda b, h, i: (0, b * N_KV_HEADS + h, 0, 0)),
            pl.BlockSpec((1, 1, HEAD_DIM, n_cmp), lambda b, h, i: (1, b * N_KV_HEADS + h, 0, 0)),
            pl.BlockSpec((n_sel, n_cmp), lambda b, h, i: (0, 0)),
            pl.BlockSpec((1, 1, seq, HEAD_DIM), lambda b, h, i: (b, N_KV_HEADS + h, 0, 0)),
            pl.BlockSpec((1, HEAD_DIM, seq), lambda b, h, i: (b, N_KV_HEADS + h, 0)),
            pl.BlockSpec((seq, n_sel), lambda b, h, i: (0, 0)),
            pl.BlockSpec((1, 1, seq, HEAD_DIM), lambda b, h, i: (b, 2 * N_KV_HEADS + h, 0, 0)),
            pl.BlockSpec((1, HEAD_DIM, seq), lambda b, h, i: (b, 2 * N_KV_HEADS + h, 0)),
            pl.BlockSpec((1, span, ATTN_Q), lambda b, h, i: (jnp.minimum(i, WINDOW // ATTN_Q), 0, 0)),
        ],
        out_specs=pl.BlockSpec((1, ATTN_Q, GW), lambda b, h, i: (b, i, h)),
        out_shape=jax.ShapeDtypeStruct((B, seq, ATTN_W), F32),
        scratch_shapes=[pltpu.VMEM((2, tk, GQ), F32),
                        pltpu.VMEM((1, GQ), F32),
                        pltpu.VMEM((VR, GQ), F32),
                        pltpu.VMEM((HEAD_DIM, GQ), F32),
                        pltpu.VMEM((HEAD_DIM, GQ), F32)],
        compiler_params=_params(("parallel", "parallel", "arbitrary"), vmem),
        name="attention",
    )(qt, zgt, kvc, kvct, ovt, kk, vt, onehot, kk, vt, _window_bias(span))


def _outproj_body(yc_ref, ya_ref, gc_ref, ga_ref, w_ref, x_ref, h_ref, a_scr):
    a_scr[:, 0:CONV_W] = _rms(yc_ref[...], gc_ref[...]).astype(BF16)
    a_scr[:, CONV_W:] = _rms(ya_ref[...], ga_ref[...]).astype(BF16)
    for c in range(0, h_ref.shape[1], PROJ_TILE_N):
        cols = slice(c, c + PROJ_TILE_N)
        h_ref[:, cols] = x_ref[:, cols] + _dot(a_scr[...], w_ref[:, cols])


def _out_projection(yc, ya, g_conv, g_attn, w, x2):
    T, D = x2.shape
    K = w.shape[0]
    tm = min(512, T)
    assert D % PROJ_TILE_N == 0
    vmem = 2 * (tm * CONV_W * 4 + tm * ATTN_W * 4 + 2 * tm * D * 4) + K * D * 2 + tm * K * 2 \
        + 2 * tm * ATTN_W * 4 + 2 * tm * PROJ_TILE_N * 4
    return pl.pallas_call(
        _outproj_body,
        grid=(T // tm,),
        in_specs=[
            pl.BlockSpec((tm, CONV_W), lambda i: (i, 0)),
            pl.BlockSpec((tm, ATTN_W), lambda i: (i, 0)),
            pl.BlockSpec((1, CONV_W), lambda i: (0, 0)),
            pl.BlockSpec((1, ATTN_W), lambda i: (0, 0)),
            pl.BlockSpec((K, D), lambda i: (0, 0), pipeline_mode=pl.Buffered(1)),
            pl.BlockSpec((tm, D), lambda i: (i, 0)),
        ],
        out_specs=pl.BlockSpec((tm, D), lambda i: (i, 0)),
        out_shape=jax.ShapeDtypeStruct((T, D), F32),
        scratch_shapes=[pltpu.VMEM((tm, K), BF16)],
        compiler_params=_params(("parallel",), vmem),
        name="out_projection",
    )(yc, ya, g_conv, g_attn, w, x2)


def _ffn_body(h_ref, g_ref, wg_ref, wu_ref, wd_ref, gf_ref, o_ref, a_scr, *, final_norm, tn):
    j = pl.program_id(1)

    @pl.when(j == 0)
    def _():
        @pl.loop(0, h_ref.shape[0] // FFN_EPILOGUE_ROWS)
        def _(r):
            rows = pl.ds(pl.multiple_of(r * FFN_EPILOGUE_ROWS, FFN_EPILOGUE_ROWS), FFN_EPILOGUE_ROWS)
            a_scr[rows, :] = _rms(h_ref[rows, :], g_ref[...]).astype(BF16)

    a = a_scr[...]
    u = jnp.concatenate(
        [(jax.nn.silu(_dot(a, wg_ref[:, c:c + PROJ_CHUNK])) * _dot(a, wu_ref[:, c:c + PROJ_CHUNK])).astype(BF16)
         for c in range(0, wg_ref.shape[1], PROJ_CHUNK)], axis=1)
    D = o_ref.shape[1]

    @pl.when(j == 0)
    def _():
        for n in range(D // tn):
            o_ref[:, n * tn:(n + 1) * tn] = _dot(u, wd_ref[:, n * tn:(n + 1) * tn])

    @pl.when(j > 0)
    def _():
        for n in range(D // tn):
            o_ref[:, n * tn:(n + 1) * tn] += _dot(u, wd_ref[:, n * tn:(n + 1) * tn])

    @pl.when(j == pl.num_programs(1) - 1)
    def _():
        @pl.loop(0, o_ref.shape[0] // FFN_EPILOGUE_ROWS)
        def _(r):
            rows = pl.ds(pl.multiple_of(r * FFN_EPILOGUE_ROWS, FFN_EPILOGUE_ROWS), FFN_EPILOGUE_ROWS)
            out = h_ref[rows, :] + o_ref[rows, :]
            o_ref[rows, :] = _rms(out, gf_ref[...]) if final_norm else out


def _ffn(h, g, wg, wu, wd, g_final, final_norm):
    T, D = h.shape
    F = wg.shape[1]
    tm = min(1024, T)
    tf = 512
    tn = PROJ_CHUNK
    assert F % tf == 0 and D % tn == 0
    body = functools.partial(_ffn_body, final_norm=final_norm, tn=tn)
    vmem = 2 * (3 * D * tf * 2 + 2 * tm * D * 4) + tm * D * 2 + 6 * tm * tf * 4 + 2 * tm * tn * 4
    return pl.pallas_call(
        body,
        grid=(T // tm, F // tf),
        in_specs=[
            pl.BlockSpec((tm, D), lambda i, j: (i, 0)),
            pl.BlockSpec((1, D), lambda i, j: (0, 0)),
            pl.BlockSpec((D, tf), lambda i, j: (0, j)),
            pl.BlockSpec((D, tf), lambda i, j: (0, j)),
            pl.BlockSpec((tf, D), lambda i, j: (j, 0)),
            pl.BlockSpec((1, D), lambda i, j: (0, 0)),
        ],
        out_specs=pl.BlockSpec((tm, D), lambda i, j: (i, 0)),
        out_shape=jax.ShapeDtypeStruct((T, D), F32),
        scratch_shapes=[pltpu.VMEM((tm, D), BF16)],
        compiler_params=_params(("parallel", "arbitrary"), vmem),
        name="ffn",
    )(h, g, wg, wu, wd, g_final)


def _rope_tables(seq):
    half = HEAD_DIM // 2
    inv = 1.0 / (ROPE_THETA ** (jnp.arange(half, dtype=F32) / half))
    inv = jnp.concatenate([inv, inv])
    sign = jnp.concatenate([-jnp.ones((half,), F32), jnp.ones((half,), F32)])
    lo = jnp.arange(ROPE_SPLIT).astype(F32)[:, None] * inv[None, :]
    hi = (jnp.arange(seq // ROPE_SPLIT) * ROPE_SPLIT).astype(F32)[:, None] * inv[None, :]
    cos_lo, sin_lo = jnp.cos(lo)[None], jnp.sin(lo)[None]
    cos_hi, sin_hi = jnp.cos(hi)[:, None, :], jnp.sin(hi)[:, None, :]
    cos = (cos_hi * cos_lo - sin_hi * sin_lo).reshape(seq, HEAD_DIM)
    sin = ((sin_hi * cos_lo + cos_hi * sin_lo) * sign).reshape(seq, HEAD_DIM)
    return cos, sin


def _overlap_t(n_sel, n_cmp):
    cs = np.arange(n_cmp)[None, :] * CMP_STRIDE
    ss = np.arange(n_sel)[:, None] * SEL_BLOCK
    return jnp.asarray(((cs < ss + SEL_BLOCK) & (cs + CMP_BLOCK > ss)).astype(np.float32), dtype=BF16)


def _window_bias(span):
    r = np.arange(span)[None, :, None]
    c = np.arange(ATTN_Q)[None, None, :]
    off = np.minimum(np.arange(WINDOW // ATTN_Q + 1) * ATTN_Q, WINDOW)[:, None, None]
    valid = (r <= off + c) & (r > off + c - WINDOW)
    return jnp.asarray(np.where(valid, 0.0, -np.inf).astype(np.float32))


def _block_onehot(seq, n_sel):
    return jnp.asarray((np.arange(seq)[:, None] // SEL_BLOCK == np.arange(n_sel)[None, :])
                       .astype(np.float32), dtype=BF16)


def kernel(x, norm_mix, w_in, conv_w, cmp_pe_k, cmp_w1_k, cmp_w2_k, cmp_pe_v, cmp_w1_v, cmp_w2_v,
           norm_conv_out, norm_attn_out, w_out, norm_ffn, w_gate, w_up, w_down, norm_final):
    B, S, D = x.shape
    T = B * S
    depth = norm_mix.shape[0]
    assert S % ATTN_Q == 0 and S % CMP_STRIDE == 0 and S % ROPE_SPLIT == 0
    n_chunks = S // CMP_STRIDE
    n_sel = S // SEL_BLOCK
    cos_t, sin_t = _rope_tables(S)
    ovt = _overlap_t(n_sel, n_chunks)
    onehot = _block_onehot(S, n_sel)

    h = x.reshape(T, D)
    for l in range(depth):
        y_conv, qt, kk, kvc_in, vt, zgt, w_out_b, w_gate_b, w_up_b, w_down_b = _in_projection(
            h, norm_mix[l][None], w_in[l].astype(BF16), _gate_weight(w_in[l]), cos_t, sin_t, conv_w[l],
            [w_out[l], w_gate[l], w_up[l], w_down[l]], B, S)
        kvc_in = kvc_in.reshape(2, B * N_KV_HEADS, S, HEAD_DIM)
        pe = jnp.stack([cmp_pe_k[l], cmp_pe_v[l]]).reshape(2, 1, CMP_BLOCK * HEAD_DIM)
        kvc, kvct = _compress(kvc_in, pe, jnp.stack([cmp_w1_k[l], cmp_w1_v[l]]),
                              jnp.stack([cmp_w2_k[l], cmp_w2_v[l]]))

        y_attn = _attention(qt, zgt, kvc, kvct, ovt, kk, vt, onehot, B, S)

        h = _out_projection(y_conv, y_attn.reshape(T, ATTN_W), norm_conv_out[l][None],
                            norm_attn_out[l][None], w_out_b, h)

        h = _ffn(h, norm_ffn[l][None], w_gate_b, w_up_b, w_down_b, norm_final[None], l == depth - 1)
    return h.reshape(B, S, D)
```

```python
import functools

import jax
import jax.numpy as jnp
import numpy as np
from jax import lax
from jax.experimental import pallas as pl
from jax.experimental.pallas import tpu as pltpu

F32 = jnp.float32
BF16 = jnp.bfloat16

HEAD_DIM = 128
N_HEADS = 8
N_KV_HEADS = 2
GROUP = N_HEADS // N_KV_HEADS
N_BRANCH = 3
CONV_TAPS = 3
CMP_BLOCK = 32
CMP_STRIDE = 16
SEL_BLOCK = 64
N_SELECT = 16
N_FORCED = 3
WINDOW = 512
ATTN_Q = 256
ROPE_THETA = 10000.0
ROPE_SPLIT = 128
RMS_EPS = 1e-6

CONV_W = 1024
ATTN_W = N_HEADS * HEAD_DIM
KV_W = N_KV_HEADS * HEAD_DIM
N_GATES = N_HEADS * N_BRANCH

V7X_VMEM_BYTES = 64 * 1024 * 1024
V7X_VMEM_BUDGET = V7X_VMEM_BYTES - 6 * 1024 * 1024
LANES = 128
SUBLANES = 8
PROJ_TILE_N = 512
PROJ_CHUNK = 256
FFN_EPILOGUE_ROWS = 256

MASK_BIAS = -float(2 ** 30)
LOG2_E = 1.4426950408889634
ONES_ROWS = 16


def _params(semantics, vmem_bytes):
    return pltpu.CompilerParams(dimension_semantics=semantics,
                                vmem_limit_bytes=int(min(vmem_bytes, V7X_VMEM_BUDGET)))


def _rms(x, g):
    return x * lax.rsqrt(jnp.mean(x * x, axis=-1, keepdims=True) + RMS_EPS) * g


def _split3(x):
    hi = x.astype(BF16)
    r1 = x - hi.astype(F32)
    mid = r1.astype(BF16)
    lo = (r1 - mid.astype(F32)).astype(BF16)
    return hi, mid, lo


def _dot(a, b):
    return jnp.dot(a, b, preferred_element_type=F32)


def _dot_f32(a, b):
    a0, a1, a2 = _split3(a)
    b0, b1, b2 = _split3(b)
    return (_dot(a0, b0) + (_dot(a0, b1) + _dot(a1, b0))
            + (_dot(a0, b2) + _dot(a1, b1) + _dot(a2, b0)))


CONV_GROUPS = CONV_W // PROJ_TILE_N
N_Q_TILES = ATTN_W // PROJ_TILE_N
N_KV_TILES = 3
GATE_STEP = 0
Q_STEP0 = GATE_STEP + 1
KV_STEP0 = Q_STEP0 + N_Q_TILES
CONV_STEP0 = KV_STEP0 + N_KV_TILES
N_PROJ_STEPS = CONV_STEP0 + CONV_GROUPS
N_WEIGHT_TILES = 3 * CONV_GROUPS + N_Q_TILES + N_KV_TILES
KV_OUT_W = N_KV_TILES * KV_W
assert 2 * KV_W == PROJ_TILE_N and N_GATES % SUBLANES == 0


def _gate_weight(w):
    gates = w[:, N_WEIGHT_TILES * PROJ_TILE_N:].astype(BF16)
    return jnp.pad(gates, ((0, 0), (0, LANES - gates.shape[1])))


def _inproj_body(*refs, q_scale, blocks_per_seq, side_blocks):
    n_side = len(side_blocks)
    x_ref, g_ref, w_ref, wc_ref, wh_ref, wg_ref, cos_ref, sin_ref, cw_ref = refs[:9]
    side_in = refs[9:9 + n_side]
    yc_ref, qt_ref, k_ref, kvc_ref, vt_ref, zgt_ref = refs[9 + n_side:15 + n_side]
    side_out = refs[15 + n_side:15 + 2 * n_side]
    a_scr, tail_scr, zg_scr = refs[15 + 2 * n_side:]
    p = pl.program_id(0)
    j = pl.program_id(1)
    i = p - 1
    live = p >= 1

    step = p * N_PROJ_STEPS + j
    for src, dst, n_blocks in zip(side_in, side_out, side_blocks):
        @pl.when(step < n_blocks)
        def _(src=src, dst=dst):
            dst[...] = src[...].astype(BF16)

    @pl.when(live & (j == 0))
    def _():
        a_scr[0] = a_scr[1]

    def normalise():
        piece = x_ref.shape[0]
        rows = pl.ds(pl.multiple_of(j * piece, piece), piece)
        a_scr[1, rows, :] = _rms(x_ref[...], g_ref[...]).astype(BF16)

    @pl.when(jnp.logical_not(live))
    def _():
        normalise()

    @pl.when((p == 0) & (j == 0))
    def _():
        tail_scr[...] = jnp.zeros(tail_scr.shape, F32)

    def a_block():
        return a_scr[0]

    @pl.when(live & (j == GATE_STEP))
    def _():
        normalise()
        zg_scr[...] = _dot(a_block(), wg_ref[...])
        zgt_ref[0] = zg_scr[...].T[0:N_GATES]

    chunks = [(c, c + PROJ_CHUNK) for c in range(0, PROJ_TILE_N, PROJ_CHUNK)]

    def z_cols(weight_ref, c0, c1):
        return _dot(a_block(), weight_ref[:, c0:c1])

    def rope(zh):
        cos = cos_ref[...]
        sin = sin_ref[...]
        heads = []
        for h in range(zh.shape[1] // HEAD_DIM):
            z1 = zh[:, h * HEAD_DIM:(h + 1) * HEAD_DIM]
            heads.append(z1 * cos + pltpu.roll(z1, HEAD_DIM // 2, axis=1) * sin)
        return jnp.concatenate(heads, axis=1)

    @pl.when(live & (j >= CONV_STEP0))
    def _():
        normalise()
        group = j - CONV_STEP0
        for c0, c1 in chunks:
            u = z_cols(wc_ref, c0, c1) * z_cols(wh_ref, c0, c1)
            tail = tail_scr[group, :, c0:c1]
            tail = jnp.where(i % blocks_per_seq == 0, 0.0, tail)
            row = lax.broadcasted_iota(jnp.int32, u.shape, 0)
            t1 = tail[SUBLANES - 1:SUBLANES]
            t2 = tail[SUBLANES - 2:SUBLANES - 1]
            u1 = jnp.where(row == 0, t1, pltpu.roll(u, 1, axis=0))
            u2 = jnp.where(row == 0, t2, jnp.where(row == 1, t1, pltpu.roll(u, 2, axis=0)))
            w = cw_ref[:, c0:c1]
            yc_ref[:, c0:c1] = z_cols(w_ref, c0, c1) * (u2 * w[0:1] + u1 * w[1:2] + u * w[2:3])
            tail_scr[group, :, c0:c1] = u[u.shape[0] - SUBLANES:, :]

    @pl.when(live & (j >= Q_STEP0) & (j < KV_STEP0))
    def _():
        normalise()
        for c0, c1 in chunks:
            qt_ref[0, c0:c1, :] = (rope(z_cols(w_ref, c0, c1)) * q_scale).T.astype(BF16)

    @pl.when(live & (j >= KV_STEP0) & (j < CONV_STEP0))
    def _():
        normalise()
        kx = rope(z_cols(w_ref, 0, KV_W)).astype(BF16)
        for h in range(N_KV_HEADS):
            k_ref[0, h] = kx[:, h * HEAD_DIM:(h + 1) * HEAD_DIM]
        zv = z_cols(w_ref, KV_W, 2 * KV_W)
        vt_ref[0] = zv.T.astype(BF16)

        @pl.when(j == KV_STEP0)
        def _():
            for h in range(N_KV_HEADS):
                kvc_ref[0, 0, h] = kx[:, h * HEAD_DIM:(h + 1) * HEAD_DIM].astype(F32)
                kvc_ref[1, 0, h] = zv[:, h * HEAD_DIM:(h + 1) * HEAD_DIM].astype(BF16).astype(F32)


def _side_rows(n_rows, n_steps):
    for rows in range(2 * SUBLANES, n_rows + 1, 2 * SUBLANES):
        if n_rows % rows == 0 and n_rows // rows <= n_steps:
            return rows
    raise ValueError((n_rows, n_steps))


def _in_projection(x2, g, w_proj, w_gates, cos_t, sin_t, conv_w, side, B, seq):
    T, D = x2.shape
    tn = PROJ_TILE_N
    assert w_proj.shape[1] >= N_WEIGHT_TILES * tn
    tm = min(1024, seq)
    nb = seq // tm
    x_rows = tm // N_PROJ_STEPS
    assert tm % N_PROJ_STEPS == 0 and x_rows % (2 * SUBLANES) == 0
    n_x_blocks = T // x_rows
    n_steps = (T // tm + 1) * N_PROJ_STEPS
    side_rows = [_side_rows(w.shape[0], n_steps) for w in side]
    side_blocks = [w.shape[0] // r for w, r in zip(side, side_rows)]
    side_specs = [pl.BlockSpec((r, w.shape[1]),
                               lambda i, j, n=n: (jnp.minimum(i * N_PROJ_STEPS + j, n - 1), 0))
                  for w, r, n in zip(side, side_rows, side_blocks)]
    body = functools.partial(_inproj_body, q_scale=HEAD_DIM ** -0.5 * LOG2_E, blocks_per_seq=nb,
                             side_blocks=tuple(side_blocks))
    vmem = 2 * (x_rows * D * 4 + 3 * D * tn * 2 + 2 * tm * HEAD_DIM * 4 + tm * tn * 4 + 4 * tm * tn * 2
                + N_GATES * tm * 4) + 2 * tm * D * 2 + 8 * tm * PROJ_CHUNK * 4 \
        + 2 * sum(r * w.shape[1] * 6 for w, r in zip(side, side_rows))

    def at(index):
        return lambda p, j: index(jnp.maximum(p - 1, 0), jnp.where(p == 0, 0, j))

    def conv_group(j):
        return jnp.maximum(j - CONV_STEP0, 0)

    def main_tile(j):
        return jnp.where(j >= CONV_STEP0, j - CONV_STEP0,
                         jnp.maximum(j - Q_STEP0, 0) + 3 * CONV_GROUPS)

    def kv_tile(j):
        return jnp.clip(j - KV_STEP0, 0, N_KV_TILES - 1)

    return pl.pallas_call(
        body,
        grid=(T // tm + 1, N_PROJ_STEPS),
        in_specs=[
            pl.BlockSpec((x_rows, D), lambda p, j: (jnp.minimum(p * N_PROJ_STEPS + j, n_x_blocks - 1), 0)),
            pl.BlockSpec((1, D), lambda p, j: (0, 0)),
            pl.BlockSpec((D, tn), at(lambda i, j: (0, main_tile(j)))),
            pl.BlockSpec((D, tn), at(lambda i, j: (0, CONV_GROUPS + conv_group(j)))),
            pl.BlockSpec((D, tn), at(lambda i, j: (0, 2 * CONV_GROUPS + conv_group(j)))),
            pl.BlockSpec((D, LANES), lambda p, j: (0, 0)),
            pl.BlockSpec((tm, HEAD_DIM), at(lambda i, j: (i % nb, 0))),
            pl.BlockSpec((tm, HEAD_DIM), at(lambda i, j: (i % nb, 0))),
            pl.BlockSpec((CONV_TAPS, tn), at(lambda i, j: (0, conv_group(j)))),
        ] + side_specs,
        out_specs=[
            pl.BlockSpec((tm, tn), at(lambda i, j: (i, conv_group(j)))),
            pl.BlockSpec((1, tn, tm), at(lambda i, j: (i // nb, jnp.clip(j - Q_STEP0, 0, N_Q_TILES - 1), i % nb))),
            pl.BlockSpec((1, N_KV_HEADS, tm, HEAD_DIM), at(lambda i, j: (i // nb, kv_tile(j), i % nb, 0))),
            pl.BlockSpec((2, 1, N_KV_HEADS, tm, HEAD_DIM), at(lambda i, j: (0, i // nb, 0, i % nb, 0))),
            pl.BlockSpec((1, KV_W, tm), at(lambda i, j: (i // nb, kv_tile(j), i % nb))),
            pl.BlockSpec((1, N_GATES, tm), at(lambda i, j: (i // nb, 0, i % nb))),
        ] + side_specs,
        out_shape=[
            jax.ShapeDtypeStruct((T, CONV_W), F32),
            jax.ShapeDtypeStruct((B, ATTN_W, seq), BF16),
            jax.ShapeDtypeStruct((B, N_KV_TILES * N_KV_HEADS, seq, HEAD_DIM), BF16),
            jax.ShapeDtypeStruct((2, B, N_KV_HEADS, seq, HEAD_DIM), F32),
            jax.ShapeDtypeStruct((B, KV_OUT_W, seq), BF16),
            jax.ShapeDtypeStruct((B, N_GATES, seq), F32),
        ] + [jax.ShapeDtypeStruct(w.shape, BF16) for w in side],
        scratch_shapes=[pltpu.VMEM((2, tm, D), BF16), pltpu.VMEM((CONV_GROUPS, SUBLANES, tn), F32),
                        pltpu.VMEM((tm, LANES), F32)],
        compiler_params=_params(("arbitrary", "arbitrary"), vmem),
        name="in_projection",
    )(x2, g, w_proj, w_proj, w_proj, w_gates, cos_t, sin_t, conv_w, *side)


def _compress_body(x_ref, pe_ref, w1_ref, w2_ref, o_ref, ot_ref, *, n_valid):
    n = x_ref.shape[2] // CMP_STRIDE
    x = jnp.concatenate([x_ref[0, 0, pl.ds(l, n, stride=CMP_STRIDE), :] for l in range(CMP_STRIDE)],
                        axis=1).astype(BF16)
    half = x.shape[1]
    w1 = w1_ref[0]
    w1_hi = w1.astype(BF16)
    w1_lo = (w1 - w1_hi.astype(F32)).astype(BF16)
    first = _dot(x, w1_hi[:half]) + _dot(x, w1_lo[:half])
    second = _dot(x, w1_hi[half:]) + _dot(x, w1_lo[half:])
    second = pltpu.roll(second, n - 1, axis=0)
    pe = jnp.broadcast_to(pe_ref[0], (SUBLANES, 2 * half))
    pe_term = _dot_f32(pe, w1)[0:1]
    h = first + second + pe_term
    out = _dot_f32(jax.nn.silu(h), w2_ref[0])
    row = lax.broadcasted_iota(jnp.int32, out.shape, 0)
    out = jnp.where(row < n_valid, out, 0.0)
    o_ref[0, 0] = out
    ot_ref[0, 0] = out.T


def _compress(kv, pe, w1, w2):
    two, BH, seq, d = kv.shape
    n = seq // CMP_STRIDE
    width = CMP_STRIDE * d
    body = functools.partial(_compress_body, n_valid=n - 1)
    vmem = 2 * (seq * d * 4 + 2 * width * 4 + 2 * width * d * 4 + d * d * 4 + n * d * 4) \
        + 8 * width * d * 4
    return pl.pallas_call(
        body,
        grid=(two, BH),
        in_specs=[
            pl.BlockSpec((1, 1, seq, d), lambda s, b: (s, b, 0, 0)),
            pl.BlockSpec((1, 1, 2 * width), lambda s, b: (s, 0, 0)),
            pl.BlockSpec((1, 2 * width, d), lambda s, b: (s, 0, 0)),
            pl.BlockSpec((1, d, d), lambda s, b: (s, 0, 0)),
        ],
        out_specs=[pl.BlockSpec((1, 1, n, d), lambda s, b: (s, b, 0, 0)),
                   pl.BlockSpec((1, 1, d, n), lambda s, b: (s, b, 0, 0))],
        out_shape=[jax.ShapeDtypeStruct((two, BH, n, d), F32),
                   jax.ShapeDtypeStruct((two, BH, d, n), F32)],
        compiler_params=_params(("parallel", "parallel"), vmem),
        name="compress",
    )(kv, pe, w1, w2)


def _select_blocks(imp, q0):
    n_sel = imp.shape[0]
    sel_i = lax.broadcasted_iota(jnp.int32, imp.shape, 0)
    sel_f = sel_i.astype(F32)
    cur = (q0 + lax.broadcasted_iota(jnp.int32, imp.shape, 1)) // SEL_BLOCK
    forced = (sel_i == 0) | (sel_i == cur) | (sel_i == cur - 1)
    v = jnp.where((sel_i > cur) | forced, -jnp.inf, imp)
    chosen = jnp.where(forced, 1.0, 0.0)
    for _ in range(min(N_SELECT, n_sel) - N_FORCED):
        top = jnp.max(v, axis=0, keepdims=True)
        first = jnp.min(jnp.where(v == top, sel_f, float(n_sel)), axis=0, keepdims=True)
        pick = sel_f == first
        chosen = jnp.where(pick, 1.0, chosen)
        v = jnp.where(pick, -jnp.inf, v)
    return jnp.where((chosen > 0.0) & (sel_i <= cur), 0.0, MASK_BIAS)


def _attn_body(qt_ref, zgt_ref, kc_ref, vct_ref, ovt_ref, ks_ref, vst_ref, e_ref, kw_ref, vwt_ref, wb_ref,
               y_ref, s_scr, m_scr, acc_scr, oc_scr, ow_scr, *, tk, span):
    q0 = pl.program_id(2) * ATTN_Q
    qt = [qt_ref[0, g * HEAD_DIM:(g + 1) * HEAD_DIM, :] for g in range(GROUP)]
    q_all = jnp.concatenate(qt, axis=1)

    def lane_time(shape):
        return q0 + (lax.broadcasted_iota(jnp.int32, shape, 1) & (ATTN_Q - 1))

    s = _dot(kc_ref[0, 0].astype(BF16), q_all)
    cmp_end = lax.broadcasted_iota(jnp.int32, s.shape, 0) * CMP_STRIDE + (CMP_BLOCK - 1)
    s = jnp.where(cmp_end <= lane_time(s.shape), s, -jnp.inf)
    m = jnp.max(s, axis=0, keepdims=True)
    m = jnp.where(m == -jnp.inf, 0.0, m)
    e = jnp.exp2(s - m)
    p = e / jnp.maximum(jnp.sum(e, axis=0, keepdims=True), 1e-30)
    oc_scr[...] = _dot(vct_ref[0, 0].astype(BF16), p.astype(BF16))
    pg = p[:, 0:ATTN_Q]
    for g in range(1, GROUP):
        pg = pg + p[:, g * ATTN_Q:(g + 1) * ATTN_Q]
    ovt = ovt_ref[...]
    p0, p1, p2 = _split3(pg)
    bias = _select_blocks(_dot(ovt, p0) + _dot(ovt, p1) + _dot(ovt, p2), q0).astype(BF16)

    w_start = pl.multiple_of(jnp.maximum(q0 - WINDOW, 0), ATTN_Q)
    sw = _dot(kw_ref[0, 0, pl.ds(w_start, span), :], q_all)
    sw = sw + jnp.concatenate([wb_ref[0]] * GROUP, axis=1)
    ew = jnp.exp2(sw - jnp.max(sw, axis=0, keepdims=True))
    vw = jnp.concatenate([vwt_ref[0, :, pl.ds(w_start, span)], jnp.ones((ONES_ROWS, span), BF16)], axis=0)
    ow = _dot(vw, ew.astype(BF16))
    ow_scr[...] = ow[0:HEAD_DIM] / ow[HEAD_DIM:HEAD_DIM + 1]

    qx = jnp.concatenate([jnp.concatenate([qt[g], bias], axis=0) for g in range(GROUP)], axis=1)

    def scores(j, slot):
        start = pl.multiple_of(j * tk, tk)
        kx = jnp.concatenate([ks_ref[0, 0, pl.ds(start, tk), :], e_ref[pl.ds(start, tk), :]], axis=1)
        s_scr[slot] = _dot(kx, qx)

    def update(j, slot, causal):
        start = pl.multiple_of(j * tk, tk)
        vt = jnp.concatenate([vst_ref[0, :, pl.ds(start, tk)], jnp.ones((ONES_ROWS, tk), BF16)], axis=0)
        sc = s_scr[slot]
        if causal:
            key = start + lax.broadcasted_iota(jnp.int32, sc.shape, 0)
            sc = jnp.where(key <= lane_time(sc.shape), sc, -jnp.inf)
        m_old = m_scr[...]
        m_new = jnp.maximum(m_old, jnp.max(sc, axis=0, keepdims=True))
        alpha = jnp.exp2(m_old - m_new)
        pj = jnp.exp2(sc - m_new)
        m_scr[...] = m_new
        acc_scr[...] = alpha * acc_scr[...] + _dot(vt, pj.astype(BF16))

    m_scr[...] = jnp.full(m_scr.shape, -jnp.inf, F32)
    acc_scr[...] = jnp.zeros(acc_scr.shape, F32)
    scores(0, 0)
    last = q0 // tk

    @pl.loop(0, last // 2)
    def _(i):
        scores(2 * i + 1, 1)
        update(2 * i, 0, False)
        scores(2 * i + 2, 0)
        update(2 * i + 1, 1, False)

    @pl.when(last % 2 == 1)
    def _():
        scores(last, 1)
        update(last - 1, 0, False)
        update(last, 1, True)

    @pl.when(last % 2 == 0)
    def _():
        update(last, 0, True)

    o_sel = acc_scr[0:HEAD_DIM, :] / acc_scr[HEAD_DIM:HEAD_DIM + 1, :]
    gates = jax.nn.sigmoid(zgt_ref[0])
    hk = pl.program_id(1)

    def gate(g, branch):
        row = gates[g * N_BRANCH + branch:g * N_BRANCH + branch + 1]
        for other in range(1, N_KV_HEADS):
            r = (other * GROUP + g) * N_BRANCH + branch
            row = jnp.where(hk == other, gates[r:r + 1], row)
        return row

    for g in range(GROUP):
        sl = slice(g * ATTN_Q, (g + 1) * ATTN_Q)
        y = (gate(g, 0) * oc_scr[:, sl] + gate(g, 1) * o_sel[:, sl]
             + gate(g, 2) * ow_scr[:, sl])
        y_ref[0, :, g * HEAD_DIM:(g + 1) * HEAD_DIM] = y.T


def _attention(qt, zgt, kvc, kvct, ovt, kk, vt, onehot, B, seq):
    n_cmp = kvc.shape[2]
    n_sel = ovt.shape[0]
    GW = GROUP * HEAD_DIM
    GQ = GROUP * ATTN_Q
    VR = HEAD_DIM + ONES_ROWS
    tk = min(512, seq)
    span = WINDOW + ATTN_Q
    assert seq >= span
    body = functools.partial(_attn_body, tk=tk, span=span)
    resident = 4 * seq * HEAD_DIM * 2 + seq * n_sel * 2
    vmem = 2 * (GW * ATTN_Q * 2 + N_GATES * ATTN_Q * 4 + 2 * n_cmp * HEAD_DIM * 4 + n_sel * n_cmp * 2
                + resident + ATTN_Q * GW * 4) \
        + (2 * tk + VR + 2 * HEAD_DIM + SUBLANES) * GQ * 4 + 8 * (n_cmp + span) * GQ * 4
    return pl.pallas_call(
        body,
        grid=(B, N_KV_HEADS, seq // ATTN_Q),
        in_specs=[
            pl.BlockSpec((1, GW, ATTN_Q), lambda b, h, i: (b, h, i)),
            pl.BlockSpec((1, N_GATES, ATTN_Q), lambda b, h, i: (b, 0, i)),
            pl.BlockSpec((1, 1, n_cmp, HEAD_DIM), lambda b, h, i: (0, b * N_KV_HEADS + h, 0, 0)),
            pl.BlockSpec((1, 1, HEAD_DIM, n_cmp), lambda b, h, i: (1, b * N_KV_HEADS + h, 0, 0)),
            pl.BlockSpec((n_sel, n_cmp), lambda b, h, i: (0, 0)),
            pl.BlockSpec((1, 1, seq, HEAD_DIM), lambda b, h, i: (b, N_KV_HEADS + h, 0, 0)),
            pl.BlockSpec((1, HEAD_DIM, seq), lambda b, h, i: (b, N_KV_HEADS + h, 0)),
            pl.BlockSpec((seq, n_sel), lambda b, h, i: (0, 0)),
            pl.BlockSpec((1, 1, seq, HEAD_DIM), lambda b, h, i: (b, 2 * N_KV_HEADS + h, 0, 0)),
            pl.BlockSpec((1, HEAD_DIM, seq), lambda b, h, i: (b, 2 * N_KV_HEADS + h, 0)),
            pl.BlockSpec((1, span, ATTN_Q), lambda b, h, i: (jnp.minimum(i, WINDOW // ATTN_Q), 0, 0)),
        ],
        out_specs=pl.BlockSpec((1, ATTN_Q, GW), lambda b, h, i: (b, i, h)),
        out_shape=jax.ShapeDtypeStruct((B, seq, ATTN_W), F32),
        scratch_shapes=[pltpu.VMEM((2, tk, GQ), F32),
                        pltpu.VMEM((1, GQ), F32),
                        pltpu.VMEM((VR, GQ), F32),
                        pltpu.VMEM((HEAD_DIM, GQ), F32),
                        pltpu.VMEM((HEAD_DIM, GQ), F32)],
        compiler_params=_params(("parallel", "parallel", "arbitrary"), vmem),
        name="attention",
    )(qt, zgt, kvc, kvct, ovt, kk, vt, onehot, kk, vt, _window_bias(span))


def _outproj_body(yc_ref, ya_ref, gc_ref, ga_ref, w_ref, x_ref, h_ref, a_scr):
    a_scr[:, 0:CONV_W] = _rms(yc_ref[...], gc_ref[...]).astype(BF16)
    a_scr[:, CONV_W:] = _rms(ya_ref[...], ga_ref[...]).astype(BF16)
    for c in range(0, h_ref.shape[1], PROJ_TILE_N):
        cols = slice(c, c + PROJ_TILE_N)
        h_ref[:, cols] = x_ref[:, cols] + _dot(a_scr[...], w_ref[:, cols])


def _out_projection(yc, ya, g_conv, g_attn, w, x2):
    T, D = x2.shape
    K = w.shape[0]
    tm = min(512, T)
    assert D % PROJ_TILE_N == 0
    vmem = 2 * (tm * CONV_W * 4 + tm * ATTN_W * 4 + 2 * tm * D * 4) + K * D * 2 + tm * K * 2 \
        + 2 * tm * ATTN_W * 4 + 2 * tm * PROJ_TILE_N * 4
    return pl.pallas_call(
        _outproj_body,
        grid=(T // tm,),
        in_specs=[
            pl.BlockSpec((tm, CONV_W), lambda i: (i, 0)),
            pl.BlockSpec((tm, ATTN_W), lambda i: (i, 0)),
            pl.BlockSpec((1, CONV_W), lambda i: (0, 0)),
            pl.BlockSpec((1, ATTN_W), lambda i: (0, 0)),
            pl.BlockSpec((K, D), lambda i: (0, 0), pipeline_mode=pl.Buffered(1)),
            pl.BlockSpec((tm, D), lambda i: (i, 0)),
        ],
        out_specs=pl.BlockSpec((tm, D), lambda i: (i, 0)),
        out_shape=jax.ShapeDtypeStruct((T, D), F32),
        scratch_shapes=[pltpu.VMEM((tm, K), BF16)],
        compiler_params=_params(("parallel",), vmem),
        name="out_projection",
    )(yc, ya, g_conv, g_attn, w, x2)


def _ffn_body(h_ref, g_ref, wg_ref, wu_ref, wd_ref, gf_ref, o_ref, a_scr, *, final_norm, tn):
    j = pl.program_id(1)

    @pl.when(j == 0)
    def _():
        @pl.loop(0, h_ref.shape[0] // FFN_EPILOGUE_ROWS)
        def _(r):
            rows = pl.ds(pl.multiple_of(r * FFN_EPILOGUE_ROWS, FFN_EPILOGUE_ROWS), FFN_EPILOGUE_ROWS)
            a_scr[rows, :] = _rms(h_ref[rows, :], g_ref[...]).astype(BF16)

    a = a_scr[...]
    u = jnp.concatenate(
        [(jax.nn.silu(_dot(a, wg_ref[:, c:c + PROJ_CHUNK])) * _dot(a, wu_ref[:, c:c + PROJ_CHUNK])).astype(BF16)
         for c in range(0, wg_ref.shape[1], PROJ_CHUNK)], axis=1)
    D = o_ref.shape[1]

    @pl.when(j == 0)
    def _():
        for n in range(D // tn):
            o_ref[:, n * tn:(n + 1) * tn] = _dot(u, wd_ref[:, n * tn:(n + 1) * tn])

    @pl.when(j > 0)
    def _():
        for n in range(D // tn):
            o_ref[:, n * tn:(n + 1) * tn] += _dot(u, wd_ref[:, n * tn:(n + 1) * tn])

    @pl.when(j == pl.num_programs(1) - 1)
    def _():
        @pl.loop(0, o_ref.shape[0] // FFN_EPILOGUE_ROWS)
        def _(r):
            rows = pl.ds(pl.multiple_of(r * FFN_EPILOGUE_ROWS, FFN_EPILOGUE_ROWS), FFN_EPILOGUE_ROWS)
            out = h_ref[rows, :] + o_ref[rows, :]
            o_ref[rows, :] = _rms(out, gf_ref[...]) if final_norm else out


def _ffn(h, g, wg, wu, wd, g_final, final_norm):
    T, D = h.shape
    F = wg.shape[1]
    tm = min(1024, T)
    tf = 512
    tn = PROJ_CHUNK
    assert F % tf == 0 and D % tn == 0
    body = functools.partial(_ffn_body, final_norm=final_norm, tn=tn)
    vmem = 2 * (3 * D * tf * 2 + 2 * tm * D * 4) + tm * D * 2 + 6 * tm * tf * 4 + 2 * tm * tn * 4
    return pl.pallas_call(
        body,
        grid=(T // tm, F // tf),
        in_specs=[
            pl.BlockSpec((tm, D), lambda i, j: (i, 0)),
            pl.BlockSpec((1, D), lambda i, j: (0, 0)),
            pl.BlockSpec((D, tf), lambda i, j: (0, j)),
            pl.BlockSpec((D, tf), lambda i, j: (0, j)),
            pl.BlockSpec((tf, D), lambda i, j: (j, 0)),
            pl.BlockSpec((1, D), lambda i, j: (0, 0)),
        ],
        out_specs=pl.BlockSpec((tm, D), lambda i, j: (i, 0)),
        out_shape=jax.ShapeDtypeStruct((T, D), F32),
        scratch_shapes=[pltpu.VMEM((tm, D), BF16)],
        compiler_params=_params(("parallel", "arbitrary"), vmem),
        name="ffn",
    )(h, g, wg, wu, wd, g_final)


def _rope_tables(seq):
    half = HEAD_DIM // 2
    inv = 1.0 / (ROPE_THETA ** (jnp.arange(half, dtype=F32) / half))
    inv = jnp.concatenate([inv, inv])
    sign = jnp.concatenate([-jnp.ones((half,), F32), jnp.ones((half,), F32)])
    lo = jnp.arange(ROPE_SPLIT).astype(F32)[:, None] * inv[None, :]
    hi = (jnp.arange(seq // ROPE_SPLIT) * ROPE_SPLIT).astype(F32)[:, None] * inv[None, :]
    cos_lo, sin_lo = jnp.cos(lo)[None], jnp.sin(lo)[None]
    cos_hi, sin_hi = jnp.cos(hi)[:, None, :], jnp.sin(hi)[:, None, :]
    cos = (cos_hi * cos_lo - sin_hi * sin_lo).reshape(seq, HEAD_DIM)
    sin = ((sin_hi * cos_lo + cos_hi * sin_lo) * sign).reshape(seq, HEAD_DIM)
    return cos, sin


def _overlap_t(n_sel, n_cmp):
    cs = np.arange(n_cmp)[None, :] * CMP_STRIDE
    ss = np.arange(n_sel)[:, None] * SEL_BLOCK
    return jnp.asarray(((cs < ss + SEL_BLOCK) & (cs + CMP_BLOCK > ss)).astype(np.float32), dtype=BF16)


def _window_bias(span):
    r = np.arange(span)[None, :, None]
    c = np.arange(ATTN_Q)[None, None, :]
    off = np.minimum(np.arange(WINDOW // ATTN_Q + 1) * ATTN_Q, WINDOW)[:, None, None]
    valid = (r <= off + c) & (r > off + c - WINDOW)
    return jnp.asarray(np.where(valid, 0.0, -np.inf).astype(np.float32))


def _block_onehot(seq, n_sel):
    return jnp.asarray((np.arange(seq)[:, None] // SEL_BLOCK == np.arange(n_sel)[None, :])
                       .astype(np.float32), dtype=BF16)


def kernel(x, norm_mix, w_in, conv_w, cmp_pe_k, cmp_w1_k, cmp_w2_k, cmp_pe_v, cmp_w1_v, cmp_w2_v,
           norm_conv_out, norm_attn_out, w_out, norm_ffn, w_gate, w_up, w_down, norm_final):
    B, S, D = x.shape
    T = B * S
    depth = norm_mix.shape[0]
    assert S % ATTN_Q == 0 and S % CMP_STRIDE == 0 and S % ROPE_SPLIT == 0
    n_chunks = S // CMP_STRIDE
    n_sel = S // SEL_BLOCK
    cos_t, sin_t = _rope_tables(S)
    ovt = _overlap_t(n_sel, n_chunks)
    onehot = _block_onehot(S, n_sel)

    h = x.reshape(T, D)
    for l in range(depth):
        y_conv, qt, kk, kvc_in, vt, zgt, w_out_b, w_gate_b, w_up_b, w_down_b = _in_projection(
            h, norm_mix[l][None], w_in[l].astype(BF16), _gate_weight(w_in[l]), cos_t, sin_t, conv_w[l],
            [w_out[l], w_gate[l], w_up[l], w_down[l]], B, S)
        kvc_in = kvc_in.reshape(2, B * N_KV_HEADS, S, HEAD_DIM)
        pe = jnp.stack([cmp_pe_k[l], cmp_pe_v[l]]).reshape(2, 1, CMP_BLOCK * HEAD_DIM)
        kvc, kvct = _compress(kvc_in, pe, jnp.stack([cmp_w1_k[l], cmp_w1_v[l]]),
                              jnp.stack([cmp_w2_k[l], cmp_w2_v[l]]))

        y_attn = _attention(qt, zgt, kvc, kvct, ovt, kk, vt, onehot, B, S)

        h = _out_projection(y_conv, y_attn.reshape(T, ATTN_W), norm_conv_out[l][None],
                            norm_attn_out[l][None], w_out_b, h)

        h = _ffn(h, norm_ffn[l][None], w_gate_b, w_up_b, w_down_b, norm_final[None], l == depth - 1)
    return h.reshape(B, S, D)
```

```python
import functools

import jax
import jax.numpy as jnp
import numpy as np
from jax import lax
from jax.experimental import pallas as pl
from jax.experimental.pallas import tpu as pltpu

F32 = jnp.float32
BF16 = jnp.bfloat16

HEAD_DIM = 128
N_HEADS = 8
N_KV_HEADS = 2
GROUP = N_HEADS // N_KV_HEADS
N_BRANCH = 3
CONV_TAPS = 3
CMP_BLOCK = 32
CMP_STRIDE = 16
SEL_BLOCK = 64
N_SELECT = 16
N_FORCED = 3
WINDOW = 512
ATTN_Q = 256
ROPE_THETA = 10000.0
ROPE_SPLIT = 128
RMS_EPS = 1e-6

CONV_W = 1024
ATTN_W = N_HEADS * HEAD_DIM
KV_W = N_KV_HEADS * HEAD_DIM
N_GATES = N_HEADS * N_BRANCH

V7X_VMEM_BYTES = 64 * 1024 * 1024
V7X_VMEM_BUDGET = V7X_VMEM_BYTES - 6 * 1024 * 1024
LANES = 128
SUBLANES = 8
PROJ_TILE_N = 512
PROJ_CHUNK = 256
FFN_EPILOGUE_ROWS = 256

MASK_BIAS = -float(2 ** 30)
LOG2_E = 1.4426950408889634
ONES_ROWS = 16


def _params(semantics, vmem_bytes):
    return pltpu.CompilerParams(dimension_semantics=semantics,
                                vmem_limit_bytes=int(min(vmem_bytes, V7X_VMEM_BUDGET)))


def _rms(x, g):
    return x * lax.rsqrt(jnp.mean(x * x, axis=-1, keepdims=True) + RMS_EPS) * g


def _split3(x):
    hi = x.astype(BF16)
    r1 = x - hi.astype(F32)
    mid = r1.astype(BF16)
    lo = (r1 - mid.astype(F32)).astype(BF16)
    return hi, mid, lo


def _dot(a, b):
    return jnp.dot(a, b, preferred_element_type=F32)


def _dot_f32(a, b):
    a0, a1, a2 = _split3(a)
    b0, b1, b2 = _split3(b)
    return (_dot(a0, b0) + (_dot(a0, b1) + _dot(a1, b0))
            + (_dot(a0, b2) + _dot(a1, b1) + _dot(a2, b0)))


CONV_GROUPS = CONV_W // PROJ_TILE_N
N_Q_TILES = ATTN_W // PROJ_TILE_N
N_KV_TILES = 3
GATE_STEP = 0
Q_STEP0 = GATE_STEP + 1
KV_STEP0 = Q_STEP0 + N_Q_TILES
CONV_STEP0 = KV_STEP0 + N_KV_TILES
N_PROJ_STEPS = CONV_STEP0 + CONV_GROUPS
N_WEIGHT_TILES = 3 * CONV_GROUPS + N_Q_TILES + N_KV_TILES
KV_OUT_W = N_KV_TILES * KV_W
assert 2 * KV_W == PROJ_TILE_N and N_GATES % SUBLANES == 0


def _gate_weight(w):
    gates = w[:, N_WEIGHT_TILES * PROJ_TILE_N:].astype(BF16)
    return jnp.pad(gates, ((0, 0), (0, LANES - gates.shape[1])))


def _inproj_body(*refs, q_scale, blocks_per_seq, side_blocks):
    n_side = len(side_blocks)
    x_ref, g_ref, w_ref, wc_ref, wh_ref, wg_ref, cos_ref, sin_ref, cw_ref = refs[:9]
    side_in = refs[9:9 + n_side]
    yc_ref, qt_ref, k_ref, kvc_ref, vt_ref, zgt_ref = refs[9 + n_side:15 + n_side]
    side_out = refs[15 + n_side:15 + 2 * n_side]
    a_scr, tail_scr, zg_scr = refs[15 + 2 * n_side:]
    i = pl.program_id(0)
    j = pl.program_id(1)

    step = i * N_PROJ_STEPS + j
    for src, dst, n_blocks in zip(side_in, side_out, side_blocks):
        @pl.when(step < n_blocks)
        def _(src=src, dst=dst):
            dst[...] = src[...].astype(BF16)

    @pl.when(j == 0)
    def _():
        a_scr[...] = _rms(x_ref[...], g_ref[...]).astype(BF16)

    @pl.when((i == 0) & (j == 0))
    def _():
        tail_scr[...] = jnp.zeros(tail_scr.shape, F32)

    @pl.when(j == GATE_STEP)
    def _():
        zg_scr[...] = _dot(a_scr[...], wg_ref[...])
        zgt_ref[0] = zg_scr[...].T[0:N_GATES]

    chunks = [(c, c + PROJ_CHUNK) for c in range(0, PROJ_TILE_N, PROJ_CHUNK)]

    def z_cols(weight_ref, c0, c1):
        return _dot(a_scr[...], weight_ref[:, c0:c1])

    def rope(zh):
        cos = cos_ref[...]
        sin = sin_ref[...]
        heads = []
        for h in range(zh.shape[1] // HEAD_DIM):
            z1 = zh[:, h * HEAD_DIM:(h + 1) * HEAD_DIM]
            heads.append(z1 * cos + pltpu.roll(z1, HEAD_DIM // 2, axis=1) * sin)
        return jnp.concatenate(heads, axis=1)

    @pl.when(j >= CONV_STEP0)
    def _():
        group = j - CONV_STEP0
        for c0, c1 in chunks:
            u = z_cols(wc_ref, c0, c1) * z_cols(wh_ref, c0, c1)
            tail = tail_scr[group, :, c0:c1]
            tail = jnp.where(i % blocks_per_seq == 0, 0.0, tail)
            row = lax.broadcasted_iota(jnp.int32, u.shape, 0)
            t1 = tail[SUBLANES - 1:SUBLANES]
            t2 = tail[SUBLANES - 2:SUBLANES - 1]
            u1 = jnp.where(row == 0, t1, pltpu.roll(u, 1, axis=0))
            u2 = jnp.where(row == 0, t2, jnp.where(row == 1, t1, pltpu.roll(u, 2, axis=0)))
            w = cw_ref[:, c0:c1]
            yc_ref[:, c0:c1] = z_cols(w_ref, c0, c1) * (u2 * w[0:1] + u1 * w[1:2] + u * w[2:3])
            tail_scr[group, :, c0:c1] = u[u.shape[0] - SUBLANES:, :]

    @pl.when((j >= Q_STEP0) & (j < KV_STEP0))
    def _():
        for c0, c1 in chunks:
            qt_ref[0, c0:c1, :] = (rope(z_cols(w_ref, c0, c1)) * q_scale).T.astype(BF16)

    @pl.when((j >= KV_STEP0) & (j < CONV_STEP0))
    def _():
        kx = rope(z_cols(w_ref, 0, KV_W)).astype(BF16)
        for h in range(N_KV_HEADS):
            k_ref[0, h] = kx[:, h * HEAD_DIM:(h + 1) * HEAD_DIM]
        zv = z_cols(w_ref, KV_W, 2 * KV_W)
        vt_ref[0] = zv.T.astype(BF16)

        @pl.when(j == KV_STEP0)
        def _():
            for h in range(N_KV_HEADS):
                kvc_ref[0, 0, h] = kx[:, h * HEAD_DIM:(h + 1) * HEAD_DIM].astype(F32)
                kvc_ref[1, 0, h] = zv[:, h * HEAD_DIM:(h + 1) * HEAD_DIM].astype(BF16).astype(F32)


def _side_rows(n_rows, n_steps):
    for rows in range(2 * SUBLANES, n_rows + 1, 2 * SUBLANES):
        if n_rows % rows == 0 and n_rows // rows <= n_steps:
            return rows
    raise ValueError((n_rows, n_steps))


def _in_projection(x2, g, w_proj, w_gates, cos_t, sin_t, conv_w, side, B, seq):
    T, D = x2.shape
    tn = PROJ_TILE_N
    assert w_proj.shape[1] >= N_WEIGHT_TILES * tn
    tm = min(1024, seq)
    nb = seq // tm
    n_steps = (T // tm) * N_PROJ_STEPS
    side_rows = [_side_rows(w.shape[0], n_steps) for w in side]
    side_blocks = [w.shape[0] // r for w, r in zip(side, side_rows)]
    side_specs = [pl.BlockSpec((r, w.shape[1]),
                               lambda i, j, n=n: (jnp.minimum(i * N_PROJ_STEPS + j, n - 1), 0))
                  for w, r, n in zip(side, side_rows, side_blocks)]
    body = functools.partial(_inproj_body, q_scale=HEAD_DIM ** -0.5 * LOG2_E, blocks_per_seq=nb,
                             side_blocks=tuple(side_blocks))
    vmem = 2 * (tm * D * 4 + 3 * D * tn * 2 + 2 * tm * HEAD_DIM * 4 + tm * tn * 4 + 4 * tm * tn * 2
                + N_GATES * tm * 4) + tm * D * 2 + 8 * tm * PROJ_CHUNK * 4 \
        + 2 * sum(r * w.shape[1] * 6 for w, r in zip(side, side_rows))

    def conv_group(j):
        return jnp.maximum(j - CONV_STEP0, 0)

    def main_tile(j):
        return jnp.where(j >= CONV_STEP0, j - CONV_STEP0,
                         jnp.maximum(j - Q_STEP0, 0) + 3 * CONV_GROUPS)

    def kv_tile(j):
        return jnp.clip(j - KV_STEP0, 0, N_KV_TILES - 1)

    return pl.pallas_call(
        body,
        grid=(T // tm, N_PROJ_STEPS),
        in_specs=[
            pl.BlockSpec((tm, D), lambda i, j: (i, 0)),
            pl.BlockSpec((1, D), lambda i, j: (0, 0)),
            pl.BlockSpec((D, tn), lambda i, j: (0, main_tile(j))),
            pl.BlockSpec((D, tn), lambda i, j: (0, CONV_GROUPS + conv_group(j))),
            pl.BlockSpec((D, tn), lambda i, j: (0, 2 * CONV_GROUPS + conv_group(j))),
            pl.BlockSpec((D, LANES), lambda i, j: (0, 0)),
            pl.BlockSpec((tm, HEAD_DIM), lambda i, j: (i % nb, 0)),
            pl.BlockSpec((tm, HEAD_DIM), lambda i, j: (i % nb, 0)),
            pl.BlockSpec((CONV_TAPS, tn), lambda i, j: (0, conv_group(j))),
        ] + side_specs,
        out_specs=[
            pl.BlockSpec((tm, tn), lambda i, j: (i, conv_group(j))),
            pl.BlockSpec((1, tn, tm), lambda i, j: (i // nb, jnp.clip(j - Q_STEP0, 0, N_Q_TILES - 1), i % nb)),
            pl.BlockSpec((1, N_KV_HEADS, tm, HEAD_DIM), lambda i, j: (i // nb, kv_tile(j), i % nb, 0)),
            pl.BlockSpec((2, 1, N_KV_HEADS, tm, HEAD_DIM), lambda i, j: (0, i // nb, 0, i % nb, 0)),
            pl.BlockSpec((1, KV_W, tm), lambda i, j: (i // nb, kv_tile(j), i % nb)),
            pl.BlockSpec((1, N_GATES, tm), lambda i, j: (i // nb, 0, i % nb)),
        ] + side_specs,
        out_shape=[
            jax.ShapeDtypeStruct((T, CONV_W), F32),
            jax.ShapeDtypeStruct((B, ATTN_W, seq), BF16),
            jax.ShapeDtypeStruct((B, N_KV_TILES * N_KV_HEADS, seq, HEAD_DIM), BF16),
            jax.ShapeDtypeStruct((2, B, N_KV_HEADS, seq, HEAD_DIM), F32),
            jax.ShapeDtypeStruct((B, KV_OUT_W, seq), BF16),
            jax.ShapeDtypeStruct((B, N_GATES, seq), F32),
        ] + [jax.ShapeDtypeStruct(w.shape, BF16) for w in side],
        scratch_shapes=[pltpu.VMEM((tm, D), BF16), pltpu.VMEM((CONV_GROUPS, SUBLANES, tn), F32),
                        pltpu.VMEM((tm, LANES), F32)],
        compiler_params=_params(("arbitrary", "arbitrary"), vmem),
        name="in_projection",
    )(x2, g, w_proj, w_proj, w_proj, w_gates, cos_t, sin_t, conv_w, *side)


def _compress_body(x_ref, pe_ref, w1_ref, w2_ref, o_ref, ot_ref, *, n_valid):
    n = x_ref.shape[2] // CMP_STRIDE
    x = jnp.concatenate([x_ref[0, 0, pl.ds(l, n, stride=CMP_STRIDE), :] for l in range(CMP_STRIDE)],
                        axis=1).astype(BF16)
    half = x.shape[1]
    w1 = w1_ref[0]
    w1_hi = w1.astype(BF16)
    w1_lo = (w1 - w1_hi.astype(F32)).astype(BF16)
    first = _dot(x, w1_hi[:half]) + _dot(x, w1_lo[:half])
    second = _dot(x, w1_hi[half:]) + _dot(x, w1_lo[half:])
    second = pltpu.roll(second, n - 1, axis=0)
    pe = jnp.broadcast_to(pe_ref[0], (SUBLANES, 2 * half))
    pe_term = _dot_f32(pe, w1)[0:1]
    h = first + second + pe_term
    out = _dot_f32(jax.nn.silu(h), w2_ref[0])
    row = lax.broadcasted_iota(jnp.int32, out.shape, 0)
    out = jnp.where(row < n_valid, out, 0.0)
    o_ref[0, 0] = out
    ot_ref[0, 0] = out.T


def _compress(kv, pe, w1, w2):
    two, BH, seq, d = kv.shape
    n = seq // CMP_STRIDE
    width = CMP_STRIDE * d
    body = functools.partial(_compress_body, n_valid=n - 1)
    vmem = 2 * (seq * d * 4 + 2 * width * 4 + 2 * width * d * 4 + d * d * 4 + n * d * 4) \
        + 8 * width * d * 4
    return pl.pallas_call(
        body,
        grid=(two, BH),
        in_specs=[
            pl.BlockSpec((1, 1, seq, d), lambda s, b: (s, b, 0, 0)),
            pl.BlockSpec((1, 1, 2 * width), lambda s, b: (s, 0, 0)),
            pl.BlockSpec((1, 2 * width, d), lambda s, b: (s, 0, 0)),
            pl.BlockSpec((1, d, d), lambda s, b: (s, 0, 0)),
        ],
        out_specs=[pl.BlockSpec((1, 1, n, d), lambda s, b: (s, b, 0, 0)),
                   pl.BlockSpec((1, 1, d, n), lambda s, b: (s, b, 0, 0))],
        out_shape=[jax.ShapeDtypeStruct((two, BH, n, d), F32),
                   jax.ShapeDtypeStruct((two, BH, d, n), F32)],
        compiler_params=_params(("parallel", "parallel"), vmem),
        name="compress",
    )(kv, pe, w1, w2)


def _select_blocks(imp, q0):
    n_sel = imp.shape[0]
    sel_i = lax.broadcasted_iota(jnp.int32, imp.shape, 0)
    sel_f = sel_i.astype(F32)
    cur = (q0 + lax.broadcasted_iota(jnp.int32, imp.shape, 1)) // SEL_BLOCK
    forced = (sel_i == 0) | (sel_i == cur) | (sel_i == cur - 1)
    v = jnp.where((sel_i > cur) | forced, -jnp.inf, imp)
    chosen = jnp.where(forced, 1.0, 0.0)
    for _ in range(min(N_SELECT, n_sel) - N_FORCED):
        top = jnp.max(v, axis=0, keepdims=True)
        first = jnp.min(jnp.where(v == top, sel_f, float(n_sel)), axis=0, keepdims=True)
        pick = sel_f == first
        chosen = jnp.where(pick, 1.0, chosen)
        v = jnp.where(pick, -jnp.inf, v)
    return jnp.where((chosen > 0.0) & (sel_i <= cur), 0.0, MASK_BIAS)


def _attn_body(qt_ref, zgt_ref, kc_ref, vct_ref, ovt_ref, ks_ref, vst_ref, e_ref, kw_ref, vwt_ref, wb_ref,
               y_ref, s_scr, m_scr, acc_scr, oc_scr, ow_scr, *, tk, span):
    q0 = pl.program_id(2) * ATTN_Q
    qt = [qt_ref[0, g * HEAD_DIM:(g + 1) * HEAD_DIM, :] for g in range(GROUP)]
    q_all = jnp.concatenate(qt, axis=1)

    def lane_time(shape):
        return q0 + (lax.broadcasted_iota(jnp.int32, shape, 1) & (ATTN_Q - 1))

    s = _dot(kc_ref[0, 0].astype(BF16), q_all)
    cmp_end = lax.broadcasted_iota(jnp.int32, s.shape, 0) * CMP_STRIDE + (CMP_BLOCK - 1)
    s = jnp.where(cmp_end <= lane_time(s.shape), s, -jnp.inf)
    m = jnp.max(s, axis=0, keepdims=True)
    m = jnp.where(m == -jnp.inf, 0.0, m)
    e = jnp.exp2(s - m)
    p = e / jnp.maximum(jnp.sum(e, axis=0, keepdims=True), 1e-30)
    oc_scr[...] = _dot(vct_ref[0, 0].astype(BF16), p.astype(BF16))
    pg = p[:, 0:ATTN_Q]
    for g in range(1, GROUP):
        pg = pg + p[:, g * ATTN_Q:(g + 1) * ATTN_Q]
    ovt = ovt_ref[...]
    p0, p1, p2 = _split3(pg)
    bias = _select_blocks(_dot(ovt, p0) + _dot(ovt, p1) + _dot(ovt, p2), q0).astype(BF16)

    w_start = pl.multiple_of(jnp.maximum(q0 - WINDOW, 0), ATTN_Q)
    sw = _dot(kw_ref[0, 0, pl.ds(w_start, span), :], q_all)
    sw = sw + jnp.concatenate([wb_ref[0]] * GROUP, axis=1)
    ew = jnp.exp2(sw - jnp.max(sw, axis=0, keepdims=True))
    vw = jnp.concatenate([vwt_ref[0, :, pl.ds(w_start, span)], jnp.ones((ONES_ROWS, span), BF16)], axis=0)
    ow = _dot(vw, ew.astype(BF16))
    ow_scr[...] = ow[0:HEAD_DIM] / ow[HEAD_DIM:HEAD_DIM + 1]

    qx = jnp.concatenate([jnp.concatenate([qt[g], bias], axis=0) for g in range(GROUP)], axis=1)

    def scores(j, slot):
        start = pl.multiple_of(j * tk, tk)
        kx = jnp.concatenate([ks_ref[0, 0, pl.ds(start, tk), :], e_ref[pl.ds(start, tk), :]], axis=1)
        s_scr[slot] = _dot(kx, qx)

    def update(j, slot, causal):
        start = pl.multiple_of(j * tk, tk)
        vt = jnp.concatenate([vst_ref[0, :, pl.ds(start, tk)], jnp.ones((HEAD_DIM, tk), BF16)], axis=0)
        sc = s_scr[slot]
        if causal:
            key = start + lax.broadcasted_iota(jnp.int32, sc.shape, 0)
            sc = jnp.where(key <= lane_time(sc.shape), sc, -jnp.inf)
        m_old = m_scr[...]
        m_new = jnp.maximum(m_old, jnp.max(sc, axis=0, keepdims=True))
        alpha = jnp.exp2(m_old - m_new)
        pj = jnp.exp2(sc - m_new)
        m_scr[...] = m_new
        pv = lax.dot_general(pj.astype(BF16), vt, (((0,), (1,)), ((), ())), preferred_element_type=F32)
        acc_scr[...] = alpha * acc_scr[...] + pv.T[0:acc_scr.shape[0]]

    m_scr[...] = jnp.full(m_scr.shape, -jnp.inf, F32)
    acc_scr[...] = jnp.zeros(acc_scr.shape, F32)
    scores(0, 0)
    last = q0 // tk

    @pl.loop(0, last // 2)
    def _(i):
        scores(2 * i + 1, 1)
        update(2 * i, 0, False)
        scores(2 * i + 2, 0)
        update(2 * i + 1, 1, False)

    @pl.when(last % 2 == 1)
    def _():
        scores(last, 1)
        update(last - 1, 0, False)
        update(last, 1, True)

    @pl.when(last % 2 == 0)
    def _():
        update(last, 0, True)

    o_sel = acc_scr[0:HEAD_DIM, :] / acc_scr[HEAD_DIM:HEAD_DIM + 1, :]
    gates = jax.nn.sigmoid(zgt_ref[0])
    hk = pl.program_id(1)

    def gate(g, branch):
        row = gates[g * N_BRANCH + branch:g * N_BRANCH + branch + 1]
        for other in range(1, N_KV_HEADS):
            r = (other * GROUP + g) * N_BRANCH + branch
            row = jnp.where(hk == other, gates[r:r + 1], row)
        return row

    for g in range(GROUP):
        sl = slice(g * ATTN_Q, (g + 1) * ATTN_Q)
        y = (gate(g, 0) * oc_scr[:, sl] + gate(g, 1) * o_sel[:, sl]
             + gate(g, 2) * ow_scr[:, sl])
        y_ref[0, :, g * HEAD_DIM:(g + 1) * HEAD_DIM] = y.T


def _attention(qt, zgt, kvc, kvct, ovt, kk, vt, onehot, B, seq):
    n_cmp = kvc.shape[2]
    n_sel = ovt.shape[0]
    GW = GROUP * HEAD_DIM
    GQ = GROUP * ATTN_Q
    VR = HEAD_DIM + ONES_ROWS
    tk = min(512, seq)
    span = WINDOW + ATTN_Q
    assert seq >= span
    body = functools.partial(_attn_body, tk=tk, span=span)
    resident = 4 * seq * HEAD_DIM * 2 + seq * n_sel * 2
    vmem = 2 * (GW * ATTN_Q * 2 + N_GATES * ATTN_Q * 4 + 2 * n_cmp * HEAD_DIM * 4 + n_sel * n_cmp * 2
                + resident + ATTN_Q * GW * 4) \
        + (2 * tk + VR + 2 * HEAD_DIM + SUBLANES) * GQ * 4 + 8 * (n_cmp + span) * GQ * 4
    return pl.pallas_call(
        body,
        grid=(B, N_KV_HEADS, seq // ATTN_Q),
        in_specs=[
            pl.BlockSpec((1, GW, ATTN_Q), lambda b, h, i: (b, h, i)),
            pl.BlockSpec((1, N_GATES, ATTN_Q), lambda b, h, i: (b, 0, i)),
            pl.BlockSpec((1, 1, n_cmp, HEAD_DIM), lambda b, h, i: (0, b * N_KV_HEADS + h, 0, 0)),
            pl.BlockSpec((1, 1, HEAD_DIM, n_cmp), lambda b, h, i: (1, b * N_KV_HEADS + h, 0, 0)),
            pl.BlockSpec((n_sel, n_cmp), lambda b, h, i: (0, 0)),
            pl.BlockSpec((1, 1, seq, HEAD_DIM), lambda b, h, i: (b, N_KV_HEADS + h, 0, 0)),
            pl.BlockSpec((1, HEAD_DIM, seq), lambda b, h, i: (b, N_KV_HEADS + h, 0)),
            pl.BlockSpec((seq, n_sel), lambda b, h, i: (0, 0)),
            pl.BlockSpec((1, 1, seq, HEAD_DIM), lambda b, h, i: (b, 2 * N_KV_HEADS + h, 0, 0)),
            pl.BlockSpec((1, HEAD_DIM, seq), lambda b, h, i: (b, 2 * N_KV_HEADS + h, 0)),
            pl.BlockSpec((1, span, ATTN_Q), lambda b, h, i: (jnp.minimum(i, WINDOW // ATTN_Q), 0, 0)),
        ],
        out_specs=pl.BlockSpec((1, ATTN_Q, GW), lambda b, h, i: (b, i, h)),
        out_shape=jax.ShapeDtypeStruct((B, seq, ATTN_W), F32),
        scratch_shapes=[pltpu.VMEM((2, tk, GQ), F32),
                        pltpu.VMEM((1, GQ), F32),
                        pltpu.VMEM((VR, GQ), F32),
                        pltpu.VMEM((HEAD_DIM, GQ), F32),
                        pltpu.VMEM((HEAD_DIM, GQ), F32)],
        compiler_params=_params(("parallel", "parallel", "arbitrary"), vmem),
        name="attention",
    )(qt, zgt, kvc, kvct, ovt, kk, vt, onehot, kk, vt, _window_bias(span))


def _outproj_body(yc_ref, ya_ref, gc_ref, ga_ref, w_ref, x_ref, h_ref, a_scr):
    a_scr[:, 0:CONV_W] = _rms(yc_ref[...], gc_ref[...]).astype(BF16)
    a_scr[:, CONV_W:] = _rms(ya_ref[...], ga_ref[...]).astype(BF16)
    for c in range(0, h_ref.shape[1], PROJ_TILE_N):
        cols = slice(c, c + PROJ_TILE_N)
        h_ref[:, cols] = x_ref[:, cols] + _dot(a_scr[...], w_ref[:, cols])


def _out_projection(yc, ya, g_conv, g_attn, w, x2):
    T, D = x2.shape
    K = w.shape[0]
    tm = min(512, T)
    assert D % PROJ_TILE_N == 0
    vmem = 2 * (tm * CONV_W * 4 + tm * ATTN_W * 4 + 2 * tm * D * 4) + K * D * 2 + tm * K * 2 \
        + 2 * tm * ATTN_W * 4 + 2 * tm * PROJ_TILE_N * 4
    return pl.pallas_call(
        _outproj_body,
        grid=(T // tm,),
        in_specs=[
            pl.BlockSpec((tm, CONV_W), lambda i: (i, 0)),
            pl.BlockSpec((tm, ATTN_W), lambda i: (i, 0)),
            pl.BlockSpec((1, CONV_W), lambda i: (0, 0)),
            pl.BlockSpec((1, ATTN_W), lambda i: (0, 0)),
            pl.BlockSpec((K, D), lambda i: (0, 0), pipeline_mode=pl.Buffered(1)),
            pl.BlockSpec((tm, D), lambda i: (i, 0)),
        ],
        out_specs=pl.BlockSpec((tm, D), lambda i: (i, 0)),
        out_shape=jax.ShapeDtypeStruct((T, D), F32),
        scratch_shapes=[pltpu.VMEM((tm, K), BF16)],
        compiler_params=_params(("parallel",), vmem),
        name="out_projection",
    )(yc, ya, g_conv, g_attn, w, x2)


def _ffn_body(h_ref, g_ref, wg_ref, wu_ref, wd_ref, gf_ref, o_ref, a_scr, *, final_norm, tn):
    j = pl.program_id(1)

    @pl.when(j == 0)
    def _():
        @pl.loop(0, h_ref.shape[0] // FFN_EPILOGUE_ROWS)
        def _(r):
            rows = pl.ds(pl.multiple_of(r * FFN_EPILOGUE_ROWS, FFN_EPILOGUE_ROWS), FFN_EPILOGUE_ROWS)
            a_scr[rows, :] = _rms(h_ref[rows, :], g_ref[...]).astype(BF16)

    a = a_scr[...]
    u = jnp.concatenate(
        [(jax.nn.silu(_dot(a, wg_ref[:, c:c + PROJ_CHUNK])) * _dot(a, wu_ref[:, c:c + PROJ_CHUNK])).astype(BF16)
         for c in range(0, wg_ref.shape[1], PROJ_CHUNK)], axis=1)
    D = o_ref.shape[1]

    @pl.when(j == 0)
    def _():
        for n in range(D // tn):
            o_ref[:, n * tn:(n + 1) * tn] = _dot(u, wd_ref[:, n * tn:(n + 1) * tn])

    @pl.when(j > 0)
    def _():
        for n in range(D // tn):
            o_ref[:, n * tn:(n + 1) * tn] += _dot(u, wd_ref[:, n * tn:(n + 1) * tn])

    @pl.when(j == pl.num_programs(1) - 1)
    def _():
        @pl.loop(0, o_ref.shape[0] // FFN_EPILOGUE_ROWS)
        def _(r):
            rows = pl.ds(pl.multiple_of(r * FFN_EPILOGUE_ROWS, FFN_EPILOGUE_ROWS), FFN_EPILOGUE_ROWS)
            out = h_ref[rows, :] + o_ref[rows, :]
            o_ref[rows, :] = _rms(out, gf_ref[...]) if final_norm else out


def _ffn(h, g, wg, wu, wd, g_final, final_norm):
    T, D = h.shape
    F = wg.shape[1]
    tm = min(1024, T)
    tf = 512
    tn = PROJ_CHUNK
    assert F % tf == 0 and D % tn == 0
    body = functools.partial(_ffn_body, final_norm=final_norm, tn=tn)
    vmem = 2 * (3 * D * tf * 2 + 2 * tm * D * 4) + tm * D * 2 + 6 * tm * tf * 4 + 2 * tm * tn * 4
    return pl.pallas_call(
        body,
        grid=(T // tm, F // tf),
        in_specs=[
            pl.BlockSpec((tm, D), lambda i, j: (i, 0)),
            pl.BlockSpec((1, D), lambda i, j: (0, 0)),
            pl.BlockSpec((D, tf), lambda i, j: (0, j)),
            pl.BlockSpec((D, tf), lambda i, j: (0, j)),
            pl.BlockSpec((tf, D), lambda i, j: (j, 0)),
            pl.BlockSpec((1, D), lambda i, j: (0, 0)),
        ],
        out_specs=pl.BlockSpec((tm, D), lambda i, j: (i, 0)),
        out_shape=jax.ShapeDtypeStruct((T, D), F32),
        scratch_shapes=[pltpu.VMEM((tm, D), BF16)],
        compiler_params=_params(("parallel", "arbitrary"), vmem),
        name="ffn",
    )(h, g, wg, wu, wd, g_final)


def _rope_tables(seq):
    half = HEAD_DIM // 2
    inv = 1.0 / (ROPE_THETA ** (jnp.arange(half, dtype=F32) / half))
    inv = jnp.concatenate([inv, inv])
    sign = jnp.concatenate([-jnp.ones((half,), F32), jnp.ones((half,), F32)])
    lo = jnp.arange(ROPE_SPLIT).astype(F32)[:, None] * inv[None, :]
    hi = (jnp.arange(seq // ROPE_SPLIT) * ROPE_SPLIT).astype(F32)[:, None] * inv[None, :]
    cos_lo, sin_lo = jnp.cos(lo)[None], jnp.sin(lo)[None]
    cos_hi, sin_hi = jnp.cos(hi)[:, None, :], jnp.sin(hi)[:, None, :]
    cos = (cos_hi * cos_lo - sin_hi * sin_lo).reshape(seq, HEAD_DIM)
    sin = ((sin_hi * cos_lo + cos_hi * sin_lo) * sign).reshape(seq, HEAD_DIM)
    return cos, sin


def _overlap_t(n_sel, n_cmp):
    cs = np.arange(n_cmp)[None, :] * CMP_STRIDE
    ss = np.arange(n_sel)[:, None] * SEL_BLOCK
    return jnp.asarray(((cs < ss + SEL_BLOCK) & (cs + CMP_BLOCK > ss)).astype(np.float32), dtype=BF16)


def _window_bias(span):
    r = np.arange(span)[None, :, None]
    c = np.arange(ATTN_Q)[None, None, :]
    off = np.minimum(np.arange(WINDOW // ATTN_Q + 1) * ATTN_Q, WINDOW)[:, None, None]
    valid = (r <= off + c) & (r > off + c - WINDOW)
    return jnp.asarray(np.where(valid, 0.0, -np.inf).astype(np.float32))


def _block_onehot(seq, n_sel):
    return jnp.asarray((np.arange(seq)[:, None] // SEL_BLOCK == np.arange(n_sel)[None, :])
                       .astype(np.float32), dtype=BF16)


def kernel(x, norm_mix, w_in, conv_w, cmp_pe_k, cmp_w1_k, cmp_w2_k, cmp_pe_v, cmp_w1_v, cmp_w2_v,
           norm_conv_out, norm_attn_out, w_out, norm_ffn, w_gate, w_up, w_down, norm_final):
    B, S, D = x.shape
    T = B * S
    depth = norm_mix.shape[0]
    assert S % ATTN_Q == 0 and S % CMP_STRIDE == 0 and S % ROPE_SPLIT == 0
    n_chunks = S // CMP_STRIDE
    n_sel = S // SEL_BLOCK
    cos_t, sin_t = _rope_tables(S)
    ovt = _overlap_t(n_sel, n_chunks)
    onehot = _block_onehot(S, n_sel)

    h = x.reshape(T, D)
    for l in range(depth):
        y_conv, qt, kk, kvc_in, vt, zgt, w_out_b, w_gate_b, w_up_b, w_down_b = _in_projection(
            h, norm_mix[l][None], w_in[l].astype(BF16), _gate_weight(w_in[l]), cos_t, sin_t, conv_w[l],
            [w_out[l], w_gate[l], w_up[l], w_down[l]], B, S)
        kvc_in = kvc_in.reshape(2, B * N_KV_HEADS, S, HEAD_DIM)
        pe = jnp.stack([cmp_pe_k[l], cmp_pe_v[l]]).reshape(2, 1, CMP_BLOCK * HEAD_DIM)
        kvc, kvct = _compress(kvc_in, pe, jnp.stack([cmp_w1_k[l], cmp_w1_v[l]]),
                              jnp.stack([cmp_w2_k[l], cmp_w2_v[l]]))

        y_attn = _attention(qt, zgt, kvc, kvct, ovt, kk, vt, onehot, B, S)

        h = _out_projection(y_conv, y_attn.reshape(T, ATTN_W), norm_conv_out[l][None],
                            norm_attn_out[l][None], w_out_b, h)

        h = _ffn(h, norm_ffn[l][None], w_gate_b, w_up_b, w_down_b, norm_final[None], l == depth - 1)
    return h.reshape(B, S, D)
```
